```python
import math
import jax, jax.numpy as jnp
from jax import lax
import numpy as np

D_MODEL = 1024
BATCH = 4
SEQ = 4096
DEPTH = 2

POOL_WIDTH = 512
POOL_WINDOWS = (2, 4, 8, 16)
POOL_GROUPS = 4
POOL_GROUP_DIM = POOL_WIDTH // POOL_GROUPS
SGU_WIDTH = 512
SGU_CHUNK = 128
SGU_HEADS = 4
SGU_HEAD_DIM = SGU_WIDTH // SGU_HEADS
SSM_WIDTH = 512
SSM_GROUP_DIM = 16
SSM_GROUPS = SSM_WIDTH // SSM_GROUP_DIM
SSM_STATE = 64
N_BRANCHES = 3
BRANCH_WIDTH = 512
IN_COLS = POOL_WIDTH + 2 * SGU_WIDTH + SSM_WIDTH + N_BRANCHES * D_MODEL
N_GROUPS = 4
EXPERTS_PER_GROUP = 8
N_EXPERTS = N_GROUPS * EXPERTS_PER_GROUP
EXPERT_TOP_K = 2
D_EXPERT = D_MODEL // 2
EPS = 1e-6

kernel_name = "hybrid_pool_sgu_s5_hmoe"

F32 = jnp.float32


def rmsnorm(x, g):
    xf = x.astype(F32)
    y = xf * lax.rsqrt(jnp.mean(xf * xf, axis=-1, keepdims=True) + EPS)
    return (y * g.astype(F32)).astype(x.dtype)


def layernorm(x, g, b):
    xf = x.astype(F32)
    mu = jnp.mean(xf, axis=-1, keepdims=True)
    var = jnp.mean(jnp.square(xf - mu), axis=-1, keepdims=True)
    y = (xf - mu) * lax.rsqrt(var + EPS)
    return (y * g.astype(F32) + b.astype(F32)).astype(x.dtype)


def pool_mixer(xa, pool_w, pool_scale):
    B_, L, _ = xa.shape
    xf = xa.astype(F32)
    maxw = POOL_WINDOWS[-1]
    cs = jnp.cumsum(xf, axis=1)
    cp = jnp.pad(cs, ((0, 0), (maxw + 1, 0), (0, 0)))
    pos = jnp.arange(L)
    outs = []
    for gi, w in enumerate(POOL_WINDOWS):
        ch = slice(gi * POOL_GROUP_DIM, (gi + 1) * POOL_GROUP_DIM)
        count = jnp.minimum(pos + 1, w).astype(F32)[:, None]
        win_sum = cp[:, maxw + 1:maxw + 1 + L, ch] - cp[:, maxw + 1 - w:maxw + 1 - w + L, ch]
        outs.append(win_sum / count - xf[..., ch])
    diff = jnp.stack(outs, axis=2)
    y = jnp.einsum('blgc,gcd->blgd', diff, pool_w.astype(F32)).reshape(B_, L, POOL_WIDTH)
    return (y * pool_scale.astype(F32)).astype(xa.dtype)


def sgu_mixer(xb, ln_g, ln_b, w_s, b_s):
    z = jax.nn.gelu(xb)
    u, v = jnp.split(z, 2, axis=-1)
    v = layernorm(v, ln_g, ln_b)
    B_, L, _ = v.shape
    n_chunks = L // SGU_CHUNK
    v = v.reshape(B_, n_chunks, SGU_CHUNK, SGU_HEADS, SGU_HEAD_DIM)
    mask = jnp.tril(jnp.ones((SGU_CHUNK, SGU_CHUNK), dtype=bool))[None]
    w = jnp.where(mask, w_s, jnp.zeros_like(w_s))
    s = jnp.einsum('hts,bnshc->bnthc', w, v) + b_s.T[None, None, :, :, None]
    return u * s.reshape(B_, L, SGU_WIDTH)


def _ssm_combine(e1, e2):
    a1, b1 = e1
    a2, b2 = e2
    return a1 * a2, a2 * b1 + b2


def s5_mixer(xc, a_re, a_im, log_dt, b_re, b_im, c_re, c_im, d_skip, glu_w, glu_b):
    B_, L, _ = xc.shape
    lam = lax.complex(a_re.astype(F32), a_im.astype(F32))
    dt = jnp.exp(log_dt.astype(F32))[:, None]
    lam_bar = jnp.exp(lam * dt)
    b = lax.complex(b_re.astype(F32), b_im.astype(F32))
    b_bar = ((lam_bar - 1.0) / lam)[..., None] * b
    c = lax.complex(c_re.astype(F32), c_im.astype(F32))
    u = xc.astype(F32).reshape(B_, L, SSM_GROUPS, SSM_GROUP_DIM)
    bu = jnp.einsum('blgc,gpc->blgp', u.astype(jnp.complex64), b_bar)
    a = jnp.broadcast_to(lam_bar, bu.shape)
    _, states = lax.associative_scan(_ssm_combine, (a, bu), axis=1)
    y = jnp.einsum('blgp,gcp->blgc', states, c).real
    y = y + d_skip.astype(F32).reshape(SSM_GROUPS, SSM_GROUP_DIM) * u
    y = jax.nn.gelu(y.reshape(B_, L, SSM_WIDTH))
    y = y * jax.nn.sigmoid(y @ glu_w.astype(F32) + glu_b.astype(F32))
    return y.astype(xc.dtype)


def hier_moe(h, rg_w, rg_b, re_w, re_b, w_gate, w_up, w_down):
    B_, L, D = h.shape
    t = h.reshape(-1, D)
    g_logits = (t @ rg_w + rg_b).astype(F32)
    g_probs = jax.nn.softmax(g_logits, axis=-1)
    g_val, g_idx = lax.top_k(g_probs, 1)
    e_logits = (t @ re_w + re_b).astype(F32).reshape(-1, N_GROUPS, EXPERTS_PER_GROUP)
    e_sel = jnp.take_along_axis(e_logits, g_idx[:, :, None], axis=1)[:, 0]
    e_val, e_idx = lax.top_k(e_sel, EXPERT_TOP_K)
    e_w = jax.nn.softmax(e_val, axis=-1) * g_val
    global_idx = g_idx * EXPERTS_PER_GROUP + e_idx
    combine = jnp.einsum('tk,tke->te', e_w,
                         jax.nn.one_hot(global_idx, N_EXPERTS, dtype=F32)).astype(h.dtype)
    out = jnp.zeros_like(t)
    for e in range(N_EXPERTS):
        y = (jax.nn.silu(t @ w_gate[e]) * (t @ w_up[e])) @ w_down[e]
        out = out + combine[:, e:e + 1] * y
    return out.reshape(B_, L, D)


def setup_inputs(seed: int = 0) -> dict:
    key = jax.random.key(seed)
    ks = jax.random.split(key, 32)
    nrm = jax.random.normal
    D = D_MODEL
    G, P, C = SSM_GROUPS, SSM_STATE, SSM_GROUP_DIM
    x = nrm(ks[0], (BATCH, SEQ, D), F32)
    norm_mix_g = 1.0 + 0.02 * nrm(ks[1], (DEPTH, D), F32)
    w_in = nrm(ks[2], (DEPTH, D, IN_COLS), F32) * D ** -0.5
    b_gate = 0.02 * nrm(ks[3], (DEPTH, N_BRANCHES * D), F32)
    pool_w = nrm(ks[4], (DEPTH, POOL_GROUPS, POOL_GROUP_DIM, POOL_GROUP_DIM), F32) * POOL_GROUP_DIM ** -0.5
    pool_scale = 1.0 + 0.02 * nrm(ks[5], (DEPTH, POOL_WIDTH), F32)
    sgu_ln_g = 1.0 + 0.02 * nrm(ks[6], (DEPTH, SGU_WIDTH), F32)
    sgu_ln_b = 0.02 * nrm(ks[7], (DEPTH, SGU_WIDTH), F32)
    sgu_w = 0.5 * nrm(ks[8], (DEPTH, SGU_HEADS, SGU_CHUNK, SGU_CHUNK), F32) * SGU_CHUNK ** -0.5
    sgu_b = 1.0 + 0.02 * nrm(ks[9], (DEPTH, SGU_HEADS, SGU_CHUNK), F32)
    ssm_a_re = -0.5 + 0.01 * nrm(ks[10], (DEPTH, G, P), F32)
    ssm_a_im = math.pi * jnp.arange(P, dtype=F32)[None, None, :] + 0.01 * nrm(ks[11], (DEPTH, G, P), F32)
    ssm_log_dt = jax.random.uniform(ks[12], (DEPTH, G), F32, math.log(1e-3), math.log(1e-1))
    ssm_b_re = nrm(ks[13], (DEPTH, G, P, C), F32) * (0.5 / C) ** 0.5
    ssm_b_im = nrm(ks[14], (DEPTH, G, P, C), F32) * (0.5 / C) ** 0.5
    ssm_c_re = nrm(ks[15], (DEPTH, G, C, P), F32) * (0.5 / P) ** 0.5
    ssm_c_im = nrm(ks[16], (DEPTH, G, C, P), F32) * (0.5 / P) ** 0.5
    ssm_d = nrm(ks[17], (DEPTH, SSM_WIDTH), F32)
    glu_w = nrm(ks[18], (DEPTH, SSM_WIDTH, SSM_WIDTH), F32) * SSM_WIDTH ** -0.5
    glu_b = 0.02 * nrm(ks[19], (DEPTH, SSM_WIDTH), F32)
    w_branch = nrm(ks[20], (DEPTH, N_BRANCHES, BRANCH_WIDTH, D), F32) * BRANCH_WIDTH ** -0.5
    w_out = nrm(ks[21], (DEPTH, D, D), F32) * D ** -0.5
    norm_ffn_g = 1.0 + 0.02 * nrm(ks[22], (DEPTH, D), F32)
    router_group_w = nrm(ks[23], (DEPTH, D, N_GROUPS), F32) * D ** -0.5
    router_group_b = 0.01 * nrm(ks[24], (DEPTH, N_GROUPS), F32)
    router_expert_w = nrm(ks[25], (DEPTH, D, N_EXPERTS), F32) * D ** -0.5
    router_expert_b = 0.01 * nrm(ks[26], (DEPTH, N_EXPERTS), F32)
    exp_w_gate = nrm(ks[27], (DEPTH, N_EXPERTS, D, D_EXPERT), F32) * D ** -0.5
    exp_w_up = nrm(ks[28], (DEPTH, N_EXPERTS, D, D_EXPERT), F32) * D ** -0.5
    exp_w_down = nrm(ks[29], (DEPTH, N_EXPERTS, D_EXPERT, D), F32) * D_EXPERT ** -0.5
    norm_final_g = 1.0 + 0.02 * nrm(ks[30], (D,), F32)
    return {"x": x, "norm_mix_g": norm_mix_g, "w_in": w_in, "b_gate": b_gate,
            "pool_w": pool_w, "pool_scale": pool_scale,
            "sgu_ln_g": sgu_ln_g, "sgu_ln_b": sgu_ln_b, "sgu_w": sgu_w, "sgu_b": sgu_b,
            "ssm_a_re": ssm_a_re, "ssm_a_im": ssm_a_im, "ssm_log_dt": ssm_log_dt,
            "ssm_b_re": ssm_b_re, "ssm_b_im": ssm_b_im, "ssm_c_re": ssm_c_re, "ssm_c_im": ssm_c_im,
            "ssm_d": ssm_d, "glu_w": glu_w, "glu_b": glu_b,
            "w_branch": w_branch, "w_out": w_out, "norm_ffn_g": norm_ffn_g,
            "router_group_w": router_group_w, "router_group_b": router_group_b,
            "router_expert_w": router_expert_w, "router_expert_b": router_expert_b,
            "exp_w_gate": exp_w_gate, "exp_w_up": exp_w_up, "exp_w_down": exp_w_down,
            "norm_final_g": norm_final_g}


def reference(x, norm_mix_g, w_in, b_gate, pool_w, pool_scale, sgu_ln_g, sgu_ln_b, sgu_w, sgu_b,
              ssm_a_re, ssm_a_im, ssm_log_dt, ssm_b_re, ssm_b_im, ssm_c_re, ssm_c_im, ssm_d,
              glu_w, glu_b, w_branch, w_out, norm_ffn_g, router_group_w, router_group_b,
              router_expert_w, router_expert_b, exp_w_gate, exp_w_up, exp_w_down, norm_final_g):
    B_, L, D = x.shape
    a_end = POOL_WIDTH
    b_end = a_end + 2 * SGU_WIDTH
    c_end = b_end + SSM_WIDTH
    for l in range(DEPTH):
        h = rmsnorm(x, norm_mix_g[l])
        proj = h @ w_in[l]
        ya = pool_mixer(proj[..., :a_end], pool_w[l], pool_scale[l])
        yb = sgu_mixer(proj[..., a_end:b_end], sgu_ln_g[l], sgu_ln_b[l], sgu_w[l], sgu_b[l])
        yc = s5_mixer(proj[..., b_end:c_end], ssm_a_re[l], ssm_a_im[l], ssm_log_dt[l],
                      ssm_b_re[l], ssm_b_im[l], ssm_c_re[l], ssm_c_im[l], ssm_d[l],
                      glu_w[l], glu_b[l])
        ys = jnp.stack([ya, yb, yc], axis=2)
        branches = jnp.einsum('blkc,kcd->blkd', ys, w_branch[l])
        gates = jax.nn.sigmoid(proj[..., c_end:].reshape(B_, L, N_BRANCHES, D)
                               + b_gate[l].reshape(N_BRANCHES, D))
        merged = jnp.sum(gates * branches, axis=2)
        x = x + merged @ w_out[l]
        h2 = rmsnorm(x, norm_ffn_g[l])
        x = x + hier_moe(h2, router_group_w[l], router_group_b[l], router_expert_w[l],
                         router_expert_b[l], exp_w_gate[l], exp_w_up[l], exp_w_down[l])
    return rmsnorm(x, norm_final_g)
```

```python
import functools
import math

import jax
import jax.numpy as jnp
from jax import lax
from jax.experimental import pallas as pl
from jax.experimental.pallas import tpu as pltpu

F32 = jnp.float32
BF16 = jnp.bfloat16
I32 = jnp.int32

D_MODEL = 1024
POOL_WINDOWS = (2, 4, 8, 16)
POOL_HALO = 16
LANE = 128
SGU_CHUNK = 128
SSM_GROUP_DIM = 16
SSM_STATE = 64
SSM_Q = 8
SSM_SLABS = 4
GROUPS_PER_SLAB = LANE // SSM_GROUP_DIM
N_GROUPS = 4
EXPERTS_PER_GROUP = 8
N_EXPERTS = 32
EPS = 1e-6
VMEM_LIMIT = 56 * 1024 * 1024

HI = lax.Precision.HIGHEST


def _rms(x, g):
    return x * lax.rsqrt(jnp.mean(x * x, axis=-1, keepdims=True) + EPS) * g


def _gelu(x):
    c = math.sqrt(2.0 / math.pi)
    return x * (0.5 * (1.0 + jnp.tanh(c * (x + 0.044715 * (x * x * x)))))


def _sigmoid(x):
    return 1.0 / (1.0 + jnp.exp(-x))


def _dot(a, b):
    return jnp.dot(a, b, preferred_element_type=F32)


def _const_spec(shape):
    nd = len(shape)
    return pl.BlockSpec(shape, lambda *_: (0,) * nd, pipeline_mode=pl.Buffered(1))


def _xc_kernel(x_ref, g_ref, w_ref, o_ref):
    h = _rms(x_ref[...], g_ref[...]).astype(BF16)
    xc = _dot(h, w_ref[...])
    for j in range(SSM_SLABS):
        o_ref[j] = xc[:, j * LANE:(j + 1) * LANE].astype(BF16)


def ssm_input(x2d, g, w_ssm, tm):
    t = x2d.shape[0]
    return pl.pallas_call(
        _xc_kernel,
        grid=(t // tm,),
        in_specs=[pl.BlockSpec((tm, D_MODEL), lambda i: (i, 0)),
                  _const_spec((1, D_MODEL)),
                  _const_spec((D_MODEL, SSM_SLABS * LANE))],
        out_specs=pl.BlockSpec((SSM_SLABS, tm, LANE), lambda i: (0, i, 0)),
        out_shape=jax.ShapeDtypeStruct((SSM_SLABS, t, LANE), BF16),
        compiler_params=pltpu.CompilerParams(dimension_semantics=("arbitrary",),
                                             vmem_limit_bytes=VMEM_LIMIT),
        name="ssm_input",
    )(x2d, g, w_ssm)


def ssm_tables(a_re, a_im, log_dt, b_re, b_im, c_re, c_im):
    q = SSM_Q
    ng = a_re.shape[0]
    dt = jnp.exp(log_dt)[:, None]
    k = jnp.arange(q + 1, dtype=F32)[:, None, None]
    mag = jnp.exp(a_re * dt * k)
    ang = a_im * dt * k
    pw_re, pw_im = mag * jnp.cos(ang), mag * jnp.sin(ang)
    nr, ni = pw_re[1] - 1.0, pw_im[1]
    den = a_re * a_re + a_im * a_im
    fr, fi = (nr * a_re + ni * a_im) / den, (ni * a_re - nr * a_im) / den
    bb_re = fr[..., None] * b_re - fi[..., None] * b_im
    bb_im = fr[..., None] * b_im + fi[..., None] * b_re
    eye = jnp.eye(GROUPS_PER_SLAB, dtype=F32)
    sl = ng // GROUPS_PER_SLAB

    def slabbed(a):
        return a.reshape(a.shape[0], sl, GROUPS_PER_SLAB, *a.shape[2:])

    k_rev = (q - 1.0) - jnp.arange(q, dtype=F32)[:, None, None]
    mag_rev = jnp.exp(a_re * dt * k_rev)
    rev_re, rev_im = mag_rev * jnp.cos(a_im * dt * k_rev), mag_rev * jnp.sin(a_im * dt * k_rev)
    m_re = rev_re[..., None] * bb_re[None] - rev_im[..., None] * bb_im[None]
    m_im = rev_re[..., None] * bb_im[None] + rev_im[..., None] * bb_re[None]
    m = jnp.stack([slabbed(m_re), slabbed(m_im)], axis=0)
    bq = jnp.einsum('ab,msjbpc->jsacmbp', eye, m, precision=HI)
    bq = bq.reshape(sl, q * LANE, 2 * GROUPS_PER_SLAB * SSM_STATE)
    pr, pi = pw_re[1:, :, None, :], pw_im[1:, :, None, :]
    n_re = c_re[None] * pr - c_im[None] * pi
    n_im = c_re[None] * pi + c_im[None] * pr
    n = jnp.stack([slabbed(n_re), -slabbed(n_im)], axis=0)
    cq = jnp.einsum('ab,mtjbcp->jmbptac', eye, n, precision=HI)
    cq = cq.reshape(sl, 2 * GROUPS_PER_SLAB * SSM_STATE, q * LANE)
    cb_re = jnp.einsum('gcp,gpd->gcpd', c_re, bb_re, precision=HI) - jnp.einsum('gcp,gpd->gcpd', c_im, bb_im, precision=HI)
    cb_im = jnp.einsum('gcp,gpd->gcpd', c_re, bb_im, precision=HI) + jnp.einsum('gcp,gpd->gcpd', c_im, bb_re, precision=HI)
    kern = (jnp.einsum('gcpd,kgp->kgcd', cb_re, pw_re[:q], precision=HI)
            - jnp.einsum('gcpd,kgp->kgcd', cb_im, pw_im[:q], precision=HI))
    s_idx = jnp.arange(q)[:, None]
    t_idx = jnp.arange(q)[None, :]
    lag = jnp.clip(t_idx - s_idx, 0, q - 1)
    kst = jnp.where((s_idx <= t_idx)[:, :, None, None, None], kern[lag], 0.0)
    kst = kst.reshape(q, q, sl, GROUPS_PER_SLAB, SSM_GROUP_DIM, SSM_GROUP_DIM)
    kq = jnp.einsum('ab,stjbcd->jsadtbc', eye, kst, precision=HI)
    kq = kq.reshape(sl, q * LANE, q * LANE)
    lam_q = jnp.concatenate([pw_re[q].reshape(sl, 1, -1), pw_im[q].reshape(sl, 1, -1)], axis=-1)
    return bq.astype(BF16), cq.astype(BF16), kq.astype(BF16), lam_q


def _s5_kernel(u_ref, bq_ref, cq_ref, kq_ref, lam_ref, y_ref, g_scr, s_scr):
    half = GROUPS_PER_SLAB * SSM_STATE
    rows = u_ref.shape[1]
    u = u_ref[0]
    g_scr[...] = _dot(u, bq_ref[0])
    lam = lam_ref[0]
    lre, lim = lam[:, :half], lam[:, half:]

    def body(n, carry):
        cre, cim = carry
        s_scr[pl.ds(n, 1), :] = jnp.concatenate([cre, cim], axis=1)
        g = g_scr[pl.ds(n, 1), :]
        nre = lre * cre - lim * cim + g[:, :half]
        nim = lre * cim + lim * cre + g[:, half:]
        return nre, nim

    zero = jnp.zeros((1, half), F32)
    lax.fori_loop(0, rows, body, (zero, zero), unroll=4)
    y = _dot(s_scr[...].astype(BF16), cq_ref[0]) + _dot(u, kq_ref[0])
    y_ref[0] = y.astype(BF16)


def s5_scan(u4, bq, cq, kq, lam_q, batch):
    sl, rows_total, width = u4.shape
    rows = rows_total // batch
    tab = pl.BlockSpec((1, width, width), lambda j, b: (j, 0, 0))
    return pl.pallas_call(
        _s5_kernel,
        grid=(sl, batch),
        in_specs=[pl.BlockSpec((1, rows, width), lambda j, b: (j, b, 0)), tab, tab, tab,
                  pl.BlockSpec((1, 1, width), lambda j, b: (j, 0, 0))],
        out_specs=pl.BlockSpec((1, rows, width), lambda j, b: (j, b, 0)),
        out_shape=jax.ShapeDtypeStruct(u4.shape, BF16),
        scratch_shapes=[pltpu.VMEM((rows, width), F32), pltpu.VMEM((rows, width), F32)],
        compiler_params=pltpu.CompilerParams(dimension_semantics=("arbitrary", "arbitrary"),
                                             vmem_limit_bytes=VMEM_LIMIT),
        name="s5_scan",
    )(u4, bq, cq, kq, lam_q)


def _mixer_kernel(x_ref, xc_ref, ys_ref, ng_ref, wpool_ref, wsgu_ref, wgate_ref, bgate_ref,
                  poolw_ref, pscale_ref, lng_ref, lnb_ref, sguw_ref, sgub_ref, ssmd_ref,
                  gluw_ref, glub_ref, wbr_ref, wout_ref, o_ref, xe_scr):
    tm = x_ref.shape[0]
    i = pl.program_id(1)
    x = x_ref[...]
    hb = _rms(x, ng_ref[...]).astype(BF16)

    xa = _dot(hb, wpool_ref[...])

    @pl.when(i == 0)
    def _():
        xe_scr[0:POOL_HALO, :] = jnp.zeros((POOL_HALO, xe_scr.shape[1]), F32)

    xe_scr[POOL_HALO:POOL_HALO + tm, :] = xa
    pos = i * tm + lax.broadcasted_iota(I32, (tm, 1), 0)
    ya_parts = []
    for gi, w in enumerate(POOL_WINDOWS):
        ch = slice(gi * LANE, (gi + 1) * LANE)
        win = xa[:, ch]
        for j in range(1, w):
            win = win + xe_scr[POOL_HALO - j:POOL_HALO - j + tm, ch]
        cnt = jnp.minimum(pos + 1, w).astype(F32)
        diff = win / cnt - xa[:, ch]
        ya_parts.append(_dot(diff.astype(BF16), poolw_ref[gi]) * pscale_ref[:, ch])
    ya = jnp.concatenate(ya_parts, axis=1)
    xe_scr[0:POOL_HALO, :] = xe_scr[tm:tm + POOL_HALO, :]

    z = _gelu(_dot(hb, wsgu_ref[...]))
    half = z.shape[1] // 2
    u, v = z[:, :half], z[:, half:]
    mu = jnp.mean(v, axis=-1, keepdims=True)
    vc = v - mu
    var = jnp.mean(vc * vc, axis=-1, keepdims=True)
    vb = (vc * lax.rsqrt(var + EPS) * lng_ref[...] + lnb_ref[...]).astype(BF16)
    r_i = lax.broadcasted_iota(I32, (SGU_CHUNK, SGU_CHUNK), 0)
    c_i = lax.broadcasted_iota(I32, (SGU_CHUNK, SGU_CHUNK), 1)
    n_heads = half // LANE
    wt = [jnp.where(r_i >= c_i, sguw_ref[hd], 0.0).astype(BF16) for hd in range(n_heads)]
    s_rows = []
    for c in range(tm // SGU_CHUNK):
        rs = slice(c * SGU_CHUNK, (c + 1) * SGU_CHUNK)
        s_rows.append(jnp.concatenate(
            [_dot(wt[hd], vb[rs, hd * LANE:(hd + 1) * LANE]) for hd in range(n_heads)], axis=1)
            + sgub_ref[...])
    yb = u * jnp.concatenate(s_rows, axis=0)

    ysc = jnp.concatenate([ys_ref[j] for j in range(SSM_SLABS)], axis=1).astype(F32)
    uc = jnp.concatenate([xc_ref[j] for j in range(SSM_SLABS)], axis=1).astype(F32)
    yc = _gelu(ysc + ssmd_ref[...] * uc)
    yc = yc * _sigmoid(_dot(yc.astype(BF16), gluw_ref[...]) + glub_ref[...])

    merged = None
    for k, yk in enumerate((ya, yb, yc)):
        cs = slice(k * D_MODEL, (k + 1) * D_MODEL)
        gate = _sigmoid(_dot(hb, wgate_ref[:, cs]) + bgate_ref[:, cs])
        term = gate * _dot(yk.astype(BF16), wbr_ref[k])
        merged = term if merged is None else merged + term
    o_ref[...] = x + _dot(merged.astype(BF16), wout_ref[...])


def mixer(x2d, xc4, ys4, p, batch, tm):
    t = x2d.shape[0]
    nt = t // batch // tm
    row = lambda b, i: (b * nt + i, 0)
    slab = lambda b, i: (0, b * nt + i, 0)
    consts = [p["norm_mix_g"], p["w_pool"], p["w_sgu"], p["w_gate"], p["b_gate"], p["pool_w"],
              p["pool_scale"], p["sgu_ln_g"], p["sgu_ln_b"], p["sgu_w"], p["sgu_b_full"], p["ssm_d"],
              p["glu_w"], p["glu_b"], p["w_branch"], p["w_out"]]
    return pl.pallas_call(
        _mixer_kernel,
        grid=(batch, nt),
        in_specs=[pl.BlockSpec((tm, D_MODEL), row),
                  pl.BlockSpec((SSM_SLABS, tm, LANE), slab),
                  pl.BlockSpec((SSM_SLABS, tm, LANE), slab)] + [_const_spec(c.shape) for c in consts],
        out_specs=pl.BlockSpec((tm, D_MODEL), row),
        out_shape=jax.ShapeDtypeStruct(x2d.shape, F32),
        scratch_shapes=[pltpu.VMEM((tm + POOL_HALO, SSM_SLABS * LANE), F32)],
        compiler_params=pltpu.CompilerParams(dimension_semantics=("arbitrary", "arbitrary"),
                                             vmem_limit_bytes=VMEM_LIMIT),
        name="mixer",
    )(x2d, xc4, ys4, *consts)


def _router_kernel(x_ref, g_ref, wr_ref, br_ref, h_ref, ew_ref, ei_ref, cnt_ref, carry_scr):
    tm = x_ref.shape[0]
    i = pl.program_id(0)

    @pl.when(i == 0)
    def _():
        carry_scr[...] = jnp.zeros_like(carry_scr)

    h = _rms(x_ref[...], g_ref[...])
    h_ref[...] = h
    logits = jnp.dot(h, wr_ref[...], preferred_element_type=F32, precision=HI) + br_ref[...]
    lane = lax.broadcasted_iota(I32, logits.shape, 1)
    neg = -jnp.inf
    gl = jnp.where(lane < N_GROUPS, logits, neg)
    gmax = jnp.max(gl, axis=-1, keepdims=True)
    gsum = jnp.sum(jnp.where(lane < N_GROUPS, jnp.exp(logits - gmax), 0.0), axis=-1, keepdims=True)
    g_val = 1.0 / gsum
    g_idx = jnp.min(jnp.where(gl == gmax, lane, LANE), axis=-1, keepdims=True)
    lo = N_GROUPS + EXPERTS_PER_GROUP * g_idx
    el = jnp.where((lane >= lo) & (lane < lo + EXPERTS_PER_GROUP), logits, neg)
    e1v = jnp.max(el, axis=-1, keepdims=True)
    e1l = jnp.min(jnp.where(el == e1v, lane, LANE), axis=-1, keepdims=True)
    el2 = jnp.where(lane == e1l, neg, el)
    e2v = jnp.max(el2, axis=-1, keepdims=True)
    e2l = jnp.min(jnp.where(el2 == e2v, lane, LANE), axis=-1, keepdims=True)
    tt = jnp.exp(e2v - e1v)
    w1 = g_val / (1.0 + tt)
    w2 = g_val * tt / (1.0 + tt)
    e1, e2 = e1l - N_GROUPS, e2l - N_GROUPS
    hit1, hit2 = lane == e1, lane == e2
    onehot = jnp.where(hit1 | hit2, 1.0, 0.0)
    r_i = lax.broadcasted_iota(I32, (tm, tm), 0)
    c_i = lax.broadcasted_iota(I32, (tm, tm), 1)
    tri = jnp.where(r_i > c_i, 1.0, 0.0).astype(BF16)
    prefix = _dot(tri, onehot.astype(BF16)) + carry_scr[0:1, :]
    rank1 = jnp.sum(jnp.where(hit1, prefix, 0.0), axis=-1, keepdims=True).astype(I32)
    rank2 = jnp.sum(jnp.where(hit2, prefix, 0.0), axis=-1, keepdims=True).astype(I32)
    total = carry_scr[0:1, :] + jnp.sum(onehot, axis=0, keepdims=True)
    carry_scr[...] = jnp.broadcast_to(total, carry_scr.shape)
    cnt_ref[...] = jnp.broadcast_to(total, cnt_ref.shape)
    ew_ref[...] = jnp.where(lane == 0, w1, jnp.where(lane == 1, w2, 0.0))
    ei_ref[...] = jnp.where(lane == 0, e1, jnp.where(lane == 1, e2, jnp.where(
        lane == 2, rank1, jnp.where(lane == 3, rank2, 0))))


def router(x2d, g, wr, br, tm):
    t = x2d.shape[0]
    row = lambda i: (i, 0)
    return pl.pallas_call(
        _router_kernel,
        grid=(t // tm,),
        in_specs=[pl.BlockSpec((tm, D_MODEL), row), _const_spec((1, D_MODEL)),
                  _const_spec((D_MODEL, LANE)), _const_spec((1, LANE))],
        out_specs=[pl.BlockSpec((tm, D_MODEL), row), pl.BlockSpec((tm, LANE), row),
                   pl.BlockSpec((tm, LANE), row), pl.BlockSpec((8, LANE), lambda i: (0, 0))],
        out_shape=[jax.ShapeDtypeStruct((t, D_MODEL), F32), jax.ShapeDtypeStruct((t, LANE), F32),
                   jax.ShapeDtypeStruct((t, LANE), I32), jax.ShapeDtypeStruct((8, LANE), F32)],
        scratch_shapes=[pltpu.VMEM((8, LANE), F32)],
        compiler_params=pltpu.CompilerParams(dimension_semantics=("arbitrary",),
                                             vmem_limit_bytes=VMEM_LIMIT),
        name="router",
    )(x2d, g, wr, br)


def _dispatch_kernel(pos_ref, h_ref, init_ref, o_ref, sem):
    del init_ref
    tm = h_ref.shape[0]
    base = pl.program_id(0) * tm

    def row_copy(r, k):
        p = pos_ref[2 * (base + r) + k]
        return pltpu.make_async_copy(h_ref.at[pl.ds(r, 1)], o_ref.at[pl.ds(p, 1)], sem)

    def start(r, c):
        row_copy(r, 0).start()
        row_copy(r, 1).start()
        return c

    def wait(r, c):
        row_copy(r, 0).wait()
        row_copy(r, 1).wait()
        return c

    lax.fori_loop(0, tm, start, 0)
    lax.fori_loop(0, tm, wait, 0)


def dispatch(pos, h, n_rows, tm):
    t, d = h.shape
    init = jnp.zeros((n_rows, d), h.dtype)
    return pl.pallas_call(
        _dispatch_kernel,
        grid_spec=pltpu.PrefetchScalarGridSpec(
            num_scalar_prefetch=1,
            grid=(t // tm,),
            in_specs=[pl.BlockSpec((tm, d), lambda i, pos: (i, 0)),
                      pl.BlockSpec(memory_space=pl.ANY)],
            out_specs=pl.BlockSpec(memory_space=pl.ANY),
            scratch_shapes=[pltpu.SemaphoreType.DMA(())]),
        out_shape=jax.ShapeDtypeStruct((n_rows, d), h.dtype),
        input_output_aliases={2: 0},
        compiler_params=pltpu.CompilerParams(dimension_semantics=("arbitrary",),
                                             vmem_limit_bytes=VMEM_LIMIT),
        name="dispatch",
    )(pos, h, init)


def _expert_kernel(bexp_ref, nused_ref, xs_ref, wg_ref, wu_ref, wd_ref, ys_ref, wg_scr, wu_scr, wd_scr):
    i = pl.program_id(0)

    @pl.when(i < nused_ref[0])
    def _():
        changed = jnp.logical_or(i == 0, bexp_ref[i] != bexp_ref[jnp.maximum(i - 1, 0)])

        @pl.when(changed)
        def _():
            wg_scr[...] = wg_ref[...].astype(BF16)
            wu_scr[...] = wu_ref[...].astype(BF16)
            wd_scr[...] = wd_ref[...].astype(BF16)

        xb = xs_ref[...].astype(BF16)
        hg = _dot(xb, wg_scr[...])
        hu = _dot(xb, wu_scr[...])
        act = (hg * _sigmoid(hg) * hu).astype(BF16)
        ys_ref[...] = _dot(act, wd_scr[...])

    @pl.when(i >= nused_ref[0])
    def _():
        ys_ref[...] = jnp.zeros_like(ys_ref)


def experts(bexp, nused, xs, w_gate, w_up, w_down, layer, bm):
    n_rows, d = xs.shape
    de = w_gate.shape[3]
    blk = lambda i, bexp, nused: (jnp.minimum(i, nused[0] - 1), 0)
    wsel = lambda i, bexp, nused: (layer, bexp[i], 0, 0)
    return pl.pallas_call(
        _expert_kernel,
        grid_spec=pltpu.PrefetchScalarGridSpec(
            num_scalar_prefetch=2,
            grid=(n_rows // bm,),
            in_specs=[pl.BlockSpec((bm, d), blk), pl.BlockSpec((None, None, d, de), wsel),
                      pl.BlockSpec((None, None, d, de), wsel), pl.BlockSpec((None, None, de, d), wsel)],
            out_specs=pl.BlockSpec((bm, d), lambda i, bexp, nused: (i, 0)),
            scratch_shapes=[pltpu.VMEM((d, de), BF16), pltpu.VMEM((d, de), BF16),
                            pltpu.VMEM((de, d), BF16)]),
        out_shape=jax.ShapeDtypeStruct((n_rows, d), F32),
        compiler_params=pltpu.CompilerParams(dimension_semantics=("arbitrary",),
                                             vmem_limit_bytes=VMEM_LIMIT),
        name="experts",
    )(bexp, nused, xs, w_gate, w_up, w_down)


def _combine_kernel(pos_ref, x_ref, ew_ref, g_ref, ys_ref, o_ref, buf, sem, *, final):
    tm = x_ref.shape[0]
    i = pl.program_id(0)
    n = pl.num_programs(0)

    def row_copy(tile, slot, r, k):
        p = pos_ref[2 * (tile * tm + r) + k]
        return pltpu.make_async_copy(ys_ref.at[pl.ds(p, 1)], buf.at[slot, k, pl.ds(r, 1)], sem.at[slot])

    def issue(tile, slot):
        def body(r, c):
            row_copy(tile, slot, r, 0).start()
            row_copy(tile, slot, r, 1).start()
            return c
        lax.fori_loop(0, tm, body, 0)

    @pl.when(i == 0)
    def _():
        issue(0, 0)

    @pl.when(i + 1 < n)
    def _():
        issue(i + 1, (i + 1) % 2)

    slot = i % 2

    def wait(r, c):
        row_copy(i, slot, r, 0).wait()
        row_copy(i, slot, r, 1).wait()
        return c

    lax.fori_loop(0, tm, wait, 0)
    ew = ew_ref[...]
    out = x_ref[...] + ew[:, 0:1] * buf[slot, 0] + ew[:, 1:2] * buf[slot, 1]
    if final:
        out = _rms(out, g_ref[...])
    o_ref[...] = out


def combine(pos, x2d, ew, g, ys, tm, final):
    t, d = x2d.shape
    row = lambda i, pos: (i, 0)
    return pl.pallas_call(
        functools.partial(_combine_kernel, final=final),
        grid_spec=pltpu.PrefetchScalarGridSpec(
            num_scalar_prefetch=1,
            grid=(t // tm,),
            in_specs=[pl.BlockSpec((tm, d), row), pl.BlockSpec((tm, LANE), row),
                      pl.BlockSpec((1, d), lambda i, pos: (0, 0)),
                      pl.BlockSpec(memory_space=pl.ANY)],
            out_specs=pl.BlockSpec((tm, d), row),
            scratch_shapes=[pltpu.VMEM((2, 2, tm, d), ys.dtype), pltpu.SemaphoreType.DMA((2,))]),
        out_shape=jax.ShapeDtypeStruct((t, d), F32),
        compiler_params=pltpu.CompilerParams(dimension_semantics=("arbitrary",),
                                             vmem_limit_bytes=VMEM_LIMIT),
        name="combine",
    )(pos, x2d, ew, g, ys)


def moe(x2d, g, rg_w, rg_b, re_w, re_b, w_gate, w_up, w_down, layer, g_final, final, tm, bm):
    t = x2d.shape[0]
    pad = LANE - N_GROUPS - N_EXPERTS
    wr = jnp.concatenate([rg_w, re_w, jnp.zeros((D_MODEL, pad), F32)], axis=1)
    br = jnp.concatenate([rg_b, re_b, jnp.zeros((pad,), F32)])[None, :]
    h, ew, ei, cnt = router(x2d, g[None, :], wr, br, tm)
    counts = cnt[0, :N_EXPERTS].astype(I32)
    padded = (counts + bm - 1) // bm * bm
    ends = jnp.cumsum(padded)
    offs = ends - padded
    n_rows = (2 * t // bm + N_EXPERTS) * bm
    pos = (offs[ei[:, 0:2]] + ei[:, 2:4]).reshape(-1)
    nused = (ends[-1] // bm).astype(I32)
    blk_start = jnp.arange(n_rows // bm, dtype=I32) * bm
    bexp = jnp.sum((ends[None, :] <= blk_start[:, None]).astype(I32), axis=1)
    last = jnp.sum((ends <= (nused - 1) * bm).astype(I32))
    bexp = jnp.minimum(bexp, last)
    xs = dispatch(pos, h, n_rows, tm)
    ys = experts(bexp, nused[None], xs, w_gate, w_up, w_down, layer, bm)
    return combine(pos, x2d, ew, g_final[None, :], ys, tm, final)


def kernel(x, norm_mix_g, w_in, b_gate, pool_w, pool_scale, sgu_ln_g, sgu_ln_b, sgu_w, sgu_b, ssm_a_re, ssm_a_im, ssm_log_dt, ssm_b_re, ssm_b_im, ssm_c_re, ssm_c_im, ssm_d, glu_w, glu_b, w_branch, w_out, norm_ffn_g, router_group_w, router_group_b, router_expert_w, router_expert_b, exp_w_gate, exp_w_up, exp_w_down, norm_final_g):
    batch, seq, d = x.shape
    depth = w_in.shape[0]
    tm = 256
    bm = 256
    x2d = x.reshape(batch * seq, d)
    a_end = SSM_SLABS * LANE
    b_end = a_end + 2 * SSM_SLABS * LANE
    c_end = b_end + SSM_SLABS * LANE
    for l in range(depth):
        wl = w_in[l].astype(BF16)
        g_mix = norm_mix_g[l][None, :]
        xc4 = ssm_input(x2d, g_mix, wl[:, b_end:c_end], tm)
        bq, cq, kq, lam_q = ssm_tables(ssm_a_re[l], ssm_a_im[l], ssm_log_dt[l], ssm_b_re[l], ssm_b_im[l],
                                       ssm_c_re[l], ssm_c_im[l])
        rows = batch * seq // SSM_Q
        ys4 = s5_scan(xc4.reshape(SSM_SLABS, rows, SSM_Q * LANE), bq, cq, kq, lam_q, batch)
        ys4 = ys4.reshape(SSM_SLABS, batch * seq, LANE)
        p = {
            "norm_mix_g": g_mix,
            "w_pool": wl[:, :a_end], "w_sgu": wl[:, a_end:b_end], "w_gate": wl[:, c_end:],
            "b_gate": b_gate[l][None, :],
            "pool_w": pool_w[l].astype(BF16), "pool_scale": pool_scale[l][None, :],
            "sgu_ln_g": sgu_ln_g[l][None, :], "sgu_ln_b": sgu_ln_b[l][None, :],
            "sgu_w": sgu_w[l],
            "sgu_b_full": jnp.repeat(sgu_b[l].T, LANE, axis=1),
            "ssm_d": ssm_d[l][None, :],
            "glu_w": glu_w[l].astype(BF16), "glu_b": glu_b[l][None, :],
            "w_branch": w_branch[l].astype(BF16), "w_out": w_out[l].astype(BF16),
        }
        x2d = mixer(x2d, xc4, ys4, p, batch, tm)
        x2d = moe(x2d, norm_ffn_g[l], router_group_w[l], router_group_b[l], router_expert_w[l],
                  router_expert_b[l], exp_w_gate, exp_w_up, exp_w_down, l, norm_final_g,
                  l == depth - 1, tm, bm)
    return x2d.reshape(batch, seq, d)
```

```python
import functools
import math

import jax
import jax.numpy as jnp
from jax import lax
from jax.experimental import pallas as pl
from jax.experimental.pallas import tpu as pltpu

F32 = jnp.float32
BF16 = jnp.bfloat16
I32 = jnp.int32

D_MODEL = 1024
POOL_WINDOWS = (2, 4, 8, 16)
POOL_HALO = 16
LANE = 128
SUBLANE = 8
ROW_TILES = D_MODEL // LANE
SGU_CHUNK = 128
SSM_GROUP_DIM = 16
SSM_STATE = 64
SSM_Q = 8
SSM_SLABS = 4
GROUPS_PER_SLAB = LANE // SSM_GROUP_DIM
N_GROUPS = 4
EXPERTS_PER_GROUP = 8
N_EXPERTS = 32
EPS = 1e-6
VMEM_LIMIT = 56 * 1024 * 1024

HI = lax.Precision.HIGHEST


def _rms(x, g):
    return x * lax.rsqrt(jnp.mean(x * x, axis=-1, keepdims=True) + EPS) * g


def _gelu(x):
    c = math.sqrt(2.0 / math.pi)
    return x * (0.5 * (1.0 + jnp.tanh(c * (x + 0.044715 * (x * x * x)))))


def _sigmoid(x):
    return 1.0 / (1.0 + jnp.exp(-x))


def _dot(a, b):
    return jnp.dot(a, b, preferred_element_type=F32)


def _const_spec(shape):
    nd = len(shape)
    return pl.BlockSpec(shape, lambda *_: (0,) * nd, pipeline_mode=pl.Buffered(1))


def _params(n_axes):
    return pltpu.CompilerParams(dimension_semantics=("arbitrary",) * n_axes, vmem_limit_bytes=VMEM_LIMIT)


def _xc_kernel(x_ref, g_ref, w_ref, o_ref, xc_scr):
    h = _rms(x_ref[...], g_ref[...]).astype(BF16)
    xc = _dot(h, w_ref[...])
    rows = xc_scr.shape[1] // SSM_Q
    for j in range(SSM_SLABS):
        xc_scr[j] = xc[:, j * LANE:(j + 1) * LANE]
        for s in range(SSM_Q):
            o_ref[j, :, s * LANE:(s + 1) * LANE] = xc_scr[j, pl.ds(s, rows, stride=SSM_Q), :].astype(BF16)


def ssm_input(x2d, g, w_in_bf, col_block, tm):
    t = x2d.shape[0]
    width = SSM_SLABS * LANE
    return pl.pallas_call(
        _xc_kernel,
        grid=(t // tm,),
        in_specs=[pl.BlockSpec((tm, D_MODEL), lambda i: (i, 0)),
                  _const_spec((1, D_MODEL)),
                  pl.BlockSpec((D_MODEL, width), lambda i: (0, col_block), pipeline_mode=pl.Buffered(1))],
        out_specs=pl.BlockSpec((SSM_SLABS, tm // SSM_Q, SSM_Q * LANE), lambda i: (0, i, 0)),
        out_shape=jax.ShapeDtypeStruct((SSM_SLABS, t // SSM_Q, SSM_Q * LANE), BF16),
        scratch_shapes=[pltpu.VMEM((SSM_SLABS, tm, LANE), F32)],
        compiler_params=_params(1),
        name="ssm_input",
    )(x2d, g, w_in_bf)


def _block_diag(small):
    eye = jnp.eye(GROUPS_PER_SLAB, dtype=F32)
    out = small[:, :, :, :, :, None, :] * eye[None, None, :, None, None, :, None]
    sl, r1, a, r2, c1, b, c2 = out.shape
    return out.reshape(sl, r1 * a * r2, c1 * b * c2).astype(BF16)


def ssm_tables(a_re, a_im, log_dt, b_re, b_im, c_re, c_im, d_skip):
    q = SSM_Q
    ng = a_re.shape[0]
    sl = ng // GROUPS_PER_SLAB
    dt = jnp.exp(log_dt)[:, None]
    k = jnp.arange(q + 1, dtype=F32)[:, None, None]
    mag = jnp.exp(a_re * dt * k)
    ang = a_im * dt * k
    pw_re, pw_im = mag * jnp.cos(ang), mag * jnp.sin(ang)
    nr, ni = pw_re[1] - 1.0, pw_im[1]
    den = a_re * a_re + a_im * a_im
    fr, fi = (nr * a_re + ni * a_im) / den, (ni * a_re - nr * a_im) / den
    bb_re = fr[..., None] * b_re - fi[..., None] * b_im
    bb_im = fr[..., None] * b_im + fi[..., None] * b_re

    def slabbed(a, axis):
        return a.reshape(*a.shape[:axis], sl, GROUPS_PER_SLAB, *a.shape[axis + 1:])

    k_rev = (q - 1.0) - jnp.arange(q, dtype=F32)[:, None, None]
    mag_rev = jnp.exp(a_re * dt * k_rev)
    rev_re, rev_im = mag_rev * jnp.cos(a_im * dt * k_rev), mag_rev * jnp.sin(a_im * dt * k_rev)
    m_re = rev_re[..., None] * bb_re[None] - rev_im[..., None] * bb_im[None]
    m_im = rev_re[..., None] * bb_im[None] + rev_im[..., None] * bb_re[None]
    m = slabbed(jnp.stack([m_re, m_im], axis=0), 2)
    bq = _block_diag(m.transpose(2, 1, 3, 5, 0, 4))
    pr, pi = pw_re[1:, :, None, :], pw_im[1:, :, None, :]
    n_re = c_re[None] * pr - c_im[None] * pi
    n_im = c_re[None] * pi + c_im[None] * pr
    n = slabbed(jnp.stack([n_re, -n_im], axis=0), 2)
    cq = _block_diag(n.transpose(2, 0, 3, 5, 1, 4))
    cb_re = c_re[:, :, :, None] * bb_re[:, None, :, :] - c_im[:, :, :, None] * bb_im[:, None, :, :]
    cb_im = c_re[:, :, :, None] * bb_im[:, None, :, :] + c_im[:, :, :, None] * bb_re[:, None, :, :]
    kern = jnp.sum(cb_re[None] * pw_re[:q, :, None, :, None] - cb_im[None] * pw_im[:q, :, None, :, None],
                   axis=3)
    skip = d_skip.reshape(ng, SSM_GROUP_DIM)[:, :, None] * jnp.eye(SSM_GROUP_DIM, dtype=F32)[None]
    kern = kern + jnp.where(jnp.arange(q)[:, None, None, None] == 0, skip[None], 0.0)
    s_idx = jnp.arange(q)[:, None]
    t_idx = jnp.arange(q)[None, :]
    onehot_lag = (t_idx - s_idx)[:, :, None] == jnp.arange(q)[None, None, :]
    kst = jnp.sum(jnp.where(onehot_lag[..., None, None, None], kern[None, None], 0.0), axis=2)
    kst = slabbed(kst, 2)
    kq = _block_diag(kst.transpose(2, 0, 3, 5, 1, 4))
    lam_q = jnp.concatenate([pw_re[q].reshape(sl, 1, -1), pw_im[q].reshape(sl, 1, -1)], axis=-1)
    return bq, cq, kq, lam_q


def _s5_kernel(u_ref, bq_ref, cq_ref, kq_ref, lam_ref, y_ref, g_scr, s_scr, t_scr):
    half = GROUPS_PER_SLAB * SSM_STATE
    rows = u_ref.shape[1]
    u = u_ref[0]
    g_scr[...] = _dot(u, bq_ref[0])
    lam = lam_ref[0]
    lre, lim = lam[:, :half], lam[:, half:]

    def body(n, carry):
        cre, cim = carry
        s_scr[pl.ds(n, 1), :] = jnp.concatenate([cre, cim], axis=1)
        g = g_scr[pl.ds(n, 1), :]
        nre = lre * cre - lim * cim + g[:, :half]
        nim = lre * cim + lim * cre + g[:, half:]
        return nre, nim

    zero = jnp.zeros((1, half), F32)
    lax.fori_loop(0, rows, body, (zero, zero), unroll=4)
    y = _dot(s_scr[...].astype(BF16), cq_ref[0]) + _dot(u, kq_ref[0])
    for t in range(SSM_Q):
        t_scr[pl.ds(t, rows, stride=SSM_Q), :] = y[:, t * LANE:(t + 1) * LANE]
    y_ref[0] = t_scr[...].astype(BF16)


def s5_scan(u4, bq, cq, kq, lam_q, batch):
    sl, rows_total, width = u4.shape
    rows = rows_total // batch
    tab = pl.BlockSpec((1, width, width), lambda j, b: (j, 0, 0))
    return pl.pallas_call(
        _s5_kernel,
        grid=(sl, batch),
        in_specs=[pl.BlockSpec((1, rows, width), lambda j, b: (j, b, 0)), tab, tab, tab,
                  pl.BlockSpec((1, 1, width), lambda j, b: (j, 0, 0))],
        out_specs=pl.BlockSpec((1, rows * SSM_Q, LANE), lambda j, b: (j, b, 0)),
        out_shape=jax.ShapeDtypeStruct((sl, rows_total * SSM_Q, LANE), BF16),
        scratch_shapes=[pltpu.VMEM((rows, width), F32), pltpu.VMEM((rows, width), F32),
                        pltpu.VMEM((rows * SSM_Q, LANE), F32)],
        compiler_params=_params(2),
        name="s5_scan",
    )(u4, bq, cq, kq, lam_q)


def _mixer_kernel(x_ref, ys_ref, ng_ref, win_ref, bgate_ref, poolw_ref, pscale_ref, lng_ref, lnb_ref,
                  sguw_ref, sgub_ref, gluw_ref, glub_ref, wbr_ref, wout_ref, o_ref, xe_scr):
    tm = x_ref.shape[0]
    width = SSM_SLABS * LANE
    a_end, b_end = width, 3 * width
    c_end = b_end + width
    i = pl.program_id(1)
    x = x_ref[...]
    hb = _rms(x, ng_ref[...]).astype(BF16)

    xa = _dot(hb, win_ref[:, 0:a_end])

    @pl.when(i == 0)
    def _():
        xe_scr[0:POOL_HALO, :] = jnp.zeros((POOL_HALO, xe_scr.shape[1]), F32)

    xe_scr[POOL_HALO:POOL_HALO + tm, :] = xa
    pos = i * tm + lax.broadcasted_iota(I32, (tm, 1), 0)
    ya_parts = []
    for gi, w in enumerate(POOL_WINDOWS):
        ch = slice(gi * LANE, (gi + 1) * LANE)
        win = xa[:, ch]
        for j in range(1, w):
            win = win + xe_scr[POOL_HALO - j:POOL_HALO - j + tm, ch]
        cnt = jnp.minimum(pos + 1, w).astype(F32)
        diff = win / cnt - xa[:, ch]
        ya_parts.append(_dot(diff.astype(BF16), poolw_ref[gi]) * pscale_ref[:, ch])
    ya = jnp.concatenate(ya_parts, axis=1)
    xe_scr[0:POOL_HALO, :] = xe_scr[tm:tm + POOL_HALO, :]

    z = _gelu(_dot(hb, win_ref[:, a_end:b_end]))
    u, v = z[:, :width], z[:, width:]
    mu = jnp.mean(v, axis=-1, keepdims=True)
    vc = v - mu
    var = jnp.mean(vc * vc, axis=-1, keepdims=True)
    vb = (vc * lax.rsqrt(var + EPS) * lng_ref[...] + lnb_ref[...]).astype(BF16)
    r_i = lax.broadcasted_iota(I32, (SGU_CHUNK, SGU_CHUNK), 0)
    c_i = lax.broadcasted_iota(I32, (SGU_CHUNK, SGU_CHUNK), 1)
    n_heads = width // LANE
    wt = [jnp.where(r_i >= c_i, sguw_ref[hd], 0.0).astype(BF16) for hd in range(n_heads)]
    s_rows = []
    for c in range(tm // SGU_CHUNK):
        rs = slice(c * SGU_CHUNK, (c + 1) * SGU_CHUNK)
        s_rows.append(jnp.concatenate(
            [_dot(wt[hd], vb[rs, hd * LANE:(hd + 1) * LANE]) for hd in range(n_heads)], axis=1)
            + sgub_ref[...])
    yb = u * jnp.concatenate(s_rows, axis=0)

    yc = _gelu(jnp.concatenate([ys_ref[j] for j in range(SSM_SLABS)], axis=1).astype(F32))
    yc = yc * _sigmoid(_dot(yc.astype(BF16), gluw_ref[...]) + glub_ref[...])

    merged = None
    for k, yk in enumerate((ya, yb, yc)):
        gate = _sigmoid(_dot(hb, win_ref[:, c_end + k * D_MODEL:c_end + (k + 1) * D_MODEL])
                        + bgate_ref[:, k * D_MODEL:(k + 1) * D_MODEL])
        term = gate * _dot(yk.astype(BF16), wbr_ref[k])
        merged = term if merged is None else merged + term
    o_ref[...] = x + _dot(merged.astype(BF16), wout_ref[...])


def mixer(x2d, ys4, p, batch, tm):
    t = x2d.shape[0]
    nt = t // batch // tm
    row = lambda b, i: (b * nt + i, 0)
    slab = lambda b, i: (0, b * nt + i, 0)
    consts = [p["norm_mix_g"], p["w_in"], p["b_gate"], p["pool_w"], p["pool_scale"], p["sgu_ln_g"],
              p["sgu_ln_b"], p["sgu_w"], p["sgu_b_full"], p["glu_w"], p["glu_b"], p["w_branch"], p["w_out"]]
    return pl.pallas_call(
        _mixer_kernel,
        grid=(batch, nt),
        in_specs=[pl.BlockSpec((tm, D_MODEL), row),
                  pl.BlockSpec((SSM_SLABS, tm, LANE), slab)] + [_const_spec(c.shape) for c in consts],
        out_specs=pl.BlockSpec((tm, D_MODEL), row),
        out_shape=jax.ShapeDtypeStruct(x2d.shape, F32),
        scratch_shapes=[pltpu.VMEM((tm + POOL_HALO, SSM_SLABS * LANE), F32)],
        compiler_params=_params(2),
        name="mixer",
    )(x2d, ys4, *consts)


def _router_kernel(x_ref, g_ref, wr_ref, br_ref, route_ref, meta_ref, carry_scr):
    tm = x_ref.shape[0]
    i = pl.program_id(0)

    @pl.when(i == 0)
    def _():
        carry_scr[...] = jnp.zeros_like(carry_scr)

    h = _rms(x_ref[...], g_ref[...])
    logits = jnp.dot(h, wr_ref[...], preferred_element_type=F32, precision=HI) + br_ref[...]
    lane = lax.broadcasted_iota(I32, logits.shape, 1)
    neg = -jnp.inf
    gl = jnp.where(lane < N_GROUPS, logits, neg)
    gmax = jnp.max(gl, axis=-1, keepdims=True)
    gsum = jnp.sum(jnp.where(lane < N_GROUPS, jnp.exp(logits - gmax), 0.0), axis=-1, keepdims=True)
    g_val = 1.0 / gsum
    g_idx = jnp.min(jnp.where(gl == gmax, lane, LANE), axis=-1, keepdims=True)
    lo = N_GROUPS + EXPERTS_PER_GROUP * g_idx
    el = jnp.where((lane >= lo) & (lane < lo + EXPERTS_PER_GROUP), logits, neg)
    e1v = jnp.max(el, axis=-1, keepdims=True)
    e1l = jnp.min(jnp.where(el == e1v, lane, LANE), axis=-1, keepdims=True)
    el2 = jnp.where(lane == e1l, neg, el)
    e2v = jnp.max(el2, axis=-1, keepdims=True)
    e2l = jnp.min(jnp.where(el2 == e2v, lane, LANE), axis=-1, keepdims=True)
    tt = jnp.exp(e2v - e1v)
    w1 = g_val / (1.0 + tt)
    w2 = g_val * tt / (1.0 + tt)
    hit1, hit2 = lane == e1l - N_GROUPS, lane == e2l - N_GROUPS
    onehot = jnp.where(hit1 | hit2, 1.0, 0.0)
    r_i = lax.broadcasted_iota(I32, (tm, tm), 0)
    c_i = lax.broadcasted_iota(I32, (tm, tm), 1)
    tri = jnp.where(r_i > c_i, 1.0, 0.0).astype(BF16)
    prefix = _dot(tri, onehot.astype(BF16))
    count = jnp.sum(onehot, axis=0, keepdims=True)
    e_r = lax.broadcasted_iota(I32, (LANE, LANE), 0)
    e_c = lax.broadcasted_iota(I32, (LANE, LANE), 1)
    before = jnp.where(e_r < e_c, 1.0, 0.0)
    run_start = jnp.dot(jnp.broadcast_to(count, (SUBLANE, LANE)), before, preferred_element_type=F32,
                        precision=HI)[0:1, :]
    local = prefix + run_start
    lp1 = jnp.sum(jnp.where(hit1, local, 0.0), axis=-1, keepdims=True)
    lp2 = jnp.sum(jnp.where(hit2, local, 0.0), axis=-1, keepdims=True)
    route_ref[...] = jnp.where(lane == 0, w1, jnp.where(lane == 1, w2, jnp.where(
        lane == 2, lp1, jnp.where(lane == 3, lp2, 0.0))))
    earlier = carry_scr[0:1, :]
    total = earlier + count
    carry_scr[...] = jnp.broadcast_to(total, carry_scr.shape)
    sub = lax.broadcasted_iota(I32, meta_ref.shape, 0)
    meta_ref[...] = jnp.where(sub == 0, count, jnp.where(sub == 1, earlier, jnp.where(
        sub == 2, run_start, jnp.where(sub == 3, total, 0.0))))


def router(x2d, g, wr, br, tm):
    t = x2d.shape[0]
    row = lambda i: (i, 0)
    return pl.pallas_call(
        _router_kernel,
        grid=(t // tm,),
        in_specs=[pl.BlockSpec((tm, D_MODEL), row), _const_spec((1, D_MODEL)),
                  _const_spec((D_MODEL, LANE)), _const_spec((1, LANE))],
        out_specs=[pl.BlockSpec((tm, LANE), row), pl.BlockSpec((SUBLANE, LANE), row)],
        out_shape=[jax.ShapeDtypeStruct((t, LANE), F32),
                   jax.ShapeDtypeStruct((t // tm * SUBLANE, LANE), F32)],
        scratch_shapes=[pltpu.VMEM((SUBLANE, LANE), F32)],
        compiler_params=_params(1),
        name="router",
    )(x2d, g, wr, br)


def _sort_matrix(route, k, n_sorted):
    col = lax.broadcasted_iota(I32, (route.shape[0], n_sorted), 1)
    return jnp.where(col == route[:, 2 + k:3 + k].astype(I32), 1.0, 0.0).astype(BF16)


def _run_copies(cnt_ref, lo_ref, dst_ref, tile, make_copy, wait):
    def body(e, c):
        idx = tile * N_EXPERTS + e
        n = cnt_ref[idx]

        @pl.when(n > 0)
        def _():
            cp = make_copy(lo_ref[idx], dst_ref[idx], n)
            if wait:
                cp.wait()
            else:
                cp.start()
        return c
    lax.fori_loop(0, N_EXPERTS, body, 0)


def _dispatch_kernel(cnt_ref, lo_ref, dst_ref, zst_ref, zln_ref, nused_ref, x_ref, g_ref, route_ref, o_ref,
                     srt_scr, zero_scr, sem, zsem):
    tm = x_ref.shape[0]
    bm = zero_scr.shape[0]
    i = pl.program_id(0)
    n = pl.num_programs(0)
    slot = i % 2

    def run_copy(s):
        return lambda lo, dst, cnt: pltpu.make_async_copy(
            srt_scr.at[s, pl.ds(lo, cnt)], o_ref.at[pl.ds(dst, cnt)], sem.at[s])

    def pad_copy(e):
        return pltpu.make_async_copy(zero_scr.at[pl.ds(0, zln_ref[e])],
                                     o_ref.at[pl.ds(zst_ref[e], zln_ref[e])], zsem)

    def tail_copy(b):
        return pltpu.make_async_copy(zero_scr, o_ref.at[pl.ds(b * bm, bm)], zsem)

    @pl.when(i == 0)
    def _():
        zero_scr[...] = jnp.zeros_like(zero_scr)
        for wait in (False, True):
            def body(e, c):
                @pl.when(zln_ref[e] > 0)
                def _():
                    cp = pad_copy(e)
                    cp.wait() if wait else cp.start()
                return c
            lax.fori_loop(0, N_EXPERTS, body, 0)

            def tail(b, c):
                cp = tail_copy(b)
                cp.wait() if wait else cp.start()
                return c
            lax.fori_loop(nused_ref[0], o_ref.shape[0] // bm, tail, 0)

    @pl.when(i >= 2)
    def _():
        _run_copies(cnt_ref, lo_ref, dst_ref, i - 2, run_copy(slot), True)

    hb = _rms(x_ref[...], g_ref[...]).astype(BF16)
    route = route_ref[...]
    pt = jnp.maximum(_sort_matrix(route, 0, 2 * tm), _sort_matrix(route, 1, 2 * tm))
    srt = lax.dot_general(pt, hb, (((0,), (0,)), ((), ())), preferred_element_type=F32)
    for a in range(ROW_TILES):
        srt_scr[slot, :, a, :] = srt[:, a * LANE:(a + 1) * LANE]
    _run_copies(cnt_ref, lo_ref, dst_ref, i, run_copy(slot), False)

    @pl.when(i == n - 1)
    def _():
        @pl.when(i >= 1)
        def _():
            _run_copies(cnt_ref, lo_ref, dst_ref, i - 1, run_copy(1 - slot), True)
        _run_copies(cnt_ref, lo_ref, dst_ref, i, run_copy(slot), True)


def dispatch(cnt, lo, dst, zst, zln, nused, x2d, g, route, n_rows, tm, bm):
    t, d = x2d.shape
    row = lambda i, *_: (i, 0)
    return pl.pallas_call(
        _dispatch_kernel,
        grid_spec=pltpu.PrefetchScalarGridSpec(
            num_scalar_prefetch=6,
            grid=(t // tm,),
            in_specs=[pl.BlockSpec((tm, d), row), pl.BlockSpec((1, d), lambda i, *_: (0, 0)),
                      pl.BlockSpec((tm, LANE), row)],
            out_specs=pl.BlockSpec(memory_space=pl.ANY),
            scratch_shapes=[pltpu.VMEM((2, 2 * tm, ROW_TILES, LANE), F32),
                            pltpu.VMEM((bm, ROW_TILES, LANE), F32),
                            pltpu.SemaphoreType.DMA((2,)), pltpu.SemaphoreType.DMA(())]),
        out_shape=jax.ShapeDtypeStruct((n_rows, ROW_TILES, LANE), F32),
        compiler_params=_params(1),
        name="dispatch",
    )(cnt, lo, dst, zst, zln, nused, x2d, g, route)


def _expert_kernel(bexp_ref, nused_ref, xs_ref, wg_ref, wu_ref, wd_ref, ys_ref, wg_scr, wu_scr, wd_scr):
    i = pl.program_id(0)

    @pl.when(i < nused_ref[0])
    def _():
        changed = jnp.logical_or(i == 0, bexp_ref[i] != bexp_ref[jnp.maximum(i - 1, 0)])

        @pl.when(changed)
        def _():
            wg_scr[...] = wg_ref[...].astype(BF16)
            wu_scr[...] = wu_ref[...].astype(BF16)
            wd_scr[...] = wd_ref[...].astype(BF16)

        xb = jnp.concatenate([xs_ref[:, a, :] for a in range(ROW_TILES)], axis=1).astype(BF16)
        hg = _dot(xb, wg_scr[...])
        hu = _dot(xb, wu_scr[...])
        act = (hg * _sigmoid(hg) * hu).astype(BF16)
        y = _dot(act, wd_scr[...])
        for a in range(ROW_TILES):
            ys_ref[:, a, :] = y[:, a * LANE:(a + 1) * LANE]

    @pl.when(i >= nused_ref[0])
    def _():
        ys_ref[...] = jnp.zeros_like(ys_ref)


def experts(bexp, nused, xs, w_gate, w_up, w_down, layer, bm):
    n_rows = xs.shape[0]
    d, de = w_gate.shape[2], w_gate.shape[3]
    wsel = lambda i, bexp, nused: (layer, bexp[i], 0, 0)
    return pl.pallas_call(
        _expert_kernel,
        grid_spec=pltpu.PrefetchScalarGridSpec(
            num_scalar_prefetch=2,
            grid=(n_rows // bm,),
            in_specs=[pl.BlockSpec((bm, ROW_TILES, LANE),
                                   lambda i, bexp, nused: (jnp.minimum(i, nused[0] - 1), 0, 0)),
                      pl.BlockSpec((None, None, d, de), wsel),
                      pl.BlockSpec((None, None, d, de), wsel), pl.BlockSpec((None, None, de, d), wsel)],
            out_specs=pl.BlockSpec((bm, ROW_TILES, LANE), lambda i, bexp, nused: (i, 0, 0)),
            scratch_shapes=[pltpu.VMEM((d, de), BF16), pltpu.VMEM((d, de), BF16),
                            pltpu.VMEM((de, d), BF16)]),
        out_shape=jax.ShapeDtypeStruct(xs.shape, F32),
        compiler_params=_params(1),
        name="experts",
    )(bexp, nused, xs, w_gate, w_up, w_down)


def _combine_kernel(cnt_ref, lo_ref, dst_ref, x_ref, route_ref, g_ref, ys_ref, o_ref, buf, sem, *, final):
    tm = x_ref.shape[0]
    i = pl.program_id(0)
    n = pl.num_programs(0)
    slot = i % 2

    def run_copy(s):
        return lambda lo, dst, cnt: pltpu.make_async_copy(
            ys_ref.at[pl.ds(dst, cnt)], buf.at[s, pl.ds(lo, cnt)], sem.at[s])

    @pl.when(i == 0)
    def _():
        _run_copies(cnt_ref, lo_ref, dst_ref, 0, run_copy(0), False)

    @pl.when(i + 1 < n)
    def _():
        _run_copies(cnt_ref, lo_ref, dst_ref, i + 1, run_copy(1 - slot), False)

    _run_copies(cnt_ref, lo_ref, dst_ref, i, run_copy(slot), True)
    ysb = jnp.concatenate([buf[slot, :, a, :] for a in range(ROW_TILES)], axis=1).astype(BF16)
    route = route_ref[...]
    out = x_ref[...]
    for k in range(2):
        out = out + route[:, k:k + 1] * _dot(_sort_matrix(route, k, 2 * tm), ysb)
    if final:
        out = _rms(out, g_ref[...])
    o_ref[...] = out


def combine(cnt, lo, dst, x2d, route, g, ys, tm, final):
    t, d = x2d.shape
    row = lambda i, *_: (i, 0)
    return pl.pallas_call(
        functools.partial(_combine_kernel, final=final),
        grid_spec=pltpu.PrefetchScalarGridSpec(
            num_scalar_prefetch=3,
            grid=(t // tm,),
            in_specs=[pl.BlockSpec((tm, d), row), pl.BlockSpec((tm, LANE), row),
                      pl.BlockSpec((1, d), lambda i, *_: (0, 0)),
                      pl.BlockSpec(memory_space=pl.ANY)],
            out_specs=pl.BlockSpec((tm, d), row),
            scratch_shapes=[pltpu.VMEM((2, 2 * tm, ROW_TILES, LANE), F32), pltpu.SemaphoreType.DMA((2,))]),
        out_shape=jax.ShapeDtypeStruct((t, d), F32),
        compiler_params=_params(1),
        name="combine",
    )(cnt, lo, dst, x2d, route, g, ys)


def moe(x2d, g, rg_w, rg_b, re_w, re_b, w_gate, w_up, w_down, layer, g_final, final, tm, bm):
    t = x2d.shape[0]
    nt = t // tm
    pad = LANE - N_GROUPS - N_EXPERTS
    wr = jnp.concatenate([rg_w, re_w, jnp.zeros((D_MODEL, pad), F32)], axis=1)
    br = jnp.concatenate([rg_b, re_b, jnp.zeros((pad,), F32)])[None, :]
    g2 = g[None, :]
    route, meta = router(x2d, g2, wr, br, tm)
    meta = meta.reshape(nt, SUBLANE, LANE)[:, :, :N_EXPERTS].astype(I32)
    cnt, earlier, lo = meta[:, 0], meta[:, 1], meta[:, 2]
    counts = meta[nt - 1, 3]
    padded = (counts + bm - 1) // bm * bm
    ends = jnp.cumsum(padded)
    offs = ends - padded
    dst = offs[None, :] + earlier
    n_rows = (2 * t // bm + N_EXPERTS) * bm
    nused = (ends[-1] // bm).astype(I32)
    blk_start = jnp.arange(n_rows // bm, dtype=I32) * bm
    bexp = jnp.sum((ends[None, :] <= blk_start[:, None]).astype(I32), axis=1)
    last = jnp.sum((ends <= (nused - 1) * bm).astype(I32))
    bexp = jnp.minimum(bexp, last)
    cnt, lo, dst = cnt.reshape(-1), lo.reshape(-1), dst.reshape(-1)
    xs = dispatch(cnt, lo, dst, offs + counts, padded - counts, nused[None], x2d, g2, route, n_rows, tm, bm)
    ys = experts(bexp, nused[None], xs, w_gate, w_up, w_down, layer, bm)
    return combine(cnt, lo, dst, x2d, route, g_final[None, :], ys, tm, final)


def kernel(x, norm_mix_g, w_in, b_gate, pool_w, pool_scale, sgu_ln_g, sgu_ln_b, sgu_w, sgu_b, ssm_a_re, ssm_a_im, ssm_log_dt, ssm_b_re, ssm_b_im, ssm_c_re, ssm_c_im, ssm_d, glu_w, glu_b, w_branch, w_out, norm_ffn_g, router_group_w, router_group_b, router_expert_w, router_expert_b, exp_w_gate, exp_w_up, exp_w_down, norm_final_g):
    batch, seq, d = x.shape
    depth = w_in.shape[0]
    tm = 256
    bm = 256
    x2d = x.reshape(batch * seq, d)
    ssm_col_block = 3
    for l in range(depth):
        wl = w_in[l].astype(BF16)
        g_mix = norm_mix_g[l][None, :]
        xc4 = ssm_input(x2d, g_mix, wl, ssm_col_block, tm)
        bq, cq, kq, lam_q = ssm_tables(ssm_a_re[l], ssm_a_im[l], ssm_log_dt[l], ssm_b_re[l], ssm_b_im[l],
                                       ssm_c_re[l], ssm_c_im[l], ssm_d[l])
        ys4 = s5_scan(xc4, bq, cq, kq, lam_q, batch)
        p = {
            "norm_mix_g": g_mix, "w_in": wl, "b_gate": b_gate[l][None, :],
            "pool_w": pool_w[l].astype(BF16), "pool_scale": pool_scale[l][None, :],
            "sgu_ln_g": sgu_ln_g[l][None, :], "sgu_ln_b": sgu_ln_b[l][None, :],
            "sgu_w": sgu_w[l],
            "sgu_b_full": jnp.repeat(sgu_b[l].T, LANE, axis=1),
            "glu_w": glu_w[l].astype(BF16), "glu_b": glu_b[l][None, :],
            "w_branch": w_branch[l].astype(BF16), "w_out": w_out[l].astype(BF16),
        }
        x2d = mixer(x2d, ys4, p, batch, tm)
        x2d = moe(x2d, norm_ffn_g[l], router_group_w[l], router_group_b[l], router_expert_w[l],
                  router_expert_b[l], exp_w_gate, exp_w_up, exp_w_down, l, norm_final_g,
                  l == depth - 1, tm, bm)
    return x2d.reshape(batch, seq, d)
```

```python
import functools
import math

import jax
import jax.numpy as jnp
from jax import lax
from jax.experimental import pallas as pl
from jax.experimental.pallas import tpu as pltpu

F32 = jnp.float32
BF16 = jnp.bfloat16
I32 = jnp.int32

D_MODEL = 1024
POOL_WINDOWS = (2, 4, 8, 16)
POOL_HALO = 16
LANE = 128
SUBLANE = 8
SGU_CHUNK = 128
SSM_GROUP_DIM = 16
SSM_STATE = 64
SSM_Q = 8
SSM_SLABS = 4
GROUPS_PER_SLAB = LANE // SSM_GROUP_DIM
N_GROUPS = 4
EXPERTS_PER_GROUP = 8
N_EXPERTS = 32
EPS = 1e-6
VMEM_LIMIT = 56 * 1024 * 1024

HI = lax.Precision.HIGHEST


def _rms(x, g):
    return x * lax.rsqrt(jnp.mean(x * x, axis=-1, keepdims=True) + EPS) * g


def _gelu(x):
    c = math.sqrt(2.0 / math.pi)
    return x * (0.5 * (1.0 + jnp.tanh(c * (x + 0.044715 * (x * x * x)))))


def _sigmoid(x):
    return 1.0 / (1.0 + jnp.exp(-x))


def _dot(a, b):
    return jnp.dot(a, b, preferred_element_type=F32)


def _const_spec(shape):
    nd = len(shape)
    return pl.BlockSpec(shape, lambda *_: (0,) * nd, pipeline_mode=pl.Buffered(1))


def _params(n_axes):
    return pltpu.CompilerParams(dimension_semantics=("arbitrary",) * n_axes, vmem_limit_bytes=VMEM_LIMIT)


def _xc_kernel(x_ref, g_ref, w_ref, o_ref, xc_scr):
    h = _rms(x_ref[...], g_ref[...]).astype(BF16)
    xc = _dot(h, w_ref[...])
    rows = xc_scr.shape[1] // SSM_Q
    for j in range(SSM_SLABS):
        xc_scr[j] = xc[:, j * LANE:(j + 1) * LANE]
        for s in range(SSM_Q):
            o_ref[j, :, s * LANE:(s + 1) * LANE] = xc_scr[j, pl.ds(s, rows, stride=SSM_Q), :].astype(BF16)


def ssm_input(x2d, g, w_in_bf, col_block, tm):
    t = x2d.shape[0]
    width = SSM_SLABS * LANE
    return pl.pallas_call(
        _xc_kernel,
        grid=(t // tm,),
        in_specs=[pl.BlockSpec((tm, D_MODEL), lambda i: (i, 0)),
                  _const_spec((1, D_MODEL)),
                  pl.BlockSpec((D_MODEL, width), lambda i: (0, col_block), pipeline_mode=pl.Buffered(1))],
        out_specs=pl.BlockSpec((SSM_SLABS, tm // SSM_Q, SSM_Q * LANE), lambda i: (0, i, 0)),
        out_shape=jax.ShapeDtypeStruct((SSM_SLABS, t // SSM_Q, SSM_Q * LANE), BF16),
        scratch_shapes=[pltpu.VMEM((SSM_SLABS, tm, LANE), F32)],
        compiler_params=_params(1),
        name="ssm_input",
    )(x2d, g, w_in_bf)


def _compact(small):
    sl, r1, a, r2, c1, c2 = small.shape
    return small.reshape(sl, r1 * a * r2, c1 * c2).astype(BF16)


def ssm_tables(a_re, a_im, log_dt, b_re, b_im, c_re, c_im, d_skip):
    q = SSM_Q
    ng = a_re.shape[0]
    sl = ng // GROUPS_PER_SLAB
    dt = jnp.exp(log_dt)[:, None]
    k = jnp.arange(q + 1, dtype=F32)[:, None, None]
    mag = jnp.exp(a_re * dt * k)
    ang = a_im * dt * k
    pw_re, pw_im = mag * jnp.cos(ang), mag * jnp.sin(ang)
    nr, ni = pw_re[1] - 1.0, pw_im[1]
    den = a_re * a_re + a_im * a_im
    fr, fi = (nr * a_re + ni * a_im) / den, (ni * a_re - nr * a_im) / den
    bb_re = fr[..., None] * b_re - fi[..., None] * b_im
    bb_im = fr[..., None] * b_im + fi[..., None] * b_re

    def slabbed(a, axis):
        return a.reshape(*a.shape[:axis], sl, GROUPS_PER_SLAB, *a.shape[axis + 1:])

    k_rev = (q - 1.0) - jnp.arange(q, dtype=F32)[:, None, None]
    mag_rev = jnp.exp(a_re * dt * k_rev)
    rev_re, rev_im = mag_rev * jnp.cos(a_im * dt * k_rev), mag_rev * jnp.sin(a_im * dt * k_rev)
    m_re = rev_re[..., None] * bb_re[None] - rev_im[..., None] * bb_im[None]
    m_im = rev_re[..., None] * bb_im[None] + rev_im[..., None] * bb_re[None]
    m = slabbed(jnp.stack([m_re, m_im], axis=0), 2)
    bq = _compact(m.transpose(2, 1, 3, 5, 0, 4))
    pr, pi = pw_re[1:, :, None, :], pw_im[1:, :, None, :]
    n_re = c_re[None] * pr - c_im[None] * pi
    n_im = c_re[None] * pi + c_im[None] * pr
    n = slabbed(jnp.stack([n_re, -n_im], axis=0), 2)
    cq = _compact(n.transpose(2, 0, 3, 5, 1, 4))
    cb_re = c_re[:, :, :, None] * bb_re[:, None, :, :] - c_im[:, :, :, None] * bb_im[:, None, :, :]
    cb_im = c_re[:, :, :, None] * bb_im[:, None, :, :] + c_im[:, :, :, None] * bb_re[:, None, :, :]
    kern = jnp.sum(cb_re[None] * pw_re[:q, :, None, :, None] - cb_im[None] * pw_im[:q, :, None, :, None],
                   axis=3)
    skip = d_skip.reshape(ng, SSM_GROUP_DIM)[:, :, None] * jnp.eye(SSM_GROUP_DIM, dtype=F32)[None]
    kern = kern + jnp.where(jnp.arange(q)[:, None, None, None] == 0, skip[None], 0.0)
    s_idx = jnp.arange(q)[:, None]
    t_idx = jnp.arange(q)[None, :]
    onehot_lag = (t_idx - s_idx)[:, :, None] == jnp.arange(q)[None, None, :]
    kst = jnp.sum(jnp.where(onehot_lag[..., None, None, None], kern[None, None], 0.0), axis=2)
    kst = slabbed(kst, 2)
    kq = _compact(kst.transpose(2, 0, 3, 5, 1, 4))
    lam_q = jnp.concatenate([pw_re[q].reshape(sl, 1, -1), pw_im[q].reshape(sl, 1, -1)], axis=-1)
    return bq, cq, kq, lam_q


def _expand_block_diag(small, row_inner, col_inner):
    n_rows, n_small = small.shape
    n_cols = n_small * GROUPS_PER_SLAB
    g_bits = GROUPS_PER_SLAB.bit_length() - 1
    r_shift, c_shift = row_inner.bit_length() - 1, col_inner.bit_length() - 1
    k = lax.broadcasted_iota(I32, (n_small, n_cols), 0)
    col = lax.broadcasted_iota(I32, (n_small, n_cols), 1)
    src = ((col >> (c_shift + g_bits)) << c_shift) | (col & (col_inner - 1))
    spread = jnp.where(k == src, 1.0, 0.0).astype(BF16)
    r = lax.broadcasted_iota(I32, (n_rows, n_cols), 0)
    c = lax.broadcasted_iota(I32, (n_rows, n_cols), 1)
    same = ((r >> r_shift) & (GROUPS_PER_SLAB - 1)) == ((c >> c_shift) & (GROUPS_PER_SLAB - 1))
    return jnp.where(same, _dot(small, spread), 0.0).astype(BF16)


def _s5_kernel(u_ref, bq_ref, cq_ref, kq_ref, lam_ref, y_ref, g_scr, s_scr, t_scr, tbq, tcq, tkq):
    half = GROUPS_PER_SLAB * SSM_STATE
    rows = u_ref.shape[1]

    @pl.when(pl.program_id(1) == 0)
    def _():
        tbq[...] = _expand_block_diag(bq_ref[0], SSM_GROUP_DIM, SSM_STATE)
        tcq[...] = _expand_block_diag(cq_ref[0], SSM_STATE, SSM_GROUP_DIM)
        tkq[...] = _expand_block_diag(kq_ref[0], SSM_GROUP_DIM, SSM_GROUP_DIM)

    u = u_ref[0]
    g_scr[...] = _dot(u, tbq[...])
    lam = lam_ref[0]
    lre, lim = lam[:, :half], lam[:, half:]

    def body(n, carry):
        cre, cim = carry
        s_scr[pl.ds(n, 1), :] = jnp.concatenate([cre, cim], axis=1)
        g = g_scr[pl.ds(n, 1), :]
        nre = lre * cre - lim * cim + g[:, :half]
        nim = lre * cim + lim * cre + g[:, half:]
        return nre, nim

    zero = jnp.zeros((1, half), F32)
    lax.fori_loop(0, rows, body, (zero, zero), unroll=4)
    y = _dot(s_scr[...].astype(BF16), tcq[...]) + _dot(u, tkq[...])
    for t in range(SSM_Q):
        t_scr[pl.ds(t, rows, stride=SSM_Q), :] = y[:, t * LANE:(t + 1) * LANE]
    y_ref[0] = t_scr[...].astype(BF16)


def s5_scan(u4, bq, cq, kq, lam_q, batch):
    sl, rows_total, width = u4.shape
    rows = rows_total // batch
    tab = pl.BlockSpec((1, width, width // GROUPS_PER_SLAB), lambda j, b: (j, 0, 0))
    return pl.pallas_call(
        _s5_kernel,
        grid=(sl, batch),
        in_specs=[pl.BlockSpec((1, rows, width), lambda j, b: (j, b, 0)), tab, tab, tab,
                  pl.BlockSpec((1, 1, width), lambda j, b: (j, 0, 0))],
        out_specs=pl.BlockSpec((1, rows * SSM_Q, LANE), lambda j, b: (j, b, 0)),
        out_shape=jax.ShapeDtypeStruct((sl, rows_total * SSM_Q, LANE), BF16),
        scratch_shapes=[pltpu.VMEM((rows, width), F32), pltpu.VMEM((rows, width), F32),
                        pltpu.VMEM((rows * SSM_Q, LANE), F32)] + [pltpu.VMEM((width, width), BF16)] * 3,
        compiler_params=_params(2),
        name="s5_scan",
    )(u4, bq, cq, kq, lam_q)


def _mixer_kernel(x_ref, ys_ref, ng_ref, win_ref, bgate_ref, poolw_ref, pscale_ref, lng_ref, lnb_ref,
                  sguw_ref, sgub_ref, gluw_ref, glub_ref, wbr_ref, wout_ref, o_ref, xe_scr):
    tm = x_ref.shape[0]
    width = SSM_SLABS * LANE
    a_end, b_end = width, 3 * width
    c_end = b_end + width
    i = pl.program_id(1)
    x = x_ref[...]
    hb = _rms(x, ng_ref[...]).astype(BF16)

    xa = _dot(hb, win_ref[:, 0:a_end])

    @pl.when(i == 0)
    def _():
        xe_scr[0:POOL_HALO, :] = jnp.zeros((POOL_HALO, xe_scr.shape[1]), F32)

    xe_scr[POOL_HALO:POOL_HALO + tm, :] = xa
    pos = i * tm + lax.broadcasted_iota(I32, (tm, 1), 0)
    ya_parts = []
    for gi, w in enumerate(POOL_WINDOWS):
        ch = slice(gi * LANE, (gi + 1) * LANE)
        win = xa[:, ch]
        for j in range(1, w):
            win = win + xe_scr[POOL_HALO - j:POOL_HALO - j + tm, ch]
        cnt = jnp.minimum(pos + 1, w).astype(F32)
        diff = win / cnt - xa[:, ch]
        ya_parts.append(_dot(diff.astype(BF16), poolw_ref[gi]) * pscale_ref[:, ch])
    ya = jnp.concatenate(ya_parts, axis=1)
    xe_scr[0:POOL_HALO, :] = xe_scr[tm:tm + POOL_HALO, :]

    z = _gelu(_dot(hb, win_ref[:, a_end:b_end]))
    u, v = z[:, :width], z[:, width:]
    mu = jnp.mean(v, axis=-1, keepdims=True)
    vc = v - mu
    var = jnp.mean(vc * vc, axis=-1, keepdims=True)
    vb = (vc * lax.rsqrt(var + EPS) * lng_ref[...] + lnb_ref[...]).astype(BF16)
    r_i = lax.broadcasted_iota(I32, (SGU_CHUNK, SGU_CHUNK), 0)
    c_i = lax.broadcasted_iota(I32, (SGU_CHUNK, SGU_CHUNK), 1)
    n_heads = width // LANE
    wt = [jnp.where(r_i >= c_i, sguw_ref[hd], 0.0).astype(BF16) for hd in range(n_heads)]
    s_rows = []
    for c in range(tm // SGU_CHUNK):
        rs = slice(c * SGU_CHUNK, (c + 1) * SGU_CHUNK)
        s_rows.append(jnp.concatenate(
            [_dot(wt[hd], vb[rs, hd * LANE:(hd + 1) * LANE]) for hd in range(n_heads)], axis=1)
            + sgub_ref[...])
    yb = u * jnp.concatenate(s_rows, axis=0)

    yc = _gelu(jnp.concatenate([ys_ref[j] for j in range(SSM_SLABS)], axis=1).astype(F32))
    yc = yc * _sigmoid(_dot(yc.astype(BF16), gluw_ref[...]) + glub_ref[...])

    merged = None
    for k, yk in enumerate((ya, yb, yc)):
        gate = _sigmoid(_dot(hb, win_ref[:, c_end + k * D_MODEL:c_end + (k + 1) * D_MODEL])
                        + bgate_ref[:, k * D_MODEL:(k + 1) * D_MODEL])
        term = gate * _dot(yk.astype(BF16), wbr_ref[k])
        merged = term if merged is None else merged + term
    o_ref[...] = x + _dot(merged.astype(BF16), wout_ref[...])


def mixer(x2d, ys4, p, batch, tm):
    t = x2d.shape[0]
    nt = t // batch // tm
    row = lambda b, i: (b * nt + i, 0)
    slab = lambda b, i: (0, b * nt + i, 0)
    consts = [p["norm_mix_g"], p["w_in"], p["b_gate"], p["pool_w"], p["pool_scale"], p["sgu_ln_g"],
              p["sgu_ln_b"], p["sgu_w"], p["sgu_b_full"], p["glu_w"], p["glu_b"], p["w_branch"], p["w_out"]]
    return pl.pallas_call(
        _mixer_kernel,
        grid=(batch, nt),
        in_specs=[pl.BlockSpec((tm, D_MODEL), row),
                  pl.BlockSpec((SSM_SLABS, tm, LANE), slab)] + [_const_spec(c.shape) for c in consts],
        out_specs=pl.BlockSpec((tm, D_MODEL), row),
        out_shape=jax.ShapeDtypeStruct(x2d.shape, F32),
        scratch_shapes=[pltpu.VMEM((tm + POOL_HALO, SSM_SLABS * LANE), F32)],
        compiler_params=_params(2),
        name="mixer",
    )(x2d, ys4, *consts)


def _router_kernel(x_ref, g_ref, wr_ref, br_ref, route_ref, meta_ref, carry_scr):
    tm = x_ref.shape[0]
    i = pl.program_id(0)

    @pl.when(i == 0)
    def _():
        carry_scr[...] = jnp.zeros_like(carry_scr)

    h = _rms(x_ref[...], g_ref[...])
    logits = jnp.dot(h, wr_ref[...], preferred_element_type=F32, precision=HI) + br_ref[...]
    lane = lax.broadcasted_iota(I32, logits.shape, 1)
    neg = -jnp.inf
    gl = jnp.where(lane < N_GROUPS, logits, neg)
    gmax = jnp.max(gl, axis=-1, keepdims=True)
    gsum = jnp.sum(jnp.where(lane < N_GROUPS, jnp.exp(logits - gmax), 0.0), axis=-1, keepdims=True)
    g_val = 1.0 / gsum
    g_idx = jnp.min(jnp.where(gl == gmax, lane, LANE), axis=-1, keepdims=True)
    lo = N_GROUPS + EXPERTS_PER_GROUP * g_idx
    el = jnp.where((lane >= lo) & (lane < lo + EXPERTS_PER_GROUP), logits, neg)
    e1v = jnp.max(el, axis=-1, keepdims=True)
    e1l = jnp.min(jnp.where(el == e1v, lane, LANE), axis=-1, keepdims=True)
    el2 = jnp.where(lane == e1l, neg, el)
    e2v = jnp.max(el2, axis=-1, keepdims=True)
    e2l = jnp.min(jnp.where(el2 == e2v, lane, LANE), axis=-1, keepdims=True)
    tt = jnp.exp(e2v - e1v)
    w1 = g_val / (1.0 + tt)
    w2 = g_val * tt / (1.0 + tt)
    hit1, hit2 = lane == e1l - N_GROUPS, lane == e2l - N_GROUPS
    onehot = jnp.where(hit1 | hit2, 1.0, 0.0)
    r_i = lax.broadcasted_iota(I32, (tm, tm), 0)
    c_i = lax.broadcasted_iota(I32, (tm, tm), 1)
    tri = jnp.where(r_i > c_i, 1.0, 0.0).astype(BF16)
    prefix = _dot(tri, onehot.astype(BF16))
    count = jnp.sum(onehot, axis=0, keepdims=True)
    count = jnp.floor((count + (SUBLANE - 1.0)) * (1.0 / SUBLANE)) * SUBLANE
    e_r = lax.broadcasted_iota(I32, (LANE, LANE), 0)
    e_c = lax.broadcasted_iota(I32, (LANE, LANE), 1)
    before = jnp.where(e_r < e_c, 1.0, 0.0)
    run_start = jnp.dot(jnp.broadcast_to(count, (SUBLANE, LANE)), before, preferred_element_type=F32,
                        precision=HI)[0:1, :]
    local = prefix + run_start
    lp1 = jnp.sum(jnp.where(hit1, local, 0.0), axis=-1, keepdims=True)
    lp2 = jnp.sum(jnp.where(hit2, local, 0.0), axis=-1, keepdims=True)
    route_ref[...] = jnp.where(lane == 0, w1, jnp.where(lane == 1, w2, jnp.where(
        lane == 2, lp1, jnp.where(lane == 3, lp2, 0.0))))
    earlier = carry_scr[0:1, :]
    total = earlier + count
    carry_scr[...] = jnp.broadcast_to(total, carry_scr.shape)
    sub = lax.broadcasted_iota(I32, meta_ref.shape, 0)
    meta_ref[...] = jnp.where(sub == 0, count, jnp.where(sub == 1, earlier, jnp.where(
        sub == 2, run_start, jnp.where(sub == 3, total, 0.0))))


def router(x2d, g, wr, br, tm):
    t = x2d.shape[0]
    row = lambda i: (i, 0)
    return pl.pallas_call(
        _router_kernel,
        grid=(t // tm,),
        in_specs=[pl.BlockSpec((tm, D_MODEL), row), _const_spec((1, D_MODEL)),
                  _const_spec((D_MODEL, LANE)), _const_spec((1, LANE))],
        out_specs=[pl.BlockSpec((tm, LANE), row), pl.BlockSpec((SUBLANE, LANE), row)],
        out_shape=[jax.ShapeDtypeStruct((t, LANE), F32),
                   jax.ShapeDtypeStruct((t // tm * SUBLANE, LANE), F32)],
        scratch_shapes=[pltpu.VMEM((SUBLANE, LANE), F32)],
        compiler_params=_params(1),
        name="router",
    )(x2d, g, wr, br)


def _sort_matrix(route, k, n_sorted):
    col = lax.broadcasted_iota(I32, (route.shape[0], n_sorted), 1)
    return jnp.where(col == route[:, 2 + k:3 + k].astype(I32), 1.0, 0.0).astype(BF16)


def _sorted_rows(tm):
    return -(-(2 * tm + N_EXPERTS * (SUBLANE - 1)) // LANE) * LANE


def _run_copies(cnt_ref, lo_ref, dst_ref, tile, make_copy, wait):
    def body(e, c):
        idx = tile * N_EXPERTS + e
        n = cnt_ref[idx]

        @pl.when(n > 0)
        def _():
            cp = make_copy(pl.multiple_of(lo_ref[idx], SUBLANE), pl.multiple_of(dst_ref[idx], SUBLANE),
                           pl.multiple_of(n, SUBLANE))
            if wait:
                cp.wait()
            else:
                cp.start()
        return c
    lax.fori_loop(0, N_EXPERTS, body, 0)


def _dispatch_kernel(cnt_ref, lo_ref, dst_ref, zst_ref, zln_ref, nused_ref, x_ref, g_ref, route_ref, o_ref,
                     srt_scr, zero_scr, sem, zsem):
    tm = x_ref.shape[0]
    bm = zero_scr.shape[0]
    i = pl.program_id(0)
    n = pl.num_programs(0)
    slot = i % 2

    def run_copy(s):
        return lambda lo, dst, cnt: pltpu.make_async_copy(
            srt_scr.at[s, pl.ds(lo, cnt)], o_ref.at[pl.ds(dst, cnt)], sem.at[s])

    def pad_copy(e):
        ln = pl.multiple_of(zln_ref[e], SUBLANE)
        return pltpu.make_async_copy(zero_scr.at[pl.ds(0, ln)],
                                     o_ref.at[pl.ds(pl.multiple_of(zst_ref[e], SUBLANE), ln)], zsem)

    def tail_copy(b):
        return pltpu.make_async_copy(zero_scr, o_ref.at[pl.ds(pl.multiple_of(b * bm, bm), bm)], zsem)

    @pl.when(i == 0)
    def _():
        zero_scr[...] = jnp.zeros_like(zero_scr)
        for wait in (False, True):
            def body(e, c):
                @pl.when(zln_ref[e] > 0)
                def _():
                    cp = pad_copy(e)
                    cp.wait() if wait else cp.start()
                return c
            lax.fori_loop(0, N_EXPERTS, body, 0)

            def tail(b, c):
                cp = tail_copy(b)
                cp.wait() if wait else cp.start()
                return c
            lax.fori_loop(nused_ref[0], o_ref.shape[0] // bm, tail, 0)

    @pl.when(i >= 2)
    def _():
        _run_copies(cnt_ref, lo_ref, dst_ref, i - 2, run_copy(slot), True)

    hb = _rms(x_ref[...], g_ref[...]).astype(BF16)
    route = route_ref[...]
    ns = srt_scr.shape[1]
    pt = jnp.maximum(_sort_matrix(route, 0, ns), _sort_matrix(route, 1, ns))
    srt_scr[slot] = lax.dot_general(pt, hb, (((0,), (0,)), ((), ())), preferred_element_type=F32)
    _run_copies(cnt_ref, lo_ref, dst_ref, i, run_copy(slot), False)

    @pl.when(i == n - 1)
    def _():
        @pl.when(i >= 1)
        def _():
            _run_copies(cnt_ref, lo_ref, dst_ref, i - 1, run_copy(1 - slot), True)
        _run_copies(cnt_ref, lo_ref, dst_ref, i, run_copy(slot), True)


def dispatch(cnt, lo, dst, zst, zln, nused, x2d, g, route, n_rows, tm, bm):
    t, d = x2d.shape
    row = lambda i, *_: (i, 0)
    return pl.pallas_call(
        _dispatch_kernel,
        grid_spec=pltpu.PrefetchScalarGridSpec(
            num_scalar_prefetch=6,
            grid=(t // tm,),
            in_specs=[pl.BlockSpec((tm, d), row), pl.BlockSpec((1, d), lambda i, *_: (0, 0)),
                      pl.BlockSpec((tm, LANE), row)],
            out_specs=pl.BlockSpec(memory_space=pl.ANY),
            scratch_shapes=[pltpu.VMEM((2, _sorted_rows(tm), d), F32), pltpu.VMEM((bm, d), F32),
                            pltpu.SemaphoreType.DMA((2,)), pltpu.SemaphoreType.DMA(())]),
        out_shape=jax.ShapeDtypeStruct((n_rows, d), F32),
        compiler_params=_params(1),
        name="dispatch",
    )(cnt, lo, dst, zst, zln, nused, x2d, g, route)


def _expert_kernel(bexp_ref, nused_ref, xs_ref, wg_ref, wu_ref, wd_ref, ys_ref, wg_scr, wu_scr, wd_scr):
    i = pl.program_id(0)

    @pl.when(i < nused_ref[0])
    def _():
        changed = jnp.logical_or(i == 0, bexp_ref[i] != bexp_ref[jnp.maximum(i - 1, 0)])

        @pl.when(changed)
        def _():
            wg_scr[...] = wg_ref[...].astype(BF16)
            wu_scr[...] = wu_ref[...].astype(BF16)
            wd_scr[...] = wd_ref[...].astype(BF16)

        xb = xs_ref[...].astype(BF16)
        hg = _dot(xb, wg_scr[...])
        hu = _dot(xb, wu_scr[...])
        act = (hg * _sigmoid(hg) * hu).astype(BF16)
        ys_ref[...] = _dot(act, wd_scr[...])

    @pl.when(i >= nused_ref[0])
    def _():
        ys_ref[...] = jnp.zeros_like(ys_ref)


def experts(bexp, nused, xs, w_gate, w_up, w_down, layer, bm):
    n_rows, d = xs.shape
    de = w_gate.shape[3]
    wsel = lambda i, bexp, nused: (layer, bexp[i], 0, 0)
    return pl.pallas_call(
        _expert_kernel,
        grid_spec=pltpu.PrefetchScalarGridSpec(
            num_scalar_prefetch=2,
            grid=(n_rows // bm,),
            in_specs=[pl.BlockSpec((bm, d), lambda i, bexp, nused: (jnp.minimum(i, nused[0] - 1), 0)),
                      pl.BlockSpec((None, None, d, de), wsel),
                      pl.BlockSpec((None, None, d, de), wsel), pl.BlockSpec((None, None, de, d), wsel)],
            out_specs=pl.BlockSpec((bm, d), lambda i, bexp, nused: (i, 0)),
            scratch_shapes=[pltpu.VMEM((d, de), BF16), pltpu.VMEM((d, de), BF16),
                            pltpu.VMEM((de, d), BF16)]),
        out_shape=jax.ShapeDtypeStruct(xs.shape, F32),
        compiler_params=_params(1),
        name="experts",
    )(bexp, nused, xs, w_gate, w_up, w_down)


def _combine_kernel(cnt_ref, lo_ref, dst_ref, x_ref, route_ref, g_ref, ys_ref, o_ref, buf, sem, *, final):
    tm = x_ref.shape[0]
    i = pl.program_id(0)
    n = pl.num_programs(0)
    slot = i % 2

    def run_copy(s):
        return lambda lo, dst, cnt: pltpu.make_async_copy(
            ys_ref.at[pl.ds(dst, cnt)], buf.at[s, pl.ds(lo, cnt)], sem.at[s])

    @pl.when(i == 0)
    def _():
        buf[...] = jnp.zeros_like(buf)
        _run_copies(cnt_ref, lo_ref, dst_ref, 0, run_copy(0), False)

    @pl.when(i + 1 < n)
    def _():
        _run_copies(cnt_ref, lo_ref, dst_ref, i + 1, run_copy(1 - slot), False)

    _run_copies(cnt_ref, lo_ref, dst_ref, i, run_copy(slot), True)
    ysb = buf[slot].astype(BF16)
    route = route_ref[...]
    out = x_ref[...]
    for k in range(2):
        out = out + route[:, k:k + 1] * _dot(_sort_matrix(route, k, buf.shape[1]), ysb)
    if final:
        out = _rms(out, g_ref[...])
    o_ref[...] = out


def combine(cnt, lo, dst, x2d, route, g, ys, tm, final):
    t, d = x2d.shape
    row = lambda i, *_: (i, 0)
    return pl.pallas_call(
        functools.partial(_combine_kernel, final=final),
        grid_spec=pltpu.PrefetchScalarGridSpec(
            num_scalar_prefetch=3,
            grid=(t // tm,),
            in_specs=[pl.BlockSpec((tm, d), row), pl.BlockSpec((tm, LANE), row),
                      pl.BlockSpec((1, d), lambda i, *_: (0, 0)),
                      pl.BlockSpec(memory_space=pl.ANY)],
            out_specs=pl.BlockSpec((tm, d), row),
            scratch_shapes=[pltpu.VMEM((2, _sorted_rows(tm), d), F32), pltpu.SemaphoreType.DMA((2,))]),
        out_shape=jax.ShapeDtypeStruct((t, d), F32),
        compiler_params=_params(1),
        name="combine",
    )(cnt, lo, dst, x2d, route, g, ys)


def moe(x2d, g, rg_w, rg_b, re_w, re_b, w_gate, w_up, w_down, layer, g_final, final, tm, bm):
    t = x2d.shape[0]
    nt = t // tm
    pad = LANE - N_GROUPS - N_EXPERTS
    wr = jnp.concatenate([rg_w, re_w, jnp.zeros((D_MODEL, pad), F32)], axis=1)
    br = jnp.concatenate([rg_b, re_b, jnp.zeros((pad,), F32)])[None, :]
    g2 = g[None, :]
    route, meta = router(x2d, g2, wr, br, tm)
    meta = meta.reshape(nt, SUBLANE, LANE)[:, :, :N_EXPERTS].astype(I32)
    cnt, earlier, lo = meta[:, 0], meta[:, 1], meta[:, 2]
    counts = meta[nt - 1, 3]
    padded = (counts + bm - 1) // bm * bm
    ends = jnp.cumsum(padded)
    offs = ends - padded
    dst = offs[None, :] + earlier
    n_rows = -(-(2 * t + nt * N_EXPERTS * (SUBLANE - 1)) // bm) * bm + N_EXPERTS * bm
    nused = (ends[-1] // bm).astype(I32)
    blk_start = jnp.arange(n_rows // bm, dtype=I32) * bm
    bexp = jnp.sum((ends[None, :] <= blk_start[:, None]).astype(I32), axis=1)
    last = jnp.sum((ends <= (nused - 1) * bm).astype(I32))
    bexp = jnp.minimum(bexp, last)
    cnt, lo, dst = cnt.reshape(-1), lo.reshape(-1), dst.reshape(-1)
    xs = dispatch(cnt, lo, dst, offs + counts, padded - counts, nused[None], x2d, g2, route, n_rows, tm, bm)
    ys = experts(bexp, nused[None], xs, w_gate, w_up, w_down, layer, bm)
    return combine(cnt, lo, dst, x2d, route, g_final[None, :], ys, tm, final)


def kernel(x, norm_mix_g, w_in, b_gate, pool_w, pool_scale, sgu_ln_g, sgu_ln_b, sgu_w, sgu_b, ssm_a_re, ssm_a_im, ssm_log_dt, ssm_b_re, ssm_b_im, ssm_c_re, ssm_c_im, ssm_d, glu_w, glu_b, w_branch, w_out, norm_ffn_g, router_group_w, router_group_b, router_expert_w, router_expert_b, exp_w_gate, exp_w_up, exp_w_down, norm_final_g):
    batch, seq, d = x.shape
    depth = w_in.shape[0]
    tm = 256
    bm = 256
    x2d = x.reshape(batch * seq, d)
    ssm_col_block = 3
    for l in range(depth):
        wl = w_in[l].astype(BF16)
        g_mix = norm_mix_g[l][None, :]
        xc4 = ssm_input(x2d, g_mix, wl, ssm_col_block, tm)
        bq, cq, kq, lam_q = ssm_tables(ssm_a_re[l], ssm_a_im[l], ssm_log_dt[l], ssm_b_re[l], ssm_b_im[l],
                                       ssm_c_re[l], ssm_c_im[l], ssm_d[l])
        ys4 = s5_scan(xc4, bq, cq, kq, lam_q, batch)
        p = {
            "norm_mix_g": g_mix, "w_in": wl, "b_gate": b_gate[l][None, :],
            "pool_w": pool_w[l].astype(BF16), "pool_scale": pool_scale[l][None, :],
            "sgu_ln_g": sgu_ln_g[l][None, :], "sgu_ln_b": sgu_ln_b[l][None, :],
            "sgu_w": sgu_w[l],
            "sgu_b_full": jnp.repeat(sgu_b[l].T, LANE, axis=1),
            "glu_w": glu_w[l].astype(BF16), "glu_b": glu_b[l][None, :],
            "w_branch": w_branch[l].astype(BF16), "w_out": w_out[l].astype(BF16),
        }
        x2d = mixer(x2d, ys4, p, batch, tm)
        x2d = moe(x2d, norm_ffn_g[l], router_group_w[l], router_group_b[l], router_expert_w[l],
                  router_expert_b[l], exp_w_gate, exp_w_up, exp_w_down, l, norm_final_g,
                  l == depth - 1, tm, bm)
    return x2d.reshape(batch, seq, d)
```

```python
import functools
import math

import jax
import jax.numpy as jnp
from jax import lax
from jax.experimental import pallas as pl
from jax.experimental.pallas import tpu as pltpu

F32 = jnp.float32
BF16 = jnp.bfloat16
I32 = jnp.int32

D_MODEL = 1024
POOL_WINDOWS = (2, 4, 8, 16)
POOL_HALO = 16
LANE = 128
SUBLANE = 8
SGU_CHUNK = 128
SSM_GROUP_DIM = 16
SSM_STATE = 64
SSM_Q = 8
SSM_SLABS = 4
GROUPS_PER_SLAB = LANE // SSM_GROUP_DIM
N_GROUPS = 4
EXPERTS_PER_GROUP = 8
N_EXPERTS = 32
EPS = 1e-6
VMEM_LIMIT = 56 * 1024 * 1024


def _rms(x, g):
    return x * lax.rsqrt(jnp.mean(x * x, axis=-1, keepdims=True) + EPS) * g


def _gelu(x):
    c = math.sqrt(2.0 / math.pi)
    return x * (0.5 * (1.0 + jnp.tanh(c * (x + 0.044715 * (x * x * x)))))


def _sigmoid(x):
    return 1.0 / (1.0 + jnp.exp(-x))


def _dot(a, b):
    return jnp.dot(a, b, preferred_element_type=F32)


def _const_spec(shape):
    nd = len(shape)
    return pl.BlockSpec(shape, lambda *_: (0,) * nd, pipeline_mode=pl.Buffered(1))


def _params(n_axes):
    return pltpu.CompilerParams(dimension_semantics=("arbitrary",) * n_axes, vmem_limit_bytes=VMEM_LIMIT)


def _xc_kernel(x_ref, g_ref, w_ref, o_ref, xc_scr):
    h = _rms(x_ref[...], g_ref[...]).astype(BF16)
    xc = _dot(h, w_ref[...])
    rows = xc_scr.shape[1] // SSM_Q
    for j in range(SSM_SLABS):
        xc_scr[j] = xc[:, j * LANE:(j + 1) * LANE]
        for s in range(SSM_Q):
            o_ref[j, :, s * LANE:(s + 1) * LANE] = xc_scr[j, pl.ds(s, rows, stride=SSM_Q), :].astype(BF16)


def ssm_input(x2d, g, w_in_bf, col_block, tm):
    t = x2d.shape[0]
    width = SSM_SLABS * LANE
    return pl.pallas_call(
        _xc_kernel,
        grid=(t // tm,),
        in_specs=[pl.BlockSpec((tm, D_MODEL), lambda i: (i, 0)),
                  _const_spec((1, D_MODEL)),
                  pl.BlockSpec((D_MODEL, width), lambda i: (0, col_block), pipeline_mode=pl.Buffered(1))],
        out_specs=pl.BlockSpec((SSM_SLABS, tm // SSM_Q, SSM_Q * LANE), lambda i: (0, i, 0)),
        out_shape=jax.ShapeDtypeStruct((SSM_SLABS, t // SSM_Q, SSM_Q * LANE), BF16),
        scratch_shapes=[pltpu.VMEM((SSM_SLABS, tm, LANE), F32)],
        compiler_params=_params(1),
        name="ssm_input",
    )(x2d, g, w_in_bf)


def _compact(small):
    sl, r1, a, r2, c1, c2 = small.shape
    return small.reshape(sl, r1 * a * r2, c1 * c2).astype(BF16)


def ssm_tables(a_re, a_im, log_dt, b_re, b_im, c_re, c_im, d_skip):
    q = SSM_Q
    ng = a_re.shape[0]
    sl = ng // GROUPS_PER_SLAB
    dt = jnp.exp(log_dt)[:, None]
    k = jnp.arange(q + 1, dtype=F32)[:, None, None]
    mag = jnp.exp(a_re * dt * k)
    ang = a_im * dt * k
    pw_re, pw_im = mag * jnp.cos(ang), mag * jnp.sin(ang)
    nr, ni = pw_re[1] - 1.0, pw_im[1]
    den = a_re * a_re + a_im * a_im
    fr, fi = (nr * a_re + ni * a_im) / den, (ni * a_re - nr * a_im) / den
    bb_re = fr[..., None] * b_re - fi[..., None] * b_im
    bb_im = fr[..., None] * b_im + fi[..., None] * b_re

    def slabbed(a, axis):
        return a.reshape(*a.shape[:axis], sl, GROUPS_PER_SLAB, *a.shape[axis + 1:])

    k_rev = (q - 1.0) - jnp.arange(q, dtype=F32)[:, None, None]
    mag_rev = jnp.exp(a_re * dt * k_rev)
    rev_re, rev_im = mag_rev * jnp.cos(a_im * dt * k_rev), mag_rev * jnp.sin(a_im * dt * k_rev)
    m_re = rev_re[..., None] * bb_re[None] - rev_im[..., None] * bb_im[None]
    m_im = rev_re[..., None] * bb_im[None] + rev_im[..., None] * bb_re[None]
    m = slabbed(jnp.stack([m_re, m_im], axis=0), 2)
    bq = _compact(m.transpose(2, 1, 3, 5, 0, 4))
    pr, pi = pw_re[1:, :, None, :], pw_im[1:, :, None, :]
    n_re = c_re[None] * pr - c_im[None] * pi
    n_im = c_re[None] * pi + c_im[None] * pr
    n = slabbed(jnp.stack([n_re, -n_im], axis=0), 2)
    cq = _compact(n.transpose(2, 0, 3, 5, 1, 4))
    cb_re = c_re[:, :, :, None] * bb_re[:, None, :, :] - c_im[:, :, :, None] * bb_im[:, None, :, :]
    cb_im = c_re[:, :, :, None] * bb_im[:, None, :, :] + c_im[:, :, :, None] * bb_re[:, None, :, :]
    kern = jnp.sum(cb_re[None] * pw_re[:q, :, None, :, None] - cb_im[None] * pw_im[:q, :, None, :, None],
                   axis=3)
    skip = d_skip.reshape(ng, SSM_GROUP_DIM)[:, :, None] * jnp.eye(SSM_GROUP_DIM, dtype=F32)[None]
    kern = kern + jnp.where(jnp.arange(q)[:, None, None, None] == 0, skip[None], 0.0)
    s_idx = jnp.arange(q)[:, None]
    t_idx = jnp.arange(q)[None, :]
    onehot_lag = (t_idx - s_idx)[:, :, None] == jnp.arange(q)[None, None, :]
    kst = jnp.sum(jnp.where(onehot_lag[..., None, None, None], kern[None, None], 0.0), axis=2)
    kst = slabbed(kst, 2)
    kq = _compact(kst.transpose(2, 0, 3, 5, 1, 4))
    lam_q = jnp.concatenate([pw_re[q].reshape(sl, 1, -1), pw_im[q].reshape(sl, 1, -1)], axis=-1)
    return bq, cq, kq, lam_q


def _expand_block_diag(small, row_inner, col_inner):
    n_rows, n_small = small.shape
    n_cols = n_small * GROUPS_PER_SLAB
    g_bits = GROUPS_PER_SLAB.bit_length() - 1
    r_shift, c_shift = row_inner.bit_length() - 1, col_inner.bit_length() - 1
    k = lax.broadcasted_iota(I32, (n_small, n_cols), 0)
    col = lax.broadcasted_iota(I32, (n_small, n_cols), 1)
    src = ((col >> (c_shift + g_bits)) << c_shift) | (col & (col_inner - 1))
    spread = jnp.where(k == src, 1.0, 0.0).astype(BF16)
    r = lax.broadcasted_iota(I32, (n_rows, n_cols), 0)
    c = lax.broadcasted_iota(I32, (n_rows, n_cols), 1)
    same = ((r >> r_shift) & (GROUPS_PER_SLAB - 1)) == ((c >> c_shift) & (GROUPS_PER_SLAB - 1))
    return jnp.where(same, _dot(small, spread), 0.0).astype(BF16)


def _s5_kernel(u_ref, bq_ref, cq_ref, kq_ref, lam_ref, y_ref, g_scr, s_scr, t_scr, tbq, tcq, tkq, *, seqs):
    half = GROUPS_PER_SLAB * SSM_STATE
    rows = u_ref.shape[1]

    @pl.when(pl.program_id(1) == 0)
    def _():
        tbq[...] = _expand_block_diag(bq_ref[0], SSM_GROUP_DIM, SSM_STATE)
        tcq[...] = _expand_block_diag(cq_ref[0], SSM_STATE, SSM_GROUP_DIM)
        tkq[...] = _expand_block_diag(kq_ref[0], SSM_GROUP_DIM, SSM_GROUP_DIM)

    u = u_ref[0]
    g_scr[...] = _dot(u, tbq[...])
    lam = lam_ref[0]
    lre, lim = lam[:, :half], lam[:, half:]
    seq_rows = rows // seqs

    def body(n, carry):
        out = []
        for b in range(seqs):
            cre, cim = carry[2 * b], carry[2 * b + 1]
            row = b * seq_rows + n
            s_scr[pl.ds(row, 1), :] = jnp.concatenate([cre, cim], axis=1)
            g = g_scr[pl.ds(row, 1), :]
            out.append(lre * cre - lim * cim + g[:, :half])
            out.append(lre * cim + lim * cre + g[:, half:])
        return tuple(out)

    zero = jnp.zeros((1, half), F32)
    lax.fori_loop(0, seq_rows, body, (zero,) * (2 * seqs), unroll=2)
    y = _dot(s_scr[...].astype(BF16), tcq[...]) + _dot(u, tkq[...])
    for t in range(SSM_Q):
        t_scr[pl.ds(t, rows, stride=SSM_Q), :] = y[:, t * LANE:(t + 1) * LANE]
    y_ref[0] = t_scr[...].astype(BF16)


def s5_scan(u4, bq, cq, kq, lam_q, batch, seqs):
    sl, rows_total, width = u4.shape
    rows = rows_total // batch * seqs
    tab = pl.BlockSpec((1, width, width // GROUPS_PER_SLAB), lambda j, b: (j, 0, 0))
    return pl.pallas_call(
        functools.partial(_s5_kernel, seqs=seqs),
        grid=(sl, batch // seqs),
        in_specs=[pl.BlockSpec((1, rows, width), lambda j, b: (j, b, 0)), tab, tab, tab,
                  pl.BlockSpec((1, 1, width), lambda j, b: (j, 0, 0))],
        out_specs=pl.BlockSpec((1, rows * SSM_Q, LANE), lambda j, b: (j, b, 0)),
        out_shape=jax.ShapeDtypeStruct((sl, rows_total * SSM_Q, LANE), BF16),
        scratch_shapes=[pltpu.VMEM((rows, width), F32), pltpu.VMEM((rows, width), F32),
                        pltpu.VMEM((rows * SSM_Q, LANE), F32)] + [pltpu.VMEM((width, width), BF16)] * 3,
        compiler_params=_params(2),
        name="s5_scan",
    )(u4, bq, cq, kq, lam_q)


def _mixer_kernel(x_ref, ys_ref, ng_ref, win_ref, bgate_ref, poolw_ref, pscale_ref, lng_ref, lnb_ref,
                  sguw_ref, sgub_ref, gluw_ref, glub_ref, wbr_ref, wout_ref, o_ref, xe_scr):
    tm = x_ref.shape[0]
    width = SSM_SLABS * LANE
    a_end, b_end = width, 3 * width
    c_end = b_end + width
    i = pl.program_id(1)
    x = x_ref[...]
    hb = _rms(x, ng_ref[...]).astype(BF16)

    xa = _dot(hb, win_ref[:, 0:a_end])

    @pl.when(i == 0)
    def _():
        xe_scr[0:POOL_HALO, :] = jnp.zeros((POOL_HALO, xe_scr.shape[1]), F32)

    xe_scr[POOL_HALO:POOL_HALO + tm, :] = xa
    pos = i * tm + lax.broadcasted_iota(I32, (tm, 1), 0)
    ya_parts = []
    for gi, w in enumerate(POOL_WINDOWS):
        ch = slice(gi * LANE, (gi + 1) * LANE)
        win = xa[:, ch]
        for j in range(1, w):
            win = win + xe_scr[POOL_HALO - j:POOL_HALO - j + tm, ch]
        cnt = jnp.minimum(pos + 1, w).astype(F32)
        diff = win / cnt - xa[:, ch]
        ya_parts.append(_dot(diff.astype(BF16), poolw_ref[gi]) * pscale_ref[:, ch])
    ya = jnp.concatenate(ya_parts, axis=1)
    xe_scr[0:POOL_HALO, :] = xe_scr[tm:tm + POOL_HALO, :]

    z = _gelu(_dot(hb, win_ref[:, a_end:b_end]))
    u, v = z[:, :width], z[:, width:]
    mu = jnp.mean(v, axis=-1, keepdims=True)
    vc = v - mu
    var = jnp.mean(vc * vc, axis=-1, keepdims=True)
    vb = (vc * lax.rsqrt(var + EPS) * lng_ref[...] + lnb_ref[...]).astype(BF16)
    r_i = lax.broadcasted_iota(I32, (SGU_CHUNK, SGU_CHUNK), 0)
    c_i = lax.broadcasted_iota(I32, (SGU_CHUNK, SGU_CHUNK), 1)
    n_heads = width // LANE
    wt = [jnp.where(r_i >= c_i, sguw_ref[hd], 0.0).astype(BF16) for hd in range(n_heads)]
    s_rows = []
    for c in range(tm // SGU_CHUNK):
        rs = slice(c * SGU_CHUNK, (c + 1) * SGU_CHUNK)
        s_rows.append(jnp.concatenate(
            [_dot(wt[hd], vb[rs, hd * LANE:(hd + 1) * LANE]) for hd in range(n_heads)], axis=1)
            + sgub_ref[...])
    yb = u * jnp.concatenate(s_rows, axis=0)

    yc = _gelu(jnp.concatenate([ys_ref[j] for j in range(SSM_SLABS)], axis=1).astype(F32))
    yc = yc * _sigmoid(_dot(yc.astype(BF16), gluw_ref[...]) + glub_ref[...])

    merged = None
    for k, yk in enumerate((ya, yb, yc)):
        gate = _sigmoid(_dot(hb, win_ref[:, c_end + k * D_MODEL:c_end + (k + 1) * D_MODEL])
                        + bgate_ref[:, k * D_MODEL:(k + 1) * D_MODEL])
        term = gate * _dot(yk.astype(BF16), wbr_ref[k])
        merged = term if merged is None else merged + term
    o_ref[...] = x + _dot(merged.astype(BF16), wout_ref[...])


def mixer(x2d, ys4, p, batch, tm):
    t = x2d.shape[0]
    nt = t // batch // tm
    row = lambda b, i: (b * nt + i, 0)
    slab = lambda b, i: (0, b * nt + i, 0)
    consts = [p["norm_mix_g"], p["w_in"], p["b_gate"], p["pool_w"], p["pool_scale"], p["sgu_ln_g"],
              p["sgu_ln_b"], p["sgu_w"], p["sgu_b_full"], p["glu_w"], p["glu_b"], p["w_branch"], p["w_out"]]
    return pl.pallas_call(
        _mixer_kernel,
        grid=(batch, nt),
        in_specs=[pl.BlockSpec((tm, D_MODEL), row),
                  pl.BlockSpec((SSM_SLABS, tm, LANE), slab)] + [_const_spec(c.shape) for c in consts],
        out_specs=pl.BlockSpec((tm, D_MODEL), row),
        out_shape=jax.ShapeDtypeStruct(x2d.shape, F32),
        scratch_shapes=[pltpu.VMEM((tm + POOL_HALO, SSM_SLABS * LANE), F32)],
        compiler_params=_params(2),
        name="mixer",
    )(x2d, ys4, *consts)


def _router_kernel(x_ref, g_ref, wr_ref, br_ref, route_ref, meta_ref, carry_scr):
    tm = x_ref.shape[0]
    i = pl.program_id(0)

    @pl.when(i == 0)
    def _():
        carry_scr[...] = jnp.zeros_like(carry_scr)

    h = _rms(x_ref[...], g_ref[...])
    h_hi = h.astype(BF16)
    h_lo = (h - h_hi.astype(F32)).astype(BF16)
    logits = (_dot(h_hi, wr_ref[0]) + _dot(h_hi, wr_ref[1]) + _dot(h_lo, wr_ref[0])) + br_ref[...]
    lane = lax.broadcasted_iota(I32, logits.shape, 1)
    neg = -jnp.inf
    gl = jnp.where(lane < N_GROUPS, logits, neg)
    gmax = jnp.max(gl, axis=-1, keepdims=True)
    gsum = jnp.sum(jnp.where(lane < N_GROUPS, jnp.exp(logits - gmax), 0.0), axis=-1, keepdims=True)
    g_val = 1.0 / gsum
    g_idx = jnp.min(jnp.where(gl == gmax, lane, LANE), axis=-1, keepdims=True)
    lo = N_GROUPS + EXPERTS_PER_GROUP * g_idx
    el = jnp.where((lane >= lo) & (lane < lo + EXPERTS_PER_GROUP), logits, neg)
    e1v = jnp.max(el, axis=-1, keepdims=True)
    e1l = jnp.min(jnp.where(el == e1v, lane, LANE), axis=-1, keepdims=True)
    el2 = jnp.where(lane == e1l, neg, el)
    e2v = jnp.max(el2, axis=-1, keepdims=True)
    e2l = jnp.min(jnp.where(el2 == e2v, lane, LANE), axis=-1, keepdims=True)
    tt = jnp.exp(e2v - e1v)
    w1 = g_val / (1.0 + tt)
    w2 = g_val * tt / (1.0 + tt)
    hit1, hit2 = lane == e1l - N_GROUPS, lane == e2l - N_GROUPS
    onehot = jnp.where(hit1 | hit2, 1.0, 0.0)
    r_i = lax.broadcasted_iota(I32, (tm, tm), 0)
    c_i = lax.broadcasted_iota(I32, (tm, tm), 1)
    tri = jnp.where(r_i > c_i, 1.0, 0.0).astype(BF16)
    prefix = _dot(tri, onehot.astype(BF16))
    count = jnp.sum(onehot, axis=0, keepdims=True)
    groups8 = jnp.floor((count + (SUBLANE - 1.0)) * (1.0 / SUBLANE))
    count = groups8 * SUBLANE
    e_r = lax.broadcasted_iota(I32, (LANE, LANE), 0)
    e_c = lax.broadcasted_iota(I32, (LANE, LANE), 1)
    before = jnp.where(e_r < e_c, 1.0, 0.0).astype(BF16)
    run_start = _dot(jnp.broadcast_to(groups8, (SUBLANE, LANE)).astype(BF16), before)[0:1, :] * SUBLANE
    local = prefix + run_start
    lp1 = jnp.sum(jnp.where(hit1, local, 0.0), axis=-1, keepdims=True)
    lp2 = jnp.sum(jnp.where(hit2, local, 0.0), axis=-1, keepdims=True)
    route_ref[...] = jnp.where(lane == 0, w1, jnp.where(lane == 1, w2, jnp.where(
        lane == 2, lp1, jnp.where(lane == 3, lp2, 0.0))))
    earlier = carry_scr[0:1, :]
    total = earlier + count
    carry_scr[...] = jnp.broadcast_to(total, carry_scr.shape)
    sub = lax.broadcasted_iota(I32, meta_ref.shape, 0)
    meta_ref[...] = jnp.where(sub == 0, count, jnp.where(sub == 1, earlier, jnp.where(
        sub == 2, run_start, jnp.where(sub == 3, total, 0.0))))


def router(x2d, g, wr, br, tm):
    t = x2d.shape[0]
    row = lambda i: (i, 0)
    return pl.pallas_call(
        _router_kernel,
        grid=(t // tm,),
        in_specs=[pl.BlockSpec((tm, D_MODEL), row), _const_spec((1, D_MODEL)),
                  _const_spec((2, D_MODEL, LANE)), _const_spec((1, LANE))],
        out_specs=[pl.BlockSpec((tm, LANE), row), pl.BlockSpec((SUBLANE, LANE), row)],
        out_shape=[jax.ShapeDtypeStruct((t, LANE), F32),
                   jax.ShapeDtypeStruct((t // tm * SUBLANE, LANE), F32)],
        scratch_shapes=[pltpu.VMEM((SUBLANE, LANE), F32)],
        compiler_params=_params(1),
        name="router",
    )(x2d, g, wr, br)


def _sort_matrix(route, k, n_sorted):
    col = lax.broadcasted_iota(I32, (route.shape[0], n_sorted), 1)
    return jnp.where(col == route[:, 2 + k:3 + k].astype(I32), 1.0, 0.0).astype(BF16)


def _sorted_rows(tm):
    return -(-(2 * tm + N_EXPERTS * (SUBLANE - 1)) // LANE) * LANE


def _run_copies(cnt_ref, lo_ref, dst_ref, tile, make_copy, wait):
    def body(e, c):
        idx = tile * N_EXPERTS + e
        n = cnt_ref[idx]

        @pl.when(n > 0)
        def _():
            cp = make_copy(pl.multiple_of(lo_ref[idx], SUBLANE), pl.multiple_of(dst_ref[idx], SUBLANE),
                           pl.multiple_of(n, SUBLANE))
            if wait:
                cp.wait()
            else:
                cp.start()
        return c
    lax.fori_loop(0, N_EXPERTS, body, 0)


def _dispatch_kernel(cnt_ref, lo_ref, dst_ref, zst_ref, zln_ref, nused_ref, x_ref, g_ref, route_ref, o_ref,
                     srt_scr, zero_scr, sem, zsem):
    tm = x_ref.shape[0]
    bm = zero_scr.shape[0]
    i = pl.program_id(0)
    n = pl.num_programs(0)
    slot = i % 2

    def run_copy(s):
        return lambda lo, dst, cnt: pltpu.make_async_copy(
            srt_scr.at[s, pl.ds(lo, cnt)], o_ref.at[pl.ds(dst, cnt)], sem.at[s])

    def pad_copy(e):
        ln = pl.multiple_of(zln_ref[e], SUBLANE)
        return pltpu.make_async_copy(zero_scr.at[pl.ds(0, ln)],
                                     o_ref.at[pl.ds(pl.multiple_of(zst_ref[e], SUBLANE), ln)], zsem)

    def tail_copy(b):
        return pltpu.make_async_copy(zero_scr, o_ref.at[pl.ds(pl.multiple_of(b * bm, bm), bm)], zsem)

    @pl.when(i == 0)
    def _():
        zero_scr[...] = jnp.zeros_like(zero_scr)
        for wait in (False, True):
            def body(e, c):
                @pl.when(zln_ref[e] > 0)
                def _():
                    cp = pad_copy(e)
                    cp.wait() if wait else cp.start()
                return c
            lax.fori_loop(0, N_EXPERTS, body, 0)

            def tail(b, c):
                cp = tail_copy(b)
                cp.wait() if wait else cp.start()
                return c
            lax.fori_loop(nused_ref[0], o_ref.shape[0] // bm, tail, 0)

    @pl.when(i >= 2)
    def _():
        _run_copies(cnt_ref, lo_ref, dst_ref, i - 2, run_copy(slot), True)

    hb = _rms(x_ref[...], g_ref[...]).astype(BF16)
    route = route_ref[...]
    ns = srt_scr.shape[1]
    pt = jnp.maximum(_sort_matrix(route, 0, ns), _sort_matrix(route, 1, ns))
    srt_scr[slot] = lax.dot_general(pt, hb, (((0,), (0,)), ((), ())), preferred_element_type=F32)
    _run_copies(cnt_ref, lo_ref, dst_ref, i, run_copy(slot), False)

    @pl.when(i == n - 1)
    def _():
        @pl.when(i >= 1)
        def _():
            _run_copies(cnt_ref, lo_ref, dst_ref, i - 1, run_copy(1 - slot), True)
        _run_copies(cnt_ref, lo_ref, dst_ref, i, run_copy(slot), True)


def dispatch(cnt, lo, dst, zst, zln, nused, x2d, g, route, n_rows, tm, bm):
    t, d = x2d.shape
    row = lambda i, *_: (i, 0)
    return pl.pallas_call(
        _dispatch_kernel,
        grid_spec=pltpu.PrefetchScalarGridSpec(
            num_scalar_prefetch=6,
            grid=(t // tm,),
            in_specs=[pl.BlockSpec((tm, d), row), pl.BlockSpec((1, d), lambda i, *_: (0, 0)),
                      pl.BlockSpec((tm, LANE), row)],
            out_specs=pl.BlockSpec(memory_space=pl.ANY),
            scratch_shapes=[pltpu.VMEM((2, _sorted_rows(tm), d), F32), pltpu.VMEM((bm, d), F32),
                            pltpu.SemaphoreType.DMA((2,)), pltpu.SemaphoreType.DMA(())]),
        out_shape=jax.ShapeDtypeStruct((n_rows, d), F32),
        compiler_params=_params(1),
        name="dispatch",
    )(cnt, lo, dst, zst, zln, nused, x2d, g, route)


def _expert_kernel(bexp_ref, nused_ref, xs_ref, wg_ref, wu_ref, wd_ref, ys_ref, wg_scr, wu_scr, wd_scr):
    i = pl.program_id(0)

    @pl.when(i < nused_ref[0])
    def _():
        changed = jnp.logical_or(i == 0, bexp_ref[i] != bexp_ref[jnp.maximum(i - 1, 0)])

        @pl.when(changed)
        def _():
            wg_scr[...] = wg_ref[...].astype(BF16)
            wu_scr[...] = wu_ref[...].astype(BF16)
            wd_scr[...] = wd_ref[...].astype(BF16)

        xb = xs_ref[...].astype(BF16)
        hg = _dot(xb, wg_scr[...])
        hu = _dot(xb, wu_scr[...])
        act = (hg * _sigmoid(hg) * hu).astype(BF16)
        ys_ref[...] = _dot(act, wd_scr[...])

    @pl.when(i >= nused_ref[0])
    def _():
        ys_ref[...] = jnp.zeros_like(ys_ref)


def experts(bexp, nused, xs, w_gate, w_up, w_down, layer, bm):
    n_rows, d = xs.shape
    de = w_gate.shape[3]
    wsel = lambda i, bexp, nused: (layer, bexp[i], 0, 0)
    return pl.pallas_call(
        _expert_kernel,
        grid_spec=pltpu.PrefetchScalarGridSpec(
            num_scalar_prefetch=2,
            grid=(n_rows // bm,),
            in_specs=[pl.BlockSpec((bm, d), lambda i, bexp, nused: (jnp.minimum(i, nused[0] - 1), 0)),
                      pl.BlockSpec((None, None, d, de), wsel),
                      pl.BlockSpec((None, None, d, de), wsel), pl.BlockSpec((None, None, de, d), wsel)],
            out_specs=pl.BlockSpec((bm, d), lambda i, bexp, nused: (i, 0)),
            scratch_shapes=[pltpu.VMEM((d, de), BF16), pltpu.VMEM((d, de), BF16),
                            pltpu.VMEM((de, d), BF16)]),
        out_shape=jax.ShapeDtypeStruct(xs.shape, F32),
        compiler_params=_params(1),
        name="experts",
    )(bexp, nused, xs, w_gate, w_up, w_down)


def _combine_kernel(cnt_ref, lo_ref, dst_ref, x_ref, route_ref, g_ref, ys_ref, o_ref, buf, sem, *, final):
    tm = x_ref.shape[0]
    i = pl.program_id(0)
    n = pl.num_programs(0)
    slot = i % 2

    def run_copy(s):
        return lambda lo, dst, cnt: pltpu.make_async_copy(
            ys_ref.at[pl.ds(dst, cnt)], buf.at[s, pl.ds(lo, cnt)], sem.at[s])

    @pl.when(i == 0)
    def _():
        buf[...] = jnp.zeros_like(buf)
        _run_copies(cnt_ref, lo_ref, dst_ref, 0, run_copy(0), False)

    @pl.when(i + 1 < n)
    def _():
        _run_copies(cnt_ref, lo_ref, dst_ref, i + 1, run_copy(1 - slot), False)

    _run_copies(cnt_ref, lo_ref, dst_ref, i, run_copy(slot), True)
    ysb = buf[slot].astype(BF16)
    route = route_ref[...]
    out = x_ref[...]
    for k in range(2):
        out = out + route[:, k:k + 1] * _dot(_sort_matrix(route, k, buf.shape[1]), ysb)
    if final:
        out = _rms(out, g_ref[...])
    o_ref[...] = out


def combine(cnt, lo, dst, x2d, route, g, ys, tm, final):
    t, d = x2d.shape
    row = lambda i, *_: (i, 0)
    return pl.pallas_call(
        functools.partial(_combine_kernel, final=final),
        grid_spec=pltpu.PrefetchScalarGridSpec(
            num_scalar_prefetch=3,
            grid=(t // tm,),
            in_specs=[pl.BlockSpec((tm, d), row), pl.BlockSpec((tm, LANE), row),
                      pl.BlockSpec((1, d), lambda i, *_: (0, 0)),
                      pl.BlockSpec(memory_space=pl.ANY)],
            out_specs=pl.BlockSpec((tm, d), row),
            scratch_shapes=[pltpu.VMEM((2, _sorted_rows(tm), d), F32), pltpu.SemaphoreType.DMA((2,))]),
        out_shape=jax.ShapeDtypeStruct((t, d), F32),
        compiler_params=_params(1),
        name="combine",
    )(cnt, lo, dst, x2d, route, g, ys)


def moe(x2d, g, rg_w, rg_b, re_w, re_b, w_gate, w_up, w_down, layer, g_final, final, tm, bm):
    t = x2d.shape[0]
    nt = t // tm
    pad = LANE - N_GROUPS - N_EXPERTS
    wr = jnp.concatenate([rg_w, re_w, jnp.zeros((D_MODEL, pad), F32)], axis=1)
    wr_hi = wr.astype(BF16)
    wr = jnp.stack([wr_hi, (wr - wr_hi.astype(F32)).astype(BF16)])
    br = jnp.concatenate([rg_b, re_b, jnp.zeros((pad,), F32)])[None, :]
    g2 = g[None, :]
    route, meta = router(x2d, g2, wr, br, tm)
    meta = meta.reshape(nt, SUBLANE, LANE)[:, :, :N_EXPERTS].astype(I32)
    cnt, earlier, lo = meta[:, 0], meta[:, 1], meta[:, 2]
    counts = meta[nt - 1, 3]
    padded = (counts + bm - 1) // bm * bm
    ends = jnp.cumsum(padded)
    offs = ends - padded
    dst = offs[None, :] + earlier
    n_rows = -(-(2 * t + nt * N_EXPERTS * (SUBLANE - 1)) // bm) * bm + N_EXPERTS * bm
    nused = (ends[-1] // bm).astype(I32)
    blk_start = jnp.arange(n_rows // bm, dtype=I32) * bm
    bexp = jnp.sum((ends[None, :] <= blk_start[:, None]).astype(I32), axis=1)
    last = jnp.sum((ends <= (nused - 1) * bm).astype(I32))
    bexp = jnp.minimum(bexp, last)
    cnt, lo, dst = cnt.reshape(-1), lo.reshape(-1), dst.reshape(-1)
    xs = dispatch(cnt, lo, dst, offs + counts, padded - counts, nused[None], x2d, g2, route, n_rows, tm, bm)
    ys = experts(bexp, nused[None], xs, w_gate, w_up, w_down, layer, bm)
    return combine(cnt, lo, dst, x2d, route, g_final[None, :], ys, tm, final)


def kernel(x, norm_mix_g, w_in, b_gate, pool_w, pool_scale, sgu_ln_g, sgu_ln_b, sgu_w, sgu_b, ssm_a_re, ssm_a_im, ssm_log_dt, ssm_b_re, ssm_b_im, ssm_c_re, ssm_c_im, ssm_d, glu_w, glu_b, w_branch, w_out, norm_ffn_g, router_group_w, router_group_b, router_expert_w, router_expert_b, exp_w_gate, exp_w_up, exp_w_down, norm_final_g):
    batch, seq, d = x.shape
    depth = w_in.shape[0]
    tm_in, tm_mix, tm_moe, bm = 1024, 512, 256, 512
    s5_seqs = 2
    x2d = x.reshape(batch * seq, d)
    ssm_col_block = 3
    for l in range(depth):
        wl = w_in[l].astype(BF16)
        g_mix = norm_mix_g[l][None, :]
        xc4 = ssm_input(x2d, g_mix, wl, ssm_col_block, tm_in)
        bq, cq, kq, lam_q = ssm_tables(ssm_a_re[l], ssm_a_im[l], ssm_log_dt[l], ssm_b_re[l], ssm_b_im[l],
                                       ssm_c_re[l], ssm_c_im[l], ssm_d[l])
        ys4 = s5_scan(xc4, bq, cq, kq, lam_q, batch, s5_seqs)
        p = {
            "norm_mix_g": g_mix, "w_in": wl, "b_gate": b_gate[l][None, :],
            "pool_w": pool_w[l].astype(BF16), "pool_scale": pool_scale[l][None, :],
            "sgu_ln_g": sgu_ln_g[l][None, :], "sgu_ln_b": sgu_ln_b[l][None, :],
            "sgu_w": sgu_w[l],
            "sgu_b_full": jnp.repeat(sgu_b[l].T, LANE, axis=1),
            "glu_w": glu_w[l].astype(BF16), "glu_b": glu_b[l][None, :],
            "w_branch": w_branch[l].astype(BF16), "w_out": w_out[l].astype(BF16),
        }
        x2d = mixer(x2d, ys4, p, batch, tm_mix)
        x2d = moe(x2d, norm_ffn_g[l], router_group_w[l], router_group_b[l], router_expert_w[l],
                  router_expert_b[l], exp_w_gate, exp_w_up, exp_w_down, l, norm_final_g,
                  l == depth - 1, tm_moe, bm)
    return x2d.reshape(batch, seq, d)
```

```python
import functools
import math

import jax
import jax.numpy as jnp
from jax import lax
from jax.experimental import pallas as pl
from jax.experimental.pallas import tpu as pltpu

F32 = jnp.float32
BF16 = jnp.bfloat16
I32 = jnp.int32
U32 = jnp.uint32

D_MODEL = 1024
POOL_WINDOWS = (2, 4, 8, 16)
POOL_HALO = 16
LANE = 128
SUBLANE = 8
SGU_CHUNK = 128
SSM_GROUP_DIM = 16
SSM_STATE = 64
SSM_Q = 8
SSM_SLABS = 4
GROUPS_PER_SLAB = LANE // SSM_GROUP_DIM
N_GROUPS = 4
EXPERTS_PER_GROUP = 8
N_EXPERTS = 32
EPS = 1e-6
VMEM_LIMIT = 56 * 1024 * 1024


def _rms(x, g):
    return x * lax.rsqrt(jnp.mean(x * x, axis=-1, keepdims=True) + EPS) * g


def _gelu(x):
    c = math.sqrt(2.0 / math.pi)
    return x * (0.5 * (1.0 + jnp.tanh(c * (x + 0.044715 * (x * x * x)))))


def _sigmoid(x):
    return 1.0 / (1.0 + jnp.exp(-x))


def _dot(a, b):
    return jnp.dot(a, b, preferred_element_type=F32)


def _const_spec(shape):
    nd = len(shape)
    return pl.BlockSpec(shape, lambda *_: (0,) * nd, pipeline_mode=pl.Buffered(1))


def _params(n_axes):
    return pltpu.CompilerParams(dimension_semantics=("arbitrary",) * n_axes, vmem_limit_bytes=VMEM_LIMIT)


def _xc_kernel(x_ref, g_ref, w_ref, o_ref, xc_scr):
    h = _rms(x_ref[...], g_ref[...]).astype(BF16)
    xc = _dot(h, w_ref[...])
    rows = xc_scr.shape[1] // SSM_Q
    for j in range(SSM_SLABS):
        xc_scr[j] = xc[:, j * LANE:(j + 1) * LANE]
        for s in range(SSM_Q):
            o_ref[j, :, s * LANE:(s + 1) * LANE] = xc_scr[j, pl.ds(s, rows, stride=SSM_Q), :].astype(BF16)


def ssm_input(x2d, g, w_in_bf, col_block, tm):
    t = x2d.shape[0]
    width = SSM_SLABS * LANE
    return pl.pallas_call(
        _xc_kernel,
        grid=(t // tm,),
        in_specs=[pl.BlockSpec((tm, D_MODEL), lambda i: (i, 0)),
                  _const_spec((1, D_MODEL)),
                  pl.BlockSpec((D_MODEL, width), lambda i: (0, col_block), pipeline_mode=pl.Buffered(1))],
        out_specs=pl.BlockSpec((SSM_SLABS, tm // SSM_Q, SSM_Q * LANE), lambda i: (0, i, 0)),
        out_shape=jax.ShapeDtypeStruct((SSM_SLABS, t // SSM_Q, SSM_Q * LANE), BF16),
        scratch_shapes=[pltpu.VMEM((SSM_SLABS, tm, LANE), F32)],
        compiler_params=_params(1),
        name="ssm_input",
    )(x2d, g, w_in_bf)


def _compact(small):
    sl, r1, a, r2, c1, c2 = small.shape
    return small.reshape(sl, r1 * a * r2, c1 * c2).astype(BF16)


def ssm_tables(a_re, a_im, log_dt, b_re, b_im, c_re, c_im, d_skip):
    q = SSM_Q
    ng = a_re.shape[0]
    sl = ng // GROUPS_PER_SLAB
    dt = jnp.exp(log_dt)[:, None]
    k = jnp.arange(q + 1, dtype=F32)[:, None, None]
    mag = jnp.exp(a_re * dt * k)
    ang = a_im * dt * k
    pw_re, pw_im = mag * jnp.cos(ang), mag * jnp.sin(ang)
    nr, ni = pw_re[1] - 1.0, pw_im[1]
    den = a_re * a_re + a_im * a_im
    fr, fi = (nr * a_re + ni * a_im) / den, (ni * a_re - nr * a_im) / den
    bb_re = fr[..., None] * b_re - fi[..., None] * b_im
    bb_im = fr[..., None] * b_im + fi[..., None] * b_re

    def slabbed(a, axis):
        return a.reshape(*a.shape[:axis], sl, GROUPS_PER_SLAB, *a.shape[axis + 1:])

    k_rev = (q - 1.0) - jnp.arange(q, dtype=F32)[:, None, None]
    mag_rev = jnp.exp(a_re * dt * k_rev)
    rev_re, rev_im = mag_rev * jnp.cos(a_im * dt * k_rev), mag_rev * jnp.sin(a_im * dt * k_rev)
    m_re = rev_re[..., None] * bb_re[None] - rev_im[..., None] * bb_im[None]
    m_im = rev_re[..., None] * bb_im[None] + rev_im[..., None] * bb_re[None]
    m = slabbed(jnp.stack([m_re, m_im], axis=0), 2)
    bq = _compact(m.transpose(2, 1, 3, 5, 0, 4))
    pr, pi = pw_re[1:, :, None, :], pw_im[1:, :, None, :]
    n_re = c_re[None] * pr - c_im[None] * pi
    n_im = c_re[None] * pi + c_im[None] * pr
    n = slabbed(jnp.stack([n_re, -n_im], axis=0), 2)
    cq = _compact(n.transpose(2, 0, 3, 5, 1, 4))
    cb_re = c_re[:, :, :, None] * bb_re[:, None, :, :] - c_im[:, :, :, None] * bb_im[:, None, :, :]
    cb_im = c_re[:, :, :, None] * bb_im[:, None, :, :] + c_im[:, :, :, None] * bb_re[:, None, :, :]
    kern = jnp.sum(cb_re[None] * pw_re[:q, :, None, :, None] - cb_im[None] * pw_im[:q, :, None, :, None],
                   axis=3)
    skip = d_skip.reshape(ng, SSM_GROUP_DIM)[:, :, None] * jnp.eye(SSM_GROUP_DIM, dtype=F32)[None]
    kern = kern + jnp.where(jnp.arange(q)[:, None, None, None] == 0, skip[None], 0.0)
    kst = jnp.stack([jnp.concatenate([jnp.zeros((s,) + kern.shape[1:], F32), kern[:q - s]], axis=0)
                     for s in range(q)], axis=0)
    kst = slabbed(kst, 2)
    kq = _compact(kst.transpose(2, 0, 3, 5, 1, 4))
    lam_q = jnp.concatenate([pw_re[q].reshape(sl, 1, -1), pw_im[q].reshape(sl, 1, -1)], axis=-1)
    return bq, cq, kq, lam_q


def _expand_block_diag(small, row_inner, col_inner):
    n_rows, n_small = small.shape
    n_cols = n_small * GROUPS_PER_SLAB
    g_bits = GROUPS_PER_SLAB.bit_length() - 1
    r_shift, c_shift = row_inner.bit_length() - 1, col_inner.bit_length() - 1
    k = lax.broadcasted_iota(I32, (n_small, n_cols), 0)
    col = lax.broadcasted_iota(I32, (n_small, n_cols), 1)
    src = ((col >> (c_shift + g_bits)) << c_shift) | (col & (col_inner - 1))
    spread = jnp.where(k == src, 1.0, 0.0).astype(BF16)
    r = lax.broadcasted_iota(I32, (n_rows, n_cols), 0)
    c = lax.broadcasted_iota(I32, (n_rows, n_cols), 1)
    same = ((r >> r_shift) & (GROUPS_PER_SLAB - 1)) == ((c >> c_shift) & (GROUPS_PER_SLAB - 1))
    return jnp.where(same, _dot(small, spread), 0.0).astype(BF16)


def _s5_kernel(u_ref, bq_ref, cq_ref, kq_ref, lam_ref, y_ref, g_scr, s_scr, t_scr, tbq, tcq, tkq, *, seqs):
    half = GROUPS_PER_SLAB * SSM_STATE
    rows = u_ref.shape[1]

    @pl.when(pl.program_id(1) == 0)
    def _():
        tbq[...] = _expand_block_diag(bq_ref[0], SSM_GROUP_DIM, SSM_STATE)
        tcq[...] = _expand_block_diag(cq_ref[0], SSM_STATE, SSM_GROUP_DIM)
        tkq[...] = _expand_block_diag(kq_ref[0], SSM_GROUP_DIM, SSM_GROUP_DIM)

    u = u_ref[0]
    g_scr[...] = _dot(u, tbq[...])
    lam = lam_ref[0]
    lre, lim = lam[:, :half], lam[:, half:]
    seq_rows = rows // seqs

    def body(n, carry):
        out = []
        for b in range(seqs):
            cre, cim = carry[2 * b], carry[2 * b + 1]
            row = b * seq_rows + n
            s_scr[pl.ds(row, 1), :] = jnp.concatenate([cre, cim], axis=1)
            g = g_scr[pl.ds(row, 1), :]
            out.append(lre * cre - lim * cim + g[:, :half])
            out.append(lre * cim + lim * cre + g[:, half:])
        return tuple(out)

    zero = jnp.zeros((1, half), F32)
    lax.fori_loop(0, seq_rows, body, (zero,) * (2 * seqs), unroll=2)
    y = _dot(s_scr[...].astype(BF16), tcq[...]) + _dot(u, tkq[...])
    for t in range(SSM_Q):
        t_scr[pl.ds(t, rows, stride=SSM_Q), :] = y[:, t * LANE:(t + 1) * LANE]
    y_ref[0] = t_scr[...].astype(BF16)


def s5_scan(u4, bq, cq, kq, lam_q, batch, seqs):
    sl, rows_total, width = u4.shape
    rows = rows_total // batch * seqs
    tab = pl.BlockSpec((1, width, width // GROUPS_PER_SLAB), lambda j, b: (j, 0, 0))
    return pl.pallas_call(
        functools.partial(_s5_kernel, seqs=seqs),
        grid=(sl, batch // seqs),
        in_specs=[pl.BlockSpec((1, rows, width), lambda j, b: (j, b, 0)), tab, tab, tab,
                  pl.BlockSpec((1, 1, width), lambda j, b: (j, 0, 0))],
        out_specs=pl.BlockSpec((1, rows * SSM_Q, LANE), lambda j, b: (j, b, 0)),
        out_shape=jax.ShapeDtypeStruct((sl, rows_total * SSM_Q, LANE), BF16),
        scratch_shapes=[pltpu.VMEM((rows, width), F32), pltpu.VMEM((rows, width), F32),
                        pltpu.VMEM((rows * SSM_Q, LANE), F32)] + [pltpu.VMEM((width, width), BF16)] * 3,
        compiler_params=_params(2),
        name="s5_scan",
    )(u4, bq, cq, kq, lam_q)


def _mixer_kernel(x_ref, ys_ref, ng_ref, win_ref, bgate_ref, poolw_ref, pscale_ref, lng_ref, lnb_ref,
                  sguw_ref, sgub_ref, gluw_ref, glub_ref, wbr_ref, wout_ref, o_ref, xe_scr):
    tm = x_ref.shape[0]
    width = SSM_SLABS * LANE
    a_end, b_end = width, 3 * width
    c_end = b_end + width
    i = pl.program_id(1)
    x = x_ref[...]
    hb = _rms(x, ng_ref[...]).astype(BF16)

    xa = _dot(hb, win_ref[:, 0:a_end])

    @pl.when(i == 0)
    def _():
        xe_scr[0:POOL_HALO, :] = jnp.zeros((POOL_HALO, xe_scr.shape[1]), F32)

    xe_scr[POOL_HALO:POOL_HALO + tm, :] = xa
    pos = i * tm + lax.broadcasted_iota(I32, (tm, 1), 0)
    ya_parts = []
    for gi, w in enumerate(POOL_WINDOWS):
        ch = slice(gi * LANE, (gi + 1) * LANE)
        win = xa[:, ch]
        for j in range(1, w):
            win = win + xe_scr[POOL_HALO - j:POOL_HALO - j + tm, ch]
        cnt = jnp.minimum(pos + 1, w).astype(F32)
        diff = win / cnt - xa[:, ch]
        ya_parts.append(_dot(diff.astype(BF16), poolw_ref[gi]) * pscale_ref[:, ch])
    ya = jnp.concatenate(ya_parts, axis=1)
    xe_scr[0:POOL_HALO, :] = xe_scr[tm:tm + POOL_HALO, :]

    z = _gelu(_dot(hb, win_ref[:, a_end:b_end]))
    u, v = z[:, :width], z[:, width:]
    mu = jnp.mean(v, axis=-1, keepdims=True)
    vc = v - mu
    var = jnp.mean(vc * vc, axis=-1, keepdims=True)
    vb = (vc * lax.rsqrt(var + EPS) * lng_ref[...] + lnb_ref[...]).astype(BF16)
    r_i = lax.broadcasted_iota(I32, (SGU_CHUNK, SGU_CHUNK), 0)
    c_i = lax.broadcasted_iota(I32, (SGU_CHUNK, SGU_CHUNK), 1)
    n_heads = width // LANE
    wt = [jnp.where(r_i >= c_i, sguw_ref[hd], 0.0).astype(BF16) for hd in range(n_heads)]
    s_rows = []
    for c in range(tm // SGU_CHUNK):
        rs = slice(c * SGU_CHUNK, (c + 1) * SGU_CHUNK)
        s_rows.append(jnp.concatenate(
            [_dot(wt[hd], vb[rs, hd * LANE:(hd + 1) * LANE]) for hd in range(n_heads)], axis=1)
            + sgub_ref[...])
    yb = u * jnp.concatenate(s_rows, axis=0)

    yc = _gelu(jnp.concatenate([ys_ref[j] for j in range(SSM_SLABS)], axis=1).astype(F32))
    yc = yc * _sigmoid(_dot(yc.astype(BF16), gluw_ref[...]) + glub_ref[...])

    merged = None
    for k, yk in enumerate((ya, yb, yc)):
        gate = _sigmoid(_dot(hb, win_ref[:, c_end + k * D_MODEL:c_end + (k + 1) * D_MODEL])
                        + bgate_ref[:, k * D_MODEL:(k + 1) * D_MODEL])
        term = gate * _dot(yk.astype(BF16), wbr_ref[k])
        merged = term if merged is None else merged + term
    o_ref[...] = x + _dot(merged.astype(BF16), wout_ref[...])


def mixer(x2d, ys4, p, batch, tm):
    t = x2d.shape[0]
    nt = t // batch // tm
    row = lambda b, i: (b * nt + i, 0)
    slab = lambda b, i: (0, b * nt + i, 0)
    consts = [p["norm_mix_g"], p["w_in"], p["b_gate"], p["pool_w"], p["pool_scale"], p["sgu_ln_g"],
              p["sgu_ln_b"], p["sgu_w"], p["sgu_b_full"], p["glu_w"], p["glu_b"], p["w_branch"], p["w_out"]]
    return pl.pallas_call(
        _mixer_kernel,
        grid=(batch, nt),
        in_specs=[pl.BlockSpec((tm, D_MODEL), row),
                  pl.BlockSpec((SSM_SLABS, tm, LANE), slab)] + [_const_spec(c.shape) for c in consts],
        out_specs=pl.BlockSpec((tm, D_MODEL), row),
        out_shape=jax.ShapeDtypeStruct(x2d.shape, F32),
        scratch_shapes=[pltpu.VMEM((tm + POOL_HALO, SSM_SLABS * LANE), F32)],
        compiler_params=_params(2),
        name="mixer",
    )(x2d, ys4, *consts)


def _router_kernel(x_ref, g_ref, wr_ref, br_ref, route_ref, meta_ref, carry_scr):
    tm = x_ref.shape[0]
    i = pl.program_id(0)

    @pl.when(i == 0)
    def _():
        carry_scr[...] = jnp.zeros_like(carry_scr)

    h = _rms(x_ref[...], g_ref[...])
    h_hi = h.astype(BF16)
    h_lo = (h - h_hi.astype(F32)).astype(BF16)
    logits = (_dot(h_hi, wr_ref[0]) + _dot(h_hi, wr_ref[1]) + _dot(h_lo, wr_ref[0])) + br_ref[...]
    lane = lax.broadcasted_iota(I32, logits.shape, 1)
    neg = -jnp.inf
    gl = jnp.where(lane < N_GROUPS, logits, neg)
    gmax = jnp.max(gl, axis=-1, keepdims=True)
    gsum = jnp.sum(jnp.where(lane < N_GROUPS, jnp.exp(logits - gmax), 0.0), axis=-1, keepdims=True)
    g_val = 1.0 / gsum
    g_idx = jnp.min(jnp.where(gl == gmax, lane, LANE), axis=-1, keepdims=True)
    lo = N_GROUPS + EXPERTS_PER_GROUP * g_idx
    el = jnp.where((lane >= lo) & (lane < lo + EXPERTS_PER_GROUP), logits, neg)
    e1v = jnp.max(el, axis=-1, keepdims=True)
    e1l = jnp.min(jnp.where(el == e1v, lane, LANE), axis=-1, keepdims=True)
    el2 = jnp.where(lane == e1l, neg, el)
    e2v = jnp.max(el2, axis=-1, keepdims=True)
    e2l = jnp.min(jnp.where(el2 == e2v, lane, LANE), axis=-1, keepdims=True)
    tt = jnp.exp(e2v - e1v)
    w1 = g_val / (1.0 + tt)
    w2 = g_val * tt / (1.0 + tt)
    hit1, hit2 = lane == e1l - N_GROUPS, lane == e2l - N_GROUPS
    onehot = jnp.where(hit1 | hit2, 1.0, 0.0)
    r_i = lax.broadcasted_iota(I32, (tm, tm), 0)
    c_i = lax.broadcasted_iota(I32, (tm, tm), 1)
    tri = jnp.where(r_i > c_i, 1.0, 0.0).astype(BF16)
    prefix = _dot(tri, onehot.astype(BF16))
    count = jnp.sum(onehot, axis=0, keepdims=True)
    groups8 = jnp.floor((count + (SUBLANE - 1.0)) * (1.0 / SUBLANE))
    count = groups8 * SUBLANE
    e_r = lax.broadcasted_iota(I32, (LANE, LANE), 0)
    e_c = lax.broadcasted_iota(I32, (LANE, LANE), 1)
    before = jnp.where(e_r < e_c, 1.0, 0.0).astype(BF16)
    run_start = _dot(jnp.broadcast_to(groups8, (SUBLANE, LANE)).astype(BF16), before)[0:1, :] * SUBLANE
    local = prefix + run_start
    lp1 = jnp.sum(jnp.where(hit1, local, 0.0), axis=-1, keepdims=True)
    lp2 = jnp.sum(jnp.where(hit2, local, 0.0), axis=-1, keepdims=True)
    route_ref[...] = jnp.where(lane == 0, w1, jnp.where(lane == 1, w2, jnp.where(
        lane == 2, lp1, jnp.where(lane == 3, lp2, 0.0))))
    earlier = carry_scr[0:1, :]
    total = earlier + count
    carry_scr[...] = jnp.broadcast_to(total, carry_scr.shape)
    sub = lax.broadcasted_iota(I32, meta_ref.shape, 0)
    meta_ref[...] = jnp.where(sub == 0, count, jnp.where(sub == 1, earlier, jnp.where(
        sub == 2, run_start, jnp.where(sub == 3, total, 0.0))))


def router(x2d, g, wr, br, tm):
    t = x2d.shape[0]
    row = lambda i: (i, 0)
    return pl.pallas_call(
        _router_kernel,
        grid=(t // tm,),
        in_specs=[pl.BlockSpec((tm, D_MODEL), row), _const_spec((1, D_MODEL)),
                  _const_spec((2, D_MODEL, LANE)), _const_spec((1, LANE))],
        out_specs=[pl.BlockSpec((tm, LANE), row), pl.BlockSpec((SUBLANE, LANE), row)],
        out_shape=[jax.ShapeDtypeStruct((t, LANE), F32),
                   jax.ShapeDtypeStruct((t // tm * SUBLANE, LANE), F32)],
        scratch_shapes=[pltpu.VMEM((SUBLANE, LANE), F32)],
        compiler_params=_params(1),
        name="router",
    )(x2d, g, wr, br)


def _sort_matrix(route, k, n_sorted):
    col = lax.broadcasted_iota(I32, (route.shape[0], n_sorted), 1)
    return jnp.where(col == route[:, 2 + k:3 + k].astype(I32), 1.0, 0.0).astype(BF16)


def _pack_rows(x):
    half = x.shape[1] // 2
    return pltpu.pack_elementwise([x[:, :half], x[:, half:]], packed_dtype=BF16)


def _unpack_rows(p):
    return jnp.concatenate([pltpu.unpack_elementwise(p, index=k, packed_dtype=BF16, unpacked_dtype=F32)
                            for k in range(2)], axis=1)


def _sorted_rows(tm):
    return -(-(2 * tm + N_EXPERTS * (SUBLANE - 1)) // LANE) * LANE


def _run_copies(cnt_ref, lo_ref, dst_ref, tile, make_copy, wait):
    def body(e, c):
        idx = tile * N_EXPERTS + e
        n = cnt_ref[idx]

        @pl.when(n > 0)
        def _():
            cp = make_copy(pl.multiple_of(lo_ref[idx], SUBLANE), pl.multiple_of(dst_ref[idx], SUBLANE),
                           pl.multiple_of(n, SUBLANE))
            if wait:
                cp.wait()
            else:
                cp.start()
        return c
    lax.fori_loop(0, N_EXPERTS, body, 0)


def _dispatch_kernel(cnt_ref, lo_ref, dst_ref, zst_ref, zln_ref, nused_ref, x_ref, g_ref, route_ref, o_ref,
                     srt_scr, zero_scr, sem, zsem):
    tm = x_ref.shape[0]
    bm = zero_scr.shape[0]
    i = pl.program_id(0)
    n = pl.num_programs(0)
    slot = i % 2

    def run_copy(s):
        return lambda lo, dst, cnt: pltpu.make_async_copy(
            srt_scr.at[s, pl.ds(lo, cnt)], o_ref.at[pl.ds(dst, cnt)], sem.at[s])

    def pad_copy(e):
        ln = pl.multiple_of(zln_ref[e], SUBLANE)
        return pltpu.make_async_copy(zero_scr.at[pl.ds(0, ln)],
                                     o_ref.at[pl.ds(pl.multiple_of(zst_ref[e], SUBLANE), ln)], zsem)

    def tail_copy(b):
        return pltpu.make_async_copy(zero_scr, o_ref.at[pl.ds(pl.multiple_of(b * bm, bm), bm)], zsem)

    @pl.when(i == 0)
    def _():
        zero_scr[...] = jnp.zeros_like(zero_scr)
        for wait in (False, True):
            def body(e, c):
                @pl.when(zln_ref[e] > 0)
                def _():
                    cp = pad_copy(e)
                    cp.wait() if wait else cp.start()
                return c
            lax.fori_loop(0, N_EXPERTS, body, 0)

            def tail(b, c):
                cp = tail_copy(b)
                cp.wait() if wait else cp.start()
                return c
            lax.fori_loop(nused_ref[0], o_ref.shape[0] // bm, tail, 0)

    @pl.when(i >= 2)
    def _():
        _run_copies(cnt_ref, lo_ref, dst_ref, i - 2, run_copy(slot), True)

    hb = _rms(x_ref[...], g_ref[...]).astype(BF16)
    route = route_ref[...]
    ns = srt_scr.shape[1]
    pt = jnp.maximum(_sort_matrix(route, 0, ns), _sort_matrix(route, 1, ns))
    srt_scr[slot] = _pack_rows(lax.dot_general(pt, hb, (((0,), (0,)), ((), ())), preferred_element_type=F32))
    _run_copies(cnt_ref, lo_ref, dst_ref, i, run_copy(slot), False)

    @pl.when(i == n - 1)
    def _():
        @pl.when(i >= 1)
        def _():
            _run_copies(cnt_ref, lo_ref, dst_ref, i - 1, run_copy(1 - slot), True)
        _run_copies(cnt_ref, lo_ref, dst_ref, i, run_copy(slot), True)


def dispatch(cnt, lo, dst, zst, zln, nused, x2d, g, route, n_rows, tm, bm):
    t, d = x2d.shape
    row = lambda i, *_: (i, 0)
    return pl.pallas_call(
        _dispatch_kernel,
        grid_spec=pltpu.PrefetchScalarGridSpec(
            num_scalar_prefetch=6,
            grid=(t // tm,),
            in_specs=[pl.BlockSpec((tm, d), row), pl.BlockSpec((1, d), lambda i, *_: (0, 0)),
                      pl.BlockSpec((tm, LANE), row)],
            out_specs=pl.BlockSpec(memory_space=pl.ANY),
            scratch_shapes=[pltpu.VMEM((2, _sorted_rows(tm), d // 2), U32), pltpu.VMEM((bm, d // 2), U32),
                            pltpu.SemaphoreType.DMA((2,)), pltpu.SemaphoreType.DMA(())]),
        out_shape=jax.ShapeDtypeStruct((n_rows, d // 2), U32),
        compiler_params=_params(1),
        name="dispatch",
    )(cnt, lo, dst, zst, zln, nused, x2d, g, route)


def _expert_kernel(bexp_ref, nused_ref, xs_ref, wg_ref, wu_ref, wd_ref, ys_ref, wg_scr, wu_scr, wd_scr):
    i = pl.program_id(0)

    @pl.when(i < nused_ref[0])
    def _():
        changed = jnp.logical_or(i == 0, bexp_ref[i] != bexp_ref[jnp.maximum(i - 1, 0)])

        @pl.when(changed)
        def _():
            wg_scr[...] = wg_ref[...].astype(BF16)
            wu_scr[...] = wu_ref[...].astype(BF16)
            wd_scr[...] = wd_ref[...].astype(BF16)

        xb = _unpack_rows(xs_ref[...]).astype(BF16)
        hg = _dot(xb, wg_scr[...])
        hu = _dot(xb, wu_scr[...])
        act = (hg * _sigmoid(hg) * hu).astype(BF16)
        ys_ref[...] = _pack_rows(_dot(act, wd_scr[...]))

    @pl.when(i >= nused_ref[0])
    def _():
        ys_ref[...] = jnp.zeros_like(ys_ref)


def experts(bexp, nused, xs, w_gate, w_up, w_down, layer, bm):
    n_rows, dp = xs.shape
    d, de = w_gate.shape[2], w_gate.shape[3]
    wsel = lambda i, bexp, nused: (layer, bexp[i], 0, 0)
    return pl.pallas_call(
        _expert_kernel,
        grid_spec=pltpu.PrefetchScalarGridSpec(
            num_scalar_prefetch=2,
            grid=(n_rows // bm,),
            in_specs=[pl.BlockSpec((bm, dp), lambda i, bexp, nused: (jnp.minimum(i, nused[0] - 1), 0)),
                      pl.BlockSpec((None, None, d, de), wsel),
                      pl.BlockSpec((None, None, d, de), wsel), pl.BlockSpec((None, None, de, d), wsel)],
            out_specs=pl.BlockSpec((bm, dp), lambda i, bexp, nused: (i, 0)),
            scratch_shapes=[pltpu.VMEM((d, de), BF16), pltpu.VMEM((d, de), BF16),
                            pltpu.VMEM((de, d), BF16)]),
        out_shape=jax.ShapeDtypeStruct(xs.shape, U32),
        compiler_params=_params(1),
        name="experts",
    )(bexp, nused, xs, w_gate, w_up, w_down)


def _combine_kernel(cnt_ref, lo_ref, dst_ref, x_ref, route_ref, g_ref, ys_ref, o_ref, buf, sem, *, final):
    tm = x_ref.shape[0]
    i = pl.program_id(0)
    n = pl.num_programs(0)
    slot = i % 2

    def run_copy(s):
        return lambda lo, dst, cnt: pltpu.make_async_copy(
            ys_ref.at[pl.ds(dst, cnt)], buf.at[s, pl.ds(lo, cnt)], sem.at[s])

    @pl.when(i == 0)
    def _():
        buf[...] = jnp.zeros_like(buf)
        _run_copies(cnt_ref, lo_ref, dst_ref, 0, run_copy(0), False)

    @pl.when(i + 1 < n)
    def _():
        _run_copies(cnt_ref, lo_ref, dst_ref, i + 1, run_copy(1 - slot), False)

    _run_copies(cnt_ref, lo_ref, dst_ref, i, run_copy(slot), True)
    ysb = _unpack_rows(buf[slot]).astype(BF16)
    route = route_ref[...]
    out = x_ref[...]
    for k in range(2):
        out = out + route[:, k:k + 1] * _dot(_sort_matrix(route, k, buf.shape[1]), ysb)
    if final:
        out = _rms(out, g_ref[...])
    o_ref[...] = out


def combine(cnt, lo, dst, x2d, route, g, ys, tm, final):
    t, d = x2d.shape
    row = lambda i, *_: (i, 0)
    return pl.pallas_call(
        functools.partial(_combine_kernel, final=final),
        grid_spec=pltpu.PrefetchScalarGridSpec(
            num_scalar_prefetch=3,
            grid=(t // tm,),
            in_specs=[pl.BlockSpec((tm, d), row), pl.BlockSpec((tm, LANE), row),
                      pl.BlockSpec((1, d), lambda i, *_: (0, 0)),
                      pl.BlockSpec(memory_space=pl.ANY)],
            out_specs=pl.BlockSpec((tm, d), row),
            scratch_shapes=[pltpu.VMEM((2, _sorted_rows(tm), d // 2), U32), pltpu.SemaphoreType.DMA((2,))]),
        out_shape=jax.ShapeDtypeStruct((t, d), F32),
        compiler_params=_params(1),
        name="combine",
    )(cnt, lo, dst, x2d, route, g, ys)


def moe(x2d, g, rg_w, rg_b, re_w, re_b, w_gate, w_up, w_down, layer, g_final, final, tm, bm):
    t = x2d.shape[0]
    nt = t // tm
    pad = LANE - N_GROUPS - N_EXPERTS
    wr = jnp.concatenate([rg_w, re_w, jnp.zeros((D_MODEL, pad), F32)], axis=1)
    wr_hi = wr.astype(BF16)
    wr = jnp.stack([wr_hi, (wr - wr_hi.astype(F32)).astype(BF16)])
    br = jnp.concatenate([rg_b, re_b, jnp.zeros((pad,), F32)])[None, :]
    g2 = g[None, :]
    route, meta = router(x2d, g2, wr, br, tm)
    meta = meta.reshape(nt, SUBLANE, LANE)[:, :, :N_EXPERTS].astype(I32)
    cnt, earlier, lo = meta[:, 0], meta[:, 1], meta[:, 2]
    counts = meta[nt - 1, 3]
    padded = (counts + bm - 1) // bm * bm
    ends = jnp.cumsum(padded)
    offs = ends - padded
    dst = offs[None, :] + earlier
    n_rows = -(-(2 * t + nt * N_EXPERTS * (SUBLANE - 1)) // bm) * bm + N_EXPERTS * bm
    nused = (ends[-1] // bm).astype(I32)
    blk_start = jnp.arange(n_rows // bm, dtype=I32) * bm
    bexp = jnp.sum((ends[None, :] <= blk_start[:, None]).astype(I32), axis=1)
    last = jnp.sum((ends <= (nused - 1) * bm).astype(I32))
    bexp = jnp.minimum(bexp, last)
    cnt, lo, dst = cnt.reshape(-1), lo.reshape(-1), dst.reshape(-1)
    xs = dispatch(cnt, lo, dst, offs + counts, padded - counts, nused[None], x2d, g2, route, n_rows, tm, bm)
    ys = experts(bexp, nused[None], xs, w_gate, w_up, w_down, layer, bm)
    return combine(cnt, lo, dst, x2d, route, g_final[None, :], ys, tm, final)


def kernel(x, norm_mix_g, w_in, b_gate, pool_w, pool_scale, sgu_ln_g, sgu_ln_b, sgu_w, sgu_b, ssm_a_re, ssm_a_im, ssm_log_dt, ssm_b_re, ssm_b_im, ssm_c_re, ssm_c_im, ssm_d, glu_w, glu_b, w_branch, w_out, norm_ffn_g, router_group_w, router_group_b, router_expert_w, router_expert_b, exp_w_gate, exp_w_up, exp_w_down, norm_final_g):
    batch, seq, d = x.shape
    depth = w_in.shape[0]
    tm_in, tm_mix, tm_moe, bm = 1024, 512, 256, 512
    s5_seqs = 2
    x2d = x.reshape(batch * seq, d)
    ssm_col_block = 3
    for l in range(depth):
        wl = w_in[l].astype(BF16)
        g_mix = norm_mix_g[l][None, :]
        xc4 = ssm_input(x2d, g_mix, wl, ssm_col_block, tm_in)
        bq, cq, kq, lam_q = ssm_tables(ssm_a_re[l], ssm_a_im[l], ssm_log_dt[l], ssm_b_re[l], ssm_b_im[l],
                                       ssm_c_re[l], ssm_c_im[l], ssm_d[l])
        ys4 = s5_scan(xc4, bq, cq, kq, lam_q, batch, s5_seqs)
        p = {
            "norm_mix_g": g_mix, "w_in": wl, "b_gate": b_gate[l][None, :],
            "pool_w": pool_w[l].astype(BF16), "pool_scale": pool_scale[l][None, :],
            "sgu_ln_g": sgu_ln_g[l][None, :], "sgu_ln_b": sgu_ln_b[l][None, :],
            "sgu_w": sgu_w[l],
            "sgu_b_full": jnp.repeat(sgu_b[l].T, LANE, axis=1),
            "glu_w": glu_w[l].astype(BF16), "glu_b": glu_b[l][None, :],
            "w_branch": w_branch[l].astype(BF16), "w_out": w_out[l].astype(BF16),
        }
        x2d = mixer(x2d, ys4, p, batch, tm_mix)
        x2d = moe(x2d, norm_ffn_g[l], router_group_w[l], router_group_b[l], router_expert_w[l],
                  router_expert_b[l], exp_w_gate, exp_w_up, exp_w_down, l, norm_final_g,
                  l == depth - 1, tm_moe, bm)
    return x2d.reshape(batch, seq, d)
```

```python
import functools
import math

import jax
import jax.numpy as jnp
from jax import lax
from jax.experimental import pallas as pl
from jax.experimental.pallas import tpu as pltpu

F32 = jnp.float32
BF16 = jnp.bfloat16
I32 = jnp.int32

D_MODEL = 1024
POOL_WINDOWS = (2, 4, 8, 16)
POOL_HALO = 16
LANE = 128
SUBLANE = 8
RUN_ALIGN = 16
SGU_CHUNK = 128
SSM_GROUP_DIM = 16
SSM_STATE = 64
SSM_Q = 8
SSM_SLABS = 4
GROUPS_PER_SLAB = LANE // SSM_GROUP_DIM
N_GROUPS = 4
EXPERTS_PER_GROUP = 8
N_EXPERTS = 32
EPS = 1e-6
VMEM_LIMIT = 56 * 1024 * 1024


def _rms(x, g):
    return x * lax.rsqrt(jnp.mean(x * x, axis=-1, keepdims=True) + EPS) * g


def _gelu(x):
    c = math.sqrt(2.0 / math.pi)
    return x * (0.5 * (1.0 + jnp.tanh(c * (x + 0.044715 * (x * x * x)))))


def _sigmoid(x):
    return 1.0 / (1.0 + jnp.exp(-x))


def _dot(a, b):
    return jnp.dot(a, b, preferred_element_type=F32)


def _const_spec(shape):
    nd = len(shape)
    return pl.BlockSpec(shape, lambda *_: (0,) * nd, pipeline_mode=pl.Buffered(1))


def _params(n_axes):
    return pltpu.CompilerParams(dimension_semantics=("arbitrary",) * n_axes, vmem_limit_bytes=VMEM_LIMIT)


def _xc_kernel(x_ref, g_ref, w_ref, o_ref, xc_scr):
    h = _rms(x_ref[...], g_ref[...]).astype(BF16)
    xc = _dot(h, w_ref[...])
    rows = xc_scr.shape[1] // SSM_Q
    for j in range(SSM_SLABS):
        xc_scr[j] = xc[:, j * LANE:(j + 1) * LANE]
        for s in range(SSM_Q):
            o_ref[j, :, s * LANE:(s + 1) * LANE] = xc_scr[j, pl.ds(s, rows, stride=SSM_Q), :].astype(BF16)


def ssm_input(x2d, g, w_in_bf, col_block, tm):
    t = x2d.shape[0]
    width = SSM_SLABS * LANE
    return pl.pallas_call(
        _xc_kernel,
        grid=(t // tm,),
        in_specs=[pl.BlockSpec((tm, D_MODEL), lambda i: (i, 0)),
                  _const_spec((1, D_MODEL)),
                  pl.BlockSpec((D_MODEL, width), lambda i: (0, col_block), pipeline_mode=pl.Buffered(1))],
        out_specs=pl.BlockSpec((SSM_SLABS, tm // SSM_Q, SSM_Q * LANE), lambda i: (0, i, 0)),
        out_shape=jax.ShapeDtypeStruct((SSM_SLABS, t // SSM_Q, SSM_Q * LANE), BF16),
        scratch_shapes=[pltpu.VMEM((SSM_SLABS, tm, LANE), F32)],
        compiler_params=_params(1),
        name="ssm_input",
    )(x2d, g, w_in_bf)


def _compact(small):
    sl, r1, a, r2, c1, c2 = small.shape
    return small.reshape(sl, r1 * a * r2, c1 * c2).astype(BF16)


def ssm_tables(a_re, a_im, log_dt, b_re, b_im, c_re, c_im, d_skip):
    q = SSM_Q
    ng = a_re.shape[0]
    sl = ng // GROUPS_PER_SLAB
    dt = jnp.exp(log_dt)[:, None]
    k = jnp.arange(q + 1, dtype=F32)[:, None, None]
    mag = jnp.exp(a_re * dt * k)
    ang = a_im * dt * k
    pw_re, pw_im = mag * jnp.cos(ang), mag * jnp.sin(ang)
    nr, ni = pw_re[1] - 1.0, pw_im[1]
    den = a_re * a_re + a_im * a_im
    fr, fi = (nr * a_re + ni * a_im) / den, (ni * a_re - nr * a_im) / den
    bb_re = fr[..., None] * b_re - fi[..., None] * b_im
    bb_im = fr[..., None] * b_im + fi[..., None] * b_re

    def slabbed(a, axis):
        return a.reshape(*a.shape[:axis], sl, GROUPS_PER_SLAB, *a.shape[axis + 1:])

    k_rev = (q - 1.0) - jnp.arange(q, dtype=F32)[:, None, None]
    mag_rev = jnp.exp(a_re * dt * k_rev)
    rev_re, rev_im = mag_rev * jnp.cos(a_im * dt * k_rev), mag_rev * jnp.sin(a_im * dt * k_rev)
    m_re = rev_re[..., None] * bb_re[None] - rev_im[..., None] * bb_im[None]
    m_im = rev_re[..., None] * bb_im[None] + rev_im[..., None] * bb_re[None]
    m = slabbed(jnp.stack([m_re, m_im], axis=0), 2)
    bq = _compact(m.transpose(2, 1, 3, 5, 0, 4))
    pr, pi = pw_re[1:, :, None, :], pw_im[1:, :, None, :]
    n_re = c_re[None] * pr - c_im[None] * pi
    n_im = c_re[None] * pi + c_im[None] * pr
    n = slabbed(jnp.stack([n_re, -n_im], axis=0), 2)
    cq = _compact(n.transpose(2, 0, 3, 5, 1, 4))
    cb_re = c_re[:, :, :, None] * bb_re[:, None, :, :] - c_im[:, :, :, None] * bb_im[:, None, :, :]
    cb_im = c_re[:, :, :, None] * bb_im[:, None, :, :] + c_im[:, :, :, None] * bb_re[:, None, :, :]
    kern = jnp.sum(cb_re[None] * pw_re[:q, :, None, :, None] - cb_im[None] * pw_im[:q, :, None, :, None],
                   axis=3)
    skip = d_skip.reshape(ng, SSM_GROUP_DIM)[:, :, None] * jnp.eye(SSM_GROUP_DIM, dtype=F32)[None]
    kern = kern + jnp.where(jnp.arange(q)[:, None, None, None] == 0, skip[None], 0.0)
    kst = jnp.stack([jnp.concatenate([jnp.zeros((s,) + kern.shape[1:], F32), kern[:q - s]], axis=0)
                     for s in range(q)], axis=0)
    kst = slabbed(kst, 2)
    kq = _compact(kst.transpose(2, 0, 3, 5, 1, 4))
    lam_q = jnp.concatenate([pw_re[q].reshape(sl, 1, -1), pw_im[q].reshape(sl, 1, -1)], axis=-1)
    return bq, cq, kq, lam_q


def _expand_block_diag(small, row_inner, col_inner):
    n_rows, n_small = small.shape
    n_cols = n_small * GROUPS_PER_SLAB
    g_bits = GROUPS_PER_SLAB.bit_length() - 1
    r_shift, c_shift = row_inner.bit_length() - 1, col_inner.bit_length() - 1
    k = lax.broadcasted_iota(I32, (n_small, n_cols), 0)
    col = lax.broadcasted_iota(I32, (n_small, n_cols), 1)
    src = ((col >> (c_shift + g_bits)) << c_shift) | (col & (col_inner - 1))
    spread = jnp.where(k == src, 1.0, 0.0).astype(BF16)
    r = lax.broadcasted_iota(I32, (n_rows, n_cols), 0)
    c = lax.broadcasted_iota(I32, (n_rows, n_cols), 1)
    same = ((r >> r_shift) & (GROUPS_PER_SLAB - 1)) == ((c >> c_shift) & (GROUPS_PER_SLAB - 1))
    return jnp.where(same, _dot(small, spread), 0.0).astype(BF16)


def _s5_kernel(u_ref, bq_ref, cq_ref, kq_ref, lam_ref, y_ref, g_scr, s_scr, t_scr, tbq, tcq, tkq, *, seqs):
    half = GROUPS_PER_SLAB * SSM_STATE
    rows = u_ref.shape[1]

    @pl.when(pl.program_id(1) == 0)
    def _():
        tbq[...] = _expand_block_diag(bq_ref[0], SSM_GROUP_DIM, SSM_STATE)
        tcq[...] = _expand_block_diag(cq_ref[0], SSM_STATE, SSM_GROUP_DIM)
        tkq[...] = _expand_block_diag(kq_ref[0], SSM_GROUP_DIM, SSM_GROUP_DIM)

    u = u_ref[0]
    g_scr[...] = _dot(u, tbq[...])
    lam = lam_ref[0]
    lre, lim = lam[:, :half], lam[:, half:]
    seq_rows = rows // seqs

    def body(n, carry):
        out = []
        for b in range(seqs):
            cre, cim = carry[2 * b], carry[2 * b + 1]
            row = b * seq_rows + n
            s_scr[pl.ds(row, 1), :] = jnp.concatenate([cre, cim], axis=1)
            g = g_scr[pl.ds(row, 1), :]
            out.append(lre * cre - lim * cim + g[:, :half])
            out.append(lre * cim + lim * cre + g[:, half:])
        return tuple(out)

    zero = jnp.zeros((1, half), F32)
    lax.fori_loop(0, seq_rows, body, (zero,) * (2 * seqs), unroll=2)
    y = _dot(s_scr[...].astype(BF16), tcq[...]) + _dot(u, tkq[...])
    for t in range(SSM_Q):
        t_scr[pl.ds(t, rows, stride=SSM_Q), :] = y[:, t * LANE:(t + 1) * LANE]
    y_ref[0] = t_scr[...].astype(BF16)


def s5_scan(u4, bq, cq, kq, lam_q, batch, seqs):
    sl, rows_total, width = u4.shape
    rows = rows_total // batch * seqs
    tab = pl.BlockSpec((1, width, width // GROUPS_PER_SLAB), lambda j, b: (j, 0, 0))
    return pl.pallas_call(
        functools.partial(_s5_kernel, seqs=seqs),
        grid=(sl, batch // seqs),
        in_specs=[pl.BlockSpec((1, rows, width), lambda j, b: (j, b, 0)), tab, tab, tab,
                  pl.BlockSpec((1, 1, width), lambda j, b: (j, 0, 0))],
        out_specs=pl.BlockSpec((1, rows * SSM_Q, LANE), lambda j, b: (j, b, 0)),
        out_shape=jax.ShapeDtypeStruct((sl, rows_total * SSM_Q, LANE), BF16),
        scratch_shapes=[pltpu.VMEM((rows, width), F32), pltpu.VMEM((rows, width), F32),
                        pltpu.VMEM((rows * SSM_Q, LANE), F32)] + [pltpu.VMEM((width, width), BF16)] * 3,
        compiler_params=_params(2),
        name="s5_scan",
    )(u4, bq, cq, kq, lam_q)


def _mixer_kernel(x_ref, ys_ref, ng_ref, win_ref, bgate_ref, poolw_ref, pscale_ref, lng_ref, lnb_ref,
                  sguw_ref, sgub_ref, gluw_ref, glub_ref, wbr_ref, wout_ref, o_ref, xe_scr):
    tm = x_ref.shape[0]
    width = SSM_SLABS * LANE
    a_end, b_end = width, 3 * width
    c_end = b_end + width
    i = pl.program_id(1)
    x = x_ref[...]
    hb = _rms(x, ng_ref[...]).astype(BF16)

    xa = _dot(hb, win_ref[:, 0:a_end])

    @pl.when(i == 0)
    def _():
        xe_scr[0:POOL_HALO, :] = jnp.zeros((POOL_HALO, xe_scr.shape[1]), F32)

    xe_scr[POOL_HALO:POOL_HALO + tm, :] = xa
    pos = i * tm + lax.broadcasted_iota(I32, (tm, 1), 0)
    ya_parts = []
    for gi, w in enumerate(POOL_WINDOWS):
        ch = slice(gi * LANE, (gi + 1) * LANE)
        win = xa[:, ch]
        for j in range(1, w):
            win = win + xe_scr[POOL_HALO - j:POOL_HALO - j + tm, ch]
        cnt = jnp.minimum(pos + 1, w).astype(F32)
        diff = win / cnt - xa[:, ch]
        ya_parts.append(_dot(diff.astype(BF16), poolw_ref[gi]) * pscale_ref[:, ch])
    ya = jnp.concatenate(ya_parts, axis=1)
    xe_scr[0:POOL_HALO, :] = xe_scr[tm:tm + POOL_HALO, :]

    z = _gelu(_dot(hb, win_ref[:, a_end:b_end]))
    u, v = z[:, :width], z[:, width:]
    mu = jnp.mean(v, axis=-1, keepdims=True)
    vc = v - mu
    var = jnp.mean(vc * vc, axis=-1, keepdims=True)
    vb = (vc * lax.rsqrt(var + EPS) * lng_ref[...] + lnb_ref[...]).astype(BF16)
    r_i = lax.broadcasted_iota(I32, (SGU_CHUNK, SGU_CHUNK), 0)
    c_i = lax.broadcasted_iota(I32, (SGU_CHUNK, SGU_CHUNK), 1)
    n_heads = width // LANE
    wt = [jnp.where(r_i >= c_i, sguw_ref[hd], 0.0).astype(BF16) for hd in range(n_heads)]
    s_rows = []
    for c in range(tm // SGU_CHUNK):
        rs = slice(c * SGU_CHUNK, (c + 1) * SGU_CHUNK)
        s_rows.append(jnp.concatenate(
            [_dot(wt[hd], vb[rs, hd * LANE:(hd + 1) * LANE]) for hd in range(n_heads)], axis=1)
            + sgub_ref[...])
    yb = u * jnp.concatenate(s_rows, axis=0)

    yc = _gelu(jnp.concatenate([ys_ref[j] for j in range(SSM_SLABS)], axis=1).astype(F32))
    yc = yc * _sigmoid(_dot(yc.astype(BF16), gluw_ref[...]) + glub_ref[...])

    merged = None
    for k, yk in enumerate((ya, yb, yc)):
        gate = _sigmoid(_dot(hb, win_ref[:, c_end + k * D_MODEL:c_end + (k + 1) * D_MODEL])
                        + bgate_ref[:, k * D_MODEL:(k + 1) * D_MODEL])
        term = gate * _dot(yk.astype(BF16), wbr_ref[k])
        merged = term if merged is None else merged + term
    o_ref[...] = x + _dot(merged.astype(BF16), wout_ref[...])


def mixer(x2d, ys4, p, batch, tm):
    t = x2d.shape[0]
    nt = t // batch // tm
    row = lambda b, i: (b * nt + i, 0)
    slab = lambda b, i: (0, b * nt + i, 0)
    consts = [p["norm_mix_g"], p["w_in"], p["b_gate"], p["pool_w"], p["pool_scale"], p["sgu_ln_g"],
              p["sgu_ln_b"], p["sgu_w"], p["sgu_b_full"], p["glu_w"], p["glu_b"], p["w_branch"], p["w_out"]]
    return pl.pallas_call(
        _mixer_kernel,
        grid=(batch, nt),
        in_specs=[pl.BlockSpec((tm, D_MODEL), row),
                  pl.BlockSpec((SSM_SLABS, tm, LANE), slab)] + [_const_spec(c.shape) for c in consts],
        out_specs=pl.BlockSpec((tm, D_MODEL), row),
        out_shape=jax.ShapeDtypeStruct(x2d.shape, F32),
        scratch_shapes=[pltpu.VMEM((tm + POOL_HALO, SSM_SLABS * LANE), F32)],
        compiler_params=_params(2),
        name="mixer",
    )(x2d, ys4, *consts)


def _router_kernel(x_ref, g_ref, wr_ref, br_ref, route_ref, meta_ref, carry_scr):
    tm = x_ref.shape[0]
    i = pl.program_id(0)

    @pl.when(i == 0)
    def _():
        carry_scr[...] = jnp.zeros_like(carry_scr)

    h = _rms(x_ref[...], g_ref[...])
    h_hi = h.astype(BF16)
    h_lo = (h - h_hi.astype(F32)).astype(BF16)
    logits = (_dot(h_hi, wr_ref[0]) + _dot(h_hi, wr_ref[1]) + _dot(h_lo, wr_ref[0])) + br_ref[...]
    lane = lax.broadcasted_iota(I32, logits.shape, 1)
    neg = -jnp.inf
    gl = jnp.where(lane < N_GROUPS, logits, neg)
    gmax = jnp.max(gl, axis=-1, keepdims=True)
    gsum = jnp.sum(jnp.where(lane < N_GROUPS, jnp.exp(logits - gmax), 0.0), axis=-1, keepdims=True)
    g_val = 1.0 / gsum
    g_idx = jnp.min(jnp.where(gl == gmax, lane, LANE), axis=-1, keepdims=True)
    lo = N_GROUPS + EXPERTS_PER_GROUP * g_idx
    el = jnp.where((lane >= lo) & (lane < lo + EXPERTS_PER_GROUP), logits, neg)
    e1v = jnp.max(el, axis=-1, keepdims=True)
    e1l = jnp.min(jnp.where(el == e1v, lane, LANE), axis=-1, keepdims=True)
    el2 = jnp.where(lane == e1l, neg, el)
    e2v = jnp.max(el2, axis=-1, keepdims=True)
    e2l = jnp.min(jnp.where(el2 == e2v, lane, LANE), axis=-1, keepdims=True)
    tt = jnp.exp(e2v - e1v)
    w1 = g_val / (1.0 + tt)
    w2 = g_val * tt / (1.0 + tt)
    hit1, hit2 = lane == e1l - N_GROUPS, lane == e2l - N_GROUPS
    onehot = jnp.where(hit1 | hit2, 1.0, 0.0)
    r_i = lax.broadcasted_iota(I32, (tm, tm), 0)
    c_i = lax.broadcasted_iota(I32, (tm, tm), 1)
    tri = jnp.where(r_i > c_i, 1.0, 0.0).astype(BF16)
    prefix = _dot(tri, onehot.astype(BF16))
    count = jnp.sum(onehot, axis=0, keepdims=True)
    groups8 = jnp.floor((count + (RUN_ALIGN - 1.0)) * (1.0 / RUN_ALIGN))
    count = groups8 * RUN_ALIGN
    e_r = lax.broadcasted_iota(I32, (LANE, LANE), 0)
    e_c = lax.broadcasted_iota(I32, (LANE, LANE), 1)
    before = jnp.where(e_r < e_c, 1.0, 0.0).astype(BF16)
    run_start = _dot(jnp.broadcast_to(groups8, (SUBLANE, LANE)).astype(BF16), before)[0:1, :] * RUN_ALIGN
    local = prefix + run_start
    lp1 = jnp.sum(jnp.where(hit1, local, 0.0), axis=-1, keepdims=True)
    lp2 = jnp.sum(jnp.where(hit2, local, 0.0), axis=-1, keepdims=True)
    route_ref[...] = jnp.where(lane == 0, w1, jnp.where(lane == 1, w2, jnp.where(
        lane == 2, lp1, jnp.where(lane == 3, lp2, 0.0))))
    earlier = carry_scr[0:1, :]
    total = earlier + count
    carry_scr[...] = jnp.broadcast_to(total, carry_scr.shape)
    sub = lax.broadcasted_iota(I32, meta_ref.shape, 0)
    meta_ref[...] = jnp.where(sub == 0, count, jnp.where(sub == 1, earlier, jnp.where(
        sub == 2, run_start, jnp.where(sub == 3, total, 0.0))))


def router(x2d, g, wr, br, tm):
    t = x2d.shape[0]
    row = lambda i: (i, 0)
    return pl.pallas_call(
        _router_kernel,
        grid=(t // tm,),
        in_specs=[pl.BlockSpec((tm, D_MODEL), row), _const_spec((1, D_MODEL)),
                  _const_spec((2, D_MODEL, LANE)), _const_spec((1, LANE))],
        out_specs=[pl.BlockSpec((tm, LANE), row), pl.BlockSpec((SUBLANE, LANE), row)],
        out_shape=[jax.ShapeDtypeStruct((t, LANE), F32),
                   jax.ShapeDtypeStruct((t // tm * SUBLANE, LANE), F32)],
        scratch_shapes=[pltpu.VMEM((SUBLANE, LANE), F32)],
        compiler_params=_params(1),
        name="router",
    )(x2d, g, wr, br)


def _sort_matrix(route, k, n_sorted):
    col = lax.broadcasted_iota(I32, (route.shape[0], n_sorted), 1)
    return jnp.where(col == route[:, 2 + k:3 + k].astype(I32), 1.0, 0.0).astype(BF16)


def _sorted_rows(tm):
    return -(-(2 * tm + N_EXPERTS * (RUN_ALIGN - 1)) // LANE) * LANE


def _run_copies(cnt_ref, lo_ref, dst_ref, tile, make_copy, wait):
    def body(e, c):
        idx = tile * N_EXPERTS + e
        n = cnt_ref[idx]

        @pl.when(n > 0)
        def _():
            cp = make_copy(pl.multiple_of(lo_ref[idx], RUN_ALIGN), pl.multiple_of(dst_ref[idx], RUN_ALIGN),
                           pl.multiple_of(n, RUN_ALIGN))
            if wait:
                cp.wait()
            else:
                cp.start()
        return c
    lax.fori_loop(0, N_EXPERTS, body, 0)


def _dispatch_kernel(cnt_ref, lo_ref, dst_ref, zst_ref, zln_ref, nused_ref, x_ref, g_ref, route_ref, o_ref,
                     srt_scr, zero_scr, sem, zsem):
    tm = x_ref.shape[0]
    bm = zero_scr.shape[0]
    i = pl.program_id(0)
    n = pl.num_programs(0)
    slot = i % 2

    def run_copy(s):
        return lambda lo, dst, cnt: pltpu.make_async_copy(
            srt_scr.at[s, pl.ds(lo, cnt)], o_ref.at[pl.ds(dst, cnt)], sem.at[s])

    def pad_copy(e):
        ln = pl.multiple_of(zln_ref[e], RUN_ALIGN)
        return pltpu.make_async_copy(zero_scr.at[pl.ds(0, ln)],
                                     o_ref.at[pl.ds(pl.multiple_of(zst_ref[e], RUN_ALIGN), ln)], zsem)

    def tail_copy(b):
        return pltpu.make_async_copy(zero_scr, o_ref.at[pl.ds(pl.multiple_of(b * bm, bm), bm)], zsem)

    @pl.when(i == 0)
    def _():
        zero_scr[...] = jnp.zeros_like(zero_scr)
        for wait in (False, True):
            def body(e, c):
                @pl.when(zln_ref[e] > 0)
                def _():
                    cp = pad_copy(e)
                    cp.wait() if wait else cp.start()
                return c
            lax.fori_loop(0, N_EXPERTS, body, 0)

            def tail(b, c):
                cp = tail_copy(b)
                cp.wait() if wait else cp.start()
                return c
            lax.fori_loop(nused_ref[0], o_ref.shape[0] // bm, tail, 0)

    @pl.when(i >= 2)
    def _():
        _run_copies(cnt_ref, lo_ref, dst_ref, i - 2, run_copy(slot), True)

    hb = _rms(x_ref[...], g_ref[...]).astype(BF16)
    route = route_ref[...]
    ns = srt_scr.shape[1]
    pt = jnp.maximum(_sort_matrix(route, 0, ns), _sort_matrix(route, 1, ns))
    srt_scr[slot] = lax.dot_general(pt, hb, (((0,), (0,)), ((), ())),
                                    preferred_element_type=F32).astype(BF16)
    _run_copies(cnt_ref, lo_ref, dst_ref, i, run_copy(slot), False)

    @pl.when(i == n - 1)
    def _():
        @pl.when(i >= 1)
        def _():
            _run_copies(cnt_ref, lo_ref, dst_ref, i - 1, run_copy(1 - slot), True)
        _run_copies(cnt_ref, lo_ref, dst_ref, i, run_copy(slot), True)


def dispatch(cnt, lo, dst, zst, zln, nused, x2d, g, route, n_rows, tm, bm):
    t, d = x2d.shape
    row = lambda i, *_: (i, 0)
    return pl.pallas_call(
        _dispatch_kernel,
        grid_spec=pltpu.PrefetchScalarGridSpec(
            num_scalar_prefetch=6,
            grid=(t // tm,),
            in_specs=[pl.BlockSpec((tm, d), row), pl.BlockSpec((1, d), lambda i, *_: (0, 0)),
                      pl.BlockSpec((tm, LANE), row)],
            out_specs=pl.BlockSpec(memory_space=pl.ANY),
            scratch_shapes=[pltpu.VMEM((2, _sorted_rows(tm), d), BF16), pltpu.VMEM((bm, d), BF16),
                            pltpu.SemaphoreType.DMA((2,)), pltpu.SemaphoreType.DMA(())]),
        out_shape=jax.ShapeDtypeStruct((n_rows, d), BF16),
        compiler_params=_params(1),
        name="dispatch",
    )(cnt, lo, dst, zst, zln, nused, x2d, g, route)


def _expert_kernel(bexp_ref, nused_ref, xs_ref, wg_ref, wu_ref, wd_ref, ys_ref, wg_scr, wu_scr, wd_scr):
    i = pl.program_id(0)

    @pl.when(i < nused_ref[0])
    def _():
        changed = jnp.logical_or(i == 0, bexp_ref[i] != bexp_ref[jnp.maximum(i - 1, 0)])

        @pl.when(changed)
        def _():
            wg_scr[...] = wg_ref[...].astype(BF16)
            wu_scr[...] = wu_ref[...].astype(BF16)
            wd_scr[...] = wd_ref[...].astype(BF16)

        xb = xs_ref[...]
        hg = _dot(xb, wg_scr[...])
        hu = _dot(xb, wu_scr[...])
        act = (hg * _sigmoid(hg) * hu).astype(BF16)
        ys_ref[...] = _dot(act, wd_scr[...]).astype(BF16)

    @pl.when(i >= nused_ref[0])
    def _():
        ys_ref[...] = jnp.zeros_like(ys_ref)


def experts(bexp, nused, xs, w_gate, w_up, w_down, layer, bm):
    n_rows, d = xs.shape
    de = w_gate.shape[3]
    wsel = lambda i, bexp, nused: (layer, bexp[i], 0, 0)
    return pl.pallas_call(
        _expert_kernel,
        grid_spec=pltpu.PrefetchScalarGridSpec(
            num_scalar_prefetch=2,
            grid=(n_rows // bm,),
            in_specs=[pl.BlockSpec((bm, d), lambda i, bexp, nused: (jnp.minimum(i, nused[0] - 1), 0)),
                      pl.BlockSpec((None, None, d, de), wsel),
                      pl.BlockSpec((None, None, d, de), wsel), pl.BlockSpec((None, None, de, d), wsel)],
            out_specs=pl.BlockSpec((bm, d), lambda i, bexp, nused: (i, 0)),
            scratch_shapes=[pltpu.VMEM((d, de), BF16), pltpu.VMEM((d, de), BF16),
                            pltpu.VMEM((de, d), BF16)]),
        out_shape=jax.ShapeDtypeStruct(xs.shape, BF16),
        compiler_params=_params(1),
        name="experts",
    )(bexp, nused, xs, w_gate, w_up, w_down)


def _combine_kernel(cnt_ref, lo_ref, dst_ref, x_ref, route_ref, g_ref, ys_ref, o_ref, buf, sem, *, final):
    tm = x_ref.shape[0]
    i = pl.program_id(0)
    n = pl.num_programs(0)
    slot = i % 2

    def run_copy(s):
        return lambda lo, dst, cnt: pltpu.make_async_copy(
            ys_ref.at[pl.ds(dst, cnt)], buf.at[s, pl.ds(lo, cnt)], sem.at[s])

    @pl.when(i == 0)
    def _():
        buf[...] = jnp.zeros_like(buf)
        _run_copies(cnt_ref, lo_ref, dst_ref, 0, run_copy(0), False)

    @pl.when(i + 1 < n)
    def _():
        _run_copies(cnt_ref, lo_ref, dst_ref, i + 1, run_copy(1 - slot), False)

    _run_copies(cnt_ref, lo_ref, dst_ref, i, run_copy(slot), True)
    ysb = buf[slot]
    route = route_ref[...]
    out = x_ref[...]
    for k in range(2):
        out = out + route[:, k:k + 1] * _dot(_sort_matrix(route, k, buf.shape[1]), ysb)
    if final:
        out = _rms(out, g_ref[...])
    o_ref[...] = out


def combine(cnt, lo, dst, x2d, route, g, ys, tm, final):
    t, d = x2d.shape
    row = lambda i, *_: (i, 0)
    return pl.pallas_call(
        functools.partial(_combine_kernel, final=final),
        grid_spec=pltpu.PrefetchScalarGridSpec(
            num_scalar_prefetch=3,
            grid=(t // tm,),
            in_specs=[pl.BlockSpec((tm, d), row), pl.BlockSpec((tm, LANE), row),
                      pl.BlockSpec((1, d), lambda i, *_: (0, 0)),
                      pl.BlockSpec(memory_space=pl.ANY)],
            out_specs=pl.BlockSpec((tm, d), row),
            scratch_shapes=[pltpu.VMEM((2, _sorted_rows(tm), d), BF16), pltpu.SemaphoreType.DMA((2,))]),
        out_shape=jax.ShapeDtypeStruct((t, d), F32),
        compiler_params=_params(1),
        name="combine",
    )(cnt, lo, dst, x2d, route, g, ys)


def moe(x2d, g, rg_w, rg_b, re_w, re_b, w_gate, w_up, w_down, layer, g_final, final, tm, bm):
    t = x2d.shape[0]
    nt = t // tm
    pad = LANE - N_GROUPS - N_EXPERTS
    wr = jnp.concatenate([rg_w, re_w, jnp.zeros((D_MODEL, pad), F32)], axis=1)
    wr_hi = wr.astype(BF16)
    wr = jnp.stack([wr_hi, (wr - wr_hi.astype(F32)).astype(BF16)])
    br = jnp.concatenate([rg_b, re_b, jnp.zeros((pad,), F32)])[None, :]
    g2 = g[None, :]
    route, meta = router(x2d, g2, wr, br, tm)
    meta = meta.reshape(nt, SUBLANE, LANE)[:, :, :N_EXPERTS].astype(I32)
    cnt, earlier, lo = meta[:, 0], meta[:, 1], meta[:, 2]
    counts = meta[nt - 1, 3]
    padded = (counts + bm - 1) // bm * bm
    ends = jnp.cumsum(padded)
    offs = ends - padded
    dst = offs[None, :] + earlier
    n_rows = -(-(2 * t + nt * N_EXPERTS * (RUN_ALIGN - 1)) // bm) * bm + N_EXPERTS * bm
    nused = (ends[-1] // bm).astype(I32)
    blk_start = jnp.arange(n_rows // bm, dtype=I32) * bm
    bexp = jnp.sum((ends[None, :] <= blk_start[:, None]).astype(I32), axis=1)
    last = jnp.sum((ends <= (nused - 1) * bm).astype(I32))
    bexp = jnp.minimum(bexp, last)
    cnt, lo, dst = cnt.reshape(-1), lo.reshape(-1), dst.reshape(-1)
    xs = dispatch(cnt, lo, dst, offs + counts, padded - counts, nused[None], x2d, g2, route, n_rows, tm, bm)
    ys = experts(bexp, nused[None], xs, w_gate, w_up, w_down, layer, bm)
    return combine(cnt, lo, dst, x2d, route, g_final[None, :], ys, tm, final)


def kernel(x, norm_mix_g, w_in, b_gate, pool_w, pool_scale, sgu_ln_g, sgu_ln_b, sgu_w, sgu_b, ssm_a_re, ssm_a_im, ssm_log_dt, ssm_b_re, ssm_b_im, ssm_c_re, ssm_c_im, ssm_d, glu_w, glu_b, w_branch, w_out, norm_ffn_g, router_group_w, router_group_b, router_expert_w, router_expert_b, exp_w_gate, exp_w_up, exp_w_down, norm_final_g):
    batch, seq, d = x.shape
    depth = w_in.shape[0]
    tm_in, tm_mix, tm_moe, bm = 1024, 512, 256, 512
    s5_seqs = 2
    x2d = x.reshape(batch * seq, d)
    ssm_col_block = 3
    for l in range(depth):
        wl = w_in[l].astype(BF16)
        g_mix = norm_mix_g[l][None, :]
        xc4 = ssm_input(x2d, g_mix, wl, ssm_col_block, tm_in)
        bq, cq, kq, lam_q = ssm_tables(ssm_a_re[l], ssm_a_im[l], ssm_log_dt[l], ssm_b_re[l], ssm_b_im[l],
                                       ssm_c_re[l], ssm_c_im[l], ssm_d[l])
        ys4 = s5_scan(xc4, bq, cq, kq, lam_q, batch, s5_seqs)
        p = {
            "norm_mix_g": g_mix, "w_in": wl, "b_gate": b_gate[l][None, :],
            "pool_w": pool_w[l].astype(BF16), "pool_scale": pool_scale[l][None, :],
            "sgu_ln_g": sgu_ln_g[l][None, :], "sgu_ln_b": sgu_ln_b[l][None, :],
            "sgu_w": sgu_w[l],
            "sgu_b_full": jnp.repeat(sgu_b[l].T, LANE, axis=1),
            "glu_w": glu_w[l].astype(BF16), "glu_b": glu_b[l][None, :],
            "w_branch": w_branch[l].astype(BF16), "w_out": w_out[l].astype(BF16),
        }
        x2d = mixer(x2d, ys4, p, batch, tm_mix)
        x2d = moe(x2d, norm_ffn_g[l], router_group_w[l], router_group_b[l], router_expert_w[l],
                  router_expert_b[l], exp_w_gate, exp_w_up, exp_w_down, l, norm_final_g,
                  l == depth - 1, tm_moe, bm)
    return x2d.reshape(batch, seq, d)
```

```python
import functools
import math

import jax
import jax.numpy as jnp
from jax import lax
from jax.experimental import pallas as pl
from jax.experimental.pallas import tpu as pltpu

F32 = jnp.float32
BF16 = jnp.bfloat16
I32 = jnp.int32

D_MODEL = 1024
POOL_WINDOWS = (2, 4, 8, 16)
POOL_HALO = 16
LANE = 128
SUBLANE = 8
RUN_ALIGN = 16
SGU_CHUNK = 128
SSM_GROUP_DIM = 16
SSM_STATE = 64
SSM_Q = 8
SSM_SLABS = 4
GROUPS_PER_SLAB = LANE // SSM_GROUP_DIM
N_GROUPS = 4
EXPERTS_PER_GROUP = 8
N_EXPERTS = 32
EPS = 1e-6
VMEM_LIMIT = 56 * 1024 * 1024


def _rms(x, g):
    return x * lax.rsqrt(jnp.mean(x * x, axis=-1, keepdims=True) + EPS) * g


def _gelu(x):
    c = math.sqrt(2.0 / math.pi)
    return x * (0.5 * (1.0 + jnp.tanh(c * (x + 0.044715 * (x * x * x)))))


def _sigmoid(x):
    return 1.0 / (1.0 + jnp.exp(-x))


def _dot(a, b):
    return jnp.dot(a, b, preferred_element_type=F32)


def _layer_spec(arr, layer):
    nd = arr.ndim - 1
    return pl.BlockSpec((None,) + arr.shape[1:], lambda *_: (layer,) + (0,) * nd, pipeline_mode=pl.Buffered(1))


def _params(n_axes):
    return pltpu.CompilerParams(dimension_semantics=("arbitrary",) * n_axes, vmem_limit_bytes=VMEM_LIMIT)


def _xc_kernel(x_ref, g_ref, w_ref, o_ref, xc_scr):
    h = _rms(x_ref[...], g_ref[...]).astype(BF16)
    xc = _dot(h, w_ref[...])
    rows = xc_scr.shape[1] // SSM_Q
    for j in range(SSM_SLABS):
        xc_scr[j] = xc[:, j * LANE:(j + 1) * LANE]
        for s in range(SSM_Q):
            o_ref[j, :, s * LANE:(s + 1) * LANE] = xc_scr[j, pl.ds(s, rows, stride=SSM_Q), :].astype(BF16)


def ssm_input(x2d, g, w_in_bf, layer, col_block, tm):
    t = x2d.shape[0]
    width = SSM_SLABS * LANE
    return pl.pallas_call(
        _xc_kernel,
        grid=(t // tm,),
        in_specs=[pl.BlockSpec((tm, D_MODEL), lambda i: (i, 0)),
                  _layer_spec(g, layer),
                  pl.BlockSpec((None, D_MODEL, width), lambda i: (layer, 0, col_block),
                               pipeline_mode=pl.Buffered(1))],
        out_specs=pl.BlockSpec((SSM_SLABS, tm // SSM_Q, SSM_Q * LANE), lambda i: (0, i, 0)),
        out_shape=jax.ShapeDtypeStruct((SSM_SLABS, t // SSM_Q, SSM_Q * LANE), BF16),
        scratch_shapes=[pltpu.VMEM((SSM_SLABS, tm, LANE), F32)],
        compiler_params=_params(1),
        name="ssm_input",
    )(x2d, g, w_in_bf)


def _compact(small):
    sl, r1, a, r2, c1, c2 = small.shape
    return small.reshape(sl, r1 * a * r2, c1 * c2).astype(BF16)


def ssm_tables(a_re, a_im, log_dt, b_re, b_im, c_re, c_im, d_skip):
    q = SSM_Q
    ng = a_re.shape[0]
    sl = ng // GROUPS_PER_SLAB
    dt = jnp.exp(log_dt)[:, None]
    k = jnp.arange(q + 1, dtype=F32)[:, None, None]
    mag = jnp.exp(a_re * dt * k)
    ang = a_im * dt * k
    pw_re, pw_im = mag * jnp.cos(ang), mag * jnp.sin(ang)
    nr, ni = pw_re[1] - 1.0, pw_im[1]
    den = a_re * a_re + a_im * a_im
    fr, fi = (nr * a_re + ni * a_im) / den, (ni * a_re - nr * a_im) / den
    bb_re = fr[..., None] * b_re - fi[..., None] * b_im
    bb_im = fr[..., None] * b_im + fi[..., None] * b_re

    def slabbed(a, axis):
        return a.reshape(*a.shape[:axis], sl, GROUPS_PER_SLAB, *a.shape[axis + 1:])

    k_rev = (q - 1.0) - jnp.arange(q, dtype=F32)[:, None, None]
    mag_rev = jnp.exp(a_re * dt * k_rev)
    rev_re, rev_im = mag_rev * jnp.cos(a_im * dt * k_rev), mag_rev * jnp.sin(a_im * dt * k_rev)
    m_re = rev_re[..., None] * bb_re[None] - rev_im[..., None] * bb_im[None]
    m_im = rev_re[..., None] * bb_im[None] + rev_im[..., None] * bb_re[None]
    m = slabbed(jnp.stack([m_re, m_im], axis=0), 2)
    bq = _compact(m.transpose(2, 1, 3, 5, 0, 4))
    pr, pi = pw_re[1:, :, None, :], pw_im[1:, :, None, :]
    n_re = c_re[None] * pr - c_im[None] * pi
    n_im = c_re[None] * pi + c_im[None] * pr
    n = slabbed(jnp.stack([n_re, -n_im], axis=0), 2)
    cq = _compact(n.transpose(2, 0, 3, 5, 1, 4))
    cb_re = c_re[:, :, :, None] * bb_re[:, None, :, :] - c_im[:, :, :, None] * bb_im[:, None, :, :]
    cb_im = c_re[:, :, :, None] * bb_im[:, None, :, :] + c_im[:, :, :, None] * bb_re[:, None, :, :]
    kern = jnp.sum(cb_re[None] * pw_re[:q, :, None, :, None] - cb_im[None] * pw_im[:q, :, None, :, None],
                   axis=3)
    skip = d_skip.reshape(ng, SSM_GROUP_DIM)[:, :, None] * jnp.eye(SSM_GROUP_DIM, dtype=F32)[None]
    kern = kern + jnp.where(jnp.arange(q)[:, None, None, None] == 0, skip[None], 0.0)
    kst = jnp.stack([jnp.concatenate([jnp.zeros((s,) + kern.shape[1:], F32), kern[:q - s]], axis=0)
                     for s in range(q)], axis=0)
    kst = slabbed(kst, 2)
    kq = _compact(kst.transpose(2, 0, 3, 5, 1, 4))
    lam_q = jnp.concatenate([pw_re[q].reshape(sl, 1, -1), pw_im[q].reshape(sl, 1, -1)], axis=-1)
    return bq, cq, kq, lam_q


def _expand_block_diag(small, row_inner, col_inner):
    n_rows, n_small = small.shape
    n_cols = n_small * GROUPS_PER_SLAB
    g_bits = GROUPS_PER_SLAB.bit_length() - 1
    r_shift, c_shift = row_inner.bit_length() - 1, col_inner.bit_length() - 1
    k = lax.broadcasted_iota(I32, (n_small, n_cols), 0)
    col = lax.broadcasted_iota(I32, (n_small, n_cols), 1)
    src = ((col >> (c_shift + g_bits)) << c_shift) | (col & (col_inner - 1))
    spread = jnp.where(k == src, 1.0, 0.0).astype(BF16)
    r = lax.broadcasted_iota(I32, (n_rows, n_cols), 0)
    c = lax.broadcasted_iota(I32, (n_rows, n_cols), 1)
    same = ((r >> r_shift) & (GROUPS_PER_SLAB - 1)) == ((c >> c_shift) & (GROUPS_PER_SLAB - 1))
    return jnp.where(same, _dot(small, spread), 0.0).astype(BF16)


def _s5_kernel(u_ref, bq_ref, cq_ref, kq_ref, lam_ref, y_ref, g_scr, s_scr, t_scr, tbq, tcq, tkq, *, seqs):
    half = GROUPS_PER_SLAB * SSM_STATE
    rows = u_ref.shape[1]

    @pl.when(pl.program_id(1) == 0)
    def _():
        tbq[...] = _expand_block_diag(bq_ref[0], SSM_GROUP_DIM, SSM_STATE)
        tcq[...] = _expand_block_diag(cq_ref[0], SSM_STATE, SSM_GROUP_DIM)
        tkq[...] = _expand_block_diag(kq_ref[0], SSM_GROUP_DIM, SSM_GROUP_DIM)

    u = u_ref[0]
    g_scr[...] = _dot(u, tbq[...])
    lam = lam_ref[0]
    lre, lim = lam[:, :half], lam[:, half:]
    seq_rows = rows // seqs

    def body(n, carry):
        out = []
        for b in range(seqs):
            cre, cim = carry[2 * b], carry[2 * b + 1]
            row = b * seq_rows + n
            s_scr[pl.ds(row, 1), :] = jnp.concatenate([cre, cim], axis=1)
            g = g_scr[pl.ds(row, 1), :]
            out.append(lre * cre - lim * cim + g[:, :half])
            out.append(lre * cim + lim * cre + g[:, half:])
        return tuple(out)

    zero = jnp.zeros((1, half), F32)
    lax.fori_loop(0, seq_rows, body, (zero,) * (2 * seqs), unroll=2)
    y = _dot(s_scr[...].astype(BF16), tcq[...]) + _dot(u, tkq[...])
    for t in range(SSM_Q):
        t_scr[pl.ds(t, rows, stride=SSM_Q), :] = y[:, t * LANE:(t + 1) * LANE]
    y_ref[0] = t_scr[...].astype(BF16)


def s5_scan(u4, bq, cq, kq, lam_q, layer, batch, seqs):
    sl, rows_total, width = u4.shape
    rows = rows_total // batch * seqs
    tab = pl.BlockSpec((None, 1, width, width // GROUPS_PER_SLAB), lambda j, b: (layer, j, 0, 0))
    return pl.pallas_call(
        functools.partial(_s5_kernel, seqs=seqs),
        grid=(sl, batch // seqs),
        in_specs=[pl.BlockSpec((1, rows, width), lambda j, b: (j, b, 0)), tab, tab, tab,
                  pl.BlockSpec((None, 1, 1, width), lambda j, b: (layer, j, 0, 0))],
        out_specs=pl.BlockSpec((1, rows * SSM_Q, LANE), lambda j, b: (j, b, 0)),
        out_shape=jax.ShapeDtypeStruct((sl, rows_total * SSM_Q, LANE), BF16),
        scratch_shapes=[pltpu.VMEM((rows, width), F32), pltpu.VMEM((rows, width), F32),
                        pltpu.VMEM((rows * SSM_Q, LANE), F32)] + [pltpu.VMEM((width, width), BF16)] * 3,
        compiler_params=_params(2),
        name="s5_scan",
    )(u4, bq, cq, kq, lam_q)


def _mixer_kernel(x_ref, ys_ref, ng_ref, win_ref, bgate_ref, poolw_ref, pscale_ref, lng_ref, lnb_ref,
                  sguw_ref, sgub_ref, gluw_ref, glub_ref, wbr_ref, wout_ref, o_ref, xe_scr):
    tm = x_ref.shape[0]
    width = SSM_SLABS * LANE
    a_end, b_end = width, 3 * width
    c_end = b_end + width
    i = pl.program_id(1)
    x = x_ref[...]
    hb = _rms(x, ng_ref[...]).astype(BF16)

    xa = _dot(hb, win_ref[:, 0:a_end])

    @pl.when(i == 0)
    def _():
        xe_scr[0:POOL_HALO, :] = jnp.zeros((POOL_HALO, xe_scr.shape[1]), F32)

    xe_scr[POOL_HALO:POOL_HALO + tm, :] = xa
    pos = i * tm + lax.broadcasted_iota(I32, (tm, 1), 0)
    ya_parts = []
    for gi, w in enumerate(POOL_WINDOWS):
        ch = slice(gi * LANE, (gi + 1) * LANE)
        win = xa[:, ch]
        for j in range(1, w):
            win = win + xe_scr[POOL_HALO - j:POOL_HALO - j + tm, ch]
        cnt = jnp.minimum(pos + 1, w).astype(F32)
        diff = win / cnt - xa[:, ch]
        ya_parts.append(_dot(diff.astype(BF16), poolw_ref[gi]) * pscale_ref[:, ch])
    ya = jnp.concatenate(ya_parts, axis=1)
    xe_scr[0:POOL_HALO, :] = xe_scr[tm:tm + POOL_HALO, :]

    z = _gelu(_dot(hb, win_ref[:, a_end:b_end]))
    u, v = z[:, :width], z[:, width:]
    mu = jnp.mean(v, axis=-1, keepdims=True)
    vc = v - mu
    var = jnp.mean(vc * vc, axis=-1, keepdims=True)
    vb = (vc * lax.rsqrt(var + EPS) * lng_ref[...] + lnb_ref[...]).astype(BF16)
    r_i = lax.broadcasted_iota(I32, (SGU_CHUNK, SGU_CHUNK), 0)
    c_i = lax.broadcasted_iota(I32, (SGU_CHUNK, SGU_CHUNK), 1)
    n_heads = width // LANE
    wt = [jnp.where(r_i >= c_i, sguw_ref[hd], 0.0).astype(BF16) for hd in range(n_heads)]
    s_rows = []
    for c in range(tm // SGU_CHUNK):
        rs = slice(c * SGU_CHUNK, (c + 1) * SGU_CHUNK)
        s_rows.append(jnp.concatenate(
            [_dot(wt[hd], vb[rs, hd * LANE:(hd + 1) * LANE]) for hd in range(n_heads)], axis=1)
            + sgub_ref[...])
    yb = u * jnp.concatenate(s_rows, axis=0)

    yc = _gelu(jnp.concatenate([ys_ref[j] for j in range(SSM_SLABS)], axis=1).astype(F32))
    yc = yc * _sigmoid(_dot(yc.astype(BF16), gluw_ref[...]) + glub_ref[...])

    merged = None
    for k, yk in enumerate((ya, yb, yc)):
        gate = _sigmoid(_dot(hb, win_ref[:, c_end + k * D_MODEL:c_end + (k + 1) * D_MODEL])
                        + bgate_ref[:, k * D_MODEL:(k + 1) * D_MODEL])
        term = gate * _dot(yk.astype(BF16), wbr_ref[k])
        merged = term if merged is None else merged + term
    o_ref[...] = x + _dot(merged.astype(BF16), wout_ref[...])


def mixer(x2d, ys4, p, layer, batch, tm):
    t = x2d.shape[0]
    nt = t // batch // tm
    row = lambda b, i: (b * nt + i, 0)
    slab = lambda b, i: (0, b * nt + i, 0)
    consts = [p["norm_mix_g"], p["w_in"], p["b_gate"], p["pool_w"], p["pool_scale"], p["sgu_ln_g"],
              p["sgu_ln_b"], p["sgu_w"], p["sgu_b_full"], p["glu_w"], p["glu_b"], p["w_branch"], p["w_out"]]
    return pl.pallas_call(
        _mixer_kernel,
        grid=(batch, nt),
        in_specs=[pl.BlockSpec((tm, D_MODEL), row),
                  pl.BlockSpec((SSM_SLABS, tm, LANE), slab)] + [_layer_spec(c, layer) for c in consts],
        out_specs=pl.BlockSpec((tm, D_MODEL), row),
        out_shape=jax.ShapeDtypeStruct(x2d.shape, F32),
        scratch_shapes=[pltpu.VMEM((tm + POOL_HALO, SSM_SLABS * LANE), F32)],
        compiler_params=_params(2),
        name="mixer",
    )(x2d, ys4, *consts)


def _router_kernel(x_ref, g_ref, wr_ref, br_ref, route_ref, meta_ref, carry_scr):
    tm = x_ref.shape[0]
    i = pl.program_id(0)

    @pl.when(i == 0)
    def _():
        carry_scr[...] = jnp.zeros_like(carry_scr)

    h = _rms(x_ref[...], g_ref[...])
    h_hi = h.astype(BF16)
    h_lo = (h - h_hi.astype(F32)).astype(BF16)
    logits = (_dot(h_hi, wr_ref[0]) + _dot(h_hi, wr_ref[1]) + _dot(h_lo, wr_ref[0])) + br_ref[...]
    lane = lax.broadcasted_iota(I32, logits.shape, 1)
    neg = -jnp.inf
    gl = jnp.where(lane < N_GROUPS, logits, neg)
    gmax = jnp.max(gl, axis=-1, keepdims=True)
    gsum = jnp.sum(jnp.where(lane < N_GROUPS, jnp.exp(logits - gmax), 0.0), axis=-1, keepdims=True)
    g_val = 1.0 / gsum
    g_idx = jnp.min(jnp.where(gl == gmax, lane, LANE), axis=-1, keepdims=True)
    lo = N_GROUPS + EXPERTS_PER_GROUP * g_idx
    el = jnp.where((lane >= lo) & (lane < lo + EXPERTS_PER_GROUP), logits, neg)
    e1v = jnp.max(el, axis=-1, keepdims=True)
    e1l = jnp.min(jnp.where(el == e1v, lane, LANE), axis=-1, keepdims=True)
    el2 = jnp.where(lane == e1l, neg, el)
    e2v = jnp.max(el2, axis=-1, keepdims=True)
    e2l = jnp.min(jnp.where(el2 == e2v, lane, LANE), axis=-1, keepdims=True)
    tt = jnp.exp(e2v - e1v)
    w1 = g_val / (1.0 + tt)
    w2 = g_val * tt / (1.0 + tt)
    hit1, hit2 = lane == e1l - N_GROUPS, lane == e2l - N_GROUPS
    onehot = jnp.where(hit1 | hit2, 1.0, 0.0)
    r_i = lax.broadcasted_iota(I32, (tm, tm), 0)
    c_i = lax.broadcasted_iota(I32, (tm, tm), 1)
    tri = jnp.where(r_i > c_i, 1.0, 0.0).astype(BF16)
    prefix = _dot(tri, onehot.astype(BF16))
    count = jnp.sum(onehot, axis=0, keepdims=True)
    groups8 = jnp.floor((count + (RUN_ALIGN - 1.0)) * (1.0 / RUN_ALIGN))
    count = groups8 * RUN_ALIGN
    e_r = lax.broadcasted_iota(I32, (LANE, LANE), 0)
    e_c = lax.broadcasted_iota(I32, (LANE, LANE), 1)
    before = jnp.where(e_r < e_c, 1.0, 0.0).astype(BF16)
    run_start = _dot(jnp.broadcast_to(groups8, (SUBLANE, LANE)).astype(BF16), before)[0:1, :] * RUN_ALIGN
    local = prefix + run_start
    lp1 = jnp.sum(jnp.where(hit1, local, 0.0), axis=-1, keepdims=True)
    lp2 = jnp.sum(jnp.where(hit2, local, 0.0), axis=-1, keepdims=True)
    route_ref[...] = jnp.where(lane == 0, w1, jnp.where(lane == 1, w2, jnp.where(
        lane == 2, lp1, jnp.where(lane == 3, lp2, 0.0))))
    earlier = carry_scr[0:1, :]
    total = earlier + count
    carry_scr[...] = jnp.broadcast_to(total, carry_scr.shape)
    sub = lax.broadcasted_iota(I32, meta_ref.shape, 0)
    meta_ref[...] = jnp.where(sub == 0, count, jnp.where(sub == 1, earlier, jnp.where(
        sub == 2, run_start, jnp.where(sub == 3, total, 0.0))))


def router(x2d, g, wr, br, layer, tm):
    t = x2d.shape[0]
    row = lambda i: (i, 0)
    return pl.pallas_call(
        _router_kernel,
        grid=(t // tm,),
        in_specs=[pl.BlockSpec((tm, D_MODEL), row), _layer_spec(g, layer),
                  _layer_spec(wr, layer), _layer_spec(br, layer)],
        out_specs=[pl.BlockSpec((tm, LANE), row), pl.BlockSpec((SUBLANE, LANE), row)],
        out_shape=[jax.ShapeDtypeStruct((t, LANE), F32),
                   jax.ShapeDtypeStruct((t // tm * SUBLANE, LANE), F32)],
        scratch_shapes=[pltpu.VMEM((SUBLANE, LANE), F32)],
        compiler_params=_params(1),
        name="router",
    )(x2d, g, wr, br)


def _sort_matrix(route, k, n_sorted):
    col = lax.broadcasted_iota(I32, (route.shape[0], n_sorted), 1)
    return jnp.where(col == route[:, 2 + k:3 + k].astype(I32), 1.0, 0.0).astype(BF16)


def _sorted_rows(tm):
    return -(-(2 * tm + N_EXPERTS * (RUN_ALIGN - 1)) // LANE) * LANE


def _run_copies(cnt_ref, lo_ref, dst_ref, tile, make_copy, wait):
    def body(e, c):
        idx = tile * N_EXPERTS + e
        n = cnt_ref[idx]

        @pl.when(n > 0)
        def _():
            cp = make_copy(pl.multiple_of(lo_ref[idx], RUN_ALIGN), pl.multiple_of(dst_ref[idx], RUN_ALIGN),
                           pl.multiple_of(n, RUN_ALIGN))
            if wait:
                cp.wait()
            else:
                cp.start()
        return c
    lax.fori_loop(0, N_EXPERTS, body, 0)


def _dispatch_kernel(cnt_ref, lo_ref, dst_ref, zst_ref, zln_ref, nused_ref, x_ref, g_ref, route_ref, o_ref,
                     srt_scr, zero_scr, sem, zsem):
    tm = x_ref.shape[0]
    bm = zero_scr.shape[0]
    i = pl.program_id(0)
    n = pl.num_programs(0)
    slot = i % 2

    def run_copy(s):
        return lambda lo, dst, cnt: pltpu.make_async_copy(
            srt_scr.at[s, pl.ds(lo, cnt)], o_ref.at[pl.ds(dst, cnt)], sem.at[s])

    def pad_copy(e):
        ln = pl.multiple_of(zln_ref[e], RUN_ALIGN)
        return pltpu.make_async_copy(zero_scr.at[pl.ds(0, ln)],
                                     o_ref.at[pl.ds(pl.multiple_of(zst_ref[e], RUN_ALIGN), ln)], zsem)

    def tail_copy(b):
        return pltpu.make_async_copy(zero_scr, o_ref.at[pl.ds(pl.multiple_of(b * bm, bm), bm)], zsem)

    @pl.when(i == 0)
    def _():
        zero_scr[...] = jnp.zeros_like(zero_scr)
        for wait in (False, True):
            def body(e, c):
                @pl.when(zln_ref[e] > 0)
                def _():
                    cp = pad_copy(e)
                    cp.wait() if wait else cp.start()
                return c
            lax.fori_loop(0, N_EXPERTS, body, 0)

            def tail(b, c):
                cp = tail_copy(b)
                cp.wait() if wait else cp.start()
                return c
            lax.fori_loop(nused_ref[0], o_ref.shape[0] // bm, tail, 0)

    @pl.when(i >= 2)
    def _():
        _run_copies(cnt_ref, lo_ref, dst_ref, i - 2, run_copy(slot), True)

    hb = _rms(x_ref[...], g_ref[...]).astype(BF16)
    route = route_ref[...]
    ns = srt_scr.shape[1]
    pt = jnp.maximum(_sort_matrix(route, 0, ns), _sort_matrix(route, 1, ns))
    srt_scr[slot] = lax.dot_general(pt, hb, (((0,), (0,)), ((), ())),
                                    preferred_element_type=F32).astype(BF16)
    _run_copies(cnt_ref, lo_ref, dst_ref, i, run_copy(slot), False)

    @pl.when(i == n - 1)
    def _():
        @pl.when(i >= 1)
        def _():
            _run_copies(cnt_ref, lo_ref, dst_ref, i - 1, run_copy(1 - slot), True)
        _run_copies(cnt_ref, lo_ref, dst_ref, i, run_copy(slot), True)


def dispatch(cnt, lo, dst, zst, zln, nused, x2d, g, layer, route, n_rows, tm, bm):
    t, d = x2d.shape
    row = lambda i, *_: (i, 0)
    return pl.pallas_call(
        _dispatch_kernel,
        grid_spec=pltpu.PrefetchScalarGridSpec(
            num_scalar_prefetch=6,
            grid=(t // tm,),
            in_specs=[pl.BlockSpec((tm, d), row), pl.BlockSpec((None, 1, d), lambda i, *_: (layer, 0, 0)),
                      pl.BlockSpec((tm, LANE), row)],
            out_specs=pl.BlockSpec(memory_space=pl.ANY),
            scratch_shapes=[pltpu.VMEM((2, _sorted_rows(tm), d), BF16), pltpu.VMEM((bm, d), BF16),
                            pltpu.SemaphoreType.DMA((2,)), pltpu.SemaphoreType.DMA(())]),
        out_shape=jax.ShapeDtypeStruct((n_rows, d), BF16),
        compiler_params=_params(1),
        name="dispatch",
    )(cnt, lo, dst, zst, zln, nused, x2d, g, route)


def _expert_kernel(bexp_ref, nused_ref, slot_ref, next_ref, xs_ref, wg_hbm, wu_hbm, wd_hbm, ys_ref,
                   wg_buf, wu_buf, wd_buf, wg_scr, wu_scr, wd_scr, sem, *, layer):
    i = pl.program_id(0)

    def weight_copies(e, s):
        return [pltpu.make_async_copy(hbm.at[layer, e], buf.at[s], sem.at[s])
                for hbm, buf in ((wg_hbm, wg_buf), (wu_hbm, wu_buf), (wd_hbm, wd_buf))]

    @pl.when(i < nused_ref[0])
    def _():
        e, s = bexp_ref[i], slot_ref[i]
        changed = jnp.logical_or(i == 0, e != bexp_ref[jnp.maximum(i - 1, 0)])

        @pl.when(i == 0)
        def _():
            for cp in weight_copies(e, s):
                cp.start()

        @pl.when(changed)
        def _():
            for cp in weight_copies(e, s):
                cp.wait()
            wg_scr[...] = wg_buf[s].astype(BF16)
            wu_scr[...] = wu_buf[s].astype(BF16)
            wd_scr[...] = wd_buf[s].astype(BF16)

            @pl.when(next_ref[i] < N_EXPERTS)
            def _():
                for cp in weight_copies(next_ref[i], 1 - s):
                    cp.start()

        xb = xs_ref[...]
        hg = _dot(xb, wg_scr[...])
        hu = _dot(xb, wu_scr[...])
        act = (hg * _sigmoid(hg) * hu).astype(BF16)
        ys_ref[...] = _dot(act, wd_scr[...]).astype(BF16)

    @pl.when(i >= nused_ref[0])
    def _():
        ys_ref[...] = jnp.zeros_like(ys_ref)


def experts(bexp, nused, slot, nxt, xs, w_gate, w_up, w_down, layer, bm):
    n_rows, d = xs.shape
    de = w_gate.shape[3]
    anyspace = pl.BlockSpec(memory_space=pl.ANY)
    return pl.pallas_call(
        functools.partial(_expert_kernel, layer=layer),
        grid_spec=pltpu.PrefetchScalarGridSpec(
            num_scalar_prefetch=4,
            grid=(n_rows // bm,),
            in_specs=[pl.BlockSpec((bm, d), lambda i, bexp, nused, *_: (jnp.minimum(i, nused[0] - 1), 0)),
                      anyspace, anyspace, anyspace],
            out_specs=pl.BlockSpec((bm, d), lambda i, *_: (i, 0)),
            scratch_shapes=[pltpu.VMEM((2, d, de), F32), pltpu.VMEM((2, d, de), F32), pltpu.VMEM((2, de, d), F32),
                            pltpu.VMEM((d, de), BF16), pltpu.VMEM((d, de), BF16), pltpu.VMEM((de, d), BF16),
                            pltpu.SemaphoreType.DMA((2,))]),
        out_shape=jax.ShapeDtypeStruct(xs.shape, BF16),
        compiler_params=_params(1),
        name="experts",
    )(bexp, nused, slot, nxt, xs, w_gate, w_up, w_down)


def _combine_kernel(cnt_ref, lo_ref, dst_ref, x_ref, route_ref, g_ref, ys_ref, o_ref, buf, sem, *, final):
    tm = x_ref.shape[0]
    i = pl.program_id(0)
    n = pl.num_programs(0)
    slot = i % 2

    def run_copy(s):
        return lambda lo, dst, cnt: pltpu.make_async_copy(
            ys_ref.at[pl.ds(dst, cnt)], buf.at[s, pl.ds(lo, cnt)], sem.at[s])

    @pl.when(i == 0)
    def _():
        buf[...] = jnp.zeros_like(buf)
        _run_copies(cnt_ref, lo_ref, dst_ref, 0, run_copy(0), False)

    @pl.when(i + 1 < n)
    def _():
        _run_copies(cnt_ref, lo_ref, dst_ref, i + 1, run_copy(1 - slot), False)

    _run_copies(cnt_ref, lo_ref, dst_ref, i, run_copy(slot), True)
    ysb = buf[slot]
    route = route_ref[...]
    out = x_ref[...]
    for k in range(2):
        out = out + route[:, k:k + 1] * _dot(_sort_matrix(route, k, buf.shape[1]), ysb)
    if final:
        out = _rms(out, g_ref[...])
    o_ref[...] = out


def combine(cnt, lo, dst, x2d, route, g, ys, tm, final):
    t, d = x2d.shape
    row = lambda i, *_: (i, 0)
    return pl.pallas_call(
        functools.partial(_combine_kernel, final=final),
        grid_spec=pltpu.PrefetchScalarGridSpec(
            num_scalar_prefetch=3,
            grid=(t // tm,),
            in_specs=[pl.BlockSpec((tm, d), row), pl.BlockSpec((tm, LANE), row),
                      pl.BlockSpec((1, d), lambda i, *_: (0, 0)),
                      pl.BlockSpec(memory_space=pl.ANY)],
            out_specs=pl.BlockSpec((tm, d), row),
            scratch_shapes=[pltpu.VMEM((2, _sorted_rows(tm), d), BF16), pltpu.SemaphoreType.DMA((2,))]),
        out_shape=jax.ShapeDtypeStruct((t, d), F32),
        compiler_params=_params(1),
        name="combine",
    )(cnt, lo, dst, x2d, route, g, ys)


def router_params(rg_w, rg_b, re_w, re_b):
    depth = rg_w.shape[0]
    pad = LANE - N_GROUPS - N_EXPERTS
    wr = jnp.concatenate([rg_w, re_w, jnp.zeros((depth, D_MODEL, pad), F32)], axis=2)
    wr_hi = wr.astype(BF16)
    wr = jnp.stack([wr_hi, (wr - wr_hi.astype(F32)).astype(BF16)], axis=1)
    br = jnp.concatenate([rg_b, re_b, jnp.zeros((depth, pad), F32)], axis=1)[:, None, :]
    return wr, br


def moe(x2d, g2, wr, br, w_gate, w_up, w_down, layer, g_final, final, tm, bm):
    t = x2d.shape[0]
    nt = t // tm
    route, meta = router(x2d, g2, wr, br, layer, tm)
    meta = meta.reshape(nt, SUBLANE, LANE)[:, :, :N_EXPERTS].astype(I32)
    cnt, earlier, lo = meta[:, 0], meta[:, 1], meta[:, 2]
    counts = meta[nt - 1, 3]
    padded = (counts + bm - 1) // bm * bm
    ends = jnp.cumsum(padded)
    offs = ends - padded
    dst = offs[None, :] + earlier
    n_rows = -(-(2 * t + nt * N_EXPERTS * (RUN_ALIGN - 1)) // bm) * bm + N_EXPERTS * bm
    nused = (ends[-1] // bm).astype(I32)
    blk_start = jnp.arange(n_rows // bm, dtype=I32) * bm
    bexp = jnp.sum((ends[None, :] <= blk_start[:, None]).astype(I32), axis=1)
    last = jnp.sum((ends <= (nused - 1) * bm).astype(I32))
    bexp = jnp.minimum(bexp, last)
    cnt, lo, dst = cnt.reshape(-1), lo.reshape(-1), dst.reshape(-1)
    xs = dispatch(cnt, lo, dst, offs + counts, padded - counts, nused[None], x2d, g2, layer, route, n_rows,
                  tm, bm)
    ar = jnp.arange(N_EXPERTS, dtype=I32)
    active = padded > 0
    rank = jnp.cumsum(active.astype(I32)) - active.astype(I32)
    later = (ar[None, :] > ar[:, None]) & active[None, :]
    next_e = jnp.min(jnp.where(later, ar[None, :], N_EXPERTS), axis=1).astype(I32)
    ys = experts(bexp, nused[None], (rank % 2)[bexp], next_e[bexp], xs, w_gate, w_up, w_down, layer, bm)
    return combine(cnt, lo, dst, x2d, route, g_final[None, :], ys, tm, final)


def kernel(x, norm_mix_g, w_in, b_gate, pool_w, pool_scale, sgu_ln_g, sgu_ln_b, sgu_w, sgu_b, ssm_a_re, ssm_a_im, ssm_log_dt, ssm_b_re, ssm_b_im, ssm_c_re, ssm_c_im, ssm_d, glu_w, glu_b, w_branch, w_out, norm_ffn_g, router_group_w, router_group_b, router_expert_w, router_expert_b, exp_w_gate, exp_w_up, exp_w_down, norm_final_g):
    batch, seq, d = x.shape
    depth = w_in.shape[0]
    tm_in, tm_mix, tm_moe, bm = 1024, 512, 256, 512
    s5_seqs = 2
    x2d = x.reshape(batch * seq, d)
    ssm_col_block = 3
    row = lambda a: a[:, None, :]
    p = {
        "norm_mix_g": row(norm_mix_g), "w_in": w_in.astype(BF16), "b_gate": row(b_gate),
        "pool_w": pool_w.astype(BF16), "pool_scale": row(pool_scale),
        "sgu_ln_g": row(sgu_ln_g), "sgu_ln_b": row(sgu_ln_b), "sgu_w": sgu_w,
        "sgu_b_full": jnp.repeat(jnp.swapaxes(sgu_b, 1, 2), LANE, axis=2),
        "glu_w": glu_w.astype(BF16), "glu_b": row(glu_b),
        "w_branch": w_branch.astype(BF16), "w_out": w_out.astype(BF16),
    }
    bq, cq, kq, lam_q = jax.vmap(ssm_tables)(ssm_a_re, ssm_a_im, ssm_log_dt, ssm_b_re, ssm_b_im,
                                             ssm_c_re, ssm_c_im, ssm_d)
    wr, br = router_params(router_group_w, router_group_b, router_expert_w, router_expert_b)
    g_ffn = row(norm_ffn_g)
    for l in range(depth):
        xc4 = ssm_input(x2d, p["norm_mix_g"], p["w_in"], l, ssm_col_block, tm_in)
        ys4 = s5_scan(xc4, bq, cq, kq, lam_q, l, batch, s5_seqs)
        x2d = mixer(x2d, ys4, p, l, batch, tm_mix)
        x2d = moe(x2d, g_ffn, wr, br, exp_w_gate, exp_w_up, exp_w_down, l, norm_final_g,
                  l == depth - 1, tm_moe, bm)
    return x2d.reshape(batch, seq, d)
```

```python
import functools
import math

import jax
import jax.numpy as jnp
from jax import lax
from jax.experimental import pallas as pl
from jax.experimental.pallas import tpu as pltpu

F32 = jnp.float32
BF16 = jnp.bfloat16
I32 = jnp.int32

D_MODEL = 1024
POOL_WINDOWS = (2, 4, 8, 16)
POOL_HALO = 16
LANE = 128
SUBLANE = 8
RUN_ALIGN = 16
SGU_CHUNK = 128
SSM_GROUP_DIM = 16
SSM_STATE = 64
SSM_Q = 8
SSM_SLABS = 4
GROUPS_PER_SLAB = LANE // SSM_GROUP_DIM
N_GROUPS = 4
EXPERTS_PER_GROUP = 8
N_EXPERTS = 32
EPS = 1e-6
VMEM_LIMIT = 56 * 1024 * 1024


def _rms(x, g):
    return x * lax.rsqrt(jnp.mean(x * x, axis=-1, keepdims=True) + EPS) * g


def _gelu(x):
    c = math.sqrt(2.0 / math.pi)
    return x * (0.5 * (1.0 + jnp.tanh(c * (x + 0.044715 * (x * x * x)))))


def _sigmoid(x):
    return 1.0 / (1.0 + jnp.exp(-x))


def _dot(a, b):
    return jnp.dot(a, b, preferred_element_type=F32)


def _layer_spec(arr, layer):
    nd = arr.ndim - 1
    return pl.BlockSpec((None,) + arr.shape[1:], lambda *_: (layer,) + (0,) * nd, pipeline_mode=pl.Buffered(1))


def _params(n_axes):
    return pltpu.CompilerParams(dimension_semantics=("arbitrary",) * n_axes, vmem_limit_bytes=VMEM_LIMIT)


def _xc_kernel(x_ref, g_ref, w_ref, o_ref, xc_scr):
    h = _rms(x_ref[...], g_ref[...]).astype(BF16)
    xc = _dot(h, w_ref[...])
    rows = xc_scr.shape[1] // SSM_Q
    for j in range(SSM_SLABS):
        xc_scr[j] = xc[:, j * LANE:(j + 1) * LANE]
        for s in range(SSM_Q):
            o_ref[j, :, s * LANE:(s + 1) * LANE] = xc_scr[j, pl.ds(s, rows, stride=SSM_Q), :].astype(BF16)


def ssm_input(x2d, g, w_in_bf, layer, col_block, tm):
    t = x2d.shape[0]
    width = SSM_SLABS * LANE
    return pl.pallas_call(
        _xc_kernel,
        grid=(t // tm,),
        in_specs=[pl.BlockSpec((tm, D_MODEL), lambda i: (i, 0)),
                  _layer_spec(g, layer),
                  pl.BlockSpec((None, D_MODEL, width), lambda i: (layer, 0, col_block),
                               pipeline_mode=pl.Buffered(1))],
        out_specs=pl.BlockSpec((SSM_SLABS, tm // SSM_Q, SSM_Q * LANE), lambda i: (0, i, 0)),
        out_shape=jax.ShapeDtypeStruct((SSM_SLABS, t // SSM_Q, SSM_Q * LANE), BF16),
        scratch_shapes=[pltpu.VMEM((SSM_SLABS, tm, LANE), F32)],
        compiler_params=_params(1),
        name="ssm_input",
    )(x2d, g, w_in_bf)


def _compact(small):
    sl, r1, a, r2, c1, c2 = small.shape
    return small.reshape(sl, r1 * a * r2, c1 * c2).astype(BF16)


def ssm_tables(a_re, a_im, log_dt, b_re, b_im, c_re, c_im, d_skip):
    q = SSM_Q
    ng = a_re.shape[0]
    sl = ng // GROUPS_PER_SLAB
    dt = jnp.exp(log_dt)[:, None]
    k = jnp.arange(q + 1, dtype=F32)[:, None, None]
    mag = jnp.exp(a_re * dt * k)
    ang = a_im * dt * k
    pw_re, pw_im = mag * jnp.cos(ang), mag * jnp.sin(ang)
    nr, ni = pw_re[1] - 1.0, pw_im[1]
    den = a_re * a_re + a_im * a_im
    fr, fi = (nr * a_re + ni * a_im) / den, (ni * a_re - nr * a_im) / den
    bb_re = fr[..., None] * b_re - fi[..., None] * b_im
    bb_im = fr[..., None] * b_im + fi[..., None] * b_re

    def slabbed(a, axis):
        return a.reshape(*a.shape[:axis], sl, GROUPS_PER_SLAB, *a.shape[axis + 1:])

    k_rev = (q - 1.0) - jnp.arange(q, dtype=F32)[:, None, None]
    mag_rev = jnp.exp(a_re * dt * k_rev)
    rev_re, rev_im = mag_rev * jnp.cos(a_im * dt * k_rev), mag_rev * jnp.sin(a_im * dt * k_rev)
    m_re = rev_re[..., None] * bb_re[None] - rev_im[..., None] * bb_im[None]
    m_im = rev_re[..., None] * bb_im[None] + rev_im[..., None] * bb_re[None]
    m = slabbed(jnp.stack([m_re, m_im], axis=0), 2)
    bq = _compact(m.transpose(2, 1, 3, 5, 0, 4))
    pr, pi = pw_re[:, :, None, :], pw_im[:, :, None, :]
    cl_re = c_re[None] * pr - c_im[None] * pi
    cl_im = c_re[None] * pi + c_im[None] * pr
    n = slabbed(jnp.stack([cl_re[1:], -cl_im[1:]], axis=0), 2)
    cq = _compact(n.transpose(2, 0, 3, 5, 1, 4))
    hi = lax.Precision.HIGHEST
    kern = (jnp.einsum('kgcp,gpd->kgcd', cl_re[:q], bb_re, precision=hi)
            - jnp.einsum('kgcp,gpd->kgcd', cl_im[:q], bb_im, precision=hi))
    skip = d_skip.reshape(ng, SSM_GROUP_DIM)[:, :, None] * jnp.eye(SSM_GROUP_DIM, dtype=F32)[None]
    kern = kern + jnp.where(jnp.arange(q)[:, None, None, None] == 0, skip[None], 0.0)
    kst = jnp.stack([jnp.concatenate([jnp.zeros((s,) + kern.shape[1:], F32), kern[:q - s]], axis=0)
                     for s in range(q)], axis=0)
    kst = slabbed(kst, 2)
    kq = _compact(kst.transpose(2, 0, 3, 5, 1, 4))
    lam_q = jnp.concatenate([pw_re[q].reshape(sl, 1, -1), pw_im[q].reshape(sl, 1, -1)], axis=-1)
    return bq, cq, kq, lam_q


def _expand_block_diag(small, row_inner, col_inner):
    n_rows, n_small = small.shape
    n_cols = n_small * GROUPS_PER_SLAB
    g_bits = GROUPS_PER_SLAB.bit_length() - 1
    r_shift, c_shift = row_inner.bit_length() - 1, col_inner.bit_length() - 1
    k = lax.broadcasted_iota(I32, (n_small, n_cols), 0)
    col = lax.broadcasted_iota(I32, (n_small, n_cols), 1)
    src = ((col >> (c_shift + g_bits)) << c_shift) | (col & (col_inner - 1))
    spread = jnp.where(k == src, 1.0, 0.0).astype(BF16)
    r = lax.broadcasted_iota(I32, (n_rows, n_cols), 0)
    c = lax.broadcasted_iota(I32, (n_rows, n_cols), 1)
    same = ((r >> r_shift) & (GROUPS_PER_SLAB - 1)) == ((c >> c_shift) & (GROUPS_PER_SLAB - 1))
    return jnp.where(same, _dot(small, spread), 0.0).astype(BF16)


def _s5_kernel(u_ref, bq_ref, cq_ref, kq_ref, lam_ref, y_ref, g_scr, s_scr, t_scr, tbq, tcq, tkq, *, seqs):
    half = GROUPS_PER_SLAB * SSM_STATE
    rows = u_ref.shape[1]

    @pl.when(pl.program_id(1) == 0)
    def _():
        tbq[...] = _expand_block_diag(bq_ref[0], SSM_GROUP_DIM, SSM_STATE)
        tcq[...] = _expand_block_diag(cq_ref[0], SSM_STATE, SSM_GROUP_DIM)
        tkq[...] = _expand_block_diag(kq_ref[0], SSM_GROUP_DIM, SSM_GROUP_DIM)

    u = u_ref[0]
    g_scr[...] = _dot(u, tbq[...])
    lam = lam_ref[0]
    lre, lim = lam[:, :half], lam[:, half:]
    seq_rows = rows // seqs

    def body(n, carry):
        out = []
        for b in range(seqs):
            cre, cim = carry[2 * b], carry[2 * b + 1]
            row = b * seq_rows + n
            s_scr[pl.ds(row, 1), :] = jnp.concatenate([cre, cim], axis=1)
            g = g_scr[pl.ds(row, 1), :]
            out.append(lre * cre - lim * cim + g[:, :half])
            out.append(lre * cim + lim * cre + g[:, half:])
        return tuple(out)

    zero = jnp.zeros((1, half), F32)
    lax.fori_loop(0, seq_rows, body, (zero,) * (2 * seqs), unroll=2)
    y = _dot(s_scr[...].astype(BF16), tcq[...]) + _dot(u, tkq[...])
    for t in range(SSM_Q):
        t_scr[pl.ds(t, rows, stride=SSM_Q), :] = y[:, t * LANE:(t + 1) * LANE]
    y_ref[0] = t_scr[...].astype(BF16)


def s5_scan(u4, bq, cq, kq, lam_q, layer, batch, seqs):
    sl, rows_total, width = u4.shape
    rows = rows_total // batch * seqs
    tab = pl.BlockSpec((None, 1, width, width // GROUPS_PER_SLAB), lambda j, b: (layer, j, 0, 0))
    return pl.pallas_call(
        functools.partial(_s5_kernel, seqs=seqs),
        grid=(sl, batch // seqs),
        in_specs=[pl.BlockSpec((1, rows, width), lambda j, b: (j, b, 0)), tab, tab, tab,
                  pl.BlockSpec((None, 1, 1, width), lambda j, b: (layer, j, 0, 0))],
        out_specs=pl.BlockSpec((1, rows * SSM_Q, LANE), lambda j, b: (j, b, 0)),
        out_shape=jax.ShapeDtypeStruct((sl, rows_total * SSM_Q, LANE), BF16),
        scratch_shapes=[pltpu.VMEM((rows, width), F32), pltpu.VMEM((rows, width), F32),
                        pltpu.VMEM((rows * SSM_Q, LANE), F32)] + [pltpu.VMEM((width, width), BF16)] * 3,
        compiler_params=_params(2),
        name="s5_scan",
    )(u4, bq, cq, kq, lam_q)


def _mixer_kernel(x_ref, ys_ref, ng_ref, win_ref, bgate_ref, poolw_ref, pscale_ref, lng_ref, lnb_ref,
                  sguw_ref, sgub_ref, gluw_ref, glub_ref, wbr_ref, wout_ref, o_ref, xe_scr):
    tm = x_ref.shape[0]
    width = SSM_SLABS * LANE
    a_end, b_end = width, 3 * width
    c_end = b_end + width
    i = pl.program_id(1)
    x = x_ref[...]
    hb = _rms(x, ng_ref[...]).astype(BF16)

    xa = _dot(hb, win_ref[:, 0:a_end])

    @pl.when(i == 0)
    def _():
        xe_scr[0:POOL_HALO, :] = jnp.zeros((POOL_HALO, xe_scr.shape[1]), F32)

    xe_scr[POOL_HALO:POOL_HALO + tm, :] = xa
    pos = i * tm + lax.broadcasted_iota(I32, (tm, 1), 0)
    ya_parts = []
    for gi, w in enumerate(POOL_WINDOWS):
        ch = slice(gi * LANE, (gi + 1) * LANE)
        win = xa[:, ch]
        for j in range(1, w):
            win = win + xe_scr[POOL_HALO - j:POOL_HALO - j + tm, ch]
        cnt = jnp.minimum(pos + 1, w).astype(F32)
        diff = win / cnt - xa[:, ch]
        ya_parts.append(_dot(diff.astype(BF16), poolw_ref[gi]) * pscale_ref[:, ch])
    ya = jnp.concatenate(ya_parts, axis=1)
    xe_scr[0:POOL_HALO, :] = xe_scr[tm:tm + POOL_HALO, :]

    z = _gelu(_dot(hb, win_ref[:, a_end:b_end]))
    u, v = z[:, :width], z[:, width:]
    mu = jnp.mean(v, axis=-1, keepdims=True)
    vc = v - mu
    var = jnp.mean(vc * vc, axis=-1, keepdims=True)
    vb = (vc * lax.rsqrt(var + EPS) * lng_ref[...] + lnb_ref[...]).astype(BF16)
    r_i = lax.broadcasted_iota(I32, (SGU_CHUNK, SGU_CHUNK), 0)
    c_i = lax.broadcasted_iota(I32, (SGU_CHUNK, SGU_CHUNK), 1)
    n_heads = width // LANE
    wt = [jnp.where(r_i >= c_i, sguw_ref[hd], 0.0).astype(BF16) for hd in range(n_heads)]
    s_rows = []
    for c in range(tm // SGU_CHUNK):
        rs = slice(c * SGU_CHUNK, (c + 1) * SGU_CHUNK)
        s_rows.append(jnp.concatenate(
            [_dot(wt[hd], vb[rs, hd * LANE:(hd + 1) * LANE]) for hd in range(n_heads)], axis=1)
            + sgub_ref[...])
    yb = u * jnp.concatenate(s_rows, axis=0)

    yc = _gelu(jnp.concatenate([ys_ref[j] for j in range(SSM_SLABS)], axis=1).astype(F32))
    yc = yc * _sigmoid(_dot(yc.astype(BF16), gluw_ref[...]) + glub_ref[...])

    merged = None
    for k, yk in enumerate((ya, yb, yc)):
        gate = _sigmoid(_dot(hb, win_ref[:, c_end + k * D_MODEL:c_end + (k + 1) * D_MODEL])
                        + bgate_ref[:, k * D_MODEL:(k + 1) * D_MODEL])
        term = gate * _dot(yk.astype(BF16), wbr_ref[k])
        merged = term if merged is None else merged + term
    o_ref[...] = x + _dot(merged.astype(BF16), wout_ref[...])


def mixer(x2d, ys4, p, layer, batch, tm):
    t = x2d.shape[0]
    nt = t // batch // tm
    row = lambda b, i: (b * nt + i, 0)
    slab = lambda b, i: (0, b * nt + i, 0)
    consts = [p["norm_mix_g"], p["w_in"], p["b_gate"], p["pool_w"], p["pool_scale"], p["sgu_ln_g"],
              p["sgu_ln_b"], p["sgu_w"], p["sgu_b_full"], p["glu_w"], p["glu_b"], p["w_branch"], p["w_out"]]
    return pl.pallas_call(
        _mixer_kernel,
        grid=(batch, nt),
        in_specs=[pl.BlockSpec((tm, D_MODEL), row),
                  pl.BlockSpec((SSM_SLABS, tm, LANE), slab)] + [_layer_spec(c, layer) for c in consts],
        out_specs=pl.BlockSpec((tm, D_MODEL), row),
        out_shape=jax.ShapeDtypeStruct(x2d.shape, F32),
        scratch_shapes=[pltpu.VMEM((tm + POOL_HALO, SSM_SLABS * LANE), F32)],
        compiler_params=_params(2),
        name="mixer",
    )(x2d, ys4, *consts)


def _router_kernel(x_ref, g_ref, wr_ref, br_ref, route_ref, meta_ref, carry_scr):
    tm = x_ref.shape[0]
    i = pl.program_id(0)

    @pl.when(i == 0)
    def _():
        carry_scr[...] = jnp.zeros_like(carry_scr)

    h = _rms(x_ref[...], g_ref[...])
    h_hi = h.astype(BF16)
    h_lo = (h - h_hi.astype(F32)).astype(BF16)
    logits = (_dot(h_hi, wr_ref[0]) + _dot(h_hi, wr_ref[1]) + _dot(h_lo, wr_ref[0])) + br_ref[...]
    lane = lax.broadcasted_iota(I32, logits.shape, 1)
    neg = -jnp.inf
    gl = jnp.where(lane < N_GROUPS, logits, neg)
    gmax = jnp.max(gl, axis=-1, keepdims=True)
    gsum = jnp.sum(jnp.where(lane < N_GROUPS, jnp.exp(logits - gmax), 0.0), axis=-1, keepdims=True)
    g_val = 1.0 / gsum
    g_idx = jnp.min(jnp.where(gl == gmax, lane, LANE), axis=-1, keepdims=True)
    lo = N_GROUPS + EXPERTS_PER_GROUP * g_idx
    el = jnp.where((lane >= lo) & (lane < lo + EXPERTS_PER_GROUP), logits, neg)
    e1v = jnp.max(el, axis=-1, keepdims=True)
    e1l = jnp.min(jnp.where(el == e1v, lane, LANE), axis=-1, keepdims=True)
    el2 = jnp.where(lane == e1l, neg, el)
    e2v = jnp.max(el2, axis=-1, keepdims=True)
    e2l = jnp.min(jnp.where(el2 == e2v, lane, LANE), axis=-1, keepdims=True)
    tt = jnp.exp(e2v - e1v)
    w1 = g_val / (1.0 + tt)
    w2 = g_val * tt / (1.0 + tt)
    hit1, hit2 = lane == e1l - N_GROUPS, lane == e2l - N_GROUPS
    onehot = jnp.where(hit1 | hit2, 1.0, 0.0)
    r_i = lax.broadcasted_iota(I32, (tm, tm), 0)
    c_i = lax.broadcasted_iota(I32, (tm, tm), 1)
    tri = jnp.where(r_i > c_i, 1.0, 0.0).astype(BF16)
    prefix = _dot(tri, onehot.astype(BF16))
    count = jnp.sum(onehot, axis=0, keepdims=True)
    groups8 = jnp.floor((count + (RUN_ALIGN - 1.0)) * (1.0 / RUN_ALIGN))
    count = groups8 * RUN_ALIGN
    e_r = lax.broadcasted_iota(I32, (LANE, LANE), 0)
    e_c = lax.broadcasted_iota(I32, (LANE, LANE), 1)
    before = jnp.where(e_r < e_c, 1.0, 0.0).astype(BF16)
    run_start = _dot(jnp.broadcast_to(groups8, (SUBLANE, LANE)).astype(BF16), before)[0:1, :] * RUN_ALIGN
    local = prefix + run_start
    lp1 = jnp.sum(jnp.where(hit1, local, 0.0), axis=-1, keepdims=True)
    lp2 = jnp.sum(jnp.where(hit2, local, 0.0), axis=-1, keepdims=True)
    route_ref[...] = jnp.where(lane == 0, w1, jnp.where(lane == 1, w2, jnp.where(
        lane == 2, lp1, jnp.where(lane == 3, lp2, 0.0))))
    earlier = carry_scr[0:1, :]
    total = earlier + count
    carry_scr[...] = jnp.broadcast_to(total, carry_scr.shape)
    sub = lax.broadcasted_iota(I32, meta_ref.shape, 0)
    meta_ref[...] = jnp.where(sub == 0, count, jnp.where(sub == 1, earlier, jnp.where(
        sub == 2, run_start, jnp.where(sub == 3, total, 0.0))))


def router(x2d, g, wr, br, layer, tm):
    t = x2d.shape[0]
    row = lambda i: (i, 0)
    return pl.pallas_call(
        _router_kernel,
        grid=(t // tm,),
        in_specs=[pl.BlockSpec((tm, D_MODEL), row), _layer_spec(g, layer),
                  _layer_spec(wr, layer), _layer_spec(br, layer)],
        out_specs=[pl.BlockSpec((tm, LANE), row), pl.BlockSpec((SUBLANE, LANE), row)],
        out_shape=[jax.ShapeDtypeStruct((t, LANE), F32),
                   jax.ShapeDtypeStruct((t // tm * SUBLANE, LANE), F32)],
        scratch_shapes=[pltpu.VMEM((SUBLANE, LANE), F32)],
        compiler_params=_params(1),
        name="router",
    )(x2d, g, wr, br)


def _sort_matrix(route, k, n_sorted):
    col = lax.broadcasted_iota(I32, (route.shape[0], n_sorted), 1)
    return jnp.where(col == route[:, 2 + k:3 + k].astype(I32), 1.0, 0.0).astype(BF16)


def _sorted_rows(tm):
    return -(-(2 * tm + N_EXPERTS * (RUN_ALIGN - 1)) // LANE) * LANE


def _start_runs(cnt_ref, lo_ref, dst_ref, tile, make_copy):
    def body(e, c):
        idx = tile * N_EXPERTS + e
        n = cnt_ref[idx]

        @pl.when(n > 0)
        def _():
            make_copy(pl.multiple_of(lo_ref[idx], RUN_ALIGN), pl.multiple_of(dst_ref[idx], RUN_ALIGN),
                      pl.multiple_of(n, RUN_ALIGN)).start()
        return c
    lax.fori_loop(0, N_EXPERTS, body, 0)


def _wait_runs(tot_ref, tile, make_copy):
    make_copy(0, 0, pl.multiple_of(tot_ref[tile], RUN_ALIGN)).wait()


def _dispatch_kernel(cnt_ref, lo_ref, dst_ref, tot_ref, zst_ref, zln_ref, nused_ref, x_ref, g_ref, route_ref,
                     o_ref, srt_scr, zero_scr, sem, zsem):
    tm = x_ref.shape[0]
    bm = zero_scr.shape[0]
    i = pl.program_id(0)
    n = pl.num_programs(0)
    slot = i % 2

    def run_copy(s):
        return lambda lo, dst, cnt: pltpu.make_async_copy(
            srt_scr.at[s, pl.ds(lo, cnt)], o_ref.at[pl.ds(dst, cnt)], sem.at[s])

    def pad_copy(e):
        ln = pl.multiple_of(zln_ref[e], RUN_ALIGN)
        return pltpu.make_async_copy(zero_scr.at[pl.ds(0, ln)],
                                     o_ref.at[pl.ds(pl.multiple_of(zst_ref[e], RUN_ALIGN), ln)], zsem)

    def tail_copy(b):
        return pltpu.make_async_copy(zero_scr, o_ref.at[pl.ds(pl.multiple_of(b * bm, bm), bm)], zsem)

    @pl.when(i == 0)
    def _():
        zero_scr[...] = jnp.zeros_like(zero_scr)
        for wait in (False, True):
            def body(e, c):
                @pl.when(zln_ref[e] > 0)
                def _():
                    cp = pad_copy(e)
                    cp.wait() if wait else cp.start()
                return c
            lax.fori_loop(0, N_EXPERTS, body, 0)

            def tail(b, c):
                cp = tail_copy(b)
                cp.wait() if wait else cp.start()
                return c
            lax.fori_loop(nused_ref[0], o_ref.shape[0] // bm, tail, 0)

    @pl.when(i >= 2)
    def _():
        _wait_runs(tot_ref, i - 2, run_copy(slot))

    hb = _rms(x_ref[...], g_ref[...]).astype(BF16)
    route = route_ref[...]
    ns = srt_scr.shape[1]
    pt = jnp.maximum(_sort_matrix(route, 0, ns), _sort_matrix(route, 1, ns))
    srt_scr[slot] = lax.dot_general(pt, hb, (((0,), (0,)), ((), ())),
                                    preferred_element_type=F32).astype(BF16)
    _start_runs(cnt_ref, lo_ref, dst_ref, i, run_copy(slot))

    @pl.when(i == n - 1)
    def _():
        @pl.when(i >= 1)
        def _():
            _wait_runs(tot_ref, i - 1, run_copy(1 - slot))
        _wait_runs(tot_ref, i, run_copy(slot))


def dispatch(cnt, lo, dst, tot, zst, zln, nused, x2d, g, layer, route, n_rows, tm, bm):
    t, d = x2d.shape
    row = lambda i, *_: (i, 0)
    return pl.pallas_call(
        _dispatch_kernel,
        grid_spec=pltpu.PrefetchScalarGridSpec(
            num_scalar_prefetch=7,
            grid=(t // tm,),
            in_specs=[pl.BlockSpec((tm, d), row), pl.BlockSpec((None, 1, d), lambda i, *_: (layer, 0, 0)),
                      pl.BlockSpec((tm, LANE), row)],
            out_specs=pl.BlockSpec(memory_space=pl.ANY),
            scratch_shapes=[pltpu.VMEM((2, _sorted_rows(tm), d), BF16), pltpu.VMEM((bm, d), BF16),
                            pltpu.SemaphoreType.DMA((2,)), pltpu.SemaphoreType.DMA(())]),
        out_shape=jax.ShapeDtypeStruct((n_rows, d), BF16),
        compiler_params=_params(1),
        name="dispatch",
    )(cnt, lo, dst, tot, zst, zln, nused, x2d, g, route)


def _expert_kernel(bexp_ref, nused_ref, slot_ref, next_ref, xs_ref, wg_hbm, wu_hbm, wd_hbm, ys_ref,
                   wg_buf, wu_buf, wd_buf, wg_scr, wu_scr, wd_scr, sem, *, layer):
    i = pl.program_id(0)

    def weight_copies(e, s):
        return [pltpu.make_async_copy(hbm.at[layer, e], buf.at[s], sem.at[s])
                for hbm, buf in ((wg_hbm, wg_buf), (wu_hbm, wu_buf), (wd_hbm, wd_buf))]

    @pl.when(i < nused_ref[0])
    def _():
        e, s = bexp_ref[i], slot_ref[i]
        changed = jnp.logical_or(i == 0, e != bexp_ref[jnp.maximum(i - 1, 0)])

        @pl.when(i == 0)
        def _():
            for cp in weight_copies(e, s):
                cp.start()

        @pl.when(changed)
        def _():
            for cp in weight_copies(e, s):
                cp.wait()
            wg_scr[...] = wg_buf[s].astype(BF16)
            wu_scr[...] = wu_buf[s].astype(BF16)
            wd_scr[...] = wd_buf[s].astype(BF16)

            @pl.when(next_ref[i] < N_EXPERTS)
            def _():
                for cp in weight_copies(next_ref[i], 1 - s):
                    cp.start()

        xb = xs_ref[...]
        hg = _dot(xb, wg_scr[...])
        hu = _dot(xb, wu_scr[...])
        act = (hg * _sigmoid(hg) * hu).astype(BF16)
        ys_ref[...] = _dot(act, wd_scr[...]).astype(BF16)

    @pl.when(i >= nused_ref[0])
    def _():
        ys_ref[...] = jnp.zeros_like(ys_ref)


def experts(bexp, nused, slot, nxt, xs, w_gate, w_up, w_down, layer, bm):
    n_rows, d = xs.shape
    de = w_gate.shape[3]
    anyspace = pl.BlockSpec(memory_space=pl.ANY)
    return pl.pallas_call(
        functools.partial(_expert_kernel, layer=layer),
        grid_spec=pltpu.PrefetchScalarGridSpec(
            num_scalar_prefetch=4,
            grid=(n_rows // bm,),
            in_specs=[pl.BlockSpec((bm, d), lambda i, bexp, nused, *_: (jnp.minimum(i, nused[0] - 1), 0)),
                      anyspace, anyspace, anyspace],
            out_specs=pl.BlockSpec((bm, d), lambda i, *_: (i, 0)),
            scratch_shapes=[pltpu.VMEM((2, d, de), F32), pltpu.VMEM((2, d, de), F32), pltpu.VMEM((2, de, d), F32),
                            pltpu.VMEM((d, de), BF16), pltpu.VMEM((d, de), BF16), pltpu.VMEM((de, d), BF16),
                            pltpu.SemaphoreType.DMA((2,))]),
        out_shape=jax.ShapeDtypeStruct(xs.shape, BF16),
        compiler_params=_params(1),
        name="experts",
    )(bexp, nused, slot, nxt, xs, w_gate, w_up, w_down)


def _combine_kernel(cnt_ref, lo_ref, dst_ref, tot_ref, x_ref, route_ref, g_ref, ys_ref, o_ref, buf, sem, *,
                    final):
    tm = x_ref.shape[0]
    i = pl.program_id(0)
    n = pl.num_programs(0)
    slot = i % 2

    def run_copy(s):
        return lambda lo, dst, cnt: pltpu.make_async_copy(
            ys_ref.at[pl.ds(dst, cnt)], buf.at[s, pl.ds(lo, cnt)], sem.at[s])

    @pl.when(i == 0)
    def _():
        buf[...] = jnp.zeros_like(buf)
        _start_runs(cnt_ref, lo_ref, dst_ref, 0, run_copy(0))

    @pl.when(i + 1 < n)
    def _():
        _start_runs(cnt_ref, lo_ref, dst_ref, i + 1, run_copy(1 - slot))

    _wait_runs(tot_ref, i, run_copy(slot))
    ysb = buf[slot]
    route = route_ref[...]
    out = x_ref[...]
    for k in range(2):
        out = out + route[:, k:k + 1] * _dot(_sort_matrix(route, k, buf.shape[1]), ysb)
    if final:
        out = _rms(out, g_ref[...])
    o_ref[...] = out


def combine(cnt, lo, dst, tot, x2d, route, g, ys, tm, final):
    t, d = x2d.shape
    row = lambda i, *_: (i, 0)
    return pl.pallas_call(
        functools.partial(_combine_kernel, final=final),
        grid_spec=pltpu.PrefetchScalarGridSpec(
            num_scalar_prefetch=4,
            grid=(t // tm,),
            in_specs=[pl.BlockSpec((tm, d), row), pl.BlockSpec((tm, LANE), row),
                      pl.BlockSpec((1, d), lambda i, *_: (0, 0)),
                      pl.BlockSpec(memory_space=pl.ANY)],
            out_specs=pl.BlockSpec((tm, d), row),
            scratch_shapes=[pltpu.VMEM((2, _sorted_rows(tm), d), BF16), pltpu.SemaphoreType.DMA((2,))]),
        out_shape=jax.ShapeDtypeStruct((t, d), F32),
        compiler_params=_params(1),
        name="combine",
    )(cnt, lo, dst, tot, x2d, route, g, ys)


def router_params(rg_w, rg_b, re_w, re_b):
    depth = rg_w.shape[0]
    pad = LANE - N_GROUPS - N_EXPERTS
    wr = jnp.concatenate([rg_w, re_w, jnp.zeros((depth, D_MODEL, pad), F32)], axis=2)
    wr_hi = wr.astype(BF16)
    wr = jnp.stack([wr_hi, (wr - wr_hi.astype(F32)).astype(BF16)], axis=1)
    br = jnp.concatenate([rg_b, re_b, jnp.zeros((depth, pad), F32)], axis=1)[:, None, :]
    return wr, br


def moe(x2d, g2, wr, br, w_gate, w_up, w_down, layer, g_final, final, tm, bm):
    t = x2d.shape[0]
    nt = t // tm
    route, meta = router(x2d, g2, wr, br, layer, tm)
    meta = meta.reshape(nt, SUBLANE, LANE)[:, :, :N_EXPERTS].astype(I32)
    cnt, earlier, lo = meta[:, 0], meta[:, 1], meta[:, 2]
    counts = meta[nt - 1, 3]
    padded = (counts + bm - 1) // bm * bm
    ends = jnp.cumsum(padded)
    offs = ends - padded
    dst = offs[None, :] + earlier
    n_rows = -(-(2 * t + nt * N_EXPERTS * (RUN_ALIGN - 1)) // bm) * bm + N_EXPERTS * bm
    nused = (ends[-1] // bm).astype(I32)
    blk_start = jnp.arange(n_rows // bm, dtype=I32) * bm
    bexp = jnp.sum((ends[None, :] <= blk_start[:, None]).astype(I32), axis=1)
    last = jnp.sum((ends <= (nused - 1) * bm).astype(I32))
    bexp = jnp.minimum(bexp, last)
    tot = jnp.sum(cnt, axis=1)
    cnt, lo, dst = cnt.reshape(-1), lo.reshape(-1), dst.reshape(-1)
    xs = dispatch(cnt, lo, dst, tot, offs + counts, padded - counts, nused[None], x2d, g2, layer, route,
                  n_rows, tm, bm)
    ar = jnp.arange(N_EXPERTS, dtype=I32)
    active = padded > 0
    rank = jnp.cumsum(active.astype(I32)) - active.astype(I32)
    later = (ar[None, :] > ar[:, None]) & active[None, :]
    next_e = jnp.min(jnp.where(later, ar[None, :], N_EXPERTS), axis=1).astype(I32)
    ys = experts(bexp, nused[None], (rank % 2)[bexp], next_e[bexp], xs, w_gate, w_up, w_down, layer, bm)
    return combine(cnt, lo, dst, tot, x2d, route, g_final[None, :], ys, tm, final)


def kernel(x, norm_mix_g, w_in, b_gate, pool_w, pool_scale, sgu_ln_g, sgu_ln_b, sgu_w, sgu_b, ssm_a_re, ssm_a_im, ssm_log_dt, ssm_b_re, ssm_b_im, ssm_c_re, ssm_c_im, ssm_d, glu_w, glu_b, w_branch, w_out, norm_ffn_g, router_group_w, router_group_b, router_expert_w, router_expert_b, exp_w_gate, exp_w_up, exp_w_down, norm_final_g):
    batch, seq, d = x.shape
    depth = w_in.shape[0]
    tm_in, tm_mix, tm_moe, bm = 1024, 512, 256, 512
    s5_seqs = 2
    x2d = x.reshape(batch * seq, d)
    ssm_col_block = 3
    row = lambda a: a[:, None, :]
    p = {
        "norm_mix_g": row(norm_mix_g), "w_in": w_in.astype(BF16), "b_gate": row(b_gate),
        "pool_w": pool_w.astype(BF16), "pool_scale": row(pool_scale),
        "sgu_ln_g": row(sgu_ln_g), "sgu_ln_b": row(sgu_ln_b), "sgu_w": sgu_w,
        "sgu_b_full": jnp.repeat(jnp.swapaxes(sgu_b, 1, 2), LANE, axis=2),
        "glu_w": glu_w.astype(BF16), "glu_b": row(glu_b),
        "w_branch": w_branch.astype(BF16), "w_out": w_out.astype(BF16),
    }
    bq, cq, kq, lam_q = jax.vmap(ssm_tables)(ssm_a_re, ssm_a_im, ssm_log_dt, ssm_b_re, ssm_b_im,
                                             ssm_c_re, ssm_c_im, ssm_d)
    wr, br = router_params(router_group_w, router_group_b, router_expert_w, router_expert_b)
    g_ffn = row(norm_ffn_g)
    for l in range(depth):
        xc4 = ssm_input(x2d, p["norm_mix_g"], p["w_in"], l, ssm_col_block, tm_in)
        ys4 = s5_scan(xc4, bq, cq, kq, lam_q, l, batch, s5_seqs)
        x2d = mixer(x2d, ys4, p, l, batch, tm_mix)
        x2d = moe(x2d, g_ffn, wr, br, exp_w_gate, exp_w_up, exp_w_down, l, norm_final_g,
                  l == depth - 1, tm_moe, bm)
    return x2d.reshape(batch, seq, d)
```

```python
import functools
import math

import jax
import jax.numpy as jnp
from jax import lax
from jax.experimental import pallas as pl
from jax.experimental.pallas import tpu as pltpu

F32 = jnp.float32
BF16 = jnp.bfloat16
I32 = jnp.int32

D_MODEL = 1024
POOL_WINDOWS = (2, 4, 8, 16)
POOL_HALO = 16
LANE = 128
SUBLANE = 8
RUN_ALIGN = 16
SGU_CHUNK = 128
SSM_GROUP_DIM = 16
SSM_STATE = 64
SSM_Q = 8
SSM_SLABS = 4
GROUPS_PER_SLAB = LANE // SSM_GROUP_DIM
N_GROUPS = 4
EXPERTS_PER_GROUP = 8
N_EXPERTS = 32
EPS = 1e-6
VMEM_LIMIT = 56 * 1024 * 1024


def _rms(x, g):
    return x * lax.rsqrt(jnp.mean(x * x, axis=-1, keepdims=True) + EPS) * g


def _gelu(x):
    c = math.sqrt(2.0 / math.pi)
    return x * (0.5 * (1.0 + jnp.tanh(c * (x + 0.044715 * (x * x * x)))))


def _sigmoid(x):
    return 1.0 / (1.0 + jnp.exp(-x))


def _dot(a, b):
    return jnp.dot(a, b, preferred_element_type=F32)


def _layer_spec(arr, layer):
    nd = arr.ndim - 1
    return pl.BlockSpec((None,) + arr.shape[1:], lambda *_: (layer,) + (0,) * nd, pipeline_mode=pl.Buffered(1))


def _params(n_axes):
    return pltpu.CompilerParams(dimension_semantics=("arbitrary",) * n_axes, vmem_limit_bytes=VMEM_LIMIT)


def _xc_kernel(x_ref, g_ref, w_ref, o_ref, xc_scr):
    h = _rms(x_ref[...], g_ref[...]).astype(BF16)
    xc = _dot(h, w_ref[...])
    rows = xc_scr.shape[1] // SSM_Q
    for j in range(SSM_SLABS):
        xc_scr[j] = xc[:, j * LANE:(j + 1) * LANE]
        for s in range(SSM_Q):
            o_ref[j, :, s * LANE:(s + 1) * LANE] = xc_scr[j, pl.ds(s, rows, stride=SSM_Q), :].astype(BF16)


def ssm_input(x2d, g, w_in_bf, layer, col_block, tm):
    t = x2d.shape[0]
    width = SSM_SLABS * LANE
    return pl.pallas_call(
        _xc_kernel,
        grid=(t // tm,),
        in_specs=[pl.BlockSpec((tm, D_MODEL), lambda i: (i, 0)),
                  _layer_spec(g, layer),
                  pl.BlockSpec((None, D_MODEL, width), lambda i: (layer, 0, col_block),
                               pipeline_mode=pl.Buffered(1))],
        out_specs=pl.BlockSpec((SSM_SLABS, tm // SSM_Q, SSM_Q * LANE), lambda i: (0, i, 0)),
        out_shape=jax.ShapeDtypeStruct((SSM_SLABS, t // SSM_Q, SSM_Q * LANE), BF16),
        scratch_shapes=[pltpu.VMEM((SSM_SLABS, tm, LANE), F32)],
        compiler_params=_params(1),
        name="ssm_input",
    )(x2d, g, w_in_bf)


def ssm_tables(a_re, a_im, log_dt, b_re, b_im, c_re, c_im, d_skip):
    q, nc = SSM_Q, SSM_GROUP_DIM
    ng = a_re.shape[0]
    sl = ng // GROUPS_PER_SLAB
    dt = jnp.exp(log_dt)[:, None]

    def powers(kvec):
        kk = kvec[:, None, None]
        mag = jnp.exp(a_re * dt * kk)
        return mag * jnp.cos(a_im * dt * kk), mag * jnp.sin(a_im * dt * kk)

    pw_re, pw_im = powers(jnp.arange(q + 1, dtype=F32))
    nr, ni = pw_re[1] - 1.0, pw_im[1]
    den = a_re * a_re + a_im * a_im
    fr, fi = (nr * a_re + ni * a_im) / den, (ni * a_re - nr * a_im) / den
    bb_re = fr[..., None] * b_re - fi[..., None] * b_im
    bb_im = fr[..., None] * b_im + fi[..., None] * b_re

    def to_slab_rows(a, lead):
        a = a.reshape(lead, sl, GROUPS_PER_SLAB, a.shape[2], a.shape[3]).transpose(1, 0, 2, 3, 4)
        return a.reshape(sl, -1, a.shape[-1]).astype(BF16)

    rev_re, rev_im = powers((q - 1.0) - jnp.arange(q, dtype=F32))
    bt_re, bt_im = bb_re.transpose(0, 2, 1), bb_im.transpose(0, 2, 1)
    m_re = rev_re[:, :, None, :] * bt_re[None] - rev_im[:, :, None, :] * bt_im[None]
    m_im = rev_re[:, :, None, :] * bt_im[None] + rev_im[:, :, None, :] * bt_re[None]
    bq = to_slab_rows(jnp.concatenate([m_re, m_im], axis=-1), q)
    ct_re, ct_im = c_re.transpose(0, 2, 1), c_im.transpose(0, 2, 1)
    ct_re, ct_im = jnp.tile(ct_re, (1, 1, q + 1)), jnp.tile(ct_im, (1, 1, q + 1))
    pl_re = jnp.repeat(pw_re.transpose(1, 2, 0), nc, axis=-1)
    pl_im = jnp.repeat(pw_im.transpose(1, 2, 0), nc, axis=-1)
    r_re = ct_re * pl_re - ct_im * pl_im
    r_im = ct_re * pl_im + ct_im * pl_re
    cq = to_slab_rows(jnp.stack([r_re[..., nc:], -r_im[..., nc:]], axis=0), 2)
    hi = lax.Precision.HIGHEST
    kern = (jnp.einsum('gpd,gpl->gdl', bb_re, r_re[..., :q * nc], precision=hi)
            - jnp.einsum('gpd,gpl->gdl', bb_im, r_im[..., :q * nc], precision=hi))
    lane = jnp.arange(q * nc)[None, None, :]
    d_pad = jnp.pad(d_skip.reshape(ng, nc), ((0, 0), (0, (q - 1) * nc)))
    kern = kern + jnp.where(lane == jnp.arange(nc)[None, :, None], d_pad[:, None, :], 0.0)
    kq = jnp.stack([jnp.pad(kern[..., :(q - s) * nc], ((0, 0), (0, 0), (s * nc, 0))) for s in range(q)], axis=0)
    kq = to_slab_rows(kq, q)
    lam_q = jnp.concatenate([pw_re[q].reshape(sl, 1, -1), pw_im[q].reshape(sl, 1, -1)], axis=-1)
    return bq, cq, kq, lam_q


def _expand_block_diag(small, row_inner, col_inner):
    n_rows, n_small = small.shape
    n_cols = n_small * GROUPS_PER_SLAB
    g_bits = GROUPS_PER_SLAB.bit_length() - 1
    r_shift, c_shift = row_inner.bit_length() - 1, col_inner.bit_length() - 1
    k = lax.broadcasted_iota(I32, (n_small, n_cols), 0)
    col = lax.broadcasted_iota(I32, (n_small, n_cols), 1)
    src = ((col >> (c_shift + g_bits)) << c_shift) | (col & (col_inner - 1))
    spread = jnp.where(k == src, 1.0, 0.0).astype(BF16)
    r = lax.broadcasted_iota(I32, (n_rows, n_cols), 0)
    c = lax.broadcasted_iota(I32, (n_rows, n_cols), 1)
    same = ((r >> r_shift) & (GROUPS_PER_SLAB - 1)) == ((c >> c_shift) & (GROUPS_PER_SLAB - 1))
    return jnp.where(same, _dot(small, spread), 0.0).astype(BF16)


def _s5_kernel(u_ref, bq_ref, cq_ref, kq_ref, lam_ref, y_ref, g_scr, s_scr, t_scr, tbq, tcq, tkq, *, seqs):
    half = GROUPS_PER_SLAB * SSM_STATE
    rows = u_ref.shape[1]

    @pl.when(pl.program_id(1) == 0)
    def _():
        tbq[...] = _expand_block_diag(bq_ref[0], SSM_GROUP_DIM, SSM_STATE)
        tcq[...] = _expand_block_diag(cq_ref[0], SSM_STATE, SSM_GROUP_DIM)
        tkq[...] = _expand_block_diag(kq_ref[0], SSM_GROUP_DIM, SSM_GROUP_DIM)

    u = u_ref[0]
    g_scr[...] = _dot(u, tbq[...])
    lam = lam_ref[0]
    lre, lim = lam[:, :half], lam[:, half:]
    seq_rows = rows // seqs

    def body(n, carry):
        out = []
        for b in range(seqs):
            cre, cim = carry[2 * b], carry[2 * b + 1]
            row = b * seq_rows + n
            s_scr[pl.ds(row, 1), :] = jnp.concatenate([cre, cim], axis=1)
            g = g_scr[pl.ds(row, 1), :]
            out.append(lre * cre - lim * cim + g[:, :half])
            out.append(lre * cim + lim * cre + g[:, half:])
        return tuple(out)

    zero = jnp.zeros((1, half), F32)
    lax.fori_loop(0, seq_rows, body, (zero,) * (2 * seqs), unroll=2)
    y = _dot(s_scr[...].astype(BF16), tcq[...]) + _dot(u, tkq[...])
    for t in range(SSM_Q):
        t_scr[pl.ds(t, rows, stride=SSM_Q), :] = y[:, t * LANE:(t + 1) * LANE]
    y_ref[0] = t_scr[...].astype(BF16)


def s5_scan(u4, bq, cq, kq, lam_q, layer, batch, seqs):
    sl, rows_total, width = u4.shape
    rows = rows_total // batch * seqs
    tab = pl.BlockSpec((None, 1, width, width // GROUPS_PER_SLAB), lambda j, b: (layer, j, 0, 0))
    return pl.pallas_call(
        functools.partial(_s5_kernel, seqs=seqs),
        grid=(sl, batch // seqs),
        in_specs=[pl.BlockSpec((1, rows, width), lambda j, b: (j, b, 0)), tab, tab, tab,
                  pl.BlockSpec((None, 1, 1, width), lambda j, b: (layer, j, 0, 0))],
        out_specs=pl.BlockSpec((1, rows * SSM_Q, LANE), lambda j, b: (j, b, 0)),
        out_shape=jax.ShapeDtypeStruct((sl, rows_total * SSM_Q, LANE), BF16),
        scratch_shapes=[pltpu.VMEM((rows, width), F32), pltpu.VMEM((rows, width), F32),
                        pltpu.VMEM((rows * SSM_Q, LANE), F32)] + [pltpu.VMEM((width, width), BF16)] * 3,
        compiler_params=_params(2),
        name="s5_scan",
    )(u4, bq, cq, kq, lam_q)


def _mixer_kernel(x_ref, ys_ref, ng_ref, win_ref, bgate_ref, poolw_ref, pscale_ref, lng_ref, lnb_ref,
                  sguw_ref, sgub_ref, gluw_ref, glub_ref, wbr_ref, wout_ref, o_ref, xe_scr):
    tm = x_ref.shape[0]
    width = SSM_SLABS * LANE
    a_end, b_end = width, 3 * width
    c_end = b_end + width
    i = pl.program_id(1)
    x = x_ref[...]
    hb = _rms(x, ng_ref[...]).astype(BF16)

    xa = _dot(hb, win_ref[:, 0:a_end])

    @pl.when(i == 0)
    def _():
        xe_scr[0:POOL_HALO, :] = jnp.zeros((POOL_HALO, xe_scr.shape[1]), F32)

    xe_scr[POOL_HALO:POOL_HALO + tm, :] = xa
    pos = i * tm + lax.broadcasted_iota(I32, (tm, 1), 0)
    ya_parts = []
    for gi, w in enumerate(POOL_WINDOWS):
        ch = slice(gi * LANE, (gi + 1) * LANE)
        win = xa[:, ch]
        for j in range(1, w):
            win = win + xe_scr[POOL_HALO - j:POOL_HALO - j + tm, ch]
        cnt = jnp.minimum(pos + 1, w).astype(F32)
        diff = win / cnt - xa[:, ch]
        ya_parts.append(_dot(diff.astype(BF16), poolw_ref[gi]) * pscale_ref[:, ch])
    ya = jnp.concatenate(ya_parts, axis=1)
    xe_scr[0:POOL_HALO, :] = xe_scr[tm:tm + POOL_HALO, :]

    z = _gelu(_dot(hb, win_ref[:, a_end:b_end]))
    u, v = z[:, :width], z[:, width:]
    mu = jnp.mean(v, axis=-1, keepdims=True)
    vc = v - mu
    var = jnp.mean(vc * vc, axis=-1, keepdims=True)
    vb = (vc * lax.rsqrt(var + EPS) * lng_ref[...] + lnb_ref[...]).astype(BF16)
    r_i = lax.broadcasted_iota(I32, (SGU_CHUNK, SGU_CHUNK), 0)
    c_i = lax.broadcasted_iota(I32, (SGU_CHUNK, SGU_CHUNK), 1)
    n_heads = width // LANE
    wt = [jnp.where(r_i >= c_i, sguw_ref[hd], 0.0).astype(BF16) for hd in range(n_heads)]
    s_rows = []
    for c in range(tm // SGU_CHUNK):
        rs = slice(c * SGU_CHUNK, (c + 1) * SGU_CHUNK)
        s_rows.append(jnp.concatenate(
            [_dot(wt[hd], vb[rs, hd * LANE:(hd + 1) * LANE]) for hd in range(n_heads)], axis=1)
            + sgub_ref[...])
    yb = u * jnp.concatenate(s_rows, axis=0)

    yc = _gelu(jnp.concatenate([ys_ref[j] for j in range(SSM_SLABS)], axis=1).astype(F32))
    yc = yc * _sigmoid(_dot(yc.astype(BF16), gluw_ref[...]) + glub_ref[...])

    merged = None
    for k, yk in enumerate((ya, yb, yc)):
        gate = _sigmoid(_dot(hb, win_ref[:, c_end + k * D_MODEL:c_end + (k + 1) * D_MODEL])
                        + bgate_ref[:, k * D_MODEL:(k + 1) * D_MODEL])
        term = gate * _dot(yk.astype(BF16), wbr_ref[k])
        merged = term if merged is None else merged + term
    o_ref[...] = x + _dot(merged.astype(BF16), wout_ref[...])


def mixer(x2d, ys4, p, layer, batch, tm):
    t = x2d.shape[0]
    nt = t // batch // tm
    row = lambda b, i: (b * nt + i, 0)
    slab = lambda b, i: (0, b * nt + i, 0)
    consts = [p["norm_mix_g"], p["w_in"], p["b_gate"], p["pool_w"], p["pool_scale"], p["sgu_ln_g"],
              p["sgu_ln_b"], p["sgu_w"], p["sgu_b_full"], p["glu_w"], p["glu_b"], p["w_branch"], p["w_out"]]
    return pl.pallas_call(
        _mixer_kernel,
        grid=(batch, nt),
        in_specs=[pl.BlockSpec((tm, D_MODEL), row),
                  pl.BlockSpec((SSM_SLABS, tm, LANE), slab)] + [_layer_spec(c, layer) for c in consts],
        out_specs=pl.BlockSpec((tm, D_MODEL), row),
        out_shape=jax.ShapeDtypeStruct(x2d.shape, F32),
        scratch_shapes=[pltpu.VMEM((tm + POOL_HALO, SSM_SLABS * LANE), F32)],
        compiler_params=_params(2),
        name="mixer",
    )(x2d, ys4, *consts)


def _router_kernel(x_ref, g_ref, wr_ref, br_ref, route_ref, meta_ref, carry_scr):
    tm = x_ref.shape[0]
    i = pl.program_id(0)

    @pl.when(i == 0)
    def _():
        carry_scr[...] = jnp.zeros_like(carry_scr)

    h = _rms(x_ref[...], g_ref[...])
    h_hi = h.astype(BF16)
    h_lo = (h - h_hi.astype(F32)).astype(BF16)
    logits = (_dot(h_hi, wr_ref[0]) + _dot(h_hi, wr_ref[1]) + _dot(h_lo, wr_ref[0])) + br_ref[...]
    lane = lax.broadcasted_iota(I32, logits.shape, 1)
    neg = -jnp.inf
    gl = jnp.where(lane < N_GROUPS, logits, neg)
    gmax = jnp.max(gl, axis=-1, keepdims=True)
    gsum = jnp.sum(jnp.where(lane < N_GROUPS, jnp.exp(logits - gmax), 0.0), axis=-1, keepdims=True)
    g_val = 1.0 / gsum
    g_idx = jnp.min(jnp.where(gl == gmax, lane, LANE), axis=-1, keepdims=True)
    lo = N_GROUPS + EXPERTS_PER_GROUP * g_idx
    el = jnp.where((lane >= lo) & (lane < lo + EXPERTS_PER_GROUP), logits, neg)
    e1v = jnp.max(el, axis=-1, keepdims=True)
    e1l = jnp.min(jnp.where(el == e1v, lane, LANE), axis=-1, keepdims=True)
    el2 = jnp.where(lane == e1l, neg, el)
    e2v = jnp.max(el2, axis=-1, keepdims=True)
    e2l = jnp.min(jnp.where(el2 == e2v, lane, LANE), axis=-1, keepdims=True)
    tt = jnp.exp(e2v - e1v)
    w1 = g_val / (1.0 + tt)
    w2 = g_val * tt / (1.0 + tt)
    hit1, hit2 = lane == e1l - N_GROUPS, lane == e2l - N_GROUPS
    onehot = jnp.where(hit1 | hit2, 1.0, 0.0)
    r_i = lax.broadcasted_iota(I32, (tm, tm), 0)
    c_i = lax.broadcasted_iota(I32, (tm, tm), 1)
    tri = jnp.where(r_i > c_i, 1.0, 0.0).astype(BF16)
    prefix = _dot(tri, onehot.astype(BF16))
    count = jnp.sum(onehot, axis=0, keepdims=True)
    groups8 = jnp.floor((count + (RUN_ALIGN - 1.0)) * (1.0 / RUN_ALIGN))
    count = groups8 * RUN_ALIGN
    e_r = lax.broadcasted_iota(I32, (LANE, LANE), 0)
    e_c = lax.broadcasted_iota(I32, (LANE, LANE), 1)
    before = jnp.where(e_r < e_c, 1.0, 0.0).astype(BF16)
    run_start = _dot(jnp.broadcast_to(groups8, (SUBLANE, LANE)).astype(BF16), before)[0:1, :] * RUN_ALIGN
    local = prefix + run_start
    lp1 = jnp.sum(jnp.where(hit1, local, 0.0), axis=-1, keepdims=True)
    lp2 = jnp.sum(jnp.where(hit2, local, 0.0), axis=-1, keepdims=True)
    route_ref[...] = jnp.where(lane == 0, w1, jnp.where(lane == 1, w2, jnp.where(
        lane == 2, lp1, jnp.where(lane == 3, lp2, 0.0))))
    earlier = carry_scr[0:1, :]
    total = earlier + count
    carry_scr[...] = jnp.broadcast_to(total, carry_scr.shape)
    sub = lax.broadcasted_iota(I32, meta_ref.shape, 0)
    meta_ref[...] = jnp.where(sub == 0, count, jnp.where(sub == 1, earlier, jnp.where(
        sub == 2, run_start, jnp.where(sub == 3, total, 0.0))))


def router(x2d, g, wr, br, layer, tm):
    t = x2d.shape[0]
    row = lambda i: (i, 0)
    return pl.pallas_call(
        _router_kernel,
        grid=(t // tm,),
        in_specs=[pl.BlockSpec((tm, D_MODEL), row), _layer_spec(g, layer),
                  _layer_spec(wr, layer), _layer_spec(br, layer)],
        out_specs=[pl.BlockSpec((tm, LANE), row), pl.BlockSpec((SUBLANE, LANE), row)],
        out_shape=[jax.ShapeDtypeStruct((t, LANE), F32),
                   jax.ShapeDtypeStruct((t // tm * SUBLANE, LANE), F32)],
        scratch_shapes=[pltpu.VMEM((SUBLANE, LANE), F32)],
        compiler_params=_params(1),
        name="router",
    )(x2d, g, wr, br)


def _sort_matrix(route, k, n_sorted):
    col = lax.broadcasted_iota(I32, (route.shape[0], n_sorted), 1)
    return jnp.where(col == route[:, 2 + k:3 + k].astype(I32), 1.0, 0.0).astype(BF16)


def _sorted_rows(tm):
    return -(-(2 * tm + N_EXPERTS * (RUN_ALIGN - 1)) // LANE) * LANE


def _start_runs(cnt_ref, lo_ref, dst_ref, tile, make_copy):
    def body(e, c):
        idx = tile * N_EXPERTS + e
        n = cnt_ref[idx]

        @pl.when(n > 0)
        def _():
            make_copy(pl.multiple_of(lo_ref[idx], RUN_ALIGN), pl.multiple_of(dst_ref[idx], RUN_ALIGN),
                      pl.multiple_of(n, RUN_ALIGN)).start()
        return c
    lax.fori_loop(0, N_EXPERTS, body, 0)


def _wait_runs(tot_ref, tile, make_copy):
    make_copy(0, 0, pl.multiple_of(tot_ref[tile], RUN_ALIGN)).wait()


def _dispatch_kernel(cnt_ref, lo_ref, dst_ref, tot_ref, zst_ref, zln_ref, nused_ref, x_ref, g_ref, route_ref,
                     o_ref, srt_scr, zero_scr, sem, zsem):
    tm = x_ref.shape[0]
    bm = zero_scr.shape[0]
    i = pl.program_id(0)
    n = pl.num_programs(0)
    slot = i % 2

    def run_copy(s):
        return lambda lo, dst, cnt: pltpu.make_async_copy(
            srt_scr.at[s, pl.ds(lo, cnt)], o_ref.at[pl.ds(dst, cnt)], sem.at[s])

    def pad_copy(e):
        ln = pl.multiple_of(zln_ref[e], RUN_ALIGN)
        return pltpu.make_async_copy(zero_scr.at[pl.ds(0, ln)],
                                     o_ref.at[pl.ds(pl.multiple_of(zst_ref[e], RUN_ALIGN), ln)], zsem)

    def tail_copy(b):
        return pltpu.make_async_copy(zero_scr, o_ref.at[pl.ds(pl.multiple_of(b * bm, bm), bm)], zsem)

    @pl.when(i == 0)
    def _():
        zero_scr[...] = jnp.zeros_like(zero_scr)
        for wait in (False, True):
            def body(e, c):
                @pl.when(zln_ref[e] > 0)
                def _():
                    cp = pad_copy(e)
                    cp.wait() if wait else cp.start()
                return c
            lax.fori_loop(0, N_EXPERTS, body, 0)

            def tail(b, c):
                cp = tail_copy(b)
                cp.wait() if wait else cp.start()
                return c
            lax.fori_loop(nused_ref[0], o_ref.shape[0] // bm, tail, 0)

    @pl.when(i >= 2)
    def _():
        _wait_runs(tot_ref, i - 2, run_copy(slot))

    hb = _rms(x_ref[...], g_ref[...]).astype(BF16)
    route = route_ref[...]
    ns = srt_scr.shape[1]
    pt = jnp.maximum(_sort_matrix(route, 0, ns), _sort_matrix(route, 1, ns))
    srt_scr[slot] = lax.dot_general(pt, hb, (((0,), (0,)), ((), ())),
                                    preferred_element_type=F32).astype(BF16)
    _start_runs(cnt_ref, lo_ref, dst_ref, i, run_copy(slot))

    @pl.when(i == n - 1)
    def _():
        @pl.when(i >= 1)
        def _():
            _wait_runs(tot_ref, i - 1, run_copy(1 - slot))
        _wait_runs(tot_ref, i, run_copy(slot))


def dispatch(cnt, lo, dst, tot, zst, zln, nused, x2d, g, layer, route, n_rows, tm, bm):
    t, d = x2d.shape
    row = lambda i, *_: (i, 0)
    return pl.pallas_call(
        _dispatch_kernel,
        grid_spec=pltpu.PrefetchScalarGridSpec(
            num_scalar_prefetch=7,
            grid=(t // tm,),
            in_specs=[pl.BlockSpec((tm, d), row), pl.BlockSpec((None, 1, d), lambda i, *_: (layer, 0, 0)),
                      pl.BlockSpec((tm, LANE), row)],
            out_specs=pl.BlockSpec(memory_space=pl.ANY),
            scratch_shapes=[pltpu.VMEM((2, _sorted_rows(tm), d), BF16), pltpu.VMEM((bm, d), BF16),
                            pltpu.SemaphoreType.DMA((2,)), pltpu.SemaphoreType.DMA(())]),
        out_shape=jax.ShapeDtypeStruct((n_rows, d), BF16),
        compiler_params=_params(1),
        name="dispatch",
    )(cnt, lo, dst, tot, zst, zln, nused, x2d, g, route)


def _expert_kernel(bexp_ref, nused_ref, slot_ref, next_ref, xs_ref, wg_hbm, wu_hbm, wd_hbm, ys_ref,
                   wg_buf, wu_buf, wd_buf, wg_scr, wu_scr, wd_scr, sem, *, layer):
    i = pl.program_id(0)

    def weight_copies(e, s):
        return [pltpu.make_async_copy(hbm.at[layer, e], buf.at[s], sem.at[s])
                for hbm, buf in ((wg_hbm, wg_buf), (wu_hbm, wu_buf), (wd_hbm, wd_buf))]

    @pl.when(i < nused_ref[0])
    def _():
        e, s = bexp_ref[i], slot_ref[i]
        changed = jnp.logical_or(i == 0, e != bexp_ref[jnp.maximum(i - 1, 0)])

        @pl.when(i == 0)
        def _():
            for cp in weight_copies(e, s):
                cp.start()

        @pl.when(changed)
        def _():
            for cp in weight_copies(e, s):
                cp.wait()
            wg_scr[...] = wg_buf[s].astype(BF16)
            wu_scr[...] = wu_buf[s].astype(BF16)
            wd_scr[...] = wd_buf[s].astype(BF16)

            @pl.when(next_ref[i] < N_EXPERTS)
            def _():
                for cp in weight_copies(next_ref[i], 1 - s):
                    cp.start()

        xb = xs_ref[...]
        hg = _dot(xb, wg_scr[...])
        hu = _dot(xb, wu_scr[...])
        act = (hg * _sigmoid(hg) * hu).astype(BF16)
        ys_ref[...] = _dot(act, wd_scr[...]).astype(BF16)

    @pl.when(i >= nused_ref[0])
    def _():
        ys_ref[...] = jnp.zeros_like(ys_ref)


def experts(bexp, nused, slot, nxt, xs, w_gate, w_up, w_down, layer, bm):
    n_rows, d = xs.shape
    de = w_gate.shape[3]
    anyspace = pl.BlockSpec(memory_space=pl.ANY)
    return pl.pallas_call(
        functools.partial(_expert_kernel, layer=layer),
        grid_spec=pltpu.PrefetchScalarGridSpec(
            num_scalar_prefetch=4,
            grid=(n_rows // bm,),
            in_specs=[pl.BlockSpec((bm, d), lambda i, bexp, nused, *_: (jnp.minimum(i, nused[0] - 1), 0)),
                      anyspace, anyspace, anyspace],
            out_specs=pl.BlockSpec((bm, d), lambda i, *_: (i, 0)),
            scratch_shapes=[pltpu.VMEM((2, d, de), F32), pltpu.VMEM((2, d, de), F32), pltpu.VMEM((2, de, d), F32),
                            pltpu.VMEM((d, de), BF16), pltpu.VMEM((d, de), BF16), pltpu.VMEM((de, d), BF16),
                            pltpu.SemaphoreType.DMA((2,))]),
        out_shape=jax.ShapeDtypeStruct(xs.shape, BF16),
        compiler_params=_params(1),
        name="experts",
    )(bexp, nused, slot, nxt, xs, w_gate, w_up, w_down)


def _combine_kernel(cnt_ref, lo_ref, dst_ref, tot_ref, x_ref, route_ref, g_ref, ys_ref, o_ref, buf, sem, *,
                    final):
    tm = x_ref.shape[0]
    i = pl.program_id(0)
    n = pl.num_programs(0)
    slot = i % 2

    def run_copy(s):
        return lambda lo, dst, cnt: pltpu.make_async_copy(
            ys_ref.at[pl.ds(dst, cnt)], buf.at[s, pl.ds(lo, cnt)], sem.at[s])

    @pl.when(i == 0)
    def _():
        buf[...] = jnp.zeros_like(buf)
        _start_runs(cnt_ref, lo_ref, dst_ref, 0, run_copy(0))

    @pl.when(i + 1 < n)
    def _():
        _start_runs(cnt_ref, lo_ref, dst_ref, i + 1, run_copy(1 - slot))

    _wait_runs(tot_ref, i, run_copy(slot))
    ysb = buf[slot]
    route = route_ref[...]
    out = x_ref[...]
    for k in range(2):
        out = out + route[:, k:k + 1] * _dot(_sort_matrix(route, k, buf.shape[1]), ysb)
    if final:
        out = _rms(out, g_ref[...])
    o_ref[...] = out


def combine(cnt, lo, dst, tot, x2d, route, g, ys, tm, final):
    t, d = x2d.shape
    row = lambda i, *_: (i, 0)
    return pl.pallas_call(
        functools.partial(_combine_kernel, final=final),
        grid_spec=pltpu.PrefetchScalarGridSpec(
            num_scalar_prefetch=4,
            grid=(t // tm,),
            in_specs=[pl.BlockSpec((tm, d), row), pl.BlockSpec((tm, LANE), row),
                      pl.BlockSpec((1, d), lambda i, *_: (0, 0)),
                      pl.BlockSpec(memory_space=pl.ANY)],
            out_specs=pl.BlockSpec((tm, d), row),
            scratch_shapes=[pltpu.VMEM((2, _sorted_rows(tm), d), BF16), pltpu.SemaphoreType.DMA((2,))]),
        out_shape=jax.ShapeDtypeStruct((t, d), F32),
        compiler_params=_params(1),
        name="combine",
    )(cnt, lo, dst, tot, x2d, route, g, ys)


def router_params(rg_w, rg_b, re_w, re_b):
    depth = rg_w.shape[0]
    pad = LANE - N_GROUPS - N_EXPERTS
    wr = jnp.concatenate([rg_w, re_w, jnp.zeros((depth, D_MODEL, pad), F32)], axis=2)
    wr_hi = wr.astype(BF16)
    wr = jnp.stack([wr_hi, (wr - wr_hi.astype(F32)).astype(BF16)], axis=1)
    br = jnp.concatenate([rg_b, re_b, jnp.zeros((depth, pad), F32)], axis=1)[:, None, :]
    return wr, br


def moe(x2d, g2, wr, br, w_gate, w_up, w_down, layer, g_final, final, tm, bm):
    t = x2d.shape[0]
    nt = t // tm
    route, meta = router(x2d, g2, wr, br, layer, tm)
    meta = meta.reshape(nt, SUBLANE, LANE)[:, :, :N_EXPERTS].astype(I32)
    cnt, earlier, lo = meta[:, 0], meta[:, 1], meta[:, 2]
    counts = meta[nt - 1, 3]
    padded = (counts + bm - 1) // bm * bm
    ends = jnp.cumsum(padded)
    offs = ends - padded
    dst = offs[None, :] + earlier
    n_rows = -(-(2 * t + nt * N_EXPERTS * (RUN_ALIGN - 1)) // bm) * bm + N_EXPERTS * bm
    nused = (ends[-1] // bm).astype(I32)
    blk_start = jnp.arange(n_rows // bm, dtype=I32) * bm
    bexp = jnp.sum((ends[None, :] <= blk_start[:, None]).astype(I32), axis=1)
    last = jnp.sum((ends <= (nused - 1) * bm).astype(I32))
    bexp = jnp.minimum(bexp, last)
    tot = jnp.sum(cnt, axis=1)
    cnt, lo, dst = cnt.reshape(-1), lo.reshape(-1), dst.reshape(-1)
    xs = dispatch(cnt, lo, dst, tot, offs + counts, padded - counts, nused[None], x2d, g2, layer, route,
                  n_rows, tm, bm)
    ar = jnp.arange(N_EXPERTS, dtype=I32)
    active = padded > 0
    rank = jnp.cumsum(active.astype(I32)) - active.astype(I32)
    later = (ar[None, :] > ar[:, None]) & active[None, :]
    next_e = jnp.min(jnp.where(later, ar[None, :], N_EXPERTS), axis=1).astype(I32)
    ys = experts(bexp, nused[None], (rank % 2)[bexp], next_e[bexp], xs, w_gate, w_up, w_down, layer, bm)
    return combine(cnt, lo, dst, tot, x2d, route, g_final[None, :], ys, tm, final)


def kernel(x, norm_mix_g, w_in, b_gate, pool_w, pool_scale, sgu_ln_g, sgu_ln_b, sgu_w, sgu_b, ssm_a_re, ssm_a_im, ssm_log_dt, ssm_b_re, ssm_b_im, ssm_c_re, ssm_c_im, ssm_d, glu_w, glu_b, w_branch, w_out, norm_ffn_g, router_group_w, router_group_b, router_expert_w, router_expert_b, exp_w_gate, exp_w_up, exp_w_down, norm_final_g):
    batch, seq, d = x.shape
    depth = w_in.shape[0]
    tm_in, tm_mix, tm_moe, bm = 1024, 512, 256, 512
    s5_seqs = 2
    x2d = x.reshape(batch * seq, d)
    ssm_col_block = 3
    row = lambda a: a[:, None, :]
    p = {
        "norm_mix_g": row(norm_mix_g), "w_in": w_in.astype(BF16), "b_gate": row(b_gate),
        "pool_w": pool_w.astype(BF16), "pool_scale": row(pool_scale),
        "sgu_ln_g": row(sgu_ln_g), "sgu_ln_b": row(sgu_ln_b), "sgu_w": sgu_w,
        "sgu_b_full": jnp.repeat(jnp.swapaxes(sgu_b, 1, 2), LANE, axis=2),
        "glu_w": glu_w.astype(BF16), "glu_b": row(glu_b),
        "w_branch": w_branch.astype(BF16), "w_out": w_out.astype(BF16),
    }
    bq, cq, kq, lam_q = jax.vmap(ssm_tables)(ssm_a_re, ssm_a_im, ssm_log_dt, ssm_b_re, ssm_b_im,
                                             ssm_c_re, ssm_c_im, ssm_d)
    wr, br = router_params(router_group_w, router_group_b, router_expert_w, router_expert_b)
    g_ffn = row(norm_ffn_g)
    for l in range(depth):
        xc4 = ssm_input(x2d, p["norm_mix_g"], p["w_in"], l, ssm_col_block, tm_in)
        ys4 = s5_scan(xc4, bq, cq, kq, lam_q, l, batch, s5_seqs)
        x2d = mixer(x2d, ys4, p, l, batch, tm_mix)
        x2d = moe(x2d, g_ffn, wr, br, exp_w_gate, exp_w_up, exp_w_down, l, norm_final_g,
                  l == depth - 1, tm_moe, bm)
    return x2d.reshape(batch, seq, d)
```

```python
import functools
import math

import jax
import jax.numpy as jnp
from jax import lax
from jax.experimental import pallas as pl
from jax.experimental.pallas import tpu as pltpu

F32 = jnp.float32
BF16 = jnp.bfloat16
I32 = jnp.int32

D_MODEL = 1024
POOL_WINDOWS = (2, 4, 8, 16)
POOL_HALO = 16
LANE = 128
SUBLANE = 8
RUN_ALIGN = 16
SGU_CHUNK = 128
SSM_GROUP_DIM = 16
SSM_STATE = 64
SSM_Q = 8
SSM_SLABS = 4
GROUPS_PER_SLAB = LANE // SSM_GROUP_DIM
N_GROUPS = 4
EXPERTS_PER_GROUP = 8
N_EXPERTS = 32
EPS = 1e-6
VMEM_LIMIT = 56 * 1024 * 1024


def _rms(x, g):
    return x * lax.rsqrt(jnp.mean(x * x, axis=-1, keepdims=True) + EPS) * g


def _gelu(x):
    c = math.sqrt(2.0 / math.pi)
    return x * (0.5 * (1.0 + jnp.tanh(c * (x + 0.044715 * (x * x * x)))))


def _sigmoid(x):
    return 1.0 / (1.0 + jnp.exp(-x))


def _dot(a, b):
    return jnp.dot(a, b, preferred_element_type=F32)


def _layer_spec(arr, layer):
    nd = arr.ndim - 1
    return pl.BlockSpec((None,) + arr.shape[1:], lambda *_: (layer,) + (0,) * nd, pipeline_mode=pl.Buffered(1))


def _params(n_axes):
    return pltpu.CompilerParams(dimension_semantics=("arbitrary",) * n_axes, vmem_limit_bytes=VMEM_LIMIT)


def _xc_kernel(x_ref, g_ref, w_ref, o_ref, xc_scr):
    h = _rms(x_ref[...], g_ref[...]).astype(BF16)
    xc = _dot(h, w_ref[...])
    rows = xc_scr.shape[1] // SSM_Q
    for j in range(SSM_SLABS):
        xc_scr[j] = xc[:, j * LANE:(j + 1) * LANE]
        for s in range(SSM_Q):
            o_ref[j, :, s * LANE:(s + 1) * LANE] = xc_scr[j, pl.ds(s, rows, stride=SSM_Q), :].astype(BF16)


def ssm_input(x2d, g, w_in_bf, layer, col_block, tm):
    t = x2d.shape[0]
    width = SSM_SLABS * LANE
    return pl.pallas_call(
        _xc_kernel,
        grid=(t // tm,),
        in_specs=[pl.BlockSpec((tm, D_MODEL), lambda i: (i, 0)),
                  _layer_spec(g, layer),
                  pl.BlockSpec((None, D_MODEL, width), lambda i: (layer, 0, col_block),
                               pipeline_mode=pl.Buffered(1))],
        out_specs=pl.BlockSpec((SSM_SLABS, tm // SSM_Q, SSM_Q * LANE), lambda i: (0, i, 0)),
        out_shape=jax.ShapeDtypeStruct((SSM_SLABS, t // SSM_Q, SSM_Q * LANE), BF16),
        scratch_shapes=[pltpu.VMEM((SSM_SLABS, tm, LANE), F32)],
        compiler_params=_params(1),
        name="ssm_input",
    )(x2d, g, w_in_bf)


def ssm_tables(a_re, a_im, log_dt, b_re, b_im, c_re, c_im, d_skip):
    q, nc = SSM_Q, SSM_GROUP_DIM
    ng = a_re.shape[0]
    sl = ng // GROUPS_PER_SLAB
    dt = jnp.exp(log_dt)[:, None]

    def powers(kvec):
        kk = kvec[:, None, None]
        mag = jnp.exp(a_re * dt * kk)
        return mag * jnp.cos(a_im * dt * kk), mag * jnp.sin(a_im * dt * kk)

    pw_re, pw_im = powers(jnp.arange(q + 1, dtype=F32))
    nr, ni = pw_re[1] - 1.0, pw_im[1]
    den = a_re * a_re + a_im * a_im
    fr, fi = (nr * a_re + ni * a_im) / den, (ni * a_re - nr * a_im) / den
    bb_re = fr[..., None] * b_re - fi[..., None] * b_im
    bb_im = fr[..., None] * b_im + fi[..., None] * b_re

    def to_slab_rows(a, lead):
        a = a.reshape(lead, sl, GROUPS_PER_SLAB, a.shape[2], a.shape[3]).transpose(1, 0, 2, 3, 4)
        return a.reshape(sl, -1, a.shape[-1]).astype(BF16)

    rev_re, rev_im = powers((q - 1.0) - jnp.arange(q, dtype=F32))
    bt_re, bt_im = bb_re.transpose(0, 2, 1), bb_im.transpose(0, 2, 1)
    m_re = rev_re[:, :, None, :] * bt_re[None] - rev_im[:, :, None, :] * bt_im[None]
    m_im = rev_re[:, :, None, :] * bt_im[None] + rev_im[:, :, None, :] * bt_re[None]
    bq = to_slab_rows(jnp.concatenate([m_re, m_im], axis=-1), q)
    ct_re, ct_im = c_re.transpose(0, 2, 1), c_im.transpose(0, 2, 1)
    ct_re, ct_im = jnp.tile(ct_re, (1, 1, q + 1)), jnp.tile(ct_im, (1, 1, q + 1))
    pl_re = jnp.repeat(pw_re.transpose(1, 2, 0), nc, axis=-1)
    pl_im = jnp.repeat(pw_im.transpose(1, 2, 0), nc, axis=-1)
    r_re = ct_re * pl_re - ct_im * pl_im
    r_im = ct_re * pl_im + ct_im * pl_re
    cq = to_slab_rows(jnp.stack([r_re[..., nc:], -r_im[..., nc:]], axis=0), 2)
    hi = lax.Precision.HIGHEST
    kern = (jnp.einsum('gpd,gpl->gdl', bb_re, r_re[..., :q * nc], precision=hi)
            - jnp.einsum('gpd,gpl->gdl', bb_im, r_im[..., :q * nc], precision=hi))
    lane = jnp.arange(q * nc)[None, None, :]
    d_pad = jnp.pad(d_skip.reshape(ng, nc), ((0, 0), (0, (q - 1) * nc)))
    kern = kern + jnp.where(lane == jnp.arange(nc)[None, :, None], d_pad[:, None, :], 0.0)
    kq = jnp.stack([jnp.pad(kern[..., :(q - s) * nc], ((0, 0), (0, 0), (s * nc, 0))) for s in range(q)], axis=0)
    kq = to_slab_rows(kq, q)
    lam_q = jnp.concatenate([pw_re[q].reshape(sl, 1, -1), pw_im[q].reshape(sl, 1, -1)], axis=-1)
    return bq, cq, kq, lam_q


def _expand_block_diag(small, row_inner, col_inner):
    n_rows, n_small = small.shape
    n_cols = n_small * GROUPS_PER_SLAB
    g_bits = GROUPS_PER_SLAB.bit_length() - 1
    r_shift, c_shift = row_inner.bit_length() - 1, col_inner.bit_length() - 1
    k = lax.broadcasted_iota(I32, (n_small, n_cols), 0)
    col = lax.broadcasted_iota(I32, (n_small, n_cols), 1)
    src = ((col >> (c_shift + g_bits)) << c_shift) | (col & (col_inner - 1))
    spread = jnp.where(k == src, 1.0, 0.0).astype(BF16)
    r = lax.broadcasted_iota(I32, (n_rows, n_cols), 0)
    c = lax.broadcasted_iota(I32, (n_rows, n_cols), 1)
    same = ((r >> r_shift) & (GROUPS_PER_SLAB - 1)) == ((c >> c_shift) & (GROUPS_PER_SLAB - 1))
    return jnp.where(same, _dot(small, spread), 0.0).astype(BF16)


def _s5_kernel(u_ref, bq_ref, cq_ref, kq_ref, lam_ref, y_ref, g_scr, s_scr, tbq, tcq, tkq, *, seqs):
    nblk = u_ref.shape[2] // LANE
    nre = nblk // 2
    rows = u_ref.shape[1]
    seq_rows = rows // seqs
    wide = 2 * LANE

    @pl.when(pl.program_id(1) == 0)
    def _():
        tbq[...] = _expand_block_diag(bq_ref[0], SSM_GROUP_DIM, SSM_STATE)
        tcq[...] = _expand_block_diag(cq_ref[0], SSM_STATE, SSM_GROUP_DIM)
        tkq[...] = _expand_block_diag(kq_ref[0], SSM_GROUP_DIM, SSM_GROUP_DIM)

    u = u_ref[0]
    for c2 in range(nblk // 2):
        g = _dot(u, tbq[:, c2 * wide:(c2 + 1) * wide])
        for h in range(2):
            c = 2 * c2 + h
            g_scr[c * rows:(c + 1) * rows, :] = g[:, h * LANE:(h + 1) * LANE]
    lam = lam_ref[0]
    lre = [lam[:, c * LANE:(c + 1) * LANE] for c in range(nre)]
    lim = [lam[:, (nre + c) * LANE:(nre + c + 1) * LANE] for c in range(nre)]

    def body(n, carry):
        re, im = carry[:nre], carry[nre:]
        new_re, new_im = [], []
        for c in range(nre):
            at_re = pl.ds(c * rows + n, seqs, stride=seq_rows)
            at_im = pl.ds((nre + c) * rows + n, seqs, stride=seq_rows)
            s_scr[at_re, :] = re[c]
            s_scr[at_im, :] = im[c]
            new_re.append(lre[c] * re[c] - lim[c] * im[c] + g_scr[at_re, :])
            new_im.append(lre[c] * im[c] + lim[c] * re[c] + g_scr[at_im, :])
        return tuple(new_re + new_im)

    zero = jnp.zeros((seqs, LANE), F32)
    lax.fori_loop(0, seq_rows, body, (zero,) * nblk, unroll=2)
    s = jnp.concatenate([s_scr[c * rows:(c + 1) * rows, :].astype(BF16) for c in range(nblk)], axis=1)
    for c2 in range(nblk // 2):
        cols = slice(c2 * wide, (c2 + 1) * wide)
        y = _dot(s, tcq[:, cols]) + _dot(u, tkq[:, cols])
        for h in range(2):
            g_scr[pl.ds(2 * c2 + h, rows, stride=SSM_Q), :] = y[:, h * LANE:(h + 1) * LANE]
    y_ref[0] = g_scr[...].astype(BF16)


def s5_scan(u4, bq, cq, kq, lam_q, layer, batch, seqs):
    sl, rows_total, width = u4.shape
    rows = rows_total // batch * seqs
    tab = pl.BlockSpec((None, 1, width, width // GROUPS_PER_SLAB), lambda j, b: (layer, j, 0, 0))
    return pl.pallas_call(
        functools.partial(_s5_kernel, seqs=seqs),
        grid=(sl, batch // seqs),
        in_specs=[pl.BlockSpec((1, rows, width), lambda j, b: (j, b, 0)), tab, tab, tab,
                  pl.BlockSpec((None, 1, 1, width), lambda j, b: (layer, j, 0, 0))],
        out_specs=pl.BlockSpec((1, rows * SSM_Q, LANE), lambda j, b: (j, b, 0)),
        out_shape=jax.ShapeDtypeStruct((sl, rows_total * SSM_Q, LANE), BF16),
        scratch_shapes=[pltpu.VMEM((rows * SSM_Q, LANE), F32), pltpu.VMEM((rows * SSM_Q, LANE), F32)]
        + [pltpu.VMEM((width, width), BF16)] * 3,
        compiler_params=_params(2),
        name="s5_scan",
    )(u4, bq, cq, kq, lam_q)


def _mixer_kernel(x_ref, ys_ref, ng_ref, win_ref, bgate_ref, poolw_ref, pscale_ref, lng_ref, lnb_ref,
                  sguw_ref, sgub_ref, gluw_ref, glub_ref, wbr_ref, wout_ref, o_ref, xe_scr):
    tm = x_ref.shape[0]
    width = SSM_SLABS * LANE
    a_end, b_end = width, 3 * width
    c_end = b_end + width
    i = pl.program_id(1)
    x = x_ref[...]
    hb = _rms(x, ng_ref[...]).astype(BF16)

    xa = _dot(hb, win_ref[:, 0:a_end])

    @pl.when(i == 0)
    def _():
        xe_scr[0:POOL_HALO, :] = jnp.zeros((POOL_HALO, xe_scr.shape[1]), F32)

    xe_scr[POOL_HALO:POOL_HALO + tm, :] = xa
    pos = i * tm + lax.broadcasted_iota(I32, (tm, 1), 0)
    ya_parts = []
    for gi, w in enumerate(POOL_WINDOWS):
        ch = slice(gi * LANE, (gi + 1) * LANE)
        win = xa[:, ch]
        for j in range(1, w):
            win = win + xe_scr[POOL_HALO - j:POOL_HALO - j + tm, ch]
        cnt = jnp.minimum(pos + 1, w).astype(F32)
        diff = win / cnt - xa[:, ch]
        ya_parts.append(_dot(diff.astype(BF16), poolw_ref[gi]) * pscale_ref[:, ch])
    ya = jnp.concatenate(ya_parts, axis=1)
    xe_scr[0:POOL_HALO, :] = xe_scr[tm:tm + POOL_HALO, :]

    z = _gelu(_dot(hb, win_ref[:, a_end:b_end]))
    u, v = z[:, :width], z[:, width:]
    mu = jnp.mean(v, axis=-1, keepdims=True)
    vc = v - mu
    var = jnp.mean(vc * vc, axis=-1, keepdims=True)
    vb = (vc * lax.rsqrt(var + EPS) * lng_ref[...] + lnb_ref[...]).astype(BF16)
    r_i = lax.broadcasted_iota(I32, (SGU_CHUNK, SGU_CHUNK), 0)
    c_i = lax.broadcasted_iota(I32, (SGU_CHUNK, SGU_CHUNK), 1)
    n_heads = width // LANE
    wt = [jnp.where(r_i >= c_i, sguw_ref[hd], 0.0).astype(BF16) for hd in range(n_heads)]
    s_rows = []
    for c in range(tm // SGU_CHUNK):
        rs = slice(c * SGU_CHUNK, (c + 1) * SGU_CHUNK)
        s_rows.append(jnp.concatenate(
            [_dot(wt[hd], vb[rs, hd * LANE:(hd + 1) * LANE]) for hd in range(n_heads)], axis=1)
            + sgub_ref[...])
    yb = u * jnp.concatenate(s_rows, axis=0)

    yc = _gelu(jnp.concatenate([ys_ref[j] for j in range(SSM_SLABS)], axis=1).astype(F32))
    yc = yc * _sigmoid(_dot(yc.astype(BF16), gluw_ref[...]) + glub_ref[...])

    merged = None
    for k, yk in enumerate((ya, yb, yc)):
        gate = _sigmoid(_dot(hb, win_ref[:, c_end + k * D_MODEL:c_end + (k + 1) * D_MODEL])
                        + bgate_ref[:, k * D_MODEL:(k + 1) * D_MODEL])
        term = gate * _dot(yk.astype(BF16), wbr_ref[k])
        merged = term if merged is None else merged + term
    o_ref[...] = x + _dot(merged.astype(BF16), wout_ref[...])


def mixer(x2d, ys4, p, layer, batch, tm):
    t = x2d.shape[0]
    nt = t // batch // tm
    row = lambda b, i: (b * nt + i, 0)
    slab = lambda b, i: (0, b * nt + i, 0)
    consts = [p["norm_mix_g"], p["w_in"], p["b_gate"], p["pool_w"], p["pool_scale"], p["sgu_ln_g"],
              p["sgu_ln_b"], p["sgu_w"], p["sgu_b_full"], p["glu_w"], p["glu_b"], p["w_branch"], p["w_out"]]
    return pl.pallas_call(
        _mixer_kernel,
        grid=(batch, nt),
        in_specs=[pl.BlockSpec((tm, D_MODEL), row),
                  pl.BlockSpec((SSM_SLABS, tm, LANE), slab)] + [_layer_spec(c, layer) for c in consts],
        out_specs=pl.BlockSpec((tm, D_MODEL), row),
        out_shape=jax.ShapeDtypeStruct(x2d.shape, F32),
        scratch_shapes=[pltpu.VMEM((tm + POOL_HALO, SSM_SLABS * LANE), F32)],
        compiler_params=_params(2),
        name="mixer",
    )(x2d, ys4, *consts)


def _router_kernel(x_ref, g_ref, wr_ref, br_ref, route_ref, meta_ref, carry_scr):
    tm = x_ref.shape[0]
    i = pl.program_id(0)

    @pl.when(i == 0)
    def _():
        carry_scr[...] = jnp.zeros_like(carry_scr)

    h = _rms(x_ref[...], g_ref[...])
    h_hi = h.astype(BF16)
    h_lo = (h - h_hi.astype(F32)).astype(BF16)
    logits = (_dot(h_hi, wr_ref[0]) + _dot(h_hi, wr_ref[1]) + _dot(h_lo, wr_ref[0])) + br_ref[...]
    lane = lax.broadcasted_iota(I32, logits.shape, 1)
    neg = -jnp.inf
    gl = jnp.where(lane < N_GROUPS, logits, neg)
    gmax = jnp.max(gl, axis=-1, keepdims=True)
    gsum = jnp.sum(jnp.where(lane < N_GROUPS, jnp.exp(logits - gmax), 0.0), axis=-1, keepdims=True)
    g_val = 1.0 / gsum
    g_idx = jnp.min(jnp.where(gl == gmax, lane, LANE), axis=-1, keepdims=True)
    lo = N_GROUPS + EXPERTS_PER_GROUP * g_idx
    el = jnp.where((lane >= lo) & (lane < lo + EXPERTS_PER_GROUP), logits, neg)
    e1v = jnp.max(el, axis=-1, keepdims=True)
    e1l = jnp.min(jnp.where(el == e1v, lane, LANE), axis=-1, keepdims=True)
    el2 = jnp.where(lane == e1l, neg, el)
    e2v = jnp.max(el2, axis=-1, keepdims=True)
    e2l = jnp.min(jnp.where(el2 == e2v, lane, LANE), axis=-1, keepdims=True)
    tt = jnp.exp(e2v - e1v)
    w1 = g_val / (1.0 + tt)
    w2 = g_val * tt / (1.0 + tt)
    hit1, hit2 = lane == e1l - N_GROUPS, lane == e2l - N_GROUPS
    onehot = jnp.where(hit1 | hit2, 1.0, 0.0)
    r_i = lax.broadcasted_iota(I32, (tm, tm), 0)
    c_i = lax.broadcasted_iota(I32, (tm, tm), 1)
    tri = jnp.where(r_i > c_i, 1.0, 0.0).astype(BF16)
    prefix = _dot(tri, onehot.astype(BF16))
    count = jnp.sum(onehot, axis=0, keepdims=True)
    groups8 = jnp.floor((count + (RUN_ALIGN - 1.0)) * (1.0 / RUN_ALIGN))
    count = groups8 * RUN_ALIGN
    e_r = lax.broadcasted_iota(I32, (LANE, LANE), 0)
    e_c = lax.broadcasted_iota(I32, (LANE, LANE), 1)
    before = jnp.where(e_r < e_c, 1.0, 0.0).astype(BF16)
    run_start = _dot(jnp.broadcast_to(groups8, (SUBLANE, LANE)).astype(BF16), before)[0:1, :] * RUN_ALIGN
    local = prefix + run_start
    lp1 = jnp.sum(jnp.where(hit1, local, 0.0), axis=-1, keepdims=True)
    lp2 = jnp.sum(jnp.where(hit2, local, 0.0), axis=-1, keepdims=True)
    route_ref[...] = jnp.where(lane == 0, w1, jnp.where(lane == 1, w2, jnp.where(
        lane == 2, lp1, jnp.where(lane == 3, lp2, 0.0))))
    earlier = carry_scr[0:1, :]
    total = earlier + count
    carry_scr[...] = jnp.broadcast_to(total, carry_scr.shape)
    sub = lax.broadcasted_iota(I32, meta_ref.shape, 0)
    meta_ref[...] = jnp.where(sub == 0, count, jnp.where(sub == 1, earlier, jnp.where(
        sub == 2, run_start, jnp.where(sub == 3, total, 0.0))))


def router(x2d, g, wr, br, layer, tm):
    t = x2d.shape[0]
    row = lambda i: (i, 0)
    return pl.pallas_call(
        _router_kernel,
        grid=(t // tm,),
        in_specs=[pl.BlockSpec((tm, D_MODEL), row), _layer_spec(g, layer),
                  _layer_spec(wr, layer), _layer_spec(br, layer)],
        out_specs=[pl.BlockSpec((tm, LANE), row), pl.BlockSpec((SUBLANE, LANE), row)],
        out_shape=[jax.ShapeDtypeStruct((t, LANE), F32),
                   jax.ShapeDtypeStruct((t // tm * SUBLANE, LANE), F32)],
        scratch_shapes=[pltpu.VMEM((SUBLANE, LANE), F32)],
        compiler_params=_params(1),
        name="router",
    )(x2d, g, wr, br)


def _sort_matrix(route, k, n_sorted):
    col = lax.broadcasted_iota(I32, (route.shape[0], n_sorted), 1)
    return jnp.where(col == route[:, 2 + k:3 + k].astype(I32), 1.0, 0.0).astype(BF16)


def _sorted_rows(tm):
    return -(-(2 * tm + N_EXPERTS * (RUN_ALIGN - 1)) // LANE) * LANE


def _start_runs(cnt_ref, lo_ref, dst_ref, tile, make_copy):
    def body(e, c):
        idx = tile * N_EXPERTS + e
        n = cnt_ref[idx]

        @pl.when(n > 0)
        def _():
            make_copy(pl.multiple_of(lo_ref[idx], RUN_ALIGN), pl.multiple_of(dst_ref[idx], RUN_ALIGN),
                      pl.multiple_of(n, RUN_ALIGN)).start()
        return c
    lax.fori_loop(0, N_EXPERTS, body, 0)


def _wait_runs(tot_ref, tile, make_copy):
    make_copy(0, 0, pl.multiple_of(tot_ref[tile], RUN_ALIGN)).wait()


def _dispatch_kernel(cnt_ref, lo_ref, dst_ref, tot_ref, zst_ref, zln_ref, nused_ref, x_ref, g_ref, route_ref,
                     o_ref, srt_scr, zero_scr, sem, zsem):
    tm = x_ref.shape[0]
    bm = zero_scr.shape[0]
    i = pl.program_id(0)
    n = pl.num_programs(0)
    slot = i % 2

    def run_copy(s):
        return lambda lo, dst, cnt: pltpu.make_async_copy(
            srt_scr.at[s, pl.ds(lo, cnt)], o_ref.at[pl.ds(dst, cnt)], sem.at[s])

    def pad_copy(e):
        ln = pl.multiple_of(zln_ref[e], RUN_ALIGN)
        return pltpu.make_async_copy(zero_scr.at[pl.ds(0, ln)],
                                     o_ref.at[pl.ds(pl.multiple_of(zst_ref[e], RUN_ALIGN), ln)], zsem)

    def tail_copy(b):
        return pltpu.make_async_copy(zero_scr, o_ref.at[pl.ds(pl.multiple_of(b * bm, bm), bm)], zsem)

    @pl.when(i == 0)
    def _():
        zero_scr[...] = jnp.zeros_like(zero_scr)
        for wait in (False, True):
            def body(e, c):
                @pl.when(zln_ref[e] > 0)
                def _():
                    cp = pad_copy(e)
                    cp.wait() if wait else cp.start()
                return c
            lax.fori_loop(0, N_EXPERTS, body, 0)

            def tail(b, c):
                cp = tail_copy(b)
                cp.wait() if wait else cp.start()
                return c
            lax.fori_loop(nused_ref[0], o_ref.shape[0] // bm, tail, 0)

    @pl.when(i >= 2)
    def _():
        _wait_runs(tot_ref, i - 2, run_copy(slot))

    hb = _rms(x_ref[...], g_ref[...]).astype(BF16)
    route = route_ref[...]
    ns = srt_scr.shape[1]
    pt = jnp.maximum(_sort_matrix(route, 0, ns), _sort_matrix(route, 1, ns))
    srt_scr[slot] = lax.dot_general(pt, hb, (((0,), (0,)), ((), ())),
                                    preferred_element_type=F32).astype(BF16)
    _start_runs(cnt_ref, lo_ref, dst_ref, i, run_copy(slot))

    @pl.when(i == n - 1)
    def _():
        @pl.when(i >= 1)
        def _():
            _wait_runs(tot_ref, i - 1, run_copy(1 - slot))
        _wait_runs(tot_ref, i, run_copy(slot))


def dispatch(cnt, lo, dst, tot, zst, zln, nused, x2d, g, layer, route, n_rows, tm, bm):
    t, d = x2d.shape
    row = lambda i, *_: (i, 0)
    return pl.pallas_call(
        _dispatch_kernel,
        grid_spec=pltpu.PrefetchScalarGridSpec(
            num_scalar_prefetch=7,
            grid=(t // tm,),
            in_specs=[pl.BlockSpec((tm, d), row), pl.BlockSpec((None, 1, d), lambda i, *_: (layer, 0, 0)),
                      pl.BlockSpec((tm, LANE), row)],
            out_specs=pl.BlockSpec(memory_space=pl.ANY),
            scratch_shapes=[pltpu.VMEM((2, _sorted_rows(tm), d), BF16), pltpu.VMEM((bm, d), BF16),
                            pltpu.SemaphoreType.DMA((2,)), pltpu.SemaphoreType.DMA(())]),
        out_shape=jax.ShapeDtypeStruct((n_rows, d), BF16),
        compiler_params=_params(1),
        name="dispatch",
    )(cnt, lo, dst, tot, zst, zln, nused, x2d, g, route)


def _expert_kernel(bexp_ref, nused_ref, slot_ref, next_ref, xs_ref, wg_hbm, wu_hbm, wd_hbm, ys_ref,
                   wg_buf, wu_buf, wd_buf, wg_scr, wu_scr, wd_scr, sem, *, layer):
    i = pl.program_id(0)

    def weight_copies(e, s):
        return [pltpu.make_async_copy(hbm.at[layer, e], buf.at[s], sem.at[s])
                for hbm, buf in ((wg_hbm, wg_buf), (wu_hbm, wu_buf), (wd_hbm, wd_buf))]

    @pl.when(i < nused_ref[0])
    def _():
        e, s = bexp_ref[i], slot_ref[i]
        changed = jnp.logical_or(i == 0, e != bexp_ref[jnp.maximum(i - 1, 0)])

        @pl.when(i == 0)
        def _():
            for cp in weight_copies(e, s):
                cp.start()

        @pl.when(changed)
        def _():
            for cp in weight_copies(e, s):
                cp.wait()
            wg_scr[...] = wg_buf[s].astype(BF16)
            wu_scr[...] = wu_buf[s].astype(BF16)
            wd_scr[...] = wd_buf[s].astype(BF16)

            @pl.when(next_ref[i] < N_EXPERTS)
            def _():
                for cp in weight_copies(next_ref[i], 1 - s):
                    cp.start()

        xb = xs_ref[...]
        hg = _dot(xb, wg_scr[...])
        hu = _dot(xb, wu_scr[...])
        act = (hg * _sigmoid(hg) * hu).astype(BF16)
        ys_ref[...] = _dot(act, wd_scr[...]).astype(BF16)

    @pl.when(i >= nused_ref[0])
    def _():
        ys_ref[...] = jnp.zeros_like(ys_ref)


def experts(bexp, nused, slot, nxt, xs, w_gate, w_up, w_down, layer, bm):
    n_rows, d = xs.shape
    de = w_gate.shape[3]
    anyspace = pl.BlockSpec(memory_space=pl.ANY)
    return pl.pallas_call(
        functools.partial(_expert_kernel, layer=layer),
        grid_spec=pltpu.PrefetchScalarGridSpec(
            num_scalar_prefetch=4,
            grid=(n_rows // bm,),
            in_specs=[pl.BlockSpec((bm, d), lambda i, bexp, nused, *_: (jnp.minimum(i, nused[0] - 1), 0)),
                      anyspace, anyspace, anyspace],
            out_specs=pl.BlockSpec((bm, d), lambda i, *_: (i, 0)),
            scratch_shapes=[pltpu.VMEM((2, d, de), F32), pltpu.VMEM((2, d, de), F32), pltpu.VMEM((2, de, d), F32),
                            pltpu.VMEM((d, de), BF16), pltpu.VMEM((d, de), BF16), pltpu.VMEM((de, d), BF16),
                            pltpu.SemaphoreType.DMA((2,))]),
        out_shape=jax.ShapeDtypeStruct(xs.shape, BF16),
        compiler_params=_params(1),
        name="experts",
    )(bexp, nused, slot, nxt, xs, w_gate, w_up, w_down)


def _combine_kernel(cnt_ref, lo_ref, dst_ref, tot_ref, x_ref, route_ref, g_ref, ys_ref, o_ref, buf, sem, *,
                    final):
    tm = x_ref.shape[0]
    i = pl.program_id(0)
    n = pl.num_programs(0)
    slot = i % 2

    def run_copy(s):
        return lambda lo, dst, cnt: pltpu.make_async_copy(
            ys_ref.at[pl.ds(dst, cnt)], buf.at[s, pl.ds(lo, cnt)], sem.at[s])

    @pl.when(i == 0)
    def _():
        buf[...] = jnp.zeros_like(buf)
        _start_runs(cnt_ref, lo_ref, dst_ref, 0, run_copy(0))

    @pl.when(i + 1 < n)
    def _():
        _start_runs(cnt_ref, lo_ref, dst_ref, i + 1, run_copy(1 - slot))

    _wait_runs(tot_ref, i, run_copy(slot))
    ysb = buf[slot]
    route = route_ref[...]
    out = x_ref[...]
    for k in range(2):
        out = out + route[:, k:k + 1] * _dot(_sort_matrix(route, k, buf.shape[1]), ysb)
    if final:
        out = _rms(out, g_ref[...])
    o_ref[...] = out


def combine(cnt, lo, dst, tot, x2d, route, g, ys, tm, final):
    t, d = x2d.shape
    row = lambda i, *_: (i, 0)
    return pl.pallas_call(
        functools.partial(_combine_kernel, final=final),
        grid_spec=pltpu.PrefetchScalarGridSpec(
            num_scalar_prefetch=4,
            grid=(t // tm,),
            in_specs=[pl.BlockSpec((tm, d), row), pl.BlockSpec((tm, LANE), row),
                      pl.BlockSpec((1, d), lambda i, *_: (0, 0)),
                      pl.BlockSpec(memory_space=pl.ANY)],
            out_specs=pl.BlockSpec((tm, d), row),
            scratch_shapes=[pltpu.VMEM((2, _sorted_rows(tm), d), BF16), pltpu.SemaphoreType.DMA((2,))]),
        out_shape=jax.ShapeDtypeStruct((t, d), F32),
        compiler_params=_params(1),
        name="combine",
    )(cnt, lo, dst, tot, x2d, route, g, ys)


def router_params(rg_w, rg_b, re_w, re_b):
    depth = rg_w.shape[0]
    pad = LANE - N_GROUPS - N_EXPERTS
    wr = jnp.concatenate([rg_w, re_w, jnp.zeros((depth, D_MODEL, pad), F32)], axis=2)
    wr_hi = wr.astype(BF16)
    wr = jnp.stack([wr_hi, (wr - wr_hi.astype(F32)).astype(BF16)], axis=1)
    br = jnp.concatenate([rg_b, re_b, jnp.zeros((depth, pad), F32)], axis=1)[:, None, :]
    return wr, br


def moe(x2d, g2, wr, br, w_gate, w_up, w_down, layer, g_final, final, tm, bm):
    t = x2d.shape[0]
    nt = t // tm
    route, meta = router(x2d, g2, wr, br, layer, tm)
    meta = meta.reshape(nt, SUBLANE, LANE)[:, :, :N_EXPERTS].astype(I32)
    cnt, earlier, lo = meta[:, 0], meta[:, 1], meta[:, 2]
    counts = meta[nt - 1, 3]
    padded = (counts + bm - 1) // bm * bm
    ends = jnp.cumsum(padded)
    offs = ends - padded
    dst = offs[None, :] + earlier
    n_rows = -(-(2 * t + nt * N_EXPERTS * (RUN_ALIGN - 1)) // bm) * bm + N_EXPERTS * bm
    nused = (ends[-1] // bm).astype(I32)
    blk_start = jnp.arange(n_rows // bm, dtype=I32) * bm
    bexp = jnp.sum((ends[None, :] <= blk_start[:, None]).astype(I32), axis=1)
    last = jnp.sum((ends <= (nused - 1) * bm).astype(I32))
    bexp = jnp.minimum(bexp, last)
    tot = jnp.sum(cnt, axis=1)
    cnt, lo, dst = cnt.reshape(-1), lo.reshape(-1), dst.reshape(-1)
    xs = dispatch(cnt, lo, dst, tot, offs + counts, padded - counts, nused[None], x2d, g2, layer, route,
                  n_rows, tm, bm)
    ar = jnp.arange(N_EXPERTS, dtype=I32)
    active = padded > 0
    rank = jnp.cumsum(active.astype(I32)) - active.astype(I32)
    later = (ar[None, :] > ar[:, None]) & active[None, :]
    next_e = jnp.min(jnp.where(later, ar[None, :], N_EXPERTS), axis=1).astype(I32)
    ys = experts(bexp, nused[None], (rank % 2)[bexp], next_e[bexp], xs, w_gate, w_up, w_down, layer, bm)
    return combine(cnt, lo, dst, tot, x2d, route, g_final[None, :], ys, tm, final)


def kernel(x, norm_mix_g, w_in, b_gate, pool_w, pool_scale, sgu_ln_g, sgu_ln_b, sgu_w, sgu_b, ssm_a_re, ssm_a_im, ssm_log_dt, ssm_b_re, ssm_b_im, ssm_c_re, ssm_c_im, ssm_d, glu_w, glu_b, w_branch, w_out, norm_ffn_g, router_group_w, router_group_b, router_expert_w, router_expert_b, exp_w_gate, exp_w_up, exp_w_down, norm_final_g):
    batch, seq, d = x.shape
    depth = w_in.shape[0]
    tm_in, tm_mix, tm_moe, bm = 1024, 512, 256, 512
    s5_seqs = 4
    x2d = x.reshape(batch * seq, d)
    ssm_col_block = 3
    row = lambda a: a[:, None, :]
    p = {
        "norm_mix_g": row(norm_mix_g), "w_in": w_in.astype(BF16), "b_gate": row(b_gate),
        "pool_w": pool_w.astype(BF16), "pool_scale": row(pool_scale),
        "sgu_ln_g": row(sgu_ln_g), "sgu_ln_b": row(sgu_ln_b), "sgu_w": sgu_w,
        "sgu_b_full": jnp.repeat(jnp.swapaxes(sgu_b, 1, 2), LANE, axis=2),
        "glu_w": glu_w.astype(BF16), "glu_b": row(glu_b),
        "w_branch": w_branch.astype(BF16), "w_out": w_out.astype(BF16),
    }
    bq, cq, kq, lam_q = jax.vmap(ssm_tables)(ssm_a_re, ssm_a_im, ssm_log_dt, ssm_b_re, ssm_b_im,
                                             ssm_c_re, ssm_c_im, ssm_d)
    wr, br = router_params(router_group_w, router_group_b, router_expert_w, router_expert_b)
    g_ffn = row(norm_ffn_g)
    for l in range(depth):
        xc4 = ssm_input(x2d, p["norm_mix_g"], p["w_in"], l, ssm_col_block, tm_in)
        ys4 = s5_scan(xc4, bq, cq, kq, lam_q, l, batch, s5_seqs)
        x2d = mixer(x2d, ys4, p, l, batch, tm_mix)
        x2d = moe(x2d, g_ffn, wr, br, exp_w_gate, exp_w_up, exp_w_down, l, norm_final_g,
                  l == depth - 1, tm_moe, bm)
    return x2d.reshape(batch, seq, d)
```

```python
import functools
import math

import jax
import jax.numpy as jnp
from jax import lax
from jax.experimental import pallas as pl
from jax.experimental.pallas import tpu as pltpu

F32 = jnp.float32
BF16 = jnp.bfloat16
I32 = jnp.int32

D_MODEL = 1024
POOL_WINDOWS = (2, 4, 8, 16)
POOL_HALO = 16
LANE = 128
SUBLANE = 8
ROW_DTYPE = F32
RUN_ALIGN = 8 * (4 // jnp.dtype(ROW_DTYPE).itemsize)
SGU_CHUNK = 128
SSM_GROUP_DIM = 16
SSM_STATE = 64
SSM_Q = 8
SSM_SLABS = 4
GROUPS_PER_SLAB = LANE // SSM_GROUP_DIM
N_GROUPS = 4
EXPERTS_PER_GROUP = 8
N_EXPERTS = 32
EPS = 1e-6
VMEM_LIMIT = 56 * 1024 * 1024


def _rms(x, g):
    return x * lax.rsqrt(jnp.mean(x * x, axis=-1, keepdims=True) + EPS) * g


def _gelu(x):
    c = math.sqrt(2.0 / math.pi)
    return x * (0.5 * (1.0 + jnp.tanh(c * (x + 0.044715 * (x * x * x)))))


def _sigmoid(x):
    return 1.0 / (1.0 + jnp.exp(-x))


def _dot(a, b):
    return jnp.dot(a, b, preferred_element_type=F32)


def _layer_spec(arr, layer):
    nd = arr.ndim - 1
    return pl.BlockSpec((None,) + arr.shape[1:], lambda *_: (layer,) + (0,) * nd, pipeline_mode=pl.Buffered(1))


def _params(n_axes):
    return pltpu.CompilerParams(dimension_semantics=("arbitrary",) * n_axes, vmem_limit_bytes=VMEM_LIMIT)


def _xc_kernel(x_ref, g_ref, w_ref, o_ref, xc_scr):
    h = _rms(x_ref[...], g_ref[...]).astype(BF16)
    xc = _dot(h, w_ref[...])
    rows = xc_scr.shape[1] // SSM_Q
    for j in range(SSM_SLABS):
        xc_scr[j] = xc[:, j * LANE:(j + 1) * LANE]
        for s in range(SSM_Q):
            o_ref[j, :, s * LANE:(s + 1) * LANE] = xc_scr[j, pl.ds(s, rows, stride=SSM_Q), :].astype(BF16)


def ssm_input(x2d, g, w_in_bf, layer, col_block, tm):
    t = x2d.shape[0]
    width = SSM_SLABS * LANE
    return pl.pallas_call(
        _xc_kernel,
        grid=(t // tm,),
        in_specs=[pl.BlockSpec((tm, D_MODEL), lambda i: (i, 0)),
                  _layer_spec(g, layer),
                  pl.BlockSpec((None, D_MODEL, width), lambda i: (layer, 0, col_block),
                               pipeline_mode=pl.Buffered(1))],
        out_specs=pl.BlockSpec((SSM_SLABS, tm // SSM_Q, SSM_Q * LANE), lambda i: (0, i, 0)),
        out_shape=jax.ShapeDtypeStruct((SSM_SLABS, t // SSM_Q, SSM_Q * LANE), BF16),
        scratch_shapes=[pltpu.VMEM((SSM_SLABS, tm, LANE), F32)],
        compiler_params=_params(1),
        name="ssm_input",
    )(x2d, g, w_in_bf)


def ssm_tables(a_re, a_im, log_dt, b_re, b_im, c_re, c_im, d_skip):
    q, nc = SSM_Q, SSM_GROUP_DIM
    ng = a_re.shape[0]
    sl = ng // GROUPS_PER_SLAB
    dt = jnp.exp(log_dt)[:, None]

    def powers(kvec):
        kk = kvec[:, None, None]
        mag = jnp.exp(a_re * dt * kk)
        return mag * jnp.cos(a_im * dt * kk), mag * jnp.sin(a_im * dt * kk)

    pw_re, pw_im = powers(jnp.arange(q + 1, dtype=F32))
    nr, ni = pw_re[1] - 1.0, pw_im[1]
    den = a_re * a_re + a_im * a_im
    fr, fi = (nr * a_re + ni * a_im) / den, (ni * a_re - nr * a_im) / den
    bb_re = fr[..., None] * b_re - fi[..., None] * b_im
    bb_im = fr[..., None] * b_im + fi[..., None] * b_re

    def to_slab_rows(a, lead):
        a = a.reshape(lead, sl, GROUPS_PER_SLAB, a.shape[2], a.shape[3]).transpose(1, 0, 2, 3, 4)
        return a.reshape(sl, -1, a.shape[-1]).astype(BF16)

    rev_re, rev_im = powers((q - 1.0) - jnp.arange(q, dtype=F32))
    bt_re, bt_im = bb_re.transpose(0, 2, 1), bb_im.transpose(0, 2, 1)
    m_re = rev_re[:, :, None, :] * bt_re[None] - rev_im[:, :, None, :] * bt_im[None]
    m_im = rev_re[:, :, None, :] * bt_im[None] + rev_im[:, :, None, :] * bt_re[None]
    bq = to_slab_rows(jnp.concatenate([m_re, m_im], axis=-1), q)
    ct_re, ct_im = c_re.transpose(0, 2, 1), c_im.transpose(0, 2, 1)
    ct_re, ct_im = jnp.tile(ct_re, (1, 1, q + 1)), jnp.tile(ct_im, (1, 1, q + 1))
    pl_re = jnp.repeat(pw_re.transpose(1, 2, 0), nc, axis=-1)
    pl_im = jnp.repeat(pw_im.transpose(1, 2, 0), nc, axis=-1)
    r_re = ct_re * pl_re - ct_im * pl_im
    r_im = ct_re * pl_im + ct_im * pl_re
    cq = to_slab_rows(jnp.stack([r_re[..., nc:], -r_im[..., nc:]], axis=0), 2)
    hi = lax.Precision.HIGHEST
    kern = (jnp.einsum('gpd,gpl->gdl', bb_re, r_re[..., :q * nc], precision=hi)
            - jnp.einsum('gpd,gpl->gdl', bb_im, r_im[..., :q * nc], precision=hi))
    lane = jnp.arange(q * nc)[None, None, :]
    d_pad = jnp.pad(d_skip.reshape(ng, nc), ((0, 0), (0, (q - 1) * nc)))
    kern = kern + jnp.where(lane == jnp.arange(nc)[None, :, None], d_pad[:, None, :], 0.0)
    kq = jnp.stack([jnp.pad(kern[..., :(q - s) * nc], ((0, 0), (0, 0), (s * nc, 0))) for s in range(q)], axis=0)
    kq = to_slab_rows(kq, q)
    lam_q = jnp.concatenate([pw_re[q].reshape(sl, 1, -1), pw_im[q].reshape(sl, 1, -1)], axis=-1)
    return bq, cq, kq, lam_q


def _expand_block_diag(small, row_inner, col_inner):
    n_rows, n_small = small.shape
    n_cols = n_small * GROUPS_PER_SLAB
    g_bits = GROUPS_PER_SLAB.bit_length() - 1
    r_shift, c_shift = row_inner.bit_length() - 1, col_inner.bit_length() - 1
    k = lax.broadcasted_iota(I32, (n_small, n_cols), 0)
    col = lax.broadcasted_iota(I32, (n_small, n_cols), 1)
    src = ((col >> (c_shift + g_bits)) << c_shift) | (col & (col_inner - 1))
    spread = jnp.where(k == src, 1.0, 0.0).astype(BF16)
    r = lax.broadcasted_iota(I32, (n_rows, n_cols), 0)
    c = lax.broadcasted_iota(I32, (n_rows, n_cols), 1)
    same = ((r >> r_shift) & (GROUPS_PER_SLAB - 1)) == ((c >> c_shift) & (GROUPS_PER_SLAB - 1))
    return jnp.where(same, _dot(small, spread), 0.0).astype(BF16)


def _s5_kernel(u_ref, bq_ref, cq_ref, kq_ref, lam_ref, y_ref, g_scr, s_scr, t_scr, tbq, tcq, tkq, *, seqs):
    half = GROUPS_PER_SLAB * SSM_STATE
    rows = u_ref.shape[1]

    @pl.when(pl.program_id(1) == 0)
    def _():
        tbq[...] = _expand_block_diag(bq_ref[0], SSM_GROUP_DIM, SSM_STATE)
        tcq[...] = _expand_block_diag(cq_ref[0], SSM_STATE, SSM_GROUP_DIM)
        tkq[...] = _expand_block_diag(kq_ref[0], SSM_GROUP_DIM, SSM_GROUP_DIM)

    u = u_ref[0]
    g_scr[...] = _dot(u, tbq[...])
    lam = lam_ref[0]
    lre, lim = lam[:, :half], lam[:, half:]
    seq_rows = rows // seqs

    def body(n, carry):
        out = []
        for b in range(seqs):
            cre, cim = carry[2 * b], carry[2 * b + 1]
            row = b * seq_rows + n
            s_scr[pl.ds(row, 1), :] = jnp.concatenate([cre, cim], axis=1)
            g = g_scr[pl.ds(row, 1), :]
            out.append(lre * cre - lim * cim + g[:, :half])
            out.append(lre * cim + lim * cre + g[:, half:])
        return tuple(out)

    zero = jnp.zeros((1, half), F32)
    lax.fori_loop(0, seq_rows, body, (zero,) * (2 * seqs), unroll=2)
    y = _dot(s_scr[...].astype(BF16), tcq[...]) + _dot(u, tkq[...])
    for t in range(SSM_Q):
        t_scr[pl.ds(t, rows, stride=SSM_Q), :] = y[:, t * LANE:(t + 1) * LANE]
    y_ref[0] = t_scr[...].astype(BF16)


def s5_scan(u4, bq, cq, kq, lam_q, layer, batch, seqs):
    sl, rows_total, width = u4.shape
    rows = rows_total // batch * seqs
    tab = pl.BlockSpec((None, 1, width, width // GROUPS_PER_SLAB), lambda j, b: (layer, j, 0, 0))
    return pl.pallas_call(
        functools.partial(_s5_kernel, seqs=seqs),
        grid=(sl, batch // seqs),
        in_specs=[pl.BlockSpec((1, rows, width), lambda j, b: (j, b, 0)), tab, tab, tab,
                  pl.BlockSpec((None, 1, 1, width), lambda j, b: (layer, j, 0, 0))],
        out_specs=pl.BlockSpec((1, rows * SSM_Q, LANE), lambda j, b: (j, b, 0)),
        out_shape=jax.ShapeDtypeStruct((sl, rows_total * SSM_Q, LANE), BF16),
        scratch_shapes=[pltpu.VMEM((rows, width), F32), pltpu.VMEM((rows, width), F32),
                        pltpu.VMEM((rows * SSM_Q, LANE), F32)] + [pltpu.VMEM((width, width), BF16)] * 3,
        compiler_params=_params(2),
        name="s5_scan",
    )(u4, bq, cq, kq, lam_q)


def _mixer_kernel(x_ref, ys_ref, ng_ref, win_ref, bgate_ref, poolw_ref, pscale_ref, lng_ref, lnb_ref,
                  sguw_ref, sgub_ref, gluw_ref, glub_ref, wbr_ref, wout_ref, o_ref, xe_scr):
    tm = x_ref.shape[0]
    width = SSM_SLABS * LANE
    a_end, b_end = width, 3 * width
    c_end = b_end + width
    i = pl.program_id(1)
    x = x_ref[...]
    hb = _rms(x, ng_ref[...]).astype(BF16)

    xa = _dot(hb, win_ref[:, 0:a_end])

    @pl.when(i == 0)
    def _():
        xe_scr[0:POOL_HALO, :] = jnp.zeros((POOL_HALO, xe_scr.shape[1]), F32)

    xe_scr[POOL_HALO:POOL_HALO + tm, :] = xa
    pos = i * tm + lax.broadcasted_iota(I32, (tm, 1), 0)
    ya_parts = []
    for gi, w in enumerate(POOL_WINDOWS):
        ch = slice(gi * LANE, (gi + 1) * LANE)
        win = xa[:, ch]
        for j in range(1, w):
            win = win + xe_scr[POOL_HALO - j:POOL_HALO - j + tm, ch]
        cnt = jnp.minimum(pos + 1, w).astype(F32)
        diff = win / cnt - xa[:, ch]
        ya_parts.append(_dot(diff.astype(BF16), poolw_ref[gi]) * pscale_ref[:, ch])
    ya = jnp.concatenate(ya_parts, axis=1)
    xe_scr[0:POOL_HALO, :] = xe_scr[tm:tm + POOL_HALO, :]

    z = _gelu(_dot(hb, win_ref[:, a_end:b_end]))
    u, v = z[:, :width], z[:, width:]
    mu = jnp.mean(v, axis=-1, keepdims=True)
    vc = v - mu
    var = jnp.mean(vc * vc, axis=-1, keepdims=True)
    vb = (vc * lax.rsqrt(var + EPS) * lng_ref[...] + lnb_ref[...]).astype(BF16)
    r_i = lax.broadcasted_iota(I32, (SGU_CHUNK, SGU_CHUNK), 0)
    c_i = lax.broadcasted_iota(I32, (SGU_CHUNK, SGU_CHUNK), 1)
    n_heads = width // LANE
    wt = [jnp.where(r_i >= c_i, sguw_ref[hd], 0.0).astype(BF16) for hd in range(n_heads)]
    s_rows = []
    for c in range(tm // SGU_CHUNK):
        rs = slice(c * SGU_CHUNK, (c + 1) * SGU_CHUNK)
        s_rows.append(jnp.concatenate(
            [_dot(wt[hd], vb[rs, hd * LANE:(hd + 1) * LANE]) for hd in range(n_heads)], axis=1)
            + sgub_ref[...])
    yb = u * jnp.concatenate(s_rows, axis=0)

    yc = _gelu(jnp.concatenate([ys_ref[j] for j in range(SSM_SLABS)], axis=1).astype(F32))
    yc = yc * _sigmoid(_dot(yc.astype(BF16), gluw_ref[...]) + glub_ref[...])

    merged = None
    for k, yk in enumerate((ya, yb, yc)):
        gate = _sigmoid(_dot(hb, win_ref[:, c_end + k * D_MODEL:c_end + (k + 1) * D_MODEL])
                        + bgate_ref[:, k * D_MODEL:(k + 1) * D_MODEL])
        term = gate * _dot(yk.astype(BF16), wbr_ref[k])
        merged = term if merged is None else merged + term
    o_ref[...] = x + _dot(merged.astype(BF16), wout_ref[...])


def mixer(x2d, ys4, p, layer, batch, tm):
    t = x2d.shape[0]
    nt = t // batch // tm
    row = lambda b, i: (b * nt + i, 0)
    slab = lambda b, i: (0, b * nt + i, 0)
    consts = [p["norm_mix_g"], p["w_in"], p["b_gate"], p["pool_w"], p["pool_scale"], p["sgu_ln_g"],
              p["sgu_ln_b"], p["sgu_w"], p["sgu_b_full"], p["glu_w"], p["glu_b"], p["w_branch"], p["w_out"]]
    return pl.pallas_call(
        _mixer_kernel,
        grid=(batch, nt),
        in_specs=[pl.BlockSpec((tm, D_MODEL), row),
                  pl.BlockSpec((SSM_SLABS, tm, LANE), slab)] + [_layer_spec(c, layer) for c in consts],
        out_specs=pl.BlockSpec((tm, D_MODEL), row),
        out_shape=jax.ShapeDtypeStruct(x2d.shape, F32),
        scratch_shapes=[pltpu.VMEM((tm + POOL_HALO, SSM_SLABS * LANE), F32)],
        compiler_params=_params(2),
        name="mixer",
    )(x2d, ys4, *consts)


def _route_tile(x, g, w_hi, w_lo, br):
    tm = x.shape[0]
    h = _rms(x, g)
    h_hi = h.astype(BF16)
    h_lo = (h - h_hi.astype(F32)).astype(BF16)
    logits = (_dot(h_hi, w_hi) + _dot(h_hi, w_lo) + _dot(h_lo, w_hi)) + br
    lane = lax.broadcasted_iota(I32, logits.shape, 1)
    neg = -jnp.inf
    gl = jnp.where(lane < N_GROUPS, logits, neg)
    gmax = jnp.max(gl, axis=-1, keepdims=True)
    gsum = jnp.sum(jnp.where(lane < N_GROUPS, jnp.exp(logits - gmax), 0.0), axis=-1, keepdims=True)
    g_val = 1.0 / gsum
    g_idx = jnp.min(jnp.where(gl == gmax, lane, LANE), axis=-1, keepdims=True)
    lo = N_GROUPS + EXPERTS_PER_GROUP * g_idx
    el = jnp.where((lane >= lo) & (lane < lo + EXPERTS_PER_GROUP), logits, neg)
    e1v = jnp.max(el, axis=-1, keepdims=True)
    e1l = jnp.min(jnp.where(el == e1v, lane, LANE), axis=-1, keepdims=True)
    el2 = jnp.where(lane == e1l, neg, el)
    e2v = jnp.max(el2, axis=-1, keepdims=True)
    e2l = jnp.min(jnp.where(el2 == e2v, lane, LANE), axis=-1, keepdims=True)
    tt = jnp.exp(e2v - e1v)
    w1 = g_val / (1.0 + tt)
    w2 = g_val * tt / (1.0 + tt)
    hit1, hit2 = lane == e1l - N_GROUPS, lane == e2l - N_GROUPS
    onehot = jnp.where(hit1 | hit2, 1.0, 0.0)
    r_i = lax.broadcasted_iota(I32, (tm, tm), 0)
    c_i = lax.broadcasted_iota(I32, (tm, tm), 1)
    tri = jnp.where(r_i > c_i, 1.0, 0.0).astype(BF16)
    prefix = _dot(tri, onehot.astype(BF16))
    count = jnp.sum(onehot, axis=0, keepdims=True)
    groups8 = jnp.floor((count + (RUN_ALIGN - 1.0)) * (1.0 / RUN_ALIGN))
    count = groups8 * RUN_ALIGN
    e_r = lax.broadcasted_iota(I32, (LANE, LANE), 0)
    e_c = lax.broadcasted_iota(I32, (LANE, LANE), 1)
    before = jnp.where(e_r < e_c, 1.0, 0.0).astype(BF16)
    run_start = _dot(jnp.broadcast_to(groups8, (SUBLANE, LANE)).astype(BF16), before)[0:1, :] * RUN_ALIGN
    local = prefix + run_start
    lp1 = jnp.sum(jnp.where(hit1, local, 0.0), axis=-1, keepdims=True)
    lp2 = jnp.sum(jnp.where(hit2, local, 0.0), axis=-1, keepdims=True)
    route = jnp.where(lane == 0, w1, jnp.where(lane == 1, w2, jnp.where(
        lane == 2, lp1, jnp.where(lane == 3, lp2, 0.0))))
    return route, count, run_start


def _router_kernel(x_ref, g_ref, wr_ref, br_ref, route_ref, meta_ref, carry_scr, *, tm):
    @pl.when(pl.program_id(0) == 0)
    def _():
        carry_scr[...] = jnp.zeros_like(carry_scr)

    tiles = [_route_tile(x_ref[k * tm:(k + 1) * tm, :], g_ref[...], wr_ref[0], wr_ref[1], br_ref[...])
             for k in range(x_ref.shape[0] // tm)]
    earlier = carry_scr[0:1, :]
    sub = lax.broadcasted_iota(I32, (SUBLANE, LANE), 0)
    for k, (route, count, run_start) in enumerate(tiles):
        route_ref[k * tm:(k + 1) * tm, :] = route
        total = earlier + count
        meta_ref[k * SUBLANE:(k + 1) * SUBLANE, :] = jnp.where(sub == 0, count, jnp.where(
            sub == 1, earlier, jnp.where(sub == 2, run_start, jnp.where(sub == 3, total, 0.0))))
        earlier = total
    carry_scr[...] = jnp.broadcast_to(earlier, carry_scr.shape)


def router(x2d, g, wr, br, layer, tm, tiles_per_step):
    t = x2d.shape[0]
    rows = tm * tiles_per_step
    row = lambda i: (i, 0)
    return pl.pallas_call(
        functools.partial(_router_kernel, tm=tm),
        grid=(t // rows,),
        in_specs=[pl.BlockSpec((rows, D_MODEL), row), _layer_spec(g, layer),
                  _layer_spec(wr, layer), _layer_spec(br, layer)],
        out_specs=[pl.BlockSpec((rows, LANE), row), pl.BlockSpec((tiles_per_step * SUBLANE, LANE), row)],
        out_shape=[jax.ShapeDtypeStruct((t, LANE), F32),
                   jax.ShapeDtypeStruct((t // tm * SUBLANE, LANE), F32)],
        scratch_shapes=[pltpu.VMEM((SUBLANE, LANE), F32)],
        compiler_params=_params(1),
        name="router",
    )(x2d, g, wr, br)


def _sort_matrix(route, k, n_sorted):
    col = lax.broadcasted_iota(I32, (route.shape[0], n_sorted), 1)
    return jnp.where(col == route[:, 2 + k:3 + k].astype(I32), 1.0, 0.0).astype(BF16)


def _sorted_rows(tm):
    return -(-(2 * tm + N_EXPERTS * (RUN_ALIGN - 1)) // LANE) * LANE


def _start_runs(cnt_ref, lo_ref, dst_ref, tile, make_copy):
    def body(e, c):
        idx = tile * N_EXPERTS + e
        n = cnt_ref[idx]

        @pl.when(n > 0)
        def _():
            make_copy(pl.multiple_of(lo_ref[idx], RUN_ALIGN), pl.multiple_of(dst_ref[idx], RUN_ALIGN),
                      pl.multiple_of(n, RUN_ALIGN)).start()
        return c
    lax.fori_loop(0, N_EXPERTS, body, 0)


def _wait_runs(tot_ref, tile, make_copy):
    make_copy(0, 0, pl.multiple_of(tot_ref[tile], RUN_ALIGN)).wait()


def _dispatch_kernel(cnt_ref, lo_ref, dst_ref, tot_ref, zst_ref, zln_ref, nused_ref, x_ref, g_ref, route_ref,
                     o_ref, srt_scr, zero_scr, sem, zsem):
    tm = x_ref.shape[0]
    bm = zero_scr.shape[0]
    i = pl.program_id(0)
    n = pl.num_programs(0)
    slot = i % 2

    def run_copy(s):
        return lambda lo, dst, cnt: pltpu.make_async_copy(
            srt_scr.at[s, pl.ds(lo, cnt)], o_ref.at[pl.ds(dst, cnt)], sem.at[s])

    def pad_copy(e):
        ln = pl.multiple_of(zln_ref[e], RUN_ALIGN)
        return pltpu.make_async_copy(zero_scr.at[pl.ds(0, ln)],
                                     o_ref.at[pl.ds(pl.multiple_of(zst_ref[e], RUN_ALIGN), ln)], zsem)

    def tail_copy(b):
        return pltpu.make_async_copy(zero_scr, o_ref.at[pl.ds(pl.multiple_of(b * bm, bm), bm)], zsem)

    @pl.when(i == 0)
    def _():
        zero_scr[...] = jnp.zeros_like(zero_scr)
        for wait in (False, True):
            def body(e, c):
                @pl.when(zln_ref[e] > 0)
                def _():
                    cp = pad_copy(e)
                    cp.wait() if wait else cp.start()
                return c
            lax.fori_loop(0, N_EXPERTS, body, 0)

            def tail(b, c):
                cp = tail_copy(b)
                cp.wait() if wait else cp.start()
                return c
            lax.fori_loop(nused_ref[0], o_ref.shape[0] // bm, tail, 0)

    @pl.when(i >= 2)
    def _():
        _wait_runs(tot_ref, i - 2, run_copy(slot))

    hb = _rms(x_ref[...], g_ref[...]).astype(BF16)
    route = route_ref[...]
    ns = srt_scr.shape[1]
    pt = jnp.maximum(_sort_matrix(route, 0, ns), _sort_matrix(route, 1, ns))
    srt_scr[slot] = lax.dot_general(pt, hb, (((0,), (0,)), ((), ())),
                                    preferred_element_type=F32).astype(ROW_DTYPE)
    _start_runs(cnt_ref, lo_ref, dst_ref, i, run_copy(slot))

    @pl.when(i == n - 1)
    def _():
        @pl.when(i >= 1)
        def _():
            _wait_runs(tot_ref, i - 1, run_copy(1 - slot))
        _wait_runs(tot_ref, i, run_copy(slot))


def dispatch(cnt, lo, dst, tot, zst, zln, nused, x2d, g, layer, route, n_rows, tm, bm):
    t, d = x2d.shape
    row = lambda i, *_: (i, 0)
    return pl.pallas_call(
        _dispatch_kernel,
        grid_spec=pltpu.PrefetchScalarGridSpec(
            num_scalar_prefetch=7,
            grid=(t // tm,),
            in_specs=[pl.BlockSpec((tm, d), row), pl.BlockSpec((None, 1, d), lambda i, *_: (layer, 0, 0)),
                      pl.BlockSpec((tm, LANE), row)],
            out_specs=pl.BlockSpec(memory_space=pl.ANY),
            scratch_shapes=[pltpu.VMEM((2, _sorted_rows(tm), d), ROW_DTYPE), pltpu.VMEM((bm, d), ROW_DTYPE),
                            pltpu.SemaphoreType.DMA((2,)), pltpu.SemaphoreType.DMA(())]),
        out_shape=jax.ShapeDtypeStruct((n_rows, d), ROW_DTYPE),
        compiler_params=_params(1),
        name="dispatch",
    )(cnt, lo, dst, tot, zst, zln, nused, x2d, g, route)


def _expert_kernel(bexp_ref, nused_ref, slot_ref, next_ref, xs_ref, wg_hbm, wu_hbm, wd_hbm, ys_ref,
                   wg_buf, wu_buf, wd_buf, wg_scr, wu_scr, wd_scr, sem, *, layer):
    i = pl.program_id(0)

    def weight_copies(e, s):
        return [pltpu.make_async_copy(hbm.at[layer, e], buf.at[s], sem.at[s])
                for hbm, buf in ((wg_hbm, wg_buf), (wu_hbm, wu_buf), (wd_hbm, wd_buf))]

    @pl.when(i < nused_ref[0])
    def _():
        e, s = bexp_ref[i], slot_ref[i]
        changed = jnp.logical_or(i == 0, e != bexp_ref[jnp.maximum(i - 1, 0)])

        @pl.when(i == 0)
        def _():
            for cp in weight_copies(e, s):
                cp.start()

        @pl.when(changed)
        def _():
            for cp in weight_copies(e, s):
                cp.wait()
            wg_scr[...] = wg_buf[s].astype(BF16)
            wu_scr[...] = wu_buf[s].astype(BF16)
            wd_scr[...] = wd_buf[s].astype(BF16)

            @pl.when(next_ref[i] < N_EXPERTS)
            def _():
                for cp in weight_copies(next_ref[i], 1 - s):
                    cp.start()

        xb = xs_ref[...].astype(BF16)
        hg = _dot(xb, wg_scr[...])
        hu = _dot(xb, wu_scr[...])
        act = (hg * _sigmoid(hg) * hu).astype(BF16)
        ys_ref[...] = _dot(act, wd_scr[...]).astype(ROW_DTYPE)

    @pl.when(i >= nused_ref[0])
    def _():
        ys_ref[...] = jnp.zeros_like(ys_ref)


def experts(bexp, nused, slot, nxt, xs, w_gate, w_up, w_down, layer, bm):
    n_rows, d = xs.shape
    de = w_gate.shape[3]
    anyspace = pl.BlockSpec(memory_space=pl.ANY)
    return pl.pallas_call(
        functools.partial(_expert_kernel, layer=layer),
        grid_spec=pltpu.PrefetchScalarGridSpec(
            num_scalar_prefetch=4,
            grid=(n_rows // bm,),
            in_specs=[pl.BlockSpec((bm, d), lambda i, bexp, nused, *_: (jnp.minimum(i, nused[0] - 1), 0)),
                      anyspace, anyspace, anyspace],
            out_specs=pl.BlockSpec((bm, d), lambda i, *_: (i, 0)),
            scratch_shapes=[pltpu.VMEM((2, d, de), F32), pltpu.VMEM((2, d, de), F32), pltpu.VMEM((2, de, d), F32),
                            pltpu.VMEM((d, de), BF16), pltpu.VMEM((d, de), BF16), pltpu.VMEM((de, d), BF16),
                            pltpu.SemaphoreType.DMA((2,))]),
        out_shape=jax.ShapeDtypeStruct(xs.shape, ROW_DTYPE),
        compiler_params=_params(1),
        name="experts",
    )(bexp, nused, slot, nxt, xs, w_gate, w_up, w_down)


def _combine_kernel(cnt_ref, lo_ref, dst_ref, tot_ref, x_ref, route_ref, g_ref, ys_ref, o_ref, buf, sem, *,
                    final):
    tm = x_ref.shape[0]
    i = pl.program_id(0)
    n = pl.num_programs(0)
    slot = i % 2

    def run_copy(s):
        return lambda lo, dst, cnt: pltpu.make_async_copy(
            ys_ref.at[pl.ds(dst, cnt)], buf.at[s, pl.ds(lo, cnt)], sem.at[s])

    @pl.when(i == 0)
    def _():
        buf[...] = jnp.zeros_like(buf)
        _start_runs(cnt_ref, lo_ref, dst_ref, 0, run_copy(0))

    @pl.when(i + 1 < n)
    def _():
        _start_runs(cnt_ref, lo_ref, dst_ref, i + 1, run_copy(1 - slot))

    _wait_runs(tot_ref, i, run_copy(slot))
    ysb = buf[slot].astype(BF16)
    route = route_ref[...]
    out = x_ref[...]
    for k in range(2):
        out = out + route[:, k:k + 1] * _dot(_sort_matrix(route, k, buf.shape[1]), ysb)
    if final:
        out = _rms(out, g_ref[...])
    o_ref[...] = out


def combine(cnt, lo, dst, tot, x2d, route, g, ys, tm, final):
    t, d = x2d.shape
    row = lambda i, *_: (i, 0)
    return pl.pallas_call(
        functools.partial(_combine_kernel, final=final),
        grid_spec=pltpu.PrefetchScalarGridSpec(
            num_scalar_prefetch=4,
            grid=(t // tm,),
            in_specs=[pl.BlockSpec((tm, d), row), pl.BlockSpec((tm, LANE), row),
                      pl.BlockSpec((1, d), lambda i, *_: (0, 0)),
                      pl.BlockSpec(memory_space=pl.ANY)],
            out_specs=pl.BlockSpec((tm, d), row),
            scratch_shapes=[pltpu.VMEM((2, _sorted_rows(tm), d), ROW_DTYPE), pltpu.SemaphoreType.DMA((2,))]),
        out_shape=jax.ShapeDtypeStruct((t, d), F32),
        compiler_params=_params(1),
        name="combine",
    )(cnt, lo, dst, tot, x2d, route, g, ys)


def router_params(rg_w, rg_b, re_w, re_b):
    depth = rg_w.shape[0]
    pad = LANE - N_GROUPS - N_EXPERTS
    wr = jnp.concatenate([rg_w, re_w, jnp.zeros((depth, D_MODEL, pad), F32)], axis=2)
    wr_hi = wr.astype(BF16)
    wr = jnp.stack([wr_hi, (wr - wr_hi.astype(F32)).astype(BF16)], axis=1)
    br = jnp.concatenate([rg_b, re_b, jnp.zeros((depth, pad), F32)], axis=1)[:, None, :]
    return wr, br


def moe(x2d, g2, wr, br, w_gate, w_up, w_down, layer, g_final, final, tm, bm):
    t = x2d.shape[0]
    nt = t // tm
    route, meta = router(x2d, g2, wr, br, layer, tm, 4)
    meta = meta.reshape(nt, SUBLANE, LANE)[:, :, :N_EXPERTS].astype(I32)
    cnt, earlier, lo = meta[:, 0], meta[:, 1], meta[:, 2]
    counts = meta[nt - 1, 3]
    padded = (counts + bm - 1) // bm * bm
    ends = jnp.cumsum(padded)
    offs = ends - padded
    dst = offs[None, :] + earlier
    n_rows = -(-(2 * t + nt * N_EXPERTS * (RUN_ALIGN - 1)) // bm) * bm + N_EXPERTS * bm
    nused = (ends[-1] // bm).astype(I32)
    blk_start = jnp.arange(n_rows // bm, dtype=I32) * bm
    bexp = jnp.sum((ends[None, :] <= blk_start[:, None]).astype(I32), axis=1)
    last = jnp.sum((ends <= (nused - 1) * bm).astype(I32))
    bexp = jnp.minimum(bexp, last)
    tot = jnp.sum(cnt, axis=1)
    cnt, lo, dst = cnt.reshape(-1), lo.reshape(-1), dst.reshape(-1)
    xs = dispatch(cnt, lo, dst, tot, offs + counts, padded - counts, nused[None], x2d, g2, layer, route,
                  n_rows, tm, bm)
    ar = jnp.arange(N_EXPERTS, dtype=I32)
    active = padded > 0
    rank = jnp.cumsum(active.astype(I32)) - active.astype(I32)
    later = (ar[None, :] > ar[:, None]) & active[None, :]
    next_e = jnp.min(jnp.where(later, ar[None, :], N_EXPERTS), axis=1).astype(I32)
    ys = experts(bexp, nused[None], (rank % 2)[bexp], next_e[bexp], xs, w_gate, w_up, w_down, layer, bm)
    return combine(cnt, lo, dst, tot, x2d, route, g_final[None, :], ys, tm, final)


def kernel(x, norm_mix_g, w_in, b_gate, pool_w, pool_scale, sgu_ln_g, sgu_ln_b, sgu_w, sgu_b, ssm_a_re, ssm_a_im, ssm_log_dt, ssm_b_re, ssm_b_im, ssm_c_re, ssm_c_im, ssm_d, glu_w, glu_b, w_branch, w_out, norm_ffn_g, router_group_w, router_group_b, router_expert_w, router_expert_b, exp_w_gate, exp_w_up, exp_w_down, norm_final_g):
    batch, seq, d = x.shape
    depth = w_in.shape[0]
    tm_in, tm_mix, tm_moe, bm = 1024, 512, 256, 512
    s5_seqs = 2
    x2d = x.reshape(batch * seq, d)
    ssm_col_block = 3
    row = lambda a: a[:, None, :]
    p = {
        "norm_mix_g": row(norm_mix_g), "w_in": w_in.astype(BF16), "b_gate": row(b_gate),
        "pool_w": pool_w.astype(BF16), "pool_scale": row(pool_scale),
        "sgu_ln_g": row(sgu_ln_g), "sgu_ln_b": row(sgu_ln_b), "sgu_w": sgu_w,
        "sgu_b_full": jnp.repeat(jnp.swapaxes(sgu_b, 1, 2), LANE, axis=2),
        "glu_w": glu_w.astype(BF16), "glu_b": row(glu_b),
        "w_branch": w_branch.astype(BF16), "w_out": w_out.astype(BF16),
    }
    bq, cq, kq, lam_q = jax.vmap(ssm_tables)(ssm_a_re, ssm_a_im, ssm_log_dt, ssm_b_re, ssm_b_im,
                                             ssm_c_re, ssm_c_im, ssm_d)
    wr, br = router_params(router_group_w, router_group_b, router_expert_w, router_expert_b)
    g_ffn = row(norm_ffn_g)
    for l in range(depth):
        xc4 = ssm_input(x2d, p["norm_mix_g"], p["w_in"], l, ssm_col_block, tm_in)
        ys4 = s5_scan(xc4, bq, cq, kq, lam_q, l, batch, s5_seqs)
        x2d = mixer(x2d, ys4, p, l, batch, tm_mix)
        x2d = moe(x2d, g_ffn, wr, br, exp_w_gate, exp_w_up, exp_w_down, l, norm_final_g,
                  l == depth - 1, tm_moe, bm)
    return x2d.reshape(batch, seq, d)
```

```python
import functools
import math

import jax
import jax.numpy as jnp
from jax import lax
from jax.experimental import pallas as pl
from jax.experimental.pallas import tpu as pltpu

F32 = jnp.float32
BF16 = jnp.bfloat16
I32 = jnp.int32

D_MODEL = 1024
POOL_WINDOWS = (2, 4, 8, 16)
POOL_HALO = 16
LANE = 128
SUBLANE = 8
ROW_DTYPE = BF16
RUN_ALIGN = 8 * (4 // jnp.dtype(ROW_DTYPE).itemsize)
SGU_CHUNK = 128
SSM_GROUP_DIM = 16
SSM_STATE = 64
SSM_Q = 8
SSM_SLABS = 4
GROUPS_PER_SLAB = LANE // SSM_GROUP_DIM
N_GROUPS = 4
EXPERTS_PER_GROUP = 8
N_EXPERTS = 32
EPS = 1e-6
VMEM_LIMIT = 56 * 1024 * 1024


def _rms(x, g):
    return x * lax.rsqrt(jnp.mean(x * x, axis=-1, keepdims=True) + EPS) * g


def _gelu(x):
    c = math.sqrt(2.0 / math.pi)
    return x * (0.5 * (1.0 + jnp.tanh(c * (x + 0.044715 * (x * x * x)))))


def _sigmoid(x):
    return 1.0 / (1.0 + jnp.exp(-x))


def _dot(a, b):
    return jnp.dot(a, b, preferred_element_type=F32)


def _layer_spec(arr, layer):
    nd = arr.ndim - 1
    return pl.BlockSpec((None,) + arr.shape[1:], lambda *_: (layer,) + (0,) * nd, pipeline_mode=pl.Buffered(1))


def _params(n_axes):
    return pltpu.CompilerParams(dimension_semantics=("arbitrary",) * n_axes, vmem_limit_bytes=VMEM_LIMIT)


def _xc_kernel(x_ref, g_ref, w_ref, o_ref, xc_scr):
    h = _rms(x_ref[...], g_ref[...]).astype(BF16)
    xc = _dot(h, w_ref[...])
    rows = xc_scr.shape[1] // SSM_Q
    for j in range(SSM_SLABS):
        xc_scr[j] = xc[:, j * LANE:(j + 1) * LANE]
        for s in range(SSM_Q):
            o_ref[j, :, s * LANE:(s + 1) * LANE] = xc_scr[j, pl.ds(s, rows, stride=SSM_Q), :].astype(BF16)


def ssm_input(x2d, g, w_in_bf, layer, col_block, tm):
    t = x2d.shape[0]
    width = SSM_SLABS * LANE
    return pl.pallas_call(
        _xc_kernel,
        grid=(t // tm,),
        in_specs=[pl.BlockSpec((tm, D_MODEL), lambda i: (i, 0)),
                  _layer_spec(g, layer),
                  pl.BlockSpec((None, D_MODEL, width), lambda i: (layer, 0, col_block),
                               pipeline_mode=pl.Buffered(1))],
        out_specs=pl.BlockSpec((SSM_SLABS, tm // SSM_Q, SSM_Q * LANE), lambda i: (0, i, 0)),
        out_shape=jax.ShapeDtypeStruct((SSM_SLABS, t // SSM_Q, SSM_Q * LANE), BF16),
        scratch_shapes=[pltpu.VMEM((SSM_SLABS, tm, LANE), F32)],
        compiler_params=_params(1),
        name="ssm_input",
    )(x2d, g, w_in_bf)


def ssm_tables(a_re, a_im, log_dt, b_re, b_im, c_re, c_im, d_skip):
    q, nc = SSM_Q, SSM_GROUP_DIM
    ng = a_re.shape[0]
    sl = ng // GROUPS_PER_SLAB
    dt = jnp.exp(log_dt)[:, None]

    def powers(kvec):
        kk = kvec[:, None, None]
        mag = jnp.exp(a_re * dt * kk)
        return mag * jnp.cos(a_im * dt * kk), mag * jnp.sin(a_im * dt * kk)

    pw_re, pw_im = powers(jnp.arange(q + 1, dtype=F32))
    nr, ni = pw_re[1] - 1.0, pw_im[1]
    den = a_re * a_re + a_im * a_im
    fr, fi = (nr * a_re + ni * a_im) / den, (ni * a_re - nr * a_im) / den
    bb_re = fr[..., None] * b_re - fi[..., None] * b_im
    bb_im = fr[..., None] * b_im + fi[..., None] * b_re

    def to_slab_rows(a, lead):
        a = a.reshape(lead, sl, GROUPS_PER_SLAB, a.shape[2], a.shape[3]).transpose(1, 0, 2, 3, 4)
        return a.reshape(sl, -1, a.shape[-1]).astype(BF16)

    rev_re, rev_im = powers((q - 1.0) - jnp.arange(q, dtype=F32))
    bt_re, bt_im = bb_re.transpose(0, 2, 1), bb_im.transpose(0, 2, 1)
    m_re = rev_re[:, :, None, :] * bt_re[None] - rev_im[:, :, None, :] * bt_im[None]
    m_im = rev_re[:, :, None, :] * bt_im[None] + rev_im[:, :, None, :] * bt_re[None]
    bq = to_slab_rows(jnp.concatenate([m_re, m_im], axis=-1), q)
    ct_re, ct_im = c_re.transpose(0, 2, 1), c_im.transpose(0, 2, 1)
    ct_re, ct_im = jnp.tile(ct_re, (1, 1, q + 1)), jnp.tile(ct_im, (1, 1, q + 1))
    pl_re = jnp.repeat(pw_re.transpose(1, 2, 0), nc, axis=-1)
    pl_im = jnp.repeat(pw_im.transpose(1, 2, 0), nc, axis=-1)
    r_re = ct_re * pl_re - ct_im * pl_im
    r_im = ct_re * pl_im + ct_im * pl_re
    cq = to_slab_rows(jnp.stack([r_re[..., nc:], -r_im[..., nc:]], axis=0), 2)
    hi = lax.Precision.HIGHEST
    kern = (jnp.einsum('gpd,gpl->gdl', bb_re, r_re[..., :q * nc], precision=hi)
            - jnp.einsum('gpd,gpl->gdl', bb_im, r_im[..., :q * nc], precision=hi))
    lane = jnp.arange(q * nc)[None, None, :]
    d_pad = jnp.pad(d_skip.reshape(ng, nc), ((0, 0), (0, (q - 1) * nc)))
    kern = kern + jnp.where(lane == jnp.arange(nc)[None, :, None], d_pad[:, None, :], 0.0)
    kq = jnp.stack([jnp.pad(kern[..., :(q - s) * nc], ((0, 0), (0, 0), (s * nc, 0))) for s in range(q)], axis=0)
    kq = to_slab_rows(kq, q)
    lam_q = jnp.concatenate([pw_re[q].reshape(sl, 1, -1), pw_im[q].reshape(sl, 1, -1)], axis=-1)
    return bq, cq, kq, lam_q


def _expand_block_diag(small, row_inner, col_inner):
    n_rows, n_small = small.shape
    n_cols = n_small * GROUPS_PER_SLAB
    g_bits = GROUPS_PER_SLAB.bit_length() - 1
    r_shift, c_shift = row_inner.bit_length() - 1, col_inner.bit_length() - 1
    k = lax.broadcasted_iota(I32, (n_small, n_cols), 0)
    col = lax.broadcasted_iota(I32, (n_small, n_cols), 1)
    src = ((col >> (c_shift + g_bits)) << c_shift) | (col & (col_inner - 1))
    spread = jnp.where(k == src, 1.0, 0.0).astype(BF16)
    r = lax.broadcasted_iota(I32, (n_rows, n_cols), 0)
    c = lax.broadcasted_iota(I32, (n_rows, n_cols), 1)
    same = ((r >> r_shift) & (GROUPS_PER_SLAB - 1)) == ((c >> c_shift) & (GROUPS_PER_SLAB - 1))
    return jnp.where(same, _dot(small, spread), 0.0).astype(BF16)


def _s5_kernel(u_ref, bq_ref, cq_ref, kq_ref, lam_ref, y_ref, g_scr, s_scr, t_scr, tbq, tcq, tkq, *, seqs):
    half = GROUPS_PER_SLAB * SSM_STATE
    rows = u_ref.shape[1]

    @pl.when(pl.program_id(1) == 0)
    def _():
        tbq[...] = _expand_block_diag(bq_ref[0], SSM_GROUP_DIM, SSM_STATE)
        tcq[...] = _expand_block_diag(cq_ref[0], SSM_STATE, SSM_GROUP_DIM)
        tkq[...] = _expand_block_diag(kq_ref[0], SSM_GROUP_DIM, SSM_GROUP_DIM)

    u = u_ref[0]
    g_scr[...] = _dot(u, tbq[...])
    lam = lam_ref[0]
    lre, lim = lam[:, :half], lam[:, half:]
    seq_rows = rows // seqs

    def body(n, carry):
        out = []
        for b in range(seqs):
            cre, cim = carry[2 * b], carry[2 * b + 1]
            row = b * seq_rows + n
            s_scr[pl.ds(row, 1), :] = jnp.concatenate([cre, cim], axis=1)
            g = g_scr[pl.ds(row, 1), :]
            out.append(lre * cre - lim * cim + g[:, :half])
            out.append(lre * cim + lim * cre + g[:, half:])
        return tuple(out)

    zero = jnp.zeros((1, half), F32)
    lax.fori_loop(0, seq_rows, body, (zero,) * (2 * seqs), unroll=2)
    y = _dot(s_scr[...].astype(BF16), tcq[...]) + _dot(u, tkq[...])
    for t in range(SSM_Q):
        t_scr[pl.ds(t, rows, stride=SSM_Q), :] = y[:, t * LANE:(t + 1) * LANE]
    y_ref[0] = t_scr[...].astype(BF16)


def s5_scan(u4, bq, cq, kq, lam_q, layer, batch, seqs):
    sl, rows_total, width = u4.shape
    rows = rows_total // batch * seqs
    tab = pl.BlockSpec((None, 1, width, width // GROUPS_PER_SLAB), lambda j, b: (layer, j, 0, 0))
    return pl.pallas_call(
        functools.partial(_s5_kernel, seqs=seqs),
        grid=(sl, batch // seqs),
        in_specs=[pl.BlockSpec((1, rows, width), lambda j, b: (j, b, 0)), tab, tab, tab,
                  pl.BlockSpec((None, 1, 1, width), lambda j, b: (layer, j, 0, 0))],
        out_specs=pl.BlockSpec((1, rows * SSM_Q, LANE), lambda j, b: (j, b, 0)),
        out_shape=jax.ShapeDtypeStruct((sl, rows_total * SSM_Q, LANE), BF16),
        scratch_shapes=[pltpu.VMEM((rows, width), F32), pltpu.VMEM((rows, width), F32),
                        pltpu.VMEM((rows * SSM_Q, LANE), F32)] + [pltpu.VMEM((width, width), BF16)] * 3,
        compiler_params=_params(2),
        name="s5_scan",
    )(u4, bq, cq, kq, lam_q)


def _mixer_kernel(x_ref, ys_ref, ng_ref, win_ref, bgate_ref, poolw_ref, pscale_ref, lng_ref, lnb_ref,
                  sguw_ref, sgub_ref, gluw_ref, glub_ref, wbr_ref, wout_ref, o_ref, xe_scr):
    tm = x_ref.shape[0]
    width = SSM_SLABS * LANE
    a_end, b_end = width, 3 * width
    c_end = b_end + width
    i = pl.program_id(1)
    x = x_ref[...]
    hb = _rms(x, ng_ref[...]).astype(BF16)

    xa = _dot(hb, win_ref[:, 0:a_end])

    @pl.when(i == 0)
    def _():
        xe_scr[0:POOL_HALO, :] = jnp.zeros((POOL_HALO, xe_scr.shape[1]), F32)

    xe_scr[POOL_HALO:POOL_HALO + tm, :] = xa
    pos = i * tm + lax.broadcasted_iota(I32, (tm, 1), 0)
    ya_parts = []
    for gi, w in enumerate(POOL_WINDOWS):
        ch = slice(gi * LANE, (gi + 1) * LANE)
        win = xa[:, ch]
        for j in range(1, w):
            win = win + xe_scr[POOL_HALO - j:POOL_HALO - j + tm, ch]
        cnt = jnp.minimum(pos + 1, w).astype(F32)
        diff = win / cnt - xa[:, ch]
        ya_parts.append(_dot(diff.astype(BF16), poolw_ref[gi]) * pscale_ref[:, ch])
    ya = jnp.concatenate(ya_parts, axis=1)
    xe_scr[0:POOL_HALO, :] = xe_scr[tm:tm + POOL_HALO, :]

    z = _gelu(_dot(hb, win_ref[:, a_end:b_end]))
    u, v = z[:, :width], z[:, width:]
    mu = jnp.mean(v, axis=-1, keepdims=True)
    vc = v - mu
    var = jnp.mean(vc * vc, axis=-1, keepdims=True)
    vb = (vc * lax.rsqrt(var + EPS) * lng_ref[...] + lnb_ref[...]).astype(BF16)
    r_i = lax.broadcasted_iota(I32, (SGU_CHUNK, SGU_CHUNK), 0)
    c_i = lax.broadcasted_iota(I32, (SGU_CHUNK, SGU_CHUNK), 1)
    n_heads = width // LANE
    wt = [jnp.where(r_i >= c_i, sguw_ref[hd], 0.0).astype(BF16) for hd in range(n_heads)]
    s_rows = []
    for c in range(tm // SGU_CHUNK):
        rs = slice(c * SGU_CHUNK, (c + 1) * SGU_CHUNK)
        s_rows.append(jnp.concatenate(
            [_dot(wt[hd], vb[rs, hd * LANE:(hd + 1) * LANE]) for hd in range(n_heads)], axis=1)
            + sgub_ref[...])
    yb = u * jnp.concatenate(s_rows, axis=0)

    yc = _gelu(jnp.concatenate([ys_ref[j] for j in range(SSM_SLABS)], axis=1).astype(F32))
    yc = yc * _sigmoid(_dot(yc.astype(BF16), gluw_ref[...]) + glub_ref[...])

    merged = None
    for k, yk in enumerate((ya, yb, yc)):
        gate = _sigmoid(_dot(hb, win_ref[:, c_end + k * D_MODEL:c_end + (k + 1) * D_MODEL])
                        + bgate_ref[:, k * D_MODEL:(k + 1) * D_MODEL])
        term = gate * _dot(yk.astype(BF16), wbr_ref[k])
        merged = term if merged is None else merged + term
    o_ref[...] = x + _dot(merged.astype(BF16), wout_ref[...])


def mixer(x2d, ys4, p, layer, batch, tm):
    t = x2d.shape[0]
    nt = t // batch // tm
    row = lambda b, i: (b * nt + i, 0)
    slab = lambda b, i: (0, b * nt + i, 0)
    consts = [p["norm_mix_g"], p["w_in"], p["b_gate"], p["pool_w"], p["pool_scale"], p["sgu_ln_g"],
              p["sgu_ln_b"], p["sgu_w"], p["sgu_b_full"], p["glu_w"], p["glu_b"], p["w_branch"], p["w_out"]]
    return pl.pallas_call(
        _mixer_kernel,
        grid=(batch, nt),
        in_specs=[pl.BlockSpec((tm, D_MODEL), row),
                  pl.BlockSpec((SSM_SLABS, tm, LANE), slab)] + [_layer_spec(c, layer) for c in consts],
        out_specs=pl.BlockSpec((tm, D_MODEL), row),
        out_shape=jax.ShapeDtypeStruct(x2d.shape, F32),
        scratch_shapes=[pltpu.VMEM((tm + POOL_HALO, SSM_SLABS * LANE), F32)],
        compiler_params=_params(2),
        name="mixer",
    )(x2d, ys4, *consts)


def _route_tile(x, g, w_hi, w_lo, br):
    tm = x.shape[0]
    h = _rms(x, g)
    h_hi = h.astype(BF16)
    h_lo = (h - h_hi.astype(F32)).astype(BF16)
    logits = (_dot(h_hi, w_hi) + _dot(h_hi, w_lo) + _dot(h_lo, w_hi)) + br
    lane = lax.broadcasted_iota(I32, logits.shape, 1)
    neg = -jnp.inf
    gl = jnp.where(lane < N_GROUPS, logits, neg)
    gmax = jnp.max(gl, axis=-1, keepdims=True)
    gsum = jnp.sum(jnp.where(lane < N_GROUPS, jnp.exp(logits - gmax), 0.0), axis=-1, keepdims=True)
    g_val = 1.0 / gsum
    g_idx = jnp.min(jnp.where(gl == gmax, lane, LANE), axis=-1, keepdims=True)
    lo = N_GROUPS + EXPERTS_PER_GROUP * g_idx
    el = jnp.where((lane >= lo) & (lane < lo + EXPERTS_PER_GROUP), logits, neg)
    e1v = jnp.max(el, axis=-1, keepdims=True)
    e1l = jnp.min(jnp.where(el == e1v, lane, LANE), axis=-1, keepdims=True)
    el2 = jnp.where(lane == e1l, neg, el)
    e2v = jnp.max(el2, axis=-1, keepdims=True)
    e2l = jnp.min(jnp.where(el2 == e2v, lane, LANE), axis=-1, keepdims=True)
    tt = jnp.exp(e2v - e1v)
    w1 = g_val / (1.0 + tt)
    w2 = g_val * tt / (1.0 + tt)
    hit1, hit2 = lane == e1l - N_GROUPS, lane == e2l - N_GROUPS
    onehot = jnp.where(hit1 | hit2, 1.0, 0.0)
    r_i = lax.broadcasted_iota(I32, (tm, tm), 0)
    c_i = lax.broadcasted_iota(I32, (tm, tm), 1)
    tri = jnp.where(r_i > c_i, 1.0, 0.0).astype(BF16)
    prefix = _dot(tri, onehot.astype(BF16))
    count = jnp.sum(onehot, axis=0, keepdims=True)
    groups8 = jnp.floor((count + (RUN_ALIGN - 1.0)) * (1.0 / RUN_ALIGN))
    count = groups8 * RUN_ALIGN
    e_r = lax.broadcasted_iota(I32, (LANE, LANE), 0)
    e_c = lax.broadcasted_iota(I32, (LANE, LANE), 1)
    before = jnp.where(e_r < e_c, 1.0, 0.0).astype(BF16)
    run_start = _dot(jnp.broadcast_to(groups8, (SUBLANE, LANE)).astype(BF16), before)[0:1, :] * RUN_ALIGN
    local = prefix + run_start
    lp1 = jnp.sum(jnp.where(hit1, local, 0.0), axis=-1, keepdims=True)
    lp2 = jnp.sum(jnp.where(hit2, local, 0.0), axis=-1, keepdims=True)
    route = jnp.where(lane == 0, w1, jnp.where(lane == 1, w2, jnp.where(
        lane == 2, lp1, jnp.where(lane == 3, lp2, 0.0))))
    return route, count, run_start


def _router_kernel(x_ref, g_ref, wr_ref, br_ref, route_ref, meta_ref, carry_scr, *, tm):
    @pl.when(pl.program_id(0) == 0)
    def _():
        carry_scr[...] = jnp.zeros_like(carry_scr)

    tiles = [_route_tile(x_ref[k * tm:(k + 1) * tm, :], g_ref[...], wr_ref[0], wr_ref[1], br_ref[...])
             for k in range(x_ref.shape[0] // tm)]
    earlier = carry_scr[0:1, :]
    sub = lax.broadcasted_iota(I32, (SUBLANE, LANE), 0)
    for k, (route, count, run_start) in enumerate(tiles):
        route_ref[k * tm:(k + 1) * tm, :] = route
        total = earlier + count
        meta_ref[k * SUBLANE:(k + 1) * SUBLANE, :] = jnp.where(sub == 0, count, jnp.where(
            sub == 1, earlier, jnp.where(sub == 2, run_start, jnp.where(sub == 3, total, 0.0))))
        earlier = total
    carry_scr[...] = jnp.broadcast_to(earlier, carry_scr.shape)


def router(x2d, g, wr, br, layer, tm, tiles_per_step):
    t = x2d.shape[0]
    rows = tm * tiles_per_step
    row = lambda i: (i, 0)
    return pl.pallas_call(
        functools.partial(_router_kernel, tm=tm),
        grid=(t // rows,),
        in_specs=[pl.BlockSpec((rows, D_MODEL), row), _layer_spec(g, layer),
                  _layer_spec(wr, layer), _layer_spec(br, layer)],
        out_specs=[pl.BlockSpec((rows, LANE), row), pl.BlockSpec((tiles_per_step * SUBLANE, LANE), row)],
        out_shape=[jax.ShapeDtypeStruct((t, LANE), F32),
                   jax.ShapeDtypeStruct((t // tm * SUBLANE, LANE), F32)],
        scratch_shapes=[pltpu.VMEM((SUBLANE, LANE), F32)],
        compiler_params=_params(1),
        name="router",
    )(x2d, g, wr, br)


def _sort_matrix(route, k, n_sorted):
    col = lax.broadcasted_iota(I32, (route.shape[0], n_sorted), 1)
    return jnp.where(col == route[:, 2 + k:3 + k].astype(I32), 1.0, 0.0).astype(BF16)


def _sorted_rows(tm):
    return -(-(2 * tm + N_EXPERTS * (RUN_ALIGN - 1)) // LANE) * LANE


def _start_runs(cnt_ref, lo_ref, dst_ref, tile, make_copy):
    def body(e, c):
        idx = tile * N_EXPERTS + e
        n = cnt_ref[idx]

        @pl.when(n > 0)
        def _():
            make_copy(pl.multiple_of(lo_ref[idx], RUN_ALIGN), pl.multiple_of(dst_ref[idx], RUN_ALIGN),
                      pl.multiple_of(n, RUN_ALIGN)).start()
        return c
    lax.fori_loop(0, N_EXPERTS, body, 0)


def _wait_runs(tot_ref, tile, make_copy):
    make_copy(0, 0, pl.multiple_of(tot_ref[tile], RUN_ALIGN)).wait()


def _dispatch_kernel(cnt_ref, lo_ref, dst_ref, tot_ref, zst_ref, zln_ref, nused_ref, x_ref, g_ref, route_ref,
                     o_ref, srt_scr, zero_scr, sem, zsem):
    tm = x_ref.shape[0]
    bm = zero_scr.shape[0]
    i = pl.program_id(0)
    n = pl.num_programs(0)
    slot = i % 2

    def run_copy(s):
        return lambda lo, dst, cnt: pltpu.make_async_copy(
            srt_scr.at[s, pl.ds(lo, cnt)], o_ref.at[pl.ds(dst, cnt)], sem.at[s])

    def pad_copy(e):
        ln = pl.multiple_of(zln_ref[e], RUN_ALIGN)
        return pltpu.make_async_copy(zero_scr.at[pl.ds(0, ln)],
                                     o_ref.at[pl.ds(pl.multiple_of(zst_ref[e], RUN_ALIGN), ln)], zsem)

    def tail_copy(b):
        return pltpu.make_async_copy(zero_scr, o_ref.at[pl.ds(pl.multiple_of(b * bm, bm), bm)], zsem)

    @pl.when(i == 0)
    def _():
        zero_scr[...] = jnp.zeros_like(zero_scr)
        for wait in (False, True):
            def body(e, c):
                @pl.when(zln_ref[e] > 0)
                def _():
                    cp = pad_copy(e)
                    cp.wait() if wait else cp.start()
                return c
            lax.fori_loop(0, N_EXPERTS, body, 0)

            def tail(b, c):
                cp = tail_copy(b)
                cp.wait() if wait else cp.start()
                return c
            lax.fori_loop(nused_ref[0], o_ref.shape[0] // bm, tail, 0)

    @pl.when(i >= 2)
    def _():
        _wait_runs(tot_ref, i - 2, run_copy(slot))

    hb = _rms(x_ref[...], g_ref[...]).astype(BF16)
    route = route_ref[...]
    ns = srt_scr.shape[1]
    pt = jnp.maximum(_sort_matrix(route, 0, ns), _sort_matrix(route, 1, ns))
    srt_scr[slot] = lax.dot_general(pt, hb, (((0,), (0,)), ((), ())),
                                    preferred_element_type=F32).astype(ROW_DTYPE)
    _start_runs(cnt_ref, lo_ref, dst_ref, i, run_copy(slot))

    @pl.when(i == n - 1)
    def _():
        @pl.when(i >= 1)
        def _():
            _wait_runs(tot_ref, i - 1, run_copy(1 - slot))
        _wait_runs(tot_ref, i, run_copy(slot))


def dispatch(cnt, lo, dst, tot, zst, zln, nused, x2d, g, layer, route, n_rows, tm, bm):
    t, d = x2d.shape
    row = lambda i, *_: (i, 0)
    return pl.pallas_call(
        _dispatch_kernel,
        grid_spec=pltpu.PrefetchScalarGridSpec(
            num_scalar_prefetch=7,
            grid=(t // tm,),
            in_specs=[pl.BlockSpec((tm, d), row), pl.BlockSpec((None, 1, d), lambda i, *_: (layer, 0, 0)),
                      pl.BlockSpec((tm, LANE), row)],
            out_specs=pl.BlockSpec(memory_space=pl.ANY),
            scratch_shapes=[pltpu.VMEM((2, _sorted_rows(tm), d), ROW_DTYPE), pltpu.VMEM((bm, d), ROW_DTYPE),
                            pltpu.SemaphoreType.DMA((2,)), pltpu.SemaphoreType.DMA(())]),
        out_shape=jax.ShapeDtypeStruct((n_rows, d), ROW_DTYPE),
        compiler_params=_params(1),
        name="dispatch",
    )(cnt, lo, dst, tot, zst, zln, nused, x2d, g, route)


def _expert_kernel(bexp_ref, nused_ref, slot_ref, next_ref, xs_ref, wg_hbm, wu_hbm, wd_hbm, ys_ref,
                   wg_buf, wu_buf, wd_buf, wg_scr, wu_scr, wd_scr, sem, *, layer):
    i = pl.program_id(0)

    def weight_copies(e, s):
        return [pltpu.make_async_copy(hbm.at[layer, e], buf.at[s], sem.at[s])
                for hbm, buf in ((wg_hbm, wg_buf), (wu_hbm, wu_buf), (wd_hbm, wd_buf))]

    @pl.when(i < nused_ref[0])
    def _():
        e, s = bexp_ref[i], slot_ref[i]
        changed = jnp.logical_or(i == 0, e != bexp_ref[jnp.maximum(i - 1, 0)])

        @pl.when(i == 0)
        def _():
            for cp in weight_copies(e, s):
                cp.start()

        @pl.when(changed)
        def _():
            for cp in weight_copies(e, s):
                cp.wait()
            wg_scr[...] = wg_buf[s].astype(BF16)
            wu_scr[...] = wu_buf[s].astype(BF16)
            wd_scr[...] = wd_buf[s].astype(BF16)

            @pl.when(next_ref[i] < N_EXPERTS)
            def _():
                for cp in weight_copies(next_ref[i], 1 - s):
                    cp.start()

        xb = xs_ref[...].astype(BF16)
        hg = _dot(xb, wg_scr[...])
        hu = _dot(xb, wu_scr[...])
        act = (hg * _sigmoid(hg) * hu).astype(BF16)
        ys_ref[...] = _dot(act, wd_scr[...]).astype(ROW_DTYPE)

    @pl.when(i >= nused_ref[0])
    def _():
        ys_ref[...] = jnp.zeros_like(ys_ref)


def experts(bexp, nused, slot, nxt, xs, w_gate, w_up, w_down, layer, bm):
    n_rows, d = xs.shape
    de = w_gate.shape[3]
    anyspace = pl.BlockSpec(memory_space=pl.ANY)
    return pl.pallas_call(
        functools.partial(_expert_kernel, layer=layer),
        grid_spec=pltpu.PrefetchScalarGridSpec(
            num_scalar_prefetch=4,
            grid=(n_rows // bm,),
            in_specs=[pl.BlockSpec((bm, d), lambda i, bexp, nused, *_: (jnp.minimum(i, nused[0] - 1), 0)),
                      anyspace, anyspace, anyspace],
            out_specs=pl.BlockSpec((bm, d), lambda i, *_: (i, 0)),
            scratch_shapes=[pltpu.VMEM((2, d, de), F32), pltpu.VMEM((2, d, de), F32), pltpu.VMEM((2, de, d), F32),
                            pltpu.VMEM((d, de), BF16), pltpu.VMEM((d, de), BF16), pltpu.VMEM((de, d), BF16),
                            pltpu.SemaphoreType.DMA((2,))]),
        out_shape=jax.ShapeDtypeStruct(xs.shape, ROW_DTYPE),
        compiler_params=_params(1),
        name="experts",
    )(bexp, nused, slot, nxt, xs, w_gate, w_up, w_down)


def _combine_kernel(cnt_ref, lo_ref, dst_ref, tot_ref, x_ref, route_ref, g_ref, ys_ref, o_ref, buf, sem, *,
                    final):
    tm = x_ref.shape[0]
    i = pl.program_id(0)
    n = pl.num_programs(0)
    slot = i % 2

    def run_copy(s):
        return lambda lo, dst, cnt: pltpu.make_async_copy(
            ys_ref.at[pl.ds(dst, cnt)], buf.at[s, pl.ds(lo, cnt)], sem.at[s])

    @pl.when(i == 0)
    def _():
        buf[...] = jnp.zeros_like(buf)
        _start_runs(cnt_ref, lo_ref, dst_ref, 0, run_copy(0))

    @pl.when(i + 1 < n)
    def _():
        _start_runs(cnt_ref, lo_ref, dst_ref, i + 1, run_copy(1 - slot))

    _wait_runs(tot_ref, i, run_copy(slot))
    ysb = buf[slot].astype(BF16)
    route = route_ref[...]
    out = x_ref[...]
    for k in range(2):
        out = out + route[:, k:k + 1] * _dot(_sort_matrix(route, k, buf.shape[1]), ysb)
    if final:
        out = _rms(out, g_ref[...])
    o_ref[...] = out


def combine(cnt, lo, dst, tot, x2d, route, g, ys, tm, final):
    t, d = x2d.shape
    row = lambda i, *_: (i, 0)
    return pl.pallas_call(
        functools.partial(_combine_kernel, final=final),
        grid_spec=pltpu.PrefetchScalarGridSpec(
            num_scalar_prefetch=4,
            grid=(t // tm,),
            in_specs=[pl.BlockSpec((tm, d), row), pl.BlockSpec((tm, LANE), row),
                      pl.BlockSpec((1, d), lambda i, *_: (0, 0)),
                      pl.BlockSpec(memory_space=pl.ANY)],
            out_specs=pl.BlockSpec((tm, d), row),
            scratch_shapes=[pltpu.VMEM((2, _sorted_rows(tm), d), ROW_DTYPE), pltpu.SemaphoreType.DMA((2,))]),
        out_shape=jax.ShapeDtypeStruct((t, d), F32),
        compiler_params=_params(1),
        name="combine",
    )(cnt, lo, dst, tot, x2d, route, g, ys)


def router_params(rg_w, rg_b, re_w, re_b):
    depth = rg_w.shape[0]
    pad = LANE - N_GROUPS - N_EXPERTS
    wr = jnp.concatenate([rg_w, re_w, jnp.zeros((depth, D_MODEL, pad), F32)], axis=2)
    wr_hi = wr.astype(BF16)
    wr = jnp.stack([wr_hi, (wr - wr_hi.astype(F32)).astype(BF16)], axis=1)
    br = jnp.concatenate([rg_b, re_b, jnp.zeros((depth, pad), F32)], axis=1)[:, None, :]
    return wr, br


def moe(x2d, g2, wr, br, w_gate, w_up, w_down, layer, g_final, final, tm, bm):
    t = x2d.shape[0]
    nt = t // tm
    route, meta = router(x2d, g2, wr, br, layer, tm, 4)
    meta = meta.reshape(nt, SUBLANE, LANE)[:, :, :N_EXPERTS].astype(I32)
    cnt, earlier, lo = meta[:, 0], meta[:, 1], meta[:, 2]
    counts = meta[nt - 1, 3]
    padded = (counts + bm - 1) // bm * bm
    ends = jnp.cumsum(padded)
    offs = ends - padded
    dst = offs[None, :] + earlier
    n_rows = -(-(2 * t + nt * N_EXPERTS * (RUN_ALIGN - 1)) // bm) * bm + N_EXPERTS * bm
    nused = (ends[-1] // bm).astype(I32)
    blk_start = jnp.arange(n_rows // bm, dtype=I32) * bm
    bexp = jnp.sum((ends[None, :] <= blk_start[:, None]).astype(I32), axis=1)
    last = jnp.sum((ends <= (nused - 1) * bm).astype(I32))
    bexp = jnp.minimum(bexp, last)
    tot = jnp.sum(cnt, axis=1)
    cnt, lo, dst = cnt.reshape(-1), lo.reshape(-1), dst.reshape(-1)
    xs = dispatch(cnt, lo, dst, tot, offs + counts, padded - counts, nused[None], x2d, g2, layer, route,
                  n_rows, tm, bm)
    ar = jnp.arange(N_EXPERTS, dtype=I32)
    active = padded > 0
    rank = jnp.cumsum(active.astype(I32)) - active.astype(I32)
    later = (ar[None, :] > ar[:, None]) & active[None, :]
    next_e = jnp.min(jnp.where(later, ar[None, :], N_EXPERTS), axis=1).astype(I32)
    ys = experts(bexp, nused[None], (rank % 2)[bexp], next_e[bexp], xs, w_gate, w_up, w_down, layer, bm)
    return combine(cnt, lo, dst, tot, x2d, route, g_final[None, :], ys, tm, final)


def kernel(x, norm_mix_g, w_in, b_gate, pool_w, pool_scale, sgu_ln_g, sgu_ln_b, sgu_w, sgu_b, ssm_a_re, ssm_a_im, ssm_log_dt, ssm_b_re, ssm_b_im, ssm_c_re, ssm_c_im, ssm_d, glu_w, glu_b, w_branch, w_out, norm_ffn_g, router_group_w, router_group_b, router_expert_w, router_expert_b, exp_w_gate, exp_w_up, exp_w_down, norm_final_g):
    batch, seq, d = x.shape
    depth = w_in.shape[0]
    tm_in, tm_mix, tm_moe, bm = 1024, 512, 256, 512
    s5_seqs = 2
    x2d = x.reshape(batch * seq, d)
    ssm_col_block = 3
    row = lambda a: a[:, None, :]
    p = {
        "norm_mix_g": row(norm_mix_g), "w_in": w_in.astype(BF16), "b_gate": row(b_gate),
        "pool_w": pool_w.astype(BF16), "pool_scale": row(pool_scale),
        "sgu_ln_g": row(sgu_ln_g), "sgu_ln_b": row(sgu_ln_b), "sgu_w": sgu_w,
        "sgu_b_full": jnp.repeat(jnp.swapaxes(sgu_b, 1, 2), LANE, axis=2),
        "glu_w": glu_w.astype(BF16), "glu_b": row(glu_b),
        "w_branch": w_branch.astype(BF16), "w_out": w_out.astype(BF16),
    }
    bq, cq, kq, lam_q = jax.vmap(ssm_tables)(ssm_a_re, ssm_a_im, ssm_log_dt, ssm_b_re, ssm_b_im,
                                             ssm_c_re, ssm_c_im, ssm_d)
    wr, br = router_params(router_group_w, router_group_b, router_expert_w, router_expert_b)
    g_ffn = row(norm_ffn_g)
    for l in range(depth):
        xc4 = ssm_input(x2d, p["norm_mix_g"], p["w_in"], l, ssm_col_block, tm_in)
        ys4 = s5_scan(xc4, bq, cq, kq, lam_q, l, batch, s5_seqs)
        x2d = mixer(x2d, ys4, p, l, batch, tm_mix)
        x2d = moe(x2d, g_ffn, wr, br, exp_w_gate, exp_w_up, exp_w_down, l, norm_final_g,
                  l == depth - 1, tm_moe, bm)
    return x2d.reshape(batch, seq, d)
```

```python
import functools
import math

import jax
import jax.numpy as jnp
from jax import lax
from jax.experimental import pallas as pl
from jax.experimental.pallas import tpu as pltpu

F32 = jnp.float32
BF16 = jnp.bfloat16
I32 = jnp.int32

D_MODEL = 1024
POOL_WINDOWS = (2, 4, 8, 16)
POOL_HALO = 16
LANE = 128
SUBLANE = 8
ROW_DTYPE = BF16
RUN_ALIGN = 8 * (4 // jnp.dtype(ROW_DTYPE).itemsize)
SGU_CHUNK = 128
SSM_GROUP_DIM = 16
SSM_STATE = 64
SSM_Q = 8
SSM_SLABS = 4
GROUPS_PER_SLAB = LANE // SSM_GROUP_DIM
N_GROUPS = 4
EXPERTS_PER_GROUP = 8
N_EXPERTS = 32
EPS = 1e-6
VMEM_LIMIT = 56 * 1024 * 1024


def _rms(x, g):
    return x * lax.rsqrt(jnp.mean(x * x, axis=-1, keepdims=True) + EPS) * g


def _gelu(x):
    c = math.sqrt(2.0 / math.pi)
    return x * (0.5 * (1.0 + jnp.tanh(c * (x + 0.044715 * (x * x * x)))))


def _sigmoid(x):
    return 0.5 * (1.0 + jnp.tanh(0.5 * x))


def _dot(a, b):
    return jnp.dot(a, b, preferred_element_type=F32)


def _layer_spec(arr, layer):
    nd = arr.ndim - 1
    return pl.BlockSpec((None,) + arr.shape[1:], lambda *_: (layer,) + (0,) * nd, pipeline_mode=pl.Buffered(1))


def _params(n_axes):
    return pltpu.CompilerParams(dimension_semantics=("arbitrary",) * n_axes, vmem_limit_bytes=VMEM_LIMIT)


def _write_ssm_input(x, g, w, o_ref, xc_scr):
    h = _rms(x, g).astype(BF16)
    xc = _dot(h, w)
    rows = xc_scr.shape[1] // SSM_Q
    for j in range(SSM_SLABS):
        xc_scr[j] = xc[:, j * LANE:(j + 1) * LANE]
        for s in range(SSM_Q):
            o_ref[j, :, s * LANE:(s + 1) * LANE] = xc_scr[j, pl.ds(s, rows, stride=SSM_Q), :].astype(BF16)


def _xc_kernel(x_ref, g_ref, w_ref, o_ref, xc_scr):
    _write_ssm_input(x_ref[...], g_ref[...], w_ref[...], o_ref, xc_scr)


def ssm_input(x2d, g, w_in_bf, layer, col_block, tm):
    t = x2d.shape[0]
    width = SSM_SLABS * LANE
    return pl.pallas_call(
        _xc_kernel,
        grid=(t // tm,),
        in_specs=[pl.BlockSpec((tm, D_MODEL), lambda i: (i, 0)),
                  _layer_spec(g, layer),
                  pl.BlockSpec((None, D_MODEL, width), lambda i: (layer, 0, col_block),
                               pipeline_mode=pl.Buffered(1))],
        out_specs=pl.BlockSpec((SSM_SLABS, tm // SSM_Q, SSM_Q * LANE), lambda i: (0, i, 0)),
        out_shape=jax.ShapeDtypeStruct((SSM_SLABS, t // SSM_Q, SSM_Q * LANE), BF16),
        scratch_shapes=[pltpu.VMEM((SSM_SLABS, tm, LANE), F32)],
        compiler_params=_params(1),
        name="ssm_input",
    )(x2d, g, w_in_bf)


def ssm_tables(a_re, a_im, log_dt, b_re, b_im, c_re, c_im, d_skip):
    q, nc = SSM_Q, SSM_GROUP_DIM
    ng = a_re.shape[0]
    sl = ng // GROUPS_PER_SLAB
    dt = jnp.exp(log_dt)[:, None]

    def powers(kvec):
        kk = kvec[:, None, None]
        mag = jnp.exp(a_re * dt * kk)
        return mag * jnp.cos(a_im * dt * kk), mag * jnp.sin(a_im * dt * kk)

    pw_re, pw_im = powers(jnp.arange(q + 1, dtype=F32))
    nr, ni = pw_re[1] - 1.0, pw_im[1]
    den = a_re * a_re + a_im * a_im
    fr, fi = (nr * a_re + ni * a_im) / den, (ni * a_re - nr * a_im) / den
    bb_re = fr[..., None] * b_re - fi[..., None] * b_im
    bb_im = fr[..., None] * b_im + fi[..., None] * b_re

    def to_slab_rows(a, lead):
        a = a.reshape(lead, sl, GROUPS_PER_SLAB, a.shape[2], a.shape[3]).transpose(1, 0, 2, 3, 4)
        return a.reshape(sl, -1, a.shape[-1]).astype(BF16)

    rev_re, rev_im = powers((q - 1.0) - jnp.arange(q, dtype=F32))
    bt_re, bt_im = bb_re.transpose(0, 2, 1), bb_im.transpose(0, 2, 1)
    m_re = rev_re[:, :, None, :] * bt_re[None] - rev_im[:, :, None, :] * bt_im[None]
    m_im = rev_re[:, :, None, :] * bt_im[None] + rev_im[:, :, None, :] * bt_re[None]
    bq = to_slab_rows(jnp.concatenate([m_re, m_im], axis=-1), q)
    ct_re, ct_im = c_re.transpose(0, 2, 1), c_im.transpose(0, 2, 1)
    ct_re, ct_im = jnp.tile(ct_re, (1, 1, q + 1)), jnp.tile(ct_im, (1, 1, q + 1))
    pl_re = jnp.repeat(pw_re.transpose(1, 2, 0), nc, axis=-1)
    pl_im = jnp.repeat(pw_im.transpose(1, 2, 0), nc, axis=-1)
    r_re = ct_re * pl_re - ct_im * pl_im
    r_im = ct_re * pl_im + ct_im * pl_re
    cq = to_slab_rows(jnp.stack([r_re[..., nc:], -r_im[..., nc:]], axis=0), 2)
    hi = lax.Precision.HIGHEST
    kern = (jnp.einsum('gpd,gpl->gdl', bb_re, r_re[..., :q * nc], precision=hi)
            - jnp.einsum('gpd,gpl->gdl', bb_im, r_im[..., :q * nc], precision=hi))
    lane = jnp.arange(q * nc)[None, None, :]
    d_pad = jnp.pad(d_skip.reshape(ng, nc), ((0, 0), (0, (q - 1) * nc)))
    kern = kern + jnp.where(lane == jnp.arange(nc)[None, :, None], d_pad[:, None, :], 0.0)
    kq = jnp.stack([jnp.pad(kern[..., :(q - s) * nc], ((0, 0), (0, 0), (s * nc, 0))) for s in range(q)], axis=0)
    kq = to_slab_rows(kq, q)
    lam_q = jnp.concatenate([pw_re[q].reshape(sl, 1, -1), pw_im[q].reshape(sl, 1, -1)], axis=-1)
    return bq, cq, kq, lam_q


def _expand_block_diag(small, row_inner, col_inner):
    n_rows, n_small = small.shape
    n_cols = n_small * GROUPS_PER_SLAB
    g_bits = GROUPS_PER_SLAB.bit_length() - 1
    r_shift, c_shift = row_inner.bit_length() - 1, col_inner.bit_length() - 1
    k = lax.broadcasted_iota(I32, (n_small, n_cols), 0)
    col = lax.broadcasted_iota(I32, (n_small, n_cols), 1)
    src = ((col >> (c_shift + g_bits)) << c_shift) | (col & (col_inner - 1))
    spread = jnp.where(k == src, 1.0, 0.0).astype(BF16)
    r = lax.broadcasted_iota(I32, (n_rows, n_cols), 0)
    c = lax.broadcasted_iota(I32, (n_rows, n_cols), 1)
    same = ((r >> r_shift) & (GROUPS_PER_SLAB - 1)) == ((c >> c_shift) & (GROUPS_PER_SLAB - 1))
    return jnp.where(same, _dot(small, spread), 0.0).astype(BF16)


def _s5_kernel(u_ref, bq_ref, cq_ref, kq_ref, lam_ref, y_ref, g_scr, s_scr, t_scr, tbq, tcq, tkq, *, seqs):
    half = GROUPS_PER_SLAB * SSM_STATE
    rows = u_ref.shape[1]

    @pl.when(pl.program_id(1) == 0)
    def _():
        tbq[...] = _expand_block_diag(bq_ref[0], SSM_GROUP_DIM, SSM_STATE)
        tcq[...] = _expand_block_diag(cq_ref[0], SSM_STATE, SSM_GROUP_DIM)
        tkq[...] = _expand_block_diag(kq_ref[0], SSM_GROUP_DIM, SSM_GROUP_DIM)

    u = u_ref[0]
    g_scr[...] = _dot(u, tbq[...])
    lam = lam_ref[0]
    lre, lim = lam[:, :half], lam[:, half:]
    seq_rows = rows // seqs

    def body(n, carry):
        out = []
        for b in range(seqs):
            cre, cim = carry[2 * b], carry[2 * b + 1]
            row = b * seq_rows + n
            s_scr[pl.ds(row, 1), :] = jnp.concatenate([cre, cim], axis=1)
            g = g_scr[pl.ds(row, 1), :]
            out.append(lre * cre - lim * cim + g[:, :half])
            out.append(lre * cim + lim * cre + g[:, half:])
        return tuple(out)

    zero = jnp.zeros((1, half), F32)
    lax.fori_loop(0, seq_rows, body, (zero,) * (2 * seqs), unroll=2)
    y = _dot(s_scr[...].astype(BF16), tcq[...]) + _dot(u, tkq[...])
    for t in range(SSM_Q):
        t_scr[pl.ds(t, rows, stride=SSM_Q), :] = y[:, t * LANE:(t + 1) * LANE]
    y_ref[0] = t_scr[...].astype(BF16)


def s5_scan(u4, bq, cq, kq, lam_q, layer, batch, seqs):
    sl, rows_total, width = u4.shape
    rows = rows_total // batch * seqs
    tab = pl.BlockSpec((None, 1, width, width // GROUPS_PER_SLAB), lambda j, b: (layer, j, 0, 0))
    return pl.pallas_call(
        functools.partial(_s5_kernel, seqs=seqs),
        grid=(sl, batch // seqs),
        in_specs=[pl.BlockSpec((1, rows, width), lambda j, b: (j, b, 0)), tab, tab, tab,
                  pl.BlockSpec((None, 1, 1, width), lambda j, b: (layer, j, 0, 0))],
        out_specs=pl.BlockSpec((1, rows * SSM_Q, LANE), lambda j, b: (j, b, 0)),
        out_shape=jax.ShapeDtypeStruct((sl, rows_total * SSM_Q, LANE), BF16),
        scratch_shapes=[pltpu.VMEM((rows, width), F32), pltpu.VMEM((rows, width), F32),
                        pltpu.VMEM((rows * SSM_Q, LANE), F32)] + [pltpu.VMEM((width, width), BF16)] * 3,
        compiler_params=_params(2),
        name="s5_scan",
    )(u4, bq, cq, kq, lam_q)


def _mixer_kernel(x_ref, ys_ref, ng_ref, win_ref, bgate_ref, poolw_ref, pscale_ref, lng_ref, lnb_ref,
                  sguw_ref, sgub_ref, gluw_ref, glub_ref, wbr_ref, wout_ref, o_ref, xe_scr):
    tm = x_ref.shape[0]
    width = SSM_SLABS * LANE
    a_end, b_end = width, 3 * width
    c_end = b_end + width
    i = pl.program_id(1)
    x = x_ref[...]
    hb = _rms(x, ng_ref[...]).astype(BF16)

    xa = _dot(hb, win_ref[:, 0:a_end])

    @pl.when(i == 0)
    def _():
        xe_scr[0:POOL_HALO, :] = jnp.zeros((POOL_HALO, xe_scr.shape[1]), F32)

    xe_scr[POOL_HALO:POOL_HALO + tm, :] = xa
    pos = i * tm + lax.broadcasted_iota(I32, (tm, 1), 0)
    ya_parts = []
    for gi, w in enumerate(POOL_WINDOWS):
        ch = slice(gi * LANE, (gi + 1) * LANE)
        win = xa[:, ch]
        for j in range(1, w):
            win = win + xe_scr[POOL_HALO - j:POOL_HALO - j + tm, ch]
        cnt = jnp.minimum(pos + 1, w).astype(F32)
        diff = win / cnt - xa[:, ch]
        ya_parts.append(_dot(diff.astype(BF16), poolw_ref[gi]) * pscale_ref[:, ch])
    ya = jnp.concatenate(ya_parts, axis=1)
    xe_scr[0:POOL_HALO, :] = xe_scr[tm:tm + POOL_HALO, :]

    z = _gelu(_dot(hb, win_ref[:, a_end:b_end]))
    u, v = z[:, :width], z[:, width:]
    mu = jnp.mean(v, axis=-1, keepdims=True)
    vc = v - mu
    var = jnp.mean(vc * vc, axis=-1, keepdims=True)
    vb = (vc * lax.rsqrt(var + EPS) * lng_ref[...] + lnb_ref[...]).astype(BF16)
    r_i = lax.broadcasted_iota(I32, (SGU_CHUNK, SGU_CHUNK), 0)
    c_i = lax.broadcasted_iota(I32, (SGU_CHUNK, SGU_CHUNK), 1)
    n_heads = width // LANE
    wt = [jnp.where(r_i >= c_i, sguw_ref[hd], 0.0).astype(BF16) for hd in range(n_heads)]
    s_rows = []
    for c in range(tm // SGU_CHUNK):
        rs = slice(c * SGU_CHUNK, (c + 1) * SGU_CHUNK)
        s_rows.append(jnp.concatenate(
            [_dot(wt[hd], vb[rs, hd * LANE:(hd + 1) * LANE]) for hd in range(n_heads)], axis=1)
            + sgub_ref[...])
    yb = u * jnp.concatenate(s_rows, axis=0)

    yc = _gelu(jnp.concatenate([ys_ref[j] for j in range(SSM_SLABS)], axis=1).astype(F32))
    yc = yc * _sigmoid(_dot(yc.astype(BF16), gluw_ref[...]) + glub_ref[...])

    merged = None
    for k, yk in enumerate((ya, yb, yc)):
        gate = _sigmoid(_dot(hb, win_ref[:, c_end + k * D_MODEL:c_end + (k + 1) * D_MODEL])
                        + bgate_ref[:, k * D_MODEL:(k + 1) * D_MODEL])
        term = gate * _dot(yk.astype(BF16), wbr_ref[k])
        merged = term if merged is None else merged + term
    o_ref[...] = x + _dot(merged.astype(BF16), wout_ref[...])


def mixer(x2d, ys4, p, layer, batch, tm):
    t = x2d.shape[0]
    nt = t // batch // tm
    row = lambda b, i: (b * nt + i, 0)
    slab = lambda b, i: (0, b * nt + i, 0)
    consts = [p["norm_mix_g"], p["w_in"], p["b_gate"], p["pool_w"], p["pool_scale"], p["sgu_ln_g"],
              p["sgu_ln_b"], p["sgu_w"], p["sgu_b_full"], p["glu_w"], p["glu_b"], p["w_branch"], p["w_out"]]
    return pl.pallas_call(
        _mixer_kernel,
        grid=(batch, nt),
        in_specs=[pl.BlockSpec((tm, D_MODEL), row),
                  pl.BlockSpec((SSM_SLABS, tm, LANE), slab)] + [_layer_spec(c, layer) for c in consts],
        out_specs=pl.BlockSpec((tm, D_MODEL), row),
        out_shape=jax.ShapeDtypeStruct(x2d.shape, F32),
        scratch_shapes=[pltpu.VMEM((tm + POOL_HALO, SSM_SLABS * LANE), F32)],
        compiler_params=_params(2),
        name="mixer",
    )(x2d, ys4, *consts)


def _route_tile(x, g, w_hi, w_lo, br):
    tm = x.shape[0]
    h = _rms(x, g)
    h_hi = h.astype(BF16)
    h_lo = (h - h_hi.astype(F32)).astype(BF16)
    logits = (_dot(h_hi, w_hi) + _dot(h_hi, w_lo) + _dot(h_lo, w_hi)) + br
    lane = lax.broadcasted_iota(I32, logits.shape, 1)
    neg = -jnp.inf
    gl = jnp.where(lane < N_GROUPS, logits, neg)
    gmax = jnp.max(gl, axis=-1, keepdims=True)
    gsum = jnp.sum(jnp.where(lane < N_GROUPS, jnp.exp(logits - gmax), 0.0), axis=-1, keepdims=True)
    g_val = 1.0 / gsum
    g_idx = jnp.min(jnp.where(gl == gmax, lane, LANE), axis=-1, keepdims=True)
    lo = N_GROUPS + EXPERTS_PER_GROUP * g_idx
    el = jnp.where((lane >= lo) & (lane < lo + EXPERTS_PER_GROUP), logits, neg)
    e1v = jnp.max(el, axis=-1, keepdims=True)
    e1l = jnp.min(jnp.where(el == e1v, lane, LANE), axis=-1, keepdims=True)
    el2 = jnp.where(lane == e1l, neg, el)
    e2v = jnp.max(el2, axis=-1, keepdims=True)
    e2l = jnp.min(jnp.where(el2 == e2v, lane, LANE), axis=-1, keepdims=True)
    tt = jnp.exp(e2v - e1v)
    w1 = g_val / (1.0 + tt)
    w2 = g_val * tt / (1.0 + tt)
    hit1, hit2 = lane == e1l - N_GROUPS, lane == e2l - N_GROUPS
    onehot = jnp.where(hit1 | hit2, 1.0, 0.0)
    r_i = lax.broadcasted_iota(I32, (tm, tm), 0)
    c_i = lax.broadcasted_iota(I32, (tm, tm), 1)
    tri = jnp.where(r_i > c_i, 1.0, 0.0).astype(BF16)
    prefix = _dot(tri, onehot.astype(BF16))
    count = jnp.sum(onehot, axis=0, keepdims=True)
    groups8 = jnp.floor((count + (RUN_ALIGN - 1.0)) * (1.0 / RUN_ALIGN))
    count = groups8 * RUN_ALIGN
    e_r = lax.broadcasted_iota(I32, (LANE, LANE), 0)
    e_c = lax.broadcasted_iota(I32, (LANE, LANE), 1)
    before = jnp.where(e_r < e_c, 1.0, 0.0).astype(BF16)
    run_start = _dot(jnp.broadcast_to(groups8, (SUBLANE, LANE)).astype(BF16), before)[0:1, :] * RUN_ALIGN
    local = prefix + run_start
    lp1 = jnp.sum(jnp.where(hit1, local, 0.0), axis=-1, keepdims=True)
    lp2 = jnp.sum(jnp.where(hit2, local, 0.0), axis=-1, keepdims=True)
    route = jnp.where(lane == 0, w1, jnp.where(lane == 1, w2, jnp.where(
        lane == 2, lp1, jnp.where(lane == 3, lp2, 0.0))))
    return route, count, run_start


def _router_kernel(x_ref, g_ref, wr_ref, br_ref, route_ref, meta_ref, carry_scr, *, tm):
    @pl.when(pl.program_id(0) == 0)
    def _():
        carry_scr[...] = jnp.zeros_like(carry_scr)

    tiles = [_route_tile(x_ref[k * tm:(k + 1) * tm, :], g_ref[...], wr_ref[0], wr_ref[1], br_ref[...])
             for k in range(x_ref.shape[0] // tm)]
    earlier = carry_scr[0:1, :]
    sub = lax.broadcasted_iota(I32, (SUBLANE, LANE), 0)
    for k, (route, count, run_start) in enumerate(tiles):
        route_ref[k * tm:(k + 1) * tm, :] = route
        total = earlier + count
        meta_ref[k * SUBLANE:(k + 1) * SUBLANE, :] = jnp.where(sub == 0, count, jnp.where(
            sub == 1, earlier, jnp.where(sub == 2, run_start, jnp.where(sub == 3, total, 0.0))))
        earlier = total
    carry_scr[...] = jnp.broadcast_to(earlier, carry_scr.shape)


def router(x2d, g, wr, br, layer, tm, tiles_per_step):
    t = x2d.shape[0]
    rows = tm * tiles_per_step
    row = lambda i: (i, 0)
    return pl.pallas_call(
        functools.partial(_router_kernel, tm=tm),
        grid=(t // rows,),
        in_specs=[pl.BlockSpec((rows, D_MODEL), row), _layer_spec(g, layer),
                  _layer_spec(wr, layer), _layer_spec(br, layer)],
        out_specs=[pl.BlockSpec((rows, LANE), row), pl.BlockSpec((tiles_per_step * SUBLANE, LANE), row)],
        out_shape=[jax.ShapeDtypeStruct((t, LANE), F32),
                   jax.ShapeDtypeStruct((t // tm * SUBLANE, LANE), F32)],
        scratch_shapes=[pltpu.VMEM((SUBLANE, LANE), F32)],
        compiler_params=_params(1),
        name="router",
    )(x2d, g, wr, br)


def _sort_matrix(route, k, n_sorted):
    col = lax.broadcasted_iota(I32, (route.shape[0], n_sorted), 1)
    return jnp.where(col == route[:, 2 + k:3 + k].astype(I32), 1.0, 0.0).astype(BF16)


def _sorted_rows(tm):
    return -(-(2 * tm + N_EXPERTS * (RUN_ALIGN - 1)) // LANE) * LANE


def _start_runs(cnt_ref, lo_ref, dst_ref, tile, make_copy):
    def body(e, c):
        idx = tile * N_EXPERTS + e
        n = cnt_ref[idx]

        @pl.when(n > 0)
        def _():
            make_copy(pl.multiple_of(lo_ref[idx], RUN_ALIGN), pl.multiple_of(dst_ref[idx], RUN_ALIGN),
                      pl.multiple_of(n, RUN_ALIGN)).start()
        return c
    lax.fori_loop(0, N_EXPERTS, body, 0)


def _wait_runs(tot_ref, tile, make_copy):
    make_copy(0, 0, pl.multiple_of(tot_ref[tile], RUN_ALIGN)).wait()


def _dispatch_kernel(cnt_ref, lo_ref, dst_ref, tot_ref, zst_ref, zln_ref, nused_ref, x_ref, g_ref, route_ref,
                     o_ref, srt_scr, zero_scr, sem, zsem):
    tm = x_ref.shape[0]
    bm = zero_scr.shape[0]
    i = pl.program_id(0)
    n = pl.num_programs(0)
    slot = i % 2

    def run_copy(s):
        return lambda lo, dst, cnt: pltpu.make_async_copy(
            srt_scr.at[s, pl.ds(lo, cnt)], o_ref.at[pl.ds(dst, cnt)], sem.at[s])

    def pad_copy(e):
        ln = pl.multiple_of(zln_ref[e], RUN_ALIGN)
        return pltpu.make_async_copy(zero_scr.at[pl.ds(0, ln)],
                                     o_ref.at[pl.ds(pl.multiple_of(zst_ref[e], RUN_ALIGN), ln)], zsem)

    def tail_copy(b):
        return pltpu.make_async_copy(zero_scr, o_ref.at[pl.ds(pl.multiple_of(b * bm, bm), bm)], zsem)

    @pl.when(i == 0)
    def _():
        zero_scr[...] = jnp.zeros_like(zero_scr)
        for wait in (False, True):
            def body(e, c):
                @pl.when(zln_ref[e] > 0)
                def _():
                    cp = pad_copy(e)
                    cp.wait() if wait else cp.start()
                return c
            lax.fori_loop(0, N_EXPERTS, body, 0)

            def tail(b, c):
                cp = tail_copy(b)
                cp.wait() if wait else cp.start()
                return c
            lax.fori_loop(nused_ref[0], o_ref.shape[0] // bm, tail, 0)

    @pl.when(i >= 2)
    def _():
        _wait_runs(tot_ref, i - 2, run_copy(slot))

    hb = _rms(x_ref[...], g_ref[...]).astype(BF16)
    route = route_ref[...]
    ns = srt_scr.shape[1]
    pt = jnp.maximum(_sort_matrix(route, 0, ns), _sort_matrix(route, 1, ns))
    srt_scr[slot] = lax.dot_general(pt, hb, (((0,), (0,)), ((), ())),
                                    preferred_element_type=F32).astype(ROW_DTYPE)
    _start_runs(cnt_ref, lo_ref, dst_ref, i, run_copy(slot))

    @pl.when(i == n - 1)
    def _():
        @pl.when(i >= 1)
        def _():
            _wait_runs(tot_ref, i - 1, run_copy(1 - slot))
        _wait_runs(tot_ref, i, run_copy(slot))


def dispatch(cnt, lo, dst, tot, zst, zln, nused, x2d, g, layer, route, n_rows, tm, bm):
    t, d = x2d.shape
    row = lambda i, *_: (i, 0)
    return pl.pallas_call(
        _dispatch_kernel,
        grid_spec=pltpu.PrefetchScalarGridSpec(
            num_scalar_prefetch=7,
            grid=(t // tm,),
            in_specs=[pl.BlockSpec((tm, d), row), pl.BlockSpec((None, 1, d), lambda i, *_: (layer, 0, 0)),
                      pl.BlockSpec((tm, LANE), row)],
            out_specs=pl.BlockSpec(memory_space=pl.ANY),
            scratch_shapes=[pltpu.VMEM((2, _sorted_rows(tm), d), ROW_DTYPE), pltpu.VMEM((bm, d), ROW_DTYPE),
                            pltpu.SemaphoreType.DMA((2,)), pltpu.SemaphoreType.DMA(())]),
        out_shape=jax.ShapeDtypeStruct((n_rows, d), ROW_DTYPE),
        compiler_params=_params(1),
        name="dispatch",
    )(cnt, lo, dst, tot, zst, zln, nused, x2d, g, route)


def _expert_kernel(bexp_ref, nused_ref, slot_ref, next_ref, xs_ref, wg_hbm, wu_hbm, wd_hbm, ys_ref,
                   wg_buf, wu_buf, wd_buf, wg_scr, wu_scr, wd_scr, sem, *, layer):
    i = pl.program_id(0)

    def weight_copies(e, s):
        return [pltpu.make_async_copy(hbm.at[layer, e], buf.at[s], sem.at[s])
                for hbm, buf in ((wg_hbm, wg_buf), (wu_hbm, wu_buf), (wd_hbm, wd_buf))]

    @pl.when(i < nused_ref[0])
    def _():
        e, s = bexp_ref[i], slot_ref[i]
        changed = jnp.logical_or(i == 0, e != bexp_ref[jnp.maximum(i - 1, 0)])

        @pl.when(i == 0)
        def _():
            for cp in weight_copies(e, s):
                cp.start()

        @pl.when(changed)
        def _():
            for cp in weight_copies(e, s):
                cp.wait()
            wg_scr[...] = wg_buf[s].astype(BF16)
            wu_scr[...] = wu_buf[s].astype(BF16)
            wd_scr[...] = wd_buf[s].astype(BF16)

            @pl.when(next_ref[i] < N_EXPERTS)
            def _():
                for cp in weight_copies(next_ref[i], 1 - s):
                    cp.start()

        xb = xs_ref[...].astype(BF16)
        hg = _dot(xb, wg_scr[...])
        hu = _dot(xb, wu_scr[...])
        act = (hg * _sigmoid(hg) * hu).astype(BF16)
        ys_ref[...] = _dot(act, wd_scr[...]).astype(ROW_DTYPE)

    @pl.when(i >= nused_ref[0])
    def _():
        ys_ref[...] = jnp.zeros_like(ys_ref)


def experts(bexp, nused, slot, nxt, xs, w_gate, w_up, w_down, layer, bm):
    n_rows, d = xs.shape
    de = w_gate.shape[3]
    anyspace = pl.BlockSpec(memory_space=pl.ANY)
    return pl.pallas_call(
        functools.partial(_expert_kernel, layer=layer),
        grid_spec=pltpu.PrefetchScalarGridSpec(
            num_scalar_prefetch=4,
            grid=(n_rows // bm,),
            in_specs=[pl.BlockSpec((bm, d), lambda i, bexp, nused, *_: (jnp.minimum(i, nused[0] - 1), 0)),
                      anyspace, anyspace, anyspace],
            out_specs=pl.BlockSpec((bm, d), lambda i, *_: (i, 0)),
            scratch_shapes=[pltpu.VMEM((2, d, de), F32), pltpu.VMEM((2, d, de), F32), pltpu.VMEM((2, de, d), F32),
                            pltpu.VMEM((d, de), BF16), pltpu.VMEM((d, de), BF16), pltpu.VMEM((de, d), BF16),
                            pltpu.SemaphoreType.DMA((2,))]),
        out_shape=jax.ShapeDtypeStruct(xs.shape, ROW_DTYPE),
        compiler_params=_params(1),
        name="experts",
    )(bexp, nused, slot, nxt, xs, w_gate, w_up, w_down)


def _combine_kernel(cnt_ref, lo_ref, dst_ref, tot_ref, x_ref, route_ref, g_ref, *rest, final):
    if final:
        ys_ref, o_ref, buf, sem = rest
    else:
        w_ref, ys_ref, o_ref, xc_ref, buf, sem, xc_scr = rest
    i = pl.program_id(0)
    n = pl.num_programs(0)
    slot = i % 2

    def run_copy(s):
        return lambda lo, dst, cnt: pltpu.make_async_copy(
            ys_ref.at[pl.ds(dst, cnt)], buf.at[s, pl.ds(lo, cnt)], sem.at[s])

    @pl.when(i == 0)
    def _():
        buf[...] = jnp.zeros_like(buf)
        _start_runs(cnt_ref, lo_ref, dst_ref, 0, run_copy(0))

    @pl.when(i + 1 < n)
    def _():
        _start_runs(cnt_ref, lo_ref, dst_ref, i + 1, run_copy(1 - slot))

    _wait_runs(tot_ref, i, run_copy(slot))
    ysb = buf[slot].astype(BF16)
    route = route_ref[...]
    out = x_ref[...]
    for k in range(2):
        out = out + route[:, k:k + 1] * _dot(_sort_matrix(route, k, buf.shape[1]), ysb)
    if final:
        out = _rms(out, g_ref[...])
    else:
        _write_ssm_input(out, g_ref[...], w_ref[...], xc_ref, xc_scr)
    o_ref[...] = out


def combine(cnt, lo, dst, tot, x2d, route, ys, tm, g_final=None, next_mix=None):
    t, d = x2d.shape
    final = next_mix is None
    row = lambda i, *_: (i, 0)
    buf = [pltpu.VMEM((2, _sorted_rows(tm), d), ROW_DTYPE), pltpu.SemaphoreType.DMA((2,))]
    if final:
        extra_in, extra_args = [pl.BlockSpec((1, d), lambda i, *_: (0, 0))], (g_final[None, :],)
        out_specs, out_shape, scratch = pl.BlockSpec((tm, d), row), jax.ShapeDtypeStruct((t, d), F32), buf
    else:
        g_next, w_next, layer, col_block = next_mix
        width = SSM_SLABS * LANE
        extra_in = [pl.BlockSpec((None, 1, d), lambda i, *_: (layer, 0, 0)),
                    pl.BlockSpec((None, d, width), lambda i, *_: (layer, 0, col_block))]
        extra_args = (g_next, w_next)
        out_specs = [pl.BlockSpec((tm, d), row),
                     pl.BlockSpec((SSM_SLABS, tm // SSM_Q, SSM_Q * LANE), lambda i, *_: (0, i, 0))]
        out_shape = [jax.ShapeDtypeStruct((t, d), F32),
                     jax.ShapeDtypeStruct((SSM_SLABS, t // SSM_Q, SSM_Q * LANE), BF16)]
        scratch = buf + [pltpu.VMEM((SSM_SLABS, tm, LANE), F32)]
    return pl.pallas_call(
        functools.partial(_combine_kernel, final=final),
        grid_spec=pltpu.PrefetchScalarGridSpec(
            num_scalar_prefetch=4,
            grid=(t // tm,),
            in_specs=[pl.BlockSpec((tm, d), row), pl.BlockSpec((tm, LANE), row)] + extra_in
            + [pl.BlockSpec(memory_space=pl.ANY)],
            out_specs=out_specs,
            scratch_shapes=scratch),
        out_shape=out_shape,
        compiler_params=_params(1),
        name="combine",
    )(cnt, lo, dst, tot, x2d, route, *extra_args, ys)


def router_params(rg_w, rg_b, re_w, re_b):
    depth = rg_w.shape[0]
    pad = LANE - N_GROUPS - N_EXPERTS
    wr = jnp.concatenate([rg_w, re_w, jnp.zeros((depth, D_MODEL, pad), F32)], axis=2)
    wr_hi = wr.astype(BF16)
    wr = jnp.stack([wr_hi, (wr - wr_hi.astype(F32)).astype(BF16)], axis=1)
    br = jnp.concatenate([rg_b, re_b, jnp.zeros((depth, pad), F32)], axis=1)[:, None, :]
    return wr, br


def moe(x2d, g2, wr, br, w_gate, w_up, w_down, layer, tm, bm, g_final=None, next_mix=None):
    t = x2d.shape[0]
    nt = t // tm
    route, meta = router(x2d, g2, wr, br, layer, tm, 4)
    meta = meta.reshape(nt, SUBLANE, LANE)[:, :, :N_EXPERTS].astype(I32)
    cnt, earlier, lo = meta[:, 0], meta[:, 1], meta[:, 2]
    counts = meta[nt - 1, 3]
    padded = (counts + bm - 1) // bm * bm
    ends = jnp.cumsum(padded)
    offs = ends - padded
    dst = offs[None, :] + earlier
    n_rows = -(-(2 * t + nt * N_EXPERTS * (RUN_ALIGN - 1)) // bm) * bm + N_EXPERTS * bm
    nused = (ends[-1] // bm).astype(I32)
    blk_start = jnp.arange(n_rows // bm, dtype=I32) * bm
    bexp = jnp.sum((ends[None, :] <= blk_start[:, None]).astype(I32), axis=1)
    last = jnp.sum((ends <= (nused - 1) * bm).astype(I32))
    bexp = jnp.minimum(bexp, last)
    tot = jnp.sum(cnt, axis=1)
    cnt, lo, dst = cnt.reshape(-1), lo.reshape(-1), dst.reshape(-1)
    xs = dispatch(cnt, lo, dst, tot, offs + counts, padded - counts, nused[None], x2d, g2, layer, route,
                  n_rows, tm, bm)
    ar = jnp.arange(N_EXPERTS, dtype=I32)
    active = padded > 0
    rank = jnp.cumsum(active.astype(I32)) - active.astype(I32)
    later = (ar[None, :] > ar[:, None]) & active[None, :]
    next_e = jnp.min(jnp.where(later, ar[None, :], N_EXPERTS), axis=1).astype(I32)
    ys = experts(bexp, nused[None], (rank % 2)[bexp], next_e[bexp], xs, w_gate, w_up, w_down, layer, bm)
    return combine(cnt, lo, dst, tot, x2d, route, ys, tm, g_final=g_final, next_mix=next_mix)


def kernel(x, norm_mix_g, w_in, b_gate, pool_w, pool_scale, sgu_ln_g, sgu_ln_b, sgu_w, sgu_b, ssm_a_re, ssm_a_im, ssm_log_dt, ssm_b_re, ssm_b_im, ssm_c_re, ssm_c_im, ssm_d, glu_w, glu_b, w_branch, w_out, norm_ffn_g, router_group_w, router_group_b, router_expert_w, router_expert_b, exp_w_gate, exp_w_up, exp_w_down, norm_final_g):
    batch, seq, d = x.shape
    depth = w_in.shape[0]
    tm_in, tm_mix, tm_moe, bm = 1024, 512, 256, 512
    s5_seqs = 2
    x2d = x.reshape(batch * seq, d)
    ssm_col_block = 3
    row = lambda a: a[:, None, :]
    p = {
        "norm_mix_g": row(norm_mix_g), "w_in": w_in.astype(BF16), "b_gate": row(b_gate),
        "pool_w": pool_w.astype(BF16), "pool_scale": row(pool_scale),
        "sgu_ln_g": row(sgu_ln_g), "sgu_ln_b": row(sgu_ln_b), "sgu_w": sgu_w,
        "sgu_b_full": jnp.repeat(jnp.swapaxes(sgu_b, 1, 2), LANE, axis=2),
        "glu_w": glu_w.astype(BF16), "glu_b": row(glu_b),
        "w_branch": w_branch.astype(BF16), "w_out": w_out.astype(BF16),
    }
    bq, cq, kq, lam_q = jax.vmap(ssm_tables)(ssm_a_re, ssm_a_im, ssm_log_dt, ssm_b_re, ssm_b_im,
                                             ssm_c_re, ssm_c_im, ssm_d)
    wr, br = router_params(router_group_w, router_group_b, router_expert_w, router_expert_b)
    g_ffn = row(norm_ffn_g)
    xc4 = ssm_input(x2d, p["norm_mix_g"], p["w_in"], 0, ssm_col_block, tm_in)
    for l in range(depth):
        ys4 = s5_scan(xc4, bq, cq, kq, lam_q, l, batch, s5_seqs)
        x2d = mixer(x2d, ys4, p, l, batch, tm_mix)
        if l == depth - 1:
            x2d = moe(x2d, g_ffn, wr, br, exp_w_gate, exp_w_up, exp_w_down, l, tm_moe, bm, g_final=norm_final_g)
        else:
            x2d, xc4 = moe(x2d, g_ffn, wr, br, exp_w_gate, exp_w_up, exp_w_down, l, tm_moe, bm,
                           next_mix=(p["norm_mix_g"], p["w_in"], l + 1, ssm_col_block))
    return x2d.reshape(batch, seq, d)
```

```python
import functools
import math

import jax
import jax.numpy as jnp
from jax import lax
from jax.experimental import pallas as pl
from jax.experimental.pallas import tpu as pltpu

F32 = jnp.float32
BF16 = jnp.bfloat16
I32 = jnp.int32

D_MODEL = 1024
POOL_WINDOWS = (2, 4, 8, 16)
POOL_HALO = 16
LANE = 128
SUBLANE = 8
ROW_DTYPE = BF16
RUN_ALIGN = 8 * (4 // jnp.dtype(ROW_DTYPE).itemsize)
SGU_CHUNK = 128
SSM_GROUP_DIM = 16
SSM_STATE = 64
SSM_Q = 8
SSM_SLABS = 4
GROUPS_PER_SLAB = LANE // SSM_GROUP_DIM
N_GROUPS = 4
EXPERTS_PER_GROUP = 8
N_EXPERTS = 32
EPS = 1e-6
VMEM_LIMIT = 56 * 1024 * 1024


def _rms(x, g):
    return x * lax.rsqrt(jnp.mean(x * x, axis=-1, keepdims=True) + EPS) * g


def _gelu(x):
    c = math.sqrt(2.0 / math.pi)
    return x * (0.5 * (1.0 + jnp.tanh(c * (x + 0.044715 * (x * x * x)))))


def _sigmoid(x):
    return 0.5 * (1.0 + jnp.tanh(0.5 * x))


def _dot(a, b):
    return jnp.dot(a, b, preferred_element_type=F32)


def _layer_spec(arr, layer):
    nd = arr.ndim - 1
    return pl.BlockSpec((None,) + arr.shape[1:], lambda *_: (layer,) + (0,) * nd, pipeline_mode=pl.Buffered(1))


def _params(n_axes):
    return pltpu.CompilerParams(dimension_semantics=("arbitrary",) * n_axes, vmem_limit_bytes=VMEM_LIMIT)


def _write_ssm_input(x, g, w, o_ref, xc_scr):
    h = _rms(x, g).astype(BF16)
    xc = _dot(h, w)
    rows = xc_scr.shape[1] // SSM_Q
    for j in range(SSM_SLABS):
        xc_scr[j] = xc[:, j * LANE:(j + 1) * LANE]
        for s in range(SSM_Q):
            o_ref[j, :, s * LANE:(s + 1) * LANE] = xc_scr[j, pl.ds(s, rows, stride=SSM_Q), :].astype(BF16)


def _xc_kernel(x_ref, g_ref, w_ref, o_ref, xc_scr):
    _write_ssm_input(x_ref[...], g_ref[...], w_ref[...], o_ref, xc_scr)


def ssm_input(x2d, g, w_in_bf, layer, col_block, tm):
    t = x2d.shape[0]
    width = SSM_SLABS * LANE
    return pl.pallas_call(
        _xc_kernel,
        grid=(t // tm,),
        in_specs=[pl.BlockSpec((tm, D_MODEL), lambda i: (i, 0)),
                  _layer_spec(g, layer),
                  pl.BlockSpec((None, D_MODEL, width), lambda i: (layer, 0, col_block),
                               pipeline_mode=pl.Buffered(1))],
        out_specs=pl.BlockSpec((SSM_SLABS, tm // SSM_Q, SSM_Q * LANE), lambda i: (0, i, 0)),
        out_shape=jax.ShapeDtypeStruct((SSM_SLABS, t // SSM_Q, SSM_Q * LANE), BF16),
        scratch_shapes=[pltpu.VMEM((SSM_SLABS, tm, LANE), F32)],
        compiler_params=_params(1),
        name="ssm_input",
    )(x2d, g, w_in_bf)


def ssm_tables(a_re, a_im, log_dt, b_re, b_im, c_re, c_im, d_skip):
    q, nc = SSM_Q, SSM_GROUP_DIM
    ng = a_re.shape[0]
    sl = ng // GROUPS_PER_SLAB
    dt = jnp.exp(log_dt)[:, None]

    def powers(kvec):
        kk = kvec[:, None, None]
        mag = jnp.exp(a_re * dt * kk)
        return mag * jnp.cos(a_im * dt * kk), mag * jnp.sin(a_im * dt * kk)

    pw_re, pw_im = powers(jnp.arange(q + 1, dtype=F32))
    nr, ni = pw_re[1] - 1.0, pw_im[1]
    den = a_re * a_re + a_im * a_im
    fr, fi = (nr * a_re + ni * a_im) / den, (ni * a_re - nr * a_im) / den
    bb_re = fr[..., None] * b_re - fi[..., None] * b_im
    bb_im = fr[..., None] * b_im + fi[..., None] * b_re

    def to_slab_rows(a, lead):
        a = a.reshape(lead, sl, GROUPS_PER_SLAB, a.shape[2], a.shape[3]).transpose(1, 0, 2, 3, 4)
        return a.reshape(sl, -1, a.shape[-1]).astype(BF16)

    rev_re, rev_im = powers((q - 1.0) - jnp.arange(q, dtype=F32))
    bt_re, bt_im = bb_re.transpose(0, 2, 1), bb_im.transpose(0, 2, 1)
    m_re = rev_re[:, :, None, :] * bt_re[None] - rev_im[:, :, None, :] * bt_im[None]
    m_im = rev_re[:, :, None, :] * bt_im[None] + rev_im[:, :, None, :] * bt_re[None]
    bq = to_slab_rows(jnp.concatenate([m_re, m_im], axis=-1), q)
    ct_re, ct_im = c_re.transpose(0, 2, 1), c_im.transpose(0, 2, 1)
    ct_re, ct_im = jnp.tile(ct_re, (1, 1, q + 1)), jnp.tile(ct_im, (1, 1, q + 1))
    pl_re = jnp.repeat(pw_re.transpose(1, 2, 0), nc, axis=-1)
    pl_im = jnp.repeat(pw_im.transpose(1, 2, 0), nc, axis=-1)
    r_re = ct_re * pl_re - ct_im * pl_im
    r_im = ct_re * pl_im + ct_im * pl_re
    cq = to_slab_rows(jnp.stack([r_re[..., nc:], -r_im[..., nc:]], axis=0), 2)
    hi = lax.Precision.HIGHEST
    kern = (jnp.einsum('gpd,gpl->gdl', bb_re, r_re[..., :q * nc], precision=hi)
            - jnp.einsum('gpd,gpl->gdl', bb_im, r_im[..., :q * nc], precision=hi))
    lane = jnp.arange(q * nc)[None, None, :]
    d_pad = jnp.pad(d_skip.reshape(ng, nc), ((0, 0), (0, (q - 1) * nc)))
    kern = kern + jnp.where(lane == jnp.arange(nc)[None, :, None], d_pad[:, None, :], 0.0)
    kq = jnp.stack([jnp.pad(kern[..., :(q - s) * nc], ((0, 0), (0, 0), (s * nc, 0))) for s in range(q)], axis=0)
    kq = to_slab_rows(kq, q)
    lam_q = jnp.concatenate([pw_re[q].reshape(sl, 1, -1), pw_im[q].reshape(sl, 1, -1)], axis=-1)
    return bq, cq, kq, lam_q


def _expand_block_diag(small, row_inner, col_inner):
    n_rows, n_small = small.shape
    n_cols = n_small * GROUPS_PER_SLAB
    g_bits = GROUPS_PER_SLAB.bit_length() - 1
    r_shift, c_shift = row_inner.bit_length() - 1, col_inner.bit_length() - 1
    k = lax.broadcasted_iota(I32, (n_small, n_cols), 0)
    col = lax.broadcasted_iota(I32, (n_small, n_cols), 1)
    src = ((col >> (c_shift + g_bits)) << c_shift) | (col & (col_inner - 1))
    spread = jnp.where(k == src, 1.0, 0.0).astype(BF16)
    r = lax.broadcasted_iota(I32, (n_rows, n_cols), 0)
    c = lax.broadcasted_iota(I32, (n_rows, n_cols), 1)
    same = ((r >> r_shift) & (GROUPS_PER_SLAB - 1)) == ((c >> c_shift) & (GROUPS_PER_SLAB - 1))
    return jnp.where(same, _dot(small, spread), 0.0).astype(BF16)


def _s5_kernel(u_ref, bq_ref, cq_ref, kq_ref, lam_ref, y_ref, g_scr, s_scr, t_scr, tbq, tcq, tkq, *, seqs):
    half = GROUPS_PER_SLAB * SSM_STATE
    rows = u_ref.shape[1]

    @pl.when(pl.program_id(1) == 0)
    def _():
        tbq[...] = _expand_block_diag(bq_ref[0], SSM_GROUP_DIM, SSM_STATE)
        tcq[...] = _expand_block_diag(cq_ref[0], SSM_STATE, SSM_GROUP_DIM)
        tkq[...] = _expand_block_diag(kq_ref[0], SSM_GROUP_DIM, SSM_GROUP_DIM)

    u = u_ref[0]
    g_scr[...] = _dot(u, tbq[...])
    lam = lam_ref[0]
    lre, lim = lam[:, :half], lam[:, half:]
    seq_rows = rows // seqs

    def body(n, carry):
        out = []
        for b in range(seqs):
            cre, cim = carry[2 * b], carry[2 * b + 1]
            row = b * seq_rows + n
            s_scr[pl.ds(row, 1), :] = jnp.concatenate([cre, cim], axis=1)
            g = g_scr[pl.ds(row, 1), :]
            out.append(lre * cre - lim * cim + g[:, :half])
            out.append(lre * cim + lim * cre + g[:, half:])
        return tuple(out)

    zero = jnp.zeros((1, half), F32)
    lax.fori_loop(0, seq_rows, body, (zero,) * (2 * seqs), unroll=2)
    y = _dot(s_scr[...].astype(BF16), tcq[...]) + _dot(u, tkq[...])
    for t in range(SSM_Q):
        t_scr[pl.ds(t, rows, stride=SSM_Q), :] = y[:, t * LANE:(t + 1) * LANE]
    y_ref[0] = t_scr[...].astype(BF16)


def s5_scan(u4, bq, cq, kq, lam_q, layer, batch, seqs):
    sl, rows_total, width = u4.shape
    rows = rows_total // batch * seqs
    tab = pl.BlockSpec((None, 1, width, width // GROUPS_PER_SLAB), lambda j, b: (layer, j, 0, 0))
    return pl.pallas_call(
        functools.partial(_s5_kernel, seqs=seqs),
        grid=(sl, batch // seqs),
        in_specs=[pl.BlockSpec((1, rows, width), lambda j, b: (j, b, 0)), tab, tab, tab,
                  pl.BlockSpec((None, 1, 1, width), lambda j, b: (layer, j, 0, 0))],
        out_specs=pl.BlockSpec((1, rows * SSM_Q, LANE), lambda j, b: (j, b, 0)),
        out_shape=jax.ShapeDtypeStruct((sl, rows_total * SSM_Q, LANE), BF16),
        scratch_shapes=[pltpu.VMEM((rows, width), F32), pltpu.VMEM((rows, width), F32),
                        pltpu.VMEM((rows * SSM_Q, LANE), F32)] + [pltpu.VMEM((width, width), BF16)] * 3,
        compiler_params=_params(2),
        name="s5_scan",
    )(u4, bq, cq, kq, lam_q)


def _mixer_kernel(x_ref, ys_ref, ng_ref, win_ref, bgate_ref, poolw_ref, pscale_ref, lng_ref, lnb_ref,
                  sguw_ref, sgub_ref, gluw_ref, glub_ref, wbr_ref, wout_ref, o_ref, xe_scr):
    tm = x_ref.shape[0]
    width = SSM_SLABS * LANE
    a_end, b_end = width, 3 * width
    c_end = b_end + width
    i = pl.program_id(1)
    x = x_ref[...]
    hb = _rms(x, ng_ref[...]).astype(BF16)

    xa = _dot(hb, win_ref[:, 0:a_end])

    @pl.when(i == 0)
    def _():
        xe_scr[0:POOL_HALO, :] = jnp.zeros((POOL_HALO, xe_scr.shape[1]), F32)

    xe_scr[POOL_HALO:POOL_HALO + tm, :] = xa
    pos = i * tm + lax.broadcasted_iota(I32, (tm, 1), 0)
    ya_parts = []
    for gi, w in enumerate(POOL_WINDOWS):
        ch = slice(gi * LANE, (gi + 1) * LANE)
        win = xa[:, ch]
        for j in range(1, w):
            win = win + xe_scr[POOL_HALO - j:POOL_HALO - j + tm, ch]
        cnt = jnp.minimum(pos + 1, w).astype(F32)
        diff = win / cnt - xa[:, ch]
        ya_parts.append(_dot(diff.astype(BF16), poolw_ref[gi]) * pscale_ref[:, ch])
    ya = jnp.concatenate(ya_parts, axis=1)
    xe_scr[0:POOL_HALO, :] = xe_scr[tm:tm + POOL_HALO, :]

    z = _gelu(_dot(hb, win_ref[:, a_end:b_end]))
    u, v = z[:, :width], z[:, width:]
    mu = jnp.mean(v, axis=-1, keepdims=True)
    vc = v - mu
    var = jnp.mean(vc * vc, axis=-1, keepdims=True)
    vb = (vc * lax.rsqrt(var + EPS) * lng_ref[...] + lnb_ref[...]).astype(BF16)
    r_i = lax.broadcasted_iota(I32, (SGU_CHUNK, SGU_CHUNK), 0)
    c_i = lax.broadcasted_iota(I32, (SGU_CHUNK, SGU_CHUNK), 1)
    n_heads = width // LANE
    wt = [jnp.where(r_i >= c_i, sguw_ref[hd], 0.0).astype(BF16) for hd in range(n_heads)]
    s_rows = []
    for c in range(tm // SGU_CHUNK):
        rs = slice(c * SGU_CHUNK, (c + 1) * SGU_CHUNK)
        s_rows.append(jnp.concatenate(
            [_dot(wt[hd], vb[rs, hd * LANE:(hd + 1) * LANE]) for hd in range(n_heads)], axis=1)
            + sgub_ref[...])
    yb = u * jnp.concatenate(s_rows, axis=0)

    yc = _gelu(jnp.concatenate([ys_ref[j] for j in range(SSM_SLABS)], axis=1).astype(F32))
    yc = yc * _sigmoid(_dot(yc.astype(BF16), gluw_ref[...]) + glub_ref[...])

    merged = None
    for k, yk in enumerate((ya, yb, yc)):
        gate = _sigmoid(_dot(hb, win_ref[:, c_end + k * D_MODEL:c_end + (k + 1) * D_MODEL])
                        + bgate_ref[:, k * D_MODEL:(k + 1) * D_MODEL])
        term = gate * _dot(yk.astype(BF16), wbr_ref[k])
        merged = term if merged is None else merged + term
    o_ref[...] = x + _dot(merged.astype(BF16), wout_ref[...])


def mixer(x2d, ys4, p, layer, batch, tm):
    t = x2d.shape[0]
    nt = t // batch // tm
    row = lambda b, i: (b * nt + i, 0)
    slab = lambda b, i: (0, b * nt + i, 0)
    consts = [p["norm_mix_g"], p["w_in"], p["b_gate"], p["pool_w"], p["pool_scale"], p["sgu_ln_g"],
              p["sgu_ln_b"], p["sgu_w"], p["sgu_b_full"], p["glu_w"], p["glu_b"], p["w_branch"], p["w_out"]]
    return pl.pallas_call(
        _mixer_kernel,
        grid=(batch, nt),
        in_specs=[pl.BlockSpec((tm, D_MODEL), row),
                  pl.BlockSpec((SSM_SLABS, tm, LANE), slab)] + [_layer_spec(c, layer) for c in consts],
        out_specs=pl.BlockSpec((tm, D_MODEL), row),
        out_shape=jax.ShapeDtypeStruct(x2d.shape, F32),
        scratch_shapes=[pltpu.VMEM((tm + POOL_HALO, SSM_SLABS * LANE), F32)],
        compiler_params=_params(2),
        name="mixer",
    )(x2d, ys4, *consts)


def _route_tile(x, g, w_hi, w_lo, br):
    tm = x.shape[0]
    h = _rms(x, g)
    h_hi = h.astype(BF16)
    h_lo = (h - h_hi.astype(F32)).astype(BF16)
    logits = (_dot(h_hi, w_hi) + _dot(h_hi, w_lo) + _dot(h_lo, w_hi)) + br
    lane = lax.broadcasted_iota(I32, logits.shape, 1)
    neg = -jnp.inf
    gl = jnp.where(lane < N_GROUPS, logits, neg)
    gmax = jnp.max(gl, axis=-1, keepdims=True)
    gsum = jnp.sum(jnp.where(lane < N_GROUPS, jnp.exp(logits - gmax), 0.0), axis=-1, keepdims=True)
    g_val = 1.0 / gsum
    g_idx = jnp.min(jnp.where(gl == gmax, lane, LANE), axis=-1, keepdims=True)
    lo = N_GROUPS + EXPERTS_PER_GROUP * g_idx
    el = jnp.where((lane >= lo) & (lane < lo + EXPERTS_PER_GROUP), logits, neg)
    e1v = jnp.max(el, axis=-1, keepdims=True)
    e1l = jnp.min(jnp.where(el == e1v, lane, LANE), axis=-1, keepdims=True)
    el2 = jnp.where(lane == e1l, neg, el)
    e2v = jnp.max(el2, axis=-1, keepdims=True)
    e2l = jnp.min(jnp.where(el2 == e2v, lane, LANE), axis=-1, keepdims=True)
    tt = jnp.exp(e2v - e1v)
    w1 = g_val / (1.0 + tt)
    w2 = g_val * tt / (1.0 + tt)
    hit1, hit2 = lane == e1l - N_GROUPS, lane == e2l - N_GROUPS
    onehot = jnp.where(hit1 | hit2, 1.0, 0.0)
    r_i = lax.broadcasted_iota(I32, (tm, tm), 0)
    c_i = lax.broadcasted_iota(I32, (tm, tm), 1)
    tri = jnp.where(r_i > c_i, 1.0, 0.0).astype(BF16)
    prefix = _dot(tri, onehot.astype(BF16))
    count = jnp.sum(onehot, axis=0, keepdims=True)
    groups8 = jnp.floor((count + (RUN_ALIGN - 1.0)) * (1.0 / RUN_ALIGN))
    count = groups8 * RUN_ALIGN
    e_r = lax.broadcasted_iota(I32, (LANE, LANE), 0)
    e_c = lax.broadcasted_iota(I32, (LANE, LANE), 1)
    before = jnp.where(e_r < e_c, 1.0, 0.0).astype(BF16)
    run_start = _dot(jnp.broadcast_to(groups8, (SUBLANE, LANE)).astype(BF16), before)[0:1, :] * RUN_ALIGN
    local = prefix + run_start
    lp1 = jnp.sum(jnp.where(hit1, local, 0.0), axis=-1, keepdims=True)
    lp2 = jnp.sum(jnp.where(hit2, local, 0.0), axis=-1, keepdims=True)
    route = jnp.where(lane == 0, w1, jnp.where(lane == 1, w2, jnp.where(
        lane == 2, lp1, jnp.where(lane == 3, lp2, 0.0))))
    return route, count, run_start


def _router_kernel(x_ref, g_ref, wr_ref, br_ref, route_ref, meta_ref, carry_scr, *, tm):
    @pl.when(pl.program_id(0) == 0)
    def _():
        carry_scr[...] = jnp.zeros_like(carry_scr)

    tiles = [_route_tile(x_ref[k * tm:(k + 1) * tm, :], g_ref[...], wr_ref[0], wr_ref[1], br_ref[...])
             for k in range(x_ref.shape[0] // tm)]
    earlier = carry_scr[0:1, :]
    sub = lax.broadcasted_iota(I32, (SUBLANE, LANE), 0)
    for k, (route, count, run_start) in enumerate(tiles):
        route_ref[k * tm:(k + 1) * tm, :] = route
        total = earlier + count
        meta_ref[k * SUBLANE:(k + 1) * SUBLANE, :] = jnp.where(sub == 0, count, jnp.where(
            sub == 1, earlier, jnp.where(sub == 2, run_start, jnp.where(sub == 3, total, 0.0))))
        earlier = total
    carry_scr[...] = jnp.broadcast_to(earlier, carry_scr.shape)


def router(x2d, g, wr, br, layer, tm, tiles_per_step):
    t = x2d.shape[0]
    rows = tm * tiles_per_step
    row = lambda i: (i, 0)
    return pl.pallas_call(
        functools.partial(_router_kernel, tm=tm),
        grid=(t // rows,),
        in_specs=[pl.BlockSpec((rows, D_MODEL), row), _layer_spec(g, layer),
                  _layer_spec(wr, layer), _layer_spec(br, layer)],
        out_specs=[pl.BlockSpec((rows, LANE), row), pl.BlockSpec((tiles_per_step * SUBLANE, LANE), row)],
        out_shape=[jax.ShapeDtypeStruct((t, LANE), F32),
                   jax.ShapeDtypeStruct((t // tm * SUBLANE, LANE), F32)],
        scratch_shapes=[pltpu.VMEM((SUBLANE, LANE), F32)],
        compiler_params=_params(1),
        name="router",
    )(x2d, g, wr, br)


def _sort_matrix(route, k, n_sorted):
    col = lax.broadcasted_iota(I32, (route.shape[0], n_sorted), 1)
    return jnp.where(col == route[:, 2 + k:3 + k].astype(I32), 1.0, 0.0).astype(BF16)


def _sorted_rows(tm):
    return -(-(2 * tm + N_EXPERTS * (RUN_ALIGN - 1)) // LANE) * LANE


def _start_runs(cnt_ref, lo_ref, dst_ref, tile, make_copy):
    def body(e, c):
        idx = tile * N_EXPERTS + e
        n = cnt_ref[idx]

        @pl.when(n > 0)
        def _():
            make_copy(pl.multiple_of(lo_ref[idx], RUN_ALIGN), pl.multiple_of(dst_ref[idx], RUN_ALIGN),
                      pl.multiple_of(n, RUN_ALIGN)).start()
        return c
    lax.fori_loop(0, N_EXPERTS, body, 0)


def _wait_runs(tot_ref, tile, make_copy):
    make_copy(0, 0, pl.multiple_of(tot_ref[tile], RUN_ALIGN)).wait()


def _dispatch_kernel(cnt_ref, lo_ref, dst_ref, tot_ref, zst_ref, zln_ref, nused_ref, x_ref, g_ref, route_ref,
                     o_ref, srt_scr, zero_scr, sem, zsem):
    tm = x_ref.shape[0]
    bm = zero_scr.shape[0]
    i = pl.program_id(0)
    n = pl.num_programs(0)
    slot = i % 2

    def run_copy(s):
        return lambda lo, dst, cnt: pltpu.make_async_copy(
            srt_scr.at[s, pl.ds(lo, cnt)], o_ref.at[pl.ds(dst, cnt)], sem.at[s])

    def pad_copy(e):
        ln = pl.multiple_of(zln_ref[e], RUN_ALIGN)
        return pltpu.make_async_copy(zero_scr.at[pl.ds(0, ln)],
                                     o_ref.at[pl.ds(pl.multiple_of(zst_ref[e], RUN_ALIGN), ln)], zsem)

    def tail_copy(b):
        return pltpu.make_async_copy(zero_scr, o_ref.at[pl.ds(pl.multiple_of(b * bm, bm), bm)], zsem)

    @pl.when(i == 0)
    def _():
        zero_scr[...] = jnp.zeros_like(zero_scr)
        for wait in (False, True):
            def body(e, c):
                @pl.when(zln_ref[e] > 0)
                def _():
                    cp = pad_copy(e)
                    cp.wait() if wait else cp.start()
                return c
            lax.fori_loop(0, N_EXPERTS, body, 0)

            def tail(b, c):
                cp = tail_copy(b)
                cp.wait() if wait else cp.start()
                return c
            lax.fori_loop(nused_ref[0], o_ref.shape[0] // bm, tail, 0)

    @pl.when(i >= 2)
    def _():
        _wait_runs(tot_ref, i - 2, run_copy(slot))

    hb = _rms(x_ref[...], g_ref[...]).astype(BF16)
    route = route_ref[...]
    ns = srt_scr.shape[1]
    pt = jnp.maximum(_sort_matrix(route, 0, ns), _sort_matrix(route, 1, ns))
    srt_scr[slot] = lax.dot_general(pt, hb, (((0,), (0,)), ((), ())),
                                    preferred_element_type=F32).astype(ROW_DTYPE)
    _start_runs(cnt_ref, lo_ref, dst_ref, i, run_copy(slot))

    @pl.when(i == n - 1)
    def _():
        @pl.when(i >= 1)
        def _():
            _wait_runs(tot_ref, i - 1, run_copy(1 - slot))
        _wait_runs(tot_ref, i, run_copy(slot))


def dispatch(cnt, lo, dst, tot, zst, zln, nused, x2d, g, layer, route, n_rows, tm, bm):
    t, d = x2d.shape
    row = lambda i, *_: (i, 0)
    return pl.pallas_call(
        _dispatch_kernel,
        grid_spec=pltpu.PrefetchScalarGridSpec(
            num_scalar_prefetch=7,
            grid=(t // tm,),
            in_specs=[pl.BlockSpec((tm, d), row), pl.BlockSpec((None, 1, d), lambda i, *_: (layer, 0, 0)),
                      pl.BlockSpec((tm, LANE), row)],
            out_specs=pl.BlockSpec(memory_space=pl.ANY),
            scratch_shapes=[pltpu.VMEM((2, _sorted_rows(tm), d), ROW_DTYPE), pltpu.VMEM((bm, d), ROW_DTYPE),
                            pltpu.SemaphoreType.DMA((2,)), pltpu.SemaphoreType.DMA(())]),
        out_shape=jax.ShapeDtypeStruct((n_rows, d), ROW_DTYPE),
        compiler_params=_params(1),
        name="dispatch",
    )(cnt, lo, dst, tot, zst, zln, nused, x2d, g, route)


def _expert_kernel(bexp_ref, nused_ref, slot_ref, next_ref, xs_ref, wg_hbm, wu_hbm, wd_hbm, ys_ref,
                   wg_buf, wu_buf, wd_buf, wg_scr, wu_scr, wd_scr, sem, *, layer):
    i = pl.program_id(0)

    def weight_copies(e, s):
        return [pltpu.make_async_copy(hbm.at[layer, e], buf.at[s], sem.at[s])
                for hbm, buf in ((wg_hbm, wg_buf), (wu_hbm, wu_buf), (wd_hbm, wd_buf))]

    @pl.when(i < nused_ref[0])
    def _():
        e, s = bexp_ref[i], slot_ref[i]
        changed = jnp.logical_or(i == 0, e != bexp_ref[jnp.maximum(i - 1, 0)])

        @pl.when(i == 0)
        def _():
            for cp in weight_copies(e, s):
                cp.start()

        @pl.when(changed)
        def _():
            for cp in weight_copies(e, s):
                cp.wait()
            wg_scr[...] = wg_buf[s].astype(BF16)
            wu_scr[...] = wu_buf[s].astype(BF16)
            wd_scr[...] = wd_buf[s].astype(BF16)

            @pl.when(next_ref[i] < N_EXPERTS)
            def _():
                for cp in weight_copies(next_ref[i], 1 - s):
                    cp.start()

        xb = xs_ref[...].astype(BF16)
        hg = _dot(xb, wg_scr[...])
        hu = _dot(xb, wu_scr[...])
        act = (hg * _sigmoid(hg) * hu).astype(BF16)
        ys_ref[...] = _dot(act, wd_scr[...]).astype(ROW_DTYPE)

    @pl.when(i >= nused_ref[0])
    def _():
        ys_ref[...] = jnp.zeros_like(ys_ref)


def experts(bexp, nused, slot, nxt, xs, w_gate, w_up, w_down, layer, bm):
    n_rows, d = xs.shape
    de = w_gate.shape[3]
    anyspace = pl.BlockSpec(memory_space=pl.ANY)
    return pl.pallas_call(
        functools.partial(_expert_kernel, layer=layer),
        grid_spec=pltpu.PrefetchScalarGridSpec(
            num_scalar_prefetch=4,
            grid=(n_rows // bm,),
            in_specs=[pl.BlockSpec((bm, d), lambda i, bexp, nused, *_: (jnp.minimum(i, nused[0] - 1), 0)),
                      anyspace, anyspace, anyspace],
            out_specs=pl.BlockSpec((bm, d), lambda i, *_: (i, 0)),
            scratch_shapes=[pltpu.VMEM((2, d, de), F32), pltpu.VMEM((2, d, de), F32), pltpu.VMEM((2, de, d), F32),
                            pltpu.VMEM((d, de), BF16), pltpu.VMEM((d, de), BF16), pltpu.VMEM((de, d), BF16),
                            pltpu.SemaphoreType.DMA((2,))]),
        out_shape=jax.ShapeDtypeStruct(xs.shape, ROW_DTYPE),
        compiler_params=_params(1),
        name="experts",
    )(bexp, nused, slot, nxt, xs, w_gate, w_up, w_down)


def _combine_kernel(cnt_ref, lo_ref, dst_ref, tot_ref, x_ref, route_ref, g_ref, *rest, final):
    if final:
        ys_ref, o_ref, buf, sem = rest
    else:
        w_ref, ys_ref, o_ref, xc_ref, buf, sem, xc_scr = rest
    i = pl.program_id(0)
    n = pl.num_programs(0)
    slot = i % 2

    def run_copy(s):
        return lambda lo, dst, cnt: pltpu.make_async_copy(
            ys_ref.at[pl.ds(dst, cnt)], buf.at[s, pl.ds(lo, cnt)], sem.at[s])

    @pl.when(i == 0)
    def _():
        buf[...] = jnp.zeros_like(buf)
        _start_runs(cnt_ref, lo_ref, dst_ref, 0, run_copy(0))

    @pl.when(i + 1 < n)
    def _():
        _start_runs(cnt_ref, lo_ref, dst_ref, i + 1, run_copy(1 - slot))

    _wait_runs(tot_ref, i, run_copy(slot))
    ysb = buf[slot].astype(BF16)
    route = route_ref[...]
    out = x_ref[...]
    for k in range(2):
        out = out + route[:, k:k + 1] * _dot(_sort_matrix(route, k, buf.shape[1]), ysb)
    if final:
        out = _rms(out, g_ref[...])
    else:
        _write_ssm_input(out, g_ref[...], w_ref[...], xc_ref, xc_scr)
    o_ref[...] = out


def combine(cnt, lo, dst, tot, x2d, route, ys, tm, g_final=None, next_mix=None):
    t, d = x2d.shape
    final = next_mix is None
    row = lambda i, *_: (i, 0)
    buf = [pltpu.VMEM((2, _sorted_rows(tm), d), ROW_DTYPE), pltpu.SemaphoreType.DMA((2,))]
    if final:
        extra_in, extra_args = [pl.BlockSpec((1, d), lambda i, *_: (0, 0))], (g_final[None, :],)
        out_specs, out_shape, scratch = pl.BlockSpec((tm, d), row), jax.ShapeDtypeStruct((t, d), F32), buf
    else:
        g_next, w_next, layer, col_block = next_mix
        width = SSM_SLABS * LANE
        extra_in = [pl.BlockSpec((None, 1, d), lambda i, *_: (layer, 0, 0)),
                    pl.BlockSpec((None, d, width), lambda i, *_: (layer, 0, col_block))]
        extra_args = (g_next, w_next)
        out_specs = [pl.BlockSpec((tm, d), row),
                     pl.BlockSpec((SSM_SLABS, tm // SSM_Q, SSM_Q * LANE), lambda i, *_: (0, i, 0))]
        out_shape = [jax.ShapeDtypeStruct((t, d), F32),
                     jax.ShapeDtypeStruct((SSM_SLABS, t // SSM_Q, SSM_Q * LANE), BF16)]
        scratch = buf + [pltpu.VMEM((SSM_SLABS, tm, LANE), F32)]
    return pl.pallas_call(
        functools.partial(_combine_kernel, final=final),
        grid_spec=pltpu.PrefetchScalarGridSpec(
            num_scalar_prefetch=4,
            grid=(t // tm,),
            in_specs=[pl.BlockSpec((tm, d), row), pl.BlockSpec((tm, LANE), row)] + extra_in
            + [pl.BlockSpec(memory_space=pl.ANY)],
            out_specs=out_specs,
            scratch_shapes=scratch),
        out_shape=out_shape,
        compiler_params=_params(1),
        name="combine",
    )(cnt, lo, dst, tot, x2d, route, *extra_args, ys)


def router_params(rg_w, rg_b, re_w, re_b):
    depth = rg_w.shape[0]
    pad = LANE - N_GROUPS - N_EXPERTS
    wr = jnp.concatenate([rg_w, re_w, jnp.zeros((depth, D_MODEL, pad), F32)], axis=2)
    wr_hi = wr.astype(BF16)
    wr = jnp.stack([wr_hi, (wr - wr_hi.astype(F32)).astype(BF16)], axis=1)
    br = jnp.concatenate([rg_b, re_b, jnp.zeros((depth, pad), F32)], axis=1)[:, None, :]
    return wr, br


def moe(x2d, g2, wr, br, w_gate, w_up, w_down, layer, tm, bm, g_final=None, next_mix=None):
    t = x2d.shape[0]
    nt = t // tm
    route, meta = router(x2d, g2, wr, br, layer, tm, 4)
    meta = meta.reshape(nt, SUBLANE, LANE)[:, :, :N_EXPERTS].astype(I32)
    cnt, earlier, lo = meta[:, 0], meta[:, 1], meta[:, 2]
    counts = meta[nt - 1, 3]
    padded = (counts + bm - 1) // bm * bm
    ends = jnp.cumsum(padded)
    offs = ends - padded
    dst = offs[None, :] + earlier
    n_rows = -(-(2 * t + nt * N_EXPERTS * (RUN_ALIGN - 1)) // bm) * bm + N_EXPERTS * bm
    nused = (ends[-1] // bm).astype(I32)
    blk_start = jnp.arange(n_rows // bm, dtype=I32) * bm
    bexp = jnp.sum((ends[None, :] <= blk_start[:, None]).astype(I32), axis=1)
    last = jnp.sum((ends <= (nused - 1) * bm).astype(I32))
    bexp = jnp.minimum(bexp, last)
    tot = jnp.sum(cnt, axis=1)
    cnt, lo, dst = cnt.reshape(-1), lo.reshape(-1), dst.reshape(-1)
    xs = dispatch(cnt, lo, dst, tot, offs + counts, padded - counts, nused[None], x2d, g2, layer, route,
                  n_rows, tm, bm)
    ar = jnp.arange(N_EXPERTS, dtype=I32)
    active = padded > 0
    rank = jnp.cumsum(active.astype(I32)) - active.astype(I32)
    later = (ar[None, :] > ar[:, None]) & active[None, :]
    next_e = jnp.min(jnp.where(later, ar[None, :], N_EXPERTS), axis=1).astype(I32)
    ys = experts(bexp, nused[None], (rank % 2)[bexp], next_e[bexp], xs, w_gate, w_up, w_down, layer, bm)
    return combine(cnt, lo, dst, tot, x2d, route, ys, tm, g_final=g_final, next_mix=next_mix)


def kernel(x, norm_mix_g, w_in, b_gate, pool_w, pool_scale, sgu_ln_g, sgu_ln_b, sgu_w, sgu_b, ssm_a_re, ssm_a_im, ssm_log_dt, ssm_b_re, ssm_b_im, ssm_c_re, ssm_c_im, ssm_d, glu_w, glu_b, w_branch, w_out, norm_ffn_g, router_group_w, router_group_b, router_expert_w, router_expert_b, exp_w_gate, exp_w_up, exp_w_down, norm_final_g):
    batch, seq, d = x.shape
    depth = w_in.shape[0]
    tm_in, tm_mix, tm_moe, bm = 1024, 512, 512, 512
    s5_seqs = 2
    x2d = x.reshape(batch * seq, d)
    ssm_col_block = 3
    row = lambda a: a[:, None, :]
    p = {
        "norm_mix_g": row(norm_mix_g), "w_in": w_in.astype(BF16), "b_gate": row(b_gate),
        "pool_w": pool_w.astype(BF16), "pool_scale": row(pool_scale),
        "sgu_ln_g": row(sgu_ln_g), "sgu_ln_b": row(sgu_ln_b), "sgu_w": sgu_w,
        "sgu_b_full": jnp.repeat(jnp.swapaxes(sgu_b, 1, 2), LANE, axis=2),
        "glu_w": glu_w.astype(BF16), "glu_b": row(glu_b),
        "w_branch": w_branch.astype(BF16), "w_out": w_out.astype(BF16),
    }
    bq, cq, kq, lam_q = jax.vmap(ssm_tables)(ssm_a_re, ssm_a_im, ssm_log_dt, ssm_b_re, ssm_b_im,
                                             ssm_c_re, ssm_c_im, ssm_d)
    wr, br = router_params(router_group_w, router_group_b, router_expert_w, router_expert_b)
    g_ffn = row(norm_ffn_g)
    xc4 = ssm_input(x2d, p["norm_mix_g"], p["w_in"], 0, ssm_col_block, tm_in)
    for l in range(depth):
        ys4 = s5_scan(xc4, bq, cq, kq, lam_q, l, batch, s5_seqs)
        x2d = mixer(x2d, ys4, p, l, batch, tm_mix)
        if l == depth - 1:
            x2d = moe(x2d, g_ffn, wr, br, exp_w_gate, exp_w_up, exp_w_down, l, tm_moe, bm, g_final=norm_final_g)
        else:
            x2d, xc4 = moe(x2d, g_ffn, wr, br, exp_w_gate, exp_w_up, exp_w_down, l, tm_moe, bm,
                           next_mix=(p["norm_mix_g"], p["w_in"], l + 1, ssm_col_block))
    return x2d.reshape(batch, seq, d)
```

```python
import functools
import math

import jax
import jax.numpy as jnp
from jax import lax
from jax.experimental import pallas as pl
from jax.experimental.pallas import tpu as pltpu

F32 = jnp.float32
BF16 = jnp.bfloat16
I32 = jnp.int32

D_MODEL = 1024
POOL_WINDOWS = (2, 4, 8, 16)
POOL_HALO = 16
LANE = 128
SUBLANE = 8
ROW_DTYPE = BF16
RUN_ALIGN = 8 * (4 // jnp.dtype(ROW_DTYPE).itemsize)
SGU_CHUNK = 128
SSM_GROUP_DIM = 16
SSM_STATE = 64
SSM_Q = 8
SSM_SLABS = 4
GROUPS_PER_SLAB = LANE // SSM_GROUP_DIM
N_GROUPS = 4
EXPERTS_PER_GROUP = 8
N_EXPERTS = 32
EPS = 1e-6
VMEM_LIMIT = 56 * 1024 * 1024


def _rms(x, g):
    return x * lax.rsqrt(jnp.mean(x * x, axis=-1, keepdims=True) + EPS) * g


def _gelu(x):
    c = math.sqrt(2.0 / math.pi)
    return x * (0.5 * (1.0 + jnp.tanh(c * (x + 0.044715 * (x * x * x)))))


def _sigmoid(x):
    return 0.5 * (1.0 + jnp.tanh(0.5 * x))


def _dot(a, b):
    return jnp.dot(a, b, preferred_element_type=F32)


def _layer_spec(arr, layer):
    nd = arr.ndim - 1
    return pl.BlockSpec((None,) + arr.shape[1:], lambda *_: (layer,) + (0,) * nd, pipeline_mode=pl.Buffered(1))


def _params(n_axes):
    return pltpu.CompilerParams(dimension_semantics=("arbitrary",) * n_axes, vmem_limit_bytes=VMEM_LIMIT)


def _write_ssm_input(x, g, w, o_ref, xc_scr):
    h = _rms(x, g).astype(BF16)
    xc = _dot(h, w)
    rows = xc_scr.shape[1] // SSM_Q
    for j in range(SSM_SLABS):
        xc_scr[j] = xc[:, j * LANE:(j + 1) * LANE]
        for s in range(SSM_Q):
            o_ref[j, :, s * LANE:(s + 1) * LANE] = xc_scr[j, pl.ds(s, rows, stride=SSM_Q), :].astype(BF16)


def _xc_kernel(x_ref, g_ref, w_ref, o_ref, xc_scr):
    _write_ssm_input(x_ref[...], g_ref[...], w_ref[...], o_ref, xc_scr)


def ssm_input(x2d, g, w_in_bf, layer, col_block, tm):
    t = x2d.shape[0]
    width = SSM_SLABS * LANE
    return pl.pallas_call(
        _xc_kernel,
        grid=(t // tm,),
        in_specs=[pl.BlockSpec((tm, D_MODEL), lambda i: (i, 0)),
                  _layer_spec(g, layer),
                  pl.BlockSpec((None, D_MODEL, width), lambda i: (layer, 0, col_block),
                               pipeline_mode=pl.Buffered(1))],
        out_specs=pl.BlockSpec((SSM_SLABS, tm // SSM_Q, SSM_Q * LANE), lambda i: (0, i, 0)),
        out_shape=jax.ShapeDtypeStruct((SSM_SLABS, t // SSM_Q, SSM_Q * LANE), BF16),
        scratch_shapes=[pltpu.VMEM((SSM_SLABS, tm, LANE), F32)],
        compiler_params=_params(1),
        name="ssm_input",
    )(x2d, g, w_in_bf)


def ssm_tables(a_re, a_im, log_dt, b_re, b_im, c_re, c_im, d_skip):
    q, nc = SSM_Q, SSM_GROUP_DIM
    ng = a_re.shape[0]
    sl = ng // GROUPS_PER_SLAB
    dt = jnp.exp(log_dt)[:, None]

    def powers(kvec):
        kk = kvec[:, None, None]
        mag = jnp.exp(a_re * dt * kk)
        return mag * jnp.cos(a_im * dt * kk), mag * jnp.sin(a_im * dt * kk)

    pw_re, pw_im = powers(jnp.arange(q + 1, dtype=F32))
    nr, ni = pw_re[1] - 1.0, pw_im[1]
    den = a_re * a_re + a_im * a_im
    fr, fi = (nr * a_re + ni * a_im) / den, (ni * a_re - nr * a_im) / den
    bb_re = fr[..., None] * b_re - fi[..., None] * b_im
    bb_im = fr[..., None] * b_im + fi[..., None] * b_re

    def to_slab_rows(a, lead):
        a = a.reshape(lead, sl, GROUPS_PER_SLAB, a.shape[2], a.shape[3]).transpose(1, 0, 2, 3, 4)
        return a.reshape(sl, -1, a.shape[-1]).astype(BF16)

    rev_re, rev_im = powers((q - 1.0) - jnp.arange(q, dtype=F32))
    bt_re, bt_im = bb_re.transpose(0, 2, 1), bb_im.transpose(0, 2, 1)
    m_re = rev_re[:, :, None, :] * bt_re[None] - rev_im[:, :, None, :] * bt_im[None]
    m_im = rev_re[:, :, None, :] * bt_im[None] + rev_im[:, :, None, :] * bt_re[None]
    bq = to_slab_rows(jnp.concatenate([m_re, m_im], axis=-1), q)
    ct_re, ct_im = c_re.transpose(0, 2, 1), c_im.transpose(0, 2, 1)
    ct_re, ct_im = jnp.tile(ct_re, (1, 1, q + 1)), jnp.tile(ct_im, (1, 1, q + 1))
    pl_re = jnp.repeat(pw_re.transpose(1, 2, 0), nc, axis=-1)
    pl_im = jnp.repeat(pw_im.transpose(1, 2, 0), nc, axis=-1)
    r_re = ct_re * pl_re - ct_im * pl_im
    r_im = ct_re * pl_im + ct_im * pl_re
    cq = to_slab_rows(jnp.stack([r_re[..., nc:], -r_im[..., nc:]], axis=0), 2)
    hi = lax.Precision.HIGHEST
    kern = (jnp.einsum('gpd,gpl->gdl', bb_re, r_re[..., :q * nc], precision=hi)
            - jnp.einsum('gpd,gpl->gdl', bb_im, r_im[..., :q * nc], precision=hi))
    lane = jnp.arange(q * nc)[None, None, :]
    d_pad = jnp.pad(d_skip.reshape(ng, nc), ((0, 0), (0, (q - 1) * nc)))
    kern = kern + jnp.where(lane == jnp.arange(nc)[None, :, None], d_pad[:, None, :], 0.0)
    kq = jnp.stack([jnp.pad(kern[..., :(q - s) * nc], ((0, 0), (0, 0), (s * nc, 0))) for s in range(q)], axis=0)
    kq = to_slab_rows(kq, q)
    lam_q = jnp.concatenate([pw_re[q].reshape(sl, 1, -1), pw_im[q].reshape(sl, 1, -1)], axis=-1)
    return bq, cq, kq, lam_q


def _expand_block_diag(small, row_inner, col_inner):
    n_rows, n_small = small.shape
    n_cols = n_small * GROUPS_PER_SLAB
    g_bits = GROUPS_PER_SLAB.bit_length() - 1
    r_shift, c_shift = row_inner.bit_length() - 1, col_inner.bit_length() - 1
    k = lax.broadcasted_iota(I32, (n_small, n_cols), 0)
    col = lax.broadcasted_iota(I32, (n_small, n_cols), 1)
    src = ((col >> (c_shift + g_bits)) << c_shift) | (col & (col_inner - 1))
    spread = jnp.where(k == src, 1.0, 0.0).astype(BF16)
    r = lax.broadcasted_iota(I32, (n_rows, n_cols), 0)
    c = lax.broadcasted_iota(I32, (n_rows, n_cols), 1)
    same = ((r >> r_shift) & (GROUPS_PER_SLAB - 1)) == ((c >> c_shift) & (GROUPS_PER_SLAB - 1))
    return jnp.where(same, _dot(small, spread), 0.0).astype(BF16)


def _s5_kernel(u_ref, bq_ref, cq_ref, kq_ref, lam_ref, y_ref, g_scr, s_scr, t_scr, tbq, tcq, tkq, *, seqs):
    half = GROUPS_PER_SLAB * SSM_STATE
    rows = u_ref.shape[1]

    @pl.when(pl.program_id(1) == 0)
    def _():
        tbq[...] = _expand_block_diag(bq_ref[0], SSM_GROUP_DIM, SSM_STATE)
        tcq[...] = _expand_block_diag(cq_ref[0], SSM_STATE, SSM_GROUP_DIM)
        tkq[...] = _expand_block_diag(kq_ref[0], SSM_GROUP_DIM, SSM_GROUP_DIM)

    u = u_ref[0]
    g_scr[...] = _dot(u, tbq[...])
    lam = lam_ref[0]
    lre, lim = lam[:, :half], lam[:, half:]
    seq_rows = rows // seqs

    def body(n, carry):
        out = []
        for b in range(seqs):
            cre, cim = carry[2 * b], carry[2 * b + 1]
            row = b * seq_rows + n
            s_scr[pl.ds(row, 1), :] = jnp.concatenate([cre, cim], axis=1)
            g = g_scr[pl.ds(row, 1), :]
            out.append(lre * cre - lim * cim + g[:, :half])
            out.append(lre * cim + lim * cre + g[:, half:])
        return tuple(out)

    zero = jnp.zeros((1, half), F32)
    lax.fori_loop(0, seq_rows, body, (zero,) * (2 * seqs), unroll=2)
    y = _dot(s_scr[...].astype(BF16), tcq[...]) + _dot(u, tkq[...])
    for t in range(SSM_Q):
        t_scr[pl.ds(t, rows, stride=SSM_Q), :] = y[:, t * LANE:(t + 1) * LANE]
    y_ref[0] = t_scr[...].astype(BF16)


def s5_scan(u4, bq, cq, kq, lam_q, layer, batch, seqs):
    sl, rows_total, width = u4.shape
    rows = rows_total // batch * seqs
    tab = pl.BlockSpec((None, 1, width, width // GROUPS_PER_SLAB), lambda j, b: (layer, j, 0, 0))
    return pl.pallas_call(
        functools.partial(_s5_kernel, seqs=seqs),
        grid=(sl, batch // seqs),
        in_specs=[pl.BlockSpec((1, rows, width), lambda j, b: (j, b, 0)), tab, tab, tab,
                  pl.BlockSpec((None, 1, 1, width), lambda j, b: (layer, j, 0, 0))],
        out_specs=pl.BlockSpec((1, rows * SSM_Q, LANE), lambda j, b: (j, b, 0)),
        out_shape=jax.ShapeDtypeStruct((sl, rows_total * SSM_Q, LANE), BF16),
        scratch_shapes=[pltpu.VMEM((rows, width), F32), pltpu.VMEM((rows, width), F32),
                        pltpu.VMEM((rows * SSM_Q, LANE), F32)] + [pltpu.VMEM((width, width), BF16)] * 3,
        compiler_params=_params(2),
        name="s5_scan",
    )(u4, bq, cq, kq, lam_q)


def _mixer_kernel(x_ref, ys_ref, ng_ref, win_ref, bgate_ref, poolw_ref, pscale_ref, lng_ref, lnb_ref,
                  sguw_ref, sgub_ref, gluw_ref, glub_ref, wbr_ref, wout_ref, o_ref, xe_scr):
    tm = x_ref.shape[0]
    width = SSM_SLABS * LANE
    a_end, b_end = width, 3 * width
    c_end = b_end + width
    i = pl.program_id(1)
    x = x_ref[...]
    hb = _rms(x, ng_ref[...]).astype(BF16)

    xa = _dot(hb, win_ref[:, 0:a_end])

    @pl.when(i == 0)
    def _():
        xe_scr[0:POOL_HALO, :] = jnp.zeros((POOL_HALO, xe_scr.shape[1]), F32)

    xe_scr[POOL_HALO:POOL_HALO + tm, :] = xa
    pos = i * tm + lax.broadcasted_iota(I32, (tm, 1), 0)
    ya_parts = []
    for gi, w in enumerate(POOL_WINDOWS):
        ch = slice(gi * LANE, (gi + 1) * LANE)
        win = xa[:, ch]
        for j in range(1, w):
            win = win + xe_scr[POOL_HALO - j:POOL_HALO - j + tm, ch]
        cnt = jnp.minimum(pos + 1, w).astype(F32)
        diff = win / cnt - xa[:, ch]
        ya_parts.append(_dot(diff.astype(BF16), poolw_ref[gi]) * pscale_ref[:, ch])
    ya = jnp.concatenate(ya_parts, axis=1)
    xe_scr[0:POOL_HALO, :] = xe_scr[tm:tm + POOL_HALO, :]

    z = _gelu(_dot(hb, win_ref[:, a_end:b_end]))
    u, v = z[:, :width], z[:, width:]
    mu = jnp.mean(v, axis=-1, keepdims=True)
    vc = v - mu
    var = jnp.mean(vc * vc, axis=-1, keepdims=True)
    vb = (vc * lax.rsqrt(var + EPS) * lng_ref[...] + lnb_ref[...]).astype(BF16)
    r_i = lax.broadcasted_iota(I32, (SGU_CHUNK, SGU_CHUNK), 0)
    c_i = lax.broadcasted_iota(I32, (SGU_CHUNK, SGU_CHUNK), 1)
    n_heads = width // LANE
    wt = [jnp.where(r_i >= c_i, sguw_ref[hd], 0.0).astype(BF16) for hd in range(n_heads)]
    s_rows = []
    for c in range(tm // SGU_CHUNK):
        rs = slice(c * SGU_CHUNK, (c + 1) * SGU_CHUNK)
        s_rows.append(jnp.concatenate(
            [_dot(wt[hd], vb[rs, hd * LANE:(hd + 1) * LANE]) for hd in range(n_heads)], axis=1)
            + sgub_ref[...])
    yb = u * jnp.concatenate(s_rows, axis=0)

    yc = _gelu(jnp.concatenate([ys_ref[j] for j in range(SSM_SLABS)], axis=1).astype(F32))
    yc = yc * _sigmoid(_dot(yc.astype(BF16), gluw_ref[...]) + glub_ref[...])

    merged = None
    for k, yk in enumerate((ya, yb, yc)):
        gate = _sigmoid(_dot(hb, win_ref[:, c_end + k * D_MODEL:c_end + (k + 1) * D_MODEL])
                        + bgate_ref[:, k * D_MODEL:(k + 1) * D_MODEL])
        term = gate * _dot(yk.astype(BF16), wbr_ref[k])
        merged = term if merged is None else merged + term
    o_ref[...] = x + _dot(merged.astype(BF16), wout_ref[...])


def mixer(x2d, ys4, p, layer, batch, tm):
    t = x2d.shape[0]
    nt = t // batch // tm
    row = lambda b, i: (b * nt + i, 0)
    slab = lambda b, i: (0, b * nt + i, 0)
    consts = [p["norm_mix_g"], p["w_in"], p["b_gate"], p["pool_w"], p["pool_scale"], p["sgu_ln_g"],
              p["sgu_ln_b"], p["sgu_w"], p["sgu_b_full"], p["glu_w"], p["glu_b"], p["w_branch"], p["w_out"]]
    return pl.pallas_call(
        _mixer_kernel,
        grid=(batch, nt),
        in_specs=[pl.BlockSpec((tm, D_MODEL), row),
                  pl.BlockSpec((SSM_SLABS, tm, LANE), slab)] + [_layer_spec(c, layer) for c in consts],
        out_specs=pl.BlockSpec((tm, D_MODEL), row),
        out_shape=jax.ShapeDtypeStruct(x2d.shape, F32),
        scratch_shapes=[pltpu.VMEM((tm + POOL_HALO, SSM_SLABS * LANE), F32)],
        compiler_params=_params(2),
        name="mixer",
    )(x2d, ys4, *consts)


def _route_tile(x, g, w_hi, w_lo, br):
    tm = x.shape[0]
    h = _rms(x, g)
    h_hi = h.astype(BF16)
    h_lo = (h - h_hi.astype(F32)).astype(BF16)
    logits = (_dot(h_hi, w_hi) + _dot(h_hi, w_lo) + _dot(h_lo, w_hi)) + br
    lane = lax.broadcasted_iota(I32, logits.shape, 1)
    neg = -jnp.inf
    gl = jnp.where(lane < N_GROUPS, logits, neg)
    gmax = jnp.max(gl, axis=-1, keepdims=True)
    gsum = jnp.sum(jnp.where(lane < N_GROUPS, jnp.exp(logits - gmax), 0.0), axis=-1, keepdims=True)
    g_val = 1.0 / gsum
    g_idx = jnp.min(jnp.where(gl == gmax, lane, LANE), axis=-1, keepdims=True)
    lo = N_GROUPS + EXPERTS_PER_GROUP * g_idx
    el = jnp.where((lane >= lo) & (lane < lo + EXPERTS_PER_GROUP), logits, neg)
    e1v = jnp.max(el, axis=-1, keepdims=True)
    e1l = jnp.min(jnp.where(el == e1v, lane, LANE), axis=-1, keepdims=True)
    el2 = jnp.where(lane == e1l, neg, el)
    e2v = jnp.max(el2, axis=-1, keepdims=True)
    e2l = jnp.min(jnp.where(el2 == e2v, lane, LANE), axis=-1, keepdims=True)
    tt = jnp.exp(e2v - e1v)
    w1 = g_val / (1.0 + tt)
    w2 = g_val * tt / (1.0 + tt)
    hit1, hit2 = lane == e1l - N_GROUPS, lane == e2l - N_GROUPS
    onehot = jnp.where(hit1 | hit2, 1.0, 0.0)
    r_i = lax.broadcasted_iota(I32, (tm, tm), 0)
    c_i = lax.broadcasted_iota(I32, (tm, tm), 1)
    tri = jnp.where(r_i > c_i, 1.0, 0.0).astype(BF16)
    prefix = _dot(tri, onehot.astype(BF16))
    count = jnp.sum(onehot, axis=0, keepdims=True)
    groups8 = jnp.floor((count + (RUN_ALIGN - 1.0)) * (1.0 / RUN_ALIGN))
    count = groups8 * RUN_ALIGN
    e_r = lax.broadcasted_iota(I32, (LANE, LANE), 0)
    e_c = lax.broadcasted_iota(I32, (LANE, LANE), 1)
    before = jnp.where(e_r < e_c, 1.0, 0.0).astype(BF16)
    run_start = _dot(jnp.broadcast_to(groups8, (SUBLANE, LANE)).astype(BF16), before)[0:1, :] * RUN_ALIGN
    local = prefix + run_start
    lp1 = jnp.sum(jnp.where(hit1, local, 0.0), axis=-1, keepdims=True)
    lp2 = jnp.sum(jnp.where(hit2, local, 0.0), axis=-1, keepdims=True)
    route = jnp.where(lane == 0, w1, jnp.where(lane == 1, w2, jnp.where(
        lane == 2, lp1, jnp.where(lane == 3, lp2, 0.0))))
    return route, count, run_start


def _router_kernel(x_ref, g_ref, wr_ref, br_ref, route_ref, meta_ref, carry_scr, *, tm):
    @pl.when(pl.program_id(0) == 0)
    def _():
        carry_scr[...] = jnp.zeros_like(carry_scr)

    tiles = [_route_tile(x_ref[k * tm:(k + 1) * tm, :], g_ref[...], wr_ref[0], wr_ref[1], br_ref[...])
             for k in range(x_ref.shape[0] // tm)]
    earlier = carry_scr[0:1, :]
    sub = lax.broadcasted_iota(I32, (SUBLANE, LANE), 0)
    for k, (route, count, run_start) in enumerate(tiles):
        route_ref[k * tm:(k + 1) * tm, :] = route
        total = earlier + count
        meta_ref[k * SUBLANE:(k + 1) * SUBLANE, :] = jnp.where(sub == 0, count, jnp.where(
            sub == 1, earlier, jnp.where(sub == 2, run_start, jnp.where(sub == 3, total, 0.0))))
        earlier = total
    carry_scr[...] = jnp.broadcast_to(earlier, carry_scr.shape)


def router(x2d, g, wr, br, layer, tm, tiles_per_step):
    t = x2d.shape[0]
    rows = tm * tiles_per_step
    row = lambda i: (i, 0)
    return pl.pallas_call(
        functools.partial(_router_kernel, tm=tm),
        grid=(t // rows,),
        in_specs=[pl.BlockSpec((rows, D_MODEL), row), _layer_spec(g, layer),
                  _layer_spec(wr, layer), _layer_spec(br, layer)],
        out_specs=[pl.BlockSpec((rows, LANE), row), pl.BlockSpec((tiles_per_step * SUBLANE, LANE), row)],
        out_shape=[jax.ShapeDtypeStruct((t, LANE), F32),
                   jax.ShapeDtypeStruct((t // tm * SUBLANE, LANE), F32)],
        scratch_shapes=[pltpu.VMEM((SUBLANE, LANE), F32)],
        compiler_params=_params(1),
        name="router",
    )(x2d, g, wr, br)


def _sort_matrix(route, k, n_sorted):
    col = lax.broadcasted_iota(I32, (route.shape[0], n_sorted), 1)
    return jnp.where(col == route[:, 2 + k:3 + k].astype(I32), 1.0, 0.0).astype(BF16)


def _sorted_rows(tm):
    return -(-(2 * tm + N_EXPERTS * (RUN_ALIGN - 1)) // LANE) * LANE


def _start_runs(cnt_ref, lo_ref, dst_ref, tile, make_copy):
    def body(e, c):
        idx = tile * N_EXPERTS + e
        n = cnt_ref[idx]

        @pl.when(n > 0)
        def _():
            make_copy(pl.multiple_of(lo_ref[idx], RUN_ALIGN), pl.multiple_of(dst_ref[idx], RUN_ALIGN),
                      pl.multiple_of(n, RUN_ALIGN)).start()
        return c
    lax.fori_loop(0, N_EXPERTS, body, 0)


def _wait_runs(tot_ref, tile, make_copy):
    make_copy(0, 0, pl.multiple_of(tot_ref[tile], RUN_ALIGN)).wait()


def _dispatch_kernel(cnt_ref, lo_ref, dst_ref, tot_ref, zst_ref, zln_ref, nused_ref, x_ref, g_ref, route_ref,
                     o_ref, srt_scr, zero_scr, sem, zsem):
    tm = x_ref.shape[0]
    bm = zero_scr.shape[0]
    i = pl.program_id(0)
    n = pl.num_programs(0)
    slot = i % 2

    def run_copy(s):
        return lambda lo, dst, cnt: pltpu.make_async_copy(
            srt_scr.at[s, pl.ds(lo, cnt)], o_ref.at[pl.ds(dst, cnt)], sem.at[s])

    def pad_copy(e):
        ln = pl.multiple_of(zln_ref[e], RUN_ALIGN)
        return pltpu.make_async_copy(zero_scr.at[pl.ds(0, ln)],
                                     o_ref.at[pl.ds(pl.multiple_of(zst_ref[e], RUN_ALIGN), ln)], zsem)

    def tail_copy(b):
        return pltpu.make_async_copy(zero_scr, o_ref.at[pl.ds(pl.multiple_of(b * bm, bm), bm)], zsem)

    @pl.when(i == 0)
    def _():
        zero_scr[...] = jnp.zeros_like(zero_scr)
        for wait in (False, True):
            def body(e, c):
                @pl.when(zln_ref[e] > 0)
                def _():
                    cp = pad_copy(e)
                    cp.wait() if wait else cp.start()
                return c
            lax.fori_loop(0, N_EXPERTS, body, 0)

            def tail(b, c):
                cp = tail_copy(b)
                cp.wait() if wait else cp.start()
                return c
            lax.fori_loop(nused_ref[0], o_ref.shape[0] // bm, tail, 0)

    @pl.when(i >= 2)
    def _():
        _wait_runs(tot_ref, i - 2, run_copy(slot))

    hb = _rms(x_ref[...], g_ref[...]).astype(BF16)
    route = route_ref[...]
    ns = srt_scr.shape[1]
    pt = jnp.maximum(_sort_matrix(route, 0, ns), _sort_matrix(route, 1, ns))
    srt_scr[slot] = lax.dot_general(pt, hb, (((0,), (0,)), ((), ())),
                                    preferred_element_type=F32).astype(ROW_DTYPE)
    _start_runs(cnt_ref, lo_ref, dst_ref, i, run_copy(slot))

    @pl.when(i == n - 1)
    def _():
        @pl.when(i >= 1)
        def _():
            _wait_runs(tot_ref, i - 1, run_copy(1 - slot))
        _wait_runs(tot_ref, i, run_copy(slot))


def dispatch(cnt, lo, dst, tot, zst, zln, nused, x2d, g, layer, route, n_rows, tm, bm):
    t, d = x2d.shape
    row = lambda i, *_: (i, 0)
    return pl.pallas_call(
        _dispatch_kernel,
        grid_spec=pltpu.PrefetchScalarGridSpec(
            num_scalar_prefetch=7,
            grid=(t // tm,),
            in_specs=[pl.BlockSpec((tm, d), row), pl.BlockSpec((None, 1, d), lambda i, *_: (layer, 0, 0)),
                      pl.BlockSpec((tm, LANE), row)],
            out_specs=pl.BlockSpec(memory_space=pl.ANY),
            scratch_shapes=[pltpu.VMEM((2, _sorted_rows(tm), d), ROW_DTYPE), pltpu.VMEM((bm, d), ROW_DTYPE),
                            pltpu.SemaphoreType.DMA((2,)), pltpu.SemaphoreType.DMA(())]),
        out_shape=jax.ShapeDtypeStruct((n_rows, d), ROW_DTYPE),
        compiler_params=_params(1),
        name="dispatch",
    )(cnt, lo, dst, tot, zst, zln, nused, x2d, g, route)


def _expert_kernel(bexp_ref, nused_ref, slot_ref, next_ref, xs_ref, wg_hbm, wu_hbm, wd_hbm, ys_ref,
                   wg_buf, wu_buf, wd_buf, wg_scr, wu_scr, wd_scr, sem, *, layer):
    i = pl.program_id(0)

    def weight_copies(e, s):
        return [pltpu.make_async_copy(hbm.at[layer, e], buf.at[s], sem.at[s])
                for hbm, buf in ((wg_hbm, wg_buf), (wu_hbm, wu_buf), (wd_hbm, wd_buf))]

    @pl.when(i < nused_ref[0])
    def _():
        e, s = bexp_ref[i], slot_ref[i]
        changed = jnp.logical_or(i == 0, e != bexp_ref[jnp.maximum(i - 1, 0)])

        @pl.when(i == 0)
        def _():
            for cp in weight_copies(e, s):
                cp.start()

        @pl.when(changed)
        def _():
            for cp in weight_copies(e, s):
                cp.wait()
            wg_scr[...] = wg_buf[s].astype(BF16)
            wu_scr[...] = wu_buf[s].astype(BF16)
            wd_scr[...] = wd_buf[s].astype(BF16)

            @pl.when(next_ref[i] < N_EXPERTS)
            def _():
                for cp in weight_copies(next_ref[i], 1 - s):
                    cp.start()

        xb = xs_ref[...].astype(BF16)
        hg = _dot(xb, wg_scr[...])
        hu = _dot(xb, wu_scr[...])
        act = (hg * _sigmoid(hg) * hu).astype(BF16)
        ys_ref[...] = _dot(act, wd_scr[...]).astype(ROW_DTYPE)

    @pl.when(i >= nused_ref[0])
    def _():
        ys_ref[...] = jnp.zeros_like(ys_ref)


def experts(bexp, nused, slot, nxt, xs, w_gate, w_up, w_down, layer, bm):
    n_rows, d = xs.shape
    de = w_gate.shape[3]
    anyspace = pl.BlockSpec(memory_space=pl.ANY)
    return pl.pallas_call(
        functools.partial(_expert_kernel, layer=layer),
        grid_spec=pltpu.PrefetchScalarGridSpec(
            num_scalar_prefetch=4,
            grid=(n_rows // bm,),
            in_specs=[pl.BlockSpec((bm, d), lambda i, bexp, nused, *_: (jnp.minimum(i, nused[0] - 1), 0)),
                      anyspace, anyspace, anyspace],
            out_specs=pl.BlockSpec((bm, d), lambda i, *_: (i, 0)),
            scratch_shapes=[pltpu.VMEM((2, d, de), F32), pltpu.VMEM((2, d, de), F32), pltpu.VMEM((2, de, d), F32),
                            pltpu.VMEM((d, de), BF16), pltpu.VMEM((d, de), BF16), pltpu.VMEM((de, d), BF16),
                            pltpu.SemaphoreType.DMA((2,))]),
        out_shape=jax.ShapeDtypeStruct(xs.shape, ROW_DTYPE),
        compiler_params=_params(1),
        name="experts",
    )(bexp, nused, slot, nxt, xs, w_gate, w_up, w_down)


def _combine_kernel(cnt_ref, lo_ref, dst_ref, tot_ref, x_ref, route_ref, g_ref, *rest, final):
    if final:
        ys_ref, o_ref, buf, sem = rest
    else:
        w_ref, ys_ref, o_ref, xc_ref, buf, sem, xc_scr = rest
    i = pl.program_id(0)
    n = pl.num_programs(0)
    slot = i % 2

    def run_copy(s):
        return lambda lo, dst, cnt: pltpu.make_async_copy(
            ys_ref.at[pl.ds(dst, cnt)], buf.at[s, pl.ds(lo, cnt)], sem.at[s])

    @pl.when(i == 0)
    def _():
        buf[...] = jnp.zeros_like(buf)
        _start_runs(cnt_ref, lo_ref, dst_ref, 0, run_copy(0))

    @pl.when(i + 1 < n)
    def _():
        _start_runs(cnt_ref, lo_ref, dst_ref, i + 1, run_copy(1 - slot))

    _wait_runs(tot_ref, i, run_copy(slot))
    ysb = buf[slot].astype(BF16)
    route = route_ref[...]
    col = lax.broadcasted_iota(I32, (route.shape[0], buf.shape[1]), 1)
    gate_at = jnp.where(col == route[:, 2:3].astype(I32), route[:, 0:1],
                        jnp.where(col == route[:, 3:4].astype(I32), route[:, 1:2], 0.0)).astype(BF16)
    out = x_ref[...] + _dot(gate_at, ysb)
    if final:
        out = _rms(out, g_ref[...])
    else:
        _write_ssm_input(out, g_ref[...], w_ref[...], xc_ref, xc_scr)
    o_ref[...] = out


def combine(cnt, lo, dst, tot, x2d, route, ys, tm, g_final=None, next_mix=None):
    t, d = x2d.shape
    final = next_mix is None
    row = lambda i, *_: (i, 0)
    buf = [pltpu.VMEM((2, _sorted_rows(tm), d), ROW_DTYPE), pltpu.SemaphoreType.DMA((2,))]
    if final:
        extra_in, extra_args = [pl.BlockSpec((1, d), lambda i, *_: (0, 0))], (g_final[None, :],)
        out_specs, out_shape, scratch = pl.BlockSpec((tm, d), row), jax.ShapeDtypeStruct((t, d), F32), buf
    else:
        g_next, w_next, layer, col_block = next_mix
        width = SSM_SLABS * LANE
        extra_in = [pl.BlockSpec((None, 1, d), lambda i, *_: (layer, 0, 0)),
                    pl.BlockSpec((None, d, width), lambda i, *_: (layer, 0, col_block))]
        extra_args = (g_next, w_next)
        out_specs = [pl.BlockSpec((tm, d), row),
                     pl.BlockSpec((SSM_SLABS, tm // SSM_Q, SSM_Q * LANE), lambda i, *_: (0, i, 0))]
        out_shape = [jax.ShapeDtypeStruct((t, d), F32),
                     jax.ShapeDtypeStruct((SSM_SLABS, t // SSM_Q, SSM_Q * LANE), BF16)]
        scratch = buf + [pltpu.VMEM((SSM_SLABS, tm, LANE), F32)]
    return pl.pallas_call(
        functools.partial(_combine_kernel, final=final),
        grid_spec=pltpu.PrefetchScalarGridSpec(
            num_scalar_prefetch=4,
            grid=(t // tm,),
            in_specs=[pl.BlockSpec((tm, d), row), pl.BlockSpec((tm, LANE), row)] + extra_in
            + [pl.BlockSpec(memory_space=pl.ANY)],
            out_specs=out_specs,
            scratch_shapes=scratch),
        out_shape=out_shape,
        compiler_params=_params(1),
        name="combine",
    )(cnt, lo, dst, tot, x2d, route, *extra_args, ys)


def router_params(rg_w, rg_b, re_w, re_b):
    depth = rg_w.shape[0]
    pad = LANE - N_GROUPS - N_EXPERTS
    wr = jnp.concatenate([rg_w, re_w, jnp.zeros((depth, D_MODEL, pad), F32)], axis=2)
    wr_hi = wr.astype(BF16)
    wr = jnp.stack([wr_hi, (wr - wr_hi.astype(F32)).astype(BF16)], axis=1)
    br = jnp.concatenate([rg_b, re_b, jnp.zeros((depth, pad), F32)], axis=1)[:, None, :]
    return wr, br


def moe(x2d, g2, wr, br, w_gate, w_up, w_down, layer, tm, bm, g_final=None, next_mix=None):
    t = x2d.shape[0]
    nt = t // tm
    route, meta = router(x2d, g2, wr, br, layer, tm, 4)
    meta = meta.reshape(nt, SUBLANE, LANE)[:, :, :N_EXPERTS].astype(I32)
    cnt, earlier, lo = meta[:, 0], meta[:, 1], meta[:, 2]
    counts = meta[nt - 1, 3]
    padded = (counts + bm - 1) // bm * bm
    ends = jnp.cumsum(padded)
    offs = ends - padded
    dst = offs[None, :] + earlier
    n_rows = -(-(2 * t + nt * N_EXPERTS * (RUN_ALIGN - 1)) // bm) * bm + N_EXPERTS * bm
    nused = (ends[-1] // bm).astype(I32)
    blk_start = jnp.arange(n_rows // bm, dtype=I32) * bm
    bexp = jnp.sum((ends[None, :] <= blk_start[:, None]).astype(I32), axis=1)
    last = jnp.sum((ends <= (nused - 1) * bm).astype(I32))
    bexp = jnp.minimum(bexp, last)
    tot = jnp.sum(cnt, axis=1)
    cnt, lo, dst = cnt.reshape(-1), lo.reshape(-1), dst.reshape(-1)
    xs = dispatch(cnt, lo, dst, tot, offs + counts, padded - counts, nused[None], x2d, g2, layer, route,
                  n_rows, tm, bm)
    ar = jnp.arange(N_EXPERTS, dtype=I32)
    active = padded > 0
    rank = jnp.cumsum(active.astype(I32)) - active.astype(I32)
    later = (ar[None, :] > ar[:, None]) & active[None, :]
    next_e = jnp.min(jnp.where(later, ar[None, :], N_EXPERTS), axis=1).astype(I32)
    ys = experts(bexp, nused[None], (rank % 2)[bexp], next_e[bexp], xs, w_gate, w_up, w_down, layer, bm)
    return combine(cnt, lo, dst, tot, x2d, route, ys, tm, g_final=g_final, next_mix=next_mix)


def kernel(x, norm_mix_g, w_in, b_gate, pool_w, pool_scale, sgu_ln_g, sgu_ln_b, sgu_w, sgu_b, ssm_a_re, ssm_a_im, ssm_log_dt, ssm_b_re, ssm_b_im, ssm_c_re, ssm_c_im, ssm_d, glu_w, glu_b, w_branch, w_out, norm_ffn_g, router_group_w, router_group_b, router_expert_w, router_expert_b, exp_w_gate, exp_w_up, exp_w_down, norm_final_g):
    batch, seq, d = x.shape
    depth = w_in.shape[0]
    tm_in, tm_mix, tm_moe, bm = 1024, 512, 512, 512
    s5_seqs = 2
    x2d = x.reshape(batch * seq, d)
    ssm_col_block = 3
    row = lambda a: a[:, None, :]
    p = {
        "norm_mix_g": row(norm_mix_g), "w_in": w_in.astype(BF16), "b_gate": row(b_gate),
        "pool_w": pool_w.astype(BF16), "pool_scale": row(pool_scale),
        "sgu_ln_g": row(sgu_ln_g), "sgu_ln_b": row(sgu_ln_b), "sgu_w": sgu_w,
        "sgu_b_full": jnp.repeat(jnp.swapaxes(sgu_b, 1, 2), LANE, axis=2),
        "glu_w": glu_w.astype(BF16), "glu_b": row(glu_b),
        "w_branch": w_branch.astype(BF16), "w_out": w_out.astype(BF16),
    }
    bq, cq, kq, lam_q = jax.vmap(ssm_tables)(ssm_a_re, ssm_a_im, ssm_log_dt, ssm_b_re, ssm_b_im,
                                             ssm_c_re, ssm_c_im, ssm_d)
    wr, br = router_params(router_group_w, router_group_b, router_expert_w, router_expert_b)
    g_ffn = row(norm_ffn_g)
    xc4 = ssm_input(x2d, p["norm_mix_g"], p["w_in"], 0, ssm_col_block, tm_in)
    for l in range(depth):
        ys4 = s5_scan(xc4, bq, cq, kq, lam_q, l, batch, s5_seqs)
        x2d = mixer(x2d, ys4, p, l, batch, tm_mix)
        if l == depth - 1:
            x2d = moe(x2d, g_ffn, wr, br, exp_w_gate, exp_w_up, exp_w_down, l, tm_moe, bm, g_final=norm_final_g)
        else:
            x2d, xc4 = moe(x2d, g_ffn, wr, br, exp_w_gate, exp_w_up, exp_w_down, l, tm_moe, bm,
                           next_mix=(p["norm_mix_g"], p["w_in"], l + 1, ssm_col_block))
    return x2d.reshape(batch, seq, d)
```

```python
import functools
import math

import jax
import jax.numpy as jnp
from jax import lax
from jax.experimental import pallas as pl
from jax.experimental.pallas import tpu as pltpu

F32 = jnp.float32
BF16 = jnp.bfloat16
I32 = jnp.int32

D_MODEL = 1024
POOL_WINDOWS = (2, 4, 8, 16)
POOL_HALO = 16
LANE = 128
SUBLANE = 8
ROW_DTYPE = BF16
RUN_ALIGN = 8 * (4 // jnp.dtype(ROW_DTYPE).itemsize)
SGU_CHUNK = 128
SSM_GROUP_DIM = 16
SSM_STATE = 64
SSM_Q = 8
SSM_SLABS = 4
GROUPS_PER_SLAB = LANE // SSM_GROUP_DIM
N_GROUPS = 4
EXPERTS_PER_GROUP = 8
N_EXPERTS = 32
EPS = 1e-6
VMEM_LIMIT = 56 * 1024 * 1024


def _rms(x, g):
    return x * lax.rsqrt(jnp.mean(x * x, axis=-1, keepdims=True) + EPS) * g


def _gelu(x):
    c = math.sqrt(2.0 / math.pi)
    return x * (0.5 * (1.0 + jnp.tanh(c * (x + 0.044715 * (x * x * x)))))


def _sigmoid(x):
    return 0.5 * (1.0 + jnp.tanh(0.5 * x))


def _dot(a, b):
    return jnp.dot(a, b, preferred_element_type=F32)


def _layer_spec(arr, layer):
    nd = arr.ndim - 1
    return pl.BlockSpec((None,) + arr.shape[1:], lambda *_: (layer,) + (0,) * nd, pipeline_mode=pl.Buffered(1))


def _params(n_axes):
    return pltpu.CompilerParams(dimension_semantics=("arbitrary",) * n_axes, vmem_limit_bytes=VMEM_LIMIT)


def _write_ssm_input(x, g, w, o_ref, xc_scr):
    h = _rms(x, g).astype(BF16)
    xc = _dot(h, w)
    rows = xc_scr.shape[1] // SSM_Q
    for j in range(SSM_SLABS):
        xc_scr[j] = xc[:, j * LANE:(j + 1) * LANE]
        for s in range(SSM_Q):
            o_ref[j, :, s * LANE:(s + 1) * LANE] = xc_scr[j, pl.ds(s, rows, stride=SSM_Q), :].astype(BF16)


def _xc_kernel(x_ref, g_ref, w_ref, o_ref, xc_scr):
    _write_ssm_input(x_ref[...], g_ref[...], w_ref[...], o_ref, xc_scr)


def ssm_input(x2d, g, w_in_bf, layer, col_block, tm):
    t = x2d.shape[0]
    width = SSM_SLABS * LANE
    return pl.pallas_call(
        _xc_kernel,
        grid=(t // tm,),
        in_specs=[pl.BlockSpec((tm, D_MODEL), lambda i: (i, 0)),
                  _layer_spec(g, layer),
                  pl.BlockSpec((None, D_MODEL, width), lambda i: (layer, 0, col_block),
                               pipeline_mode=pl.Buffered(1))],
        out_specs=pl.BlockSpec((SSM_SLABS, tm // SSM_Q, SSM_Q * LANE), lambda i: (0, i, 0)),
        out_shape=jax.ShapeDtypeStruct((SSM_SLABS, t // SSM_Q, SSM_Q * LANE), BF16),
        scratch_shapes=[pltpu.VMEM((SSM_SLABS, tm, LANE), F32)],
        compiler_params=_params(1),
        name="ssm_input",
    )(x2d, g, w_in_bf)


def ssm_tables(a_re, a_im, log_dt, b_re, b_im, c_re, c_im, d_skip):
    q, nc = SSM_Q, SSM_GROUP_DIM
    ng = a_re.shape[0]
    sl = ng // GROUPS_PER_SLAB
    dt = jnp.exp(log_dt)[:, None]

    def powers(kvec):
        kk = kvec[:, None, None]
        mag = jnp.exp(a_re * dt * kk)
        return mag * jnp.cos(a_im * dt * kk), mag * jnp.sin(a_im * dt * kk)

    pw_re, pw_im = powers(jnp.arange(q + 1, dtype=F32))
    nr, ni = pw_re[1] - 1.0, pw_im[1]
    den = a_re * a_re + a_im * a_im
    fr, fi = (nr * a_re + ni * a_im) / den, (ni * a_re - nr * a_im) / den
    bb_re = fr[..., None] * b_re - fi[..., None] * b_im
    bb_im = fr[..., None] * b_im + fi[..., None] * b_re

    def to_slab_rows(a, lead):
        a = a.reshape(lead, sl, GROUPS_PER_SLAB, a.shape[2], a.shape[3]).transpose(1, 0, 2, 3, 4)
        return a.reshape(sl, -1, a.shape[-1]).astype(BF16)

    rev_re, rev_im = powers((q - 1.0) - jnp.arange(q, dtype=F32))
    bt_re, bt_im = bb_re.transpose(0, 2, 1), bb_im.transpose(0, 2, 1)
    m_re = rev_re[:, :, None, :] * bt_re[None] - rev_im[:, :, None, :] * bt_im[None]
    m_im = rev_re[:, :, None, :] * bt_im[None] + rev_im[:, :, None, :] * bt_re[None]
    bq = to_slab_rows(jnp.concatenate([m_re, m_im], axis=-1), q)
    ct_re, ct_im = c_re.transpose(0, 2, 1), c_im.transpose(0, 2, 1)
    ct_re, ct_im = jnp.tile(ct_re, (1, 1, q + 1)), jnp.tile(ct_im, (1, 1, q + 1))
    pl_re = jnp.repeat(pw_re.transpose(1, 2, 0), nc, axis=-1)
    pl_im = jnp.repeat(pw_im.transpose(1, 2, 0), nc, axis=-1)
    r_re = ct_re * pl_re - ct_im * pl_im
    r_im = ct_re * pl_im + ct_im * pl_re
    cq = to_slab_rows(jnp.stack([r_re[..., nc:], -r_im[..., nc:]], axis=0), 2)
    hi = lax.Precision.HIGHEST
    kern = (jnp.einsum('gpd,gpl->gdl', bb_re, r_re[..., :q * nc], precision=hi)
            - jnp.einsum('gpd,gpl->gdl', bb_im, r_im[..., :q * nc], precision=hi))
    lane = jnp.arange(q * nc)[None, None, :]
    d_pad = jnp.pad(d_skip.reshape(ng, nc), ((0, 0), (0, (q - 1) * nc)))
    kern = kern + jnp.where(lane == jnp.arange(nc)[None, :, None], d_pad[:, None, :], 0.0)
    kq = jnp.stack([jnp.pad(kern[..., :(q - s) * nc], ((0, 0), (0, 0), (s * nc, 0))) for s in range(q)], axis=0)
    kq = to_slab_rows(kq, q)
    lam_q = jnp.concatenate([pw_re[q].reshape(sl, 1, -1), pw_im[q].reshape(sl, 1, -1)], axis=-1)
    return bq, cq, kq, lam_q


def _expand_block_diag(small, row_inner, col_inner):
    n_rows, n_small = small.shape
    n_cols = n_small * GROUPS_PER_SLAB
    g_bits = GROUPS_PER_SLAB.bit_length() - 1
    r_shift, c_shift = row_inner.bit_length() - 1, col_inner.bit_length() - 1
    k = lax.broadcasted_iota(I32, (n_small, n_cols), 0)
    col = lax.broadcasted_iota(I32, (n_small, n_cols), 1)
    src = ((col >> (c_shift + g_bits)) << c_shift) | (col & (col_inner - 1))
    spread = jnp.where(k == src, 1.0, 0.0).astype(BF16)
    r = lax.broadcasted_iota(I32, (n_rows, n_cols), 0)
    c = lax.broadcasted_iota(I32, (n_rows, n_cols), 1)
    same = ((r >> r_shift) & (GROUPS_PER_SLAB - 1)) == ((c >> c_shift) & (GROUPS_PER_SLAB - 1))
    return jnp.where(same, _dot(small, spread), 0.0).astype(BF16)


def _s5_kernel(u_ref, bq_ref, cq_ref, kq_ref, lam_ref, y_ref, g_scr, s_scr, t_scr, tbq, tcq, tkq, *, seqs):
    half = GROUPS_PER_SLAB * SSM_STATE
    rows = u_ref.shape[1]

    @pl.when(pl.program_id(1) == 0)
    def _():
        tbq[...] = _expand_block_diag(bq_ref[0], SSM_GROUP_DIM, SSM_STATE)
        tcq[...] = _expand_block_diag(cq_ref[0], SSM_STATE, SSM_GROUP_DIM)
        tkq[...] = _expand_block_diag(kq_ref[0], SSM_GROUP_DIM, SSM_GROUP_DIM)

    u = u_ref[0]
    g_scr[...] = _dot(u, tbq[...])
    lam = lam_ref[0]
    lre, lim = lam[:, :half], lam[:, half:]
    seq_rows = rows // seqs

    def body(n, carry):
        out = []
        for b in range(seqs):
            cre, cim = carry[2 * b], carry[2 * b + 1]
            row = b * seq_rows + n
            s_scr[pl.ds(row, 1), :] = jnp.concatenate([cre, cim], axis=1)
            g = g_scr[pl.ds(row, 1), :]
            out.append(lre * cre - lim * cim + g[:, :half])
            out.append(lre * cim + lim * cre + g[:, half:])
        return tuple(out)

    zero = jnp.zeros((1, half), F32)
    lax.fori_loop(0, seq_rows, body, (zero,) * (2 * seqs), unroll=4)
    y = _dot(s_scr[...].astype(BF16), tcq[...]) + _dot(u, tkq[...])
    for t in range(SSM_Q):
        t_scr[pl.ds(t, rows, stride=SSM_Q), :] = y[:, t * LANE:(t + 1) * LANE]
    y_ref[0] = t_scr[...].astype(BF16)


def s5_scan(u4, bq, cq, kq, lam_q, layer, batch, seqs):
    sl, rows_total, width = u4.shape
    rows = rows_total // batch * seqs
    tab = pl.BlockSpec((None, 1, width, width // GROUPS_PER_SLAB), lambda j, b: (layer, j, 0, 0))
    return pl.pallas_call(
        functools.partial(_s5_kernel, seqs=seqs),
        grid=(sl, batch // seqs),
        in_specs=[pl.BlockSpec((1, rows, width), lambda j, b: (j, b, 0)), tab, tab, tab,
                  pl.BlockSpec((None, 1, 1, width), lambda j, b: (layer, j, 0, 0))],
        out_specs=pl.BlockSpec((1, rows * SSM_Q, LANE), lambda j, b: (j, b, 0)),
        out_shape=jax.ShapeDtypeStruct((sl, rows_total * SSM_Q, LANE), BF16),
        scratch_shapes=[pltpu.VMEM((rows, width), F32), pltpu.VMEM((rows, width), F32),
                        pltpu.VMEM((rows * SSM_Q, LANE), F32)] + [pltpu.VMEM((width, width), BF16)] * 3,
        compiler_params=_params(2),
        name="s5_scan",
    )(u4, bq, cq, kq, lam_q)


def _mixer_kernel(x_ref, ys_ref, ng_ref, win_ref, bgate_ref, poolw_ref, pscale_ref, lng_ref, lnb_ref,
                  sguw_ref, sgub_ref, gluw_ref, glub_ref, wbr_ref, wout_ref, o_ref, xe_scr):
    tm = x_ref.shape[0]
    width = SSM_SLABS * LANE
    a_end, b_end = width, 3 * width
    c_end = b_end + width
    i = pl.program_id(1)
    x = x_ref[...]
    hb = _rms(x, ng_ref[...]).astype(BF16)

    xa = _dot(hb, win_ref[:, 0:a_end])

    @pl.when(i == 0)
    def _():
        xe_scr[0:POOL_HALO, :] = jnp.zeros((POOL_HALO, xe_scr.shape[1]), F32)

    xe_scr[POOL_HALO:POOL_HALO + tm, :] = xa
    pos = i * tm + lax.broadcasted_iota(I32, (tm, 1), 0)
    ya_parts = []
    for gi, w in enumerate(POOL_WINDOWS):
        ch = slice(gi * LANE, (gi + 1) * LANE)
        win = xa[:, ch]
        for j in range(1, w):
            win = win + xe_scr[POOL_HALO - j:POOL_HALO - j + tm, ch]
        cnt = jnp.minimum(pos + 1, w).astype(F32)
        diff = win / cnt - xa[:, ch]
        ya_parts.append(_dot(diff.astype(BF16), poolw_ref[gi]) * pscale_ref[:, ch])
    ya = jnp.concatenate(ya_parts, axis=1)
    xe_scr[0:POOL_HALO, :] = xe_scr[tm:tm + POOL_HALO, :]

    z = _gelu(_dot(hb, win_ref[:, a_end:b_end]))
    u, v = z[:, :width], z[:, width:]
    mu = jnp.mean(v, axis=-1, keepdims=True)
    vc = v - mu
    var = jnp.mean(vc * vc, axis=-1, keepdims=True)
    vb = (vc * lax.rsqrt(var + EPS) * lng_ref[...] + lnb_ref[...]).astype(BF16)
    r_i = lax.broadcasted_iota(I32, (SGU_CHUNK, SGU_CHUNK), 0)
    c_i = lax.broadcasted_iota(I32, (SGU_CHUNK, SGU_CHUNK), 1)
    n_heads = width // LANE
    wt = [jnp.where(r_i >= c_i, sguw_ref[hd], 0.0).astype(BF16) for hd in range(n_heads)]
    s_rows = []
    for c in range(tm // SGU_CHUNK):
        rs = slice(c * SGU_CHUNK, (c + 1) * SGU_CHUNK)
        s_rows.append(jnp.concatenate(
            [_dot(wt[hd], vb[rs, hd * LANE:(hd + 1) * LANE]) for hd in range(n_heads)], axis=1)
            + sgub_ref[...])
    yb = u * jnp.concatenate(s_rows, axis=0)

    yc = _gelu(jnp.concatenate([ys_ref[j] for j in range(SSM_SLABS)], axis=1).astype(F32))
    yc = yc * _sigmoid(_dot(yc.astype(BF16), gluw_ref[...]) + glub_ref[...])

    merged = None
    for k, yk in enumerate((ya, yb, yc)):
        gate = _sigmoid(_dot(hb, win_ref[:, c_end + k * D_MODEL:c_end + (k + 1) * D_MODEL])
                        + bgate_ref[:, k * D_MODEL:(k + 1) * D_MODEL])
        term = gate * _dot(yk.astype(BF16), wbr_ref[k])
        merged = term if merged is None else merged + term
    o_ref[...] = x + _dot(merged.astype(BF16), wout_ref[...])


def mixer(x2d, ys4, p, layer, batch, tm):
    t = x2d.shape[0]
    nt = t // batch // tm
    row = lambda b, i: (b * nt + i, 0)
    slab = lambda b, i: (0, b * nt + i, 0)
    consts = [p["norm_mix_g"], p["w_in"], p["b_gate"], p["pool_w"], p["pool_scale"], p["sgu_ln_g"],
              p["sgu_ln_b"], p["sgu_w"], p["sgu_b_full"], p["glu_w"], p["glu_b"], p["w_branch"], p["w_out"]]
    return pl.pallas_call(
        _mixer_kernel,
        grid=(batch, nt),
        in_specs=[pl.BlockSpec((tm, D_MODEL), row),
                  pl.BlockSpec((SSM_SLABS, tm, LANE), slab)] + [_layer_spec(c, layer) for c in consts],
        out_specs=pl.BlockSpec((tm, D_MODEL), row),
        out_shape=jax.ShapeDtypeStruct(x2d.shape, F32),
        scratch_shapes=[pltpu.VMEM((tm + POOL_HALO, SSM_SLABS * LANE), F32)],
        compiler_params=_params(2),
        name="mixer",
    )(x2d, ys4, *consts)


def _route_tile(x, g, w_hi, w_lo, br):
    tm = x.shape[0]
    h = _rms(x, g)
    h_hi = h.astype(BF16)
    h_lo = (h - h_hi.astype(F32)).astype(BF16)
    logits = (_dot(h_hi, w_hi) + _dot(h_hi, w_lo) + _dot(h_lo, w_hi)) + br
    lane = lax.broadcasted_iota(I32, logits.shape, 1)
    neg = -jnp.inf
    gl = jnp.where(lane < N_GROUPS, logits, neg)
    gmax = jnp.max(gl, axis=-1, keepdims=True)
    gsum = jnp.sum(jnp.where(lane < N_GROUPS, jnp.exp(logits - gmax), 0.0), axis=-1, keepdims=True)
    g_val = 1.0 / gsum
    g_idx = jnp.min(jnp.where(gl == gmax, lane, LANE), axis=-1, keepdims=True)
    lo = N_GROUPS + EXPERTS_PER_GROUP * g_idx
    el = jnp.where((lane >= lo) & (lane < lo + EXPERTS_PER_GROUP), logits, neg)
    e1v = jnp.max(el, axis=-1, keepdims=True)
    e1l = jnp.min(jnp.where(el == e1v, lane, LANE), axis=-1, keepdims=True)
    el2 = jnp.where(lane == e1l, neg, el)
    e2v = jnp.max(el2, axis=-1, keepdims=True)
    e2l = jnp.min(jnp.where(el2 == e2v, lane, LANE), axis=-1, keepdims=True)
    tt = jnp.exp(e2v - e1v)
    w1 = g_val / (1.0 + tt)
    w2 = g_val * tt / (1.0 + tt)
    hit1, hit2 = lane == e1l - N_GROUPS, lane == e2l - N_GROUPS
    onehot = jnp.where(hit1 | hit2, 1.0, 0.0)
    r_i = lax.broadcasted_iota(I32, (tm, tm), 0)
    c_i = lax.broadcasted_iota(I32, (tm, tm), 1)
    tri = jnp.where(r_i > c_i, 1.0, 0.0).astype(BF16)
    prefix = _dot(tri, onehot.astype(BF16))
    count = jnp.sum(onehot, axis=0, keepdims=True)
    groups8 = jnp.floor((count + (RUN_ALIGN - 1.0)) * (1.0 / RUN_ALIGN))
    count = groups8 * RUN_ALIGN
    e_r = lax.broadcasted_iota(I32, (LANE, LANE), 0)
    e_c = lax.broadcasted_iota(I32, (LANE, LANE), 1)
    before = jnp.where(e_r < e_c, 1.0, 0.0).astype(BF16)
    run_start = _dot(jnp.broadcast_to(groups8, (SUBLANE, LANE)).astype(BF16), before)[0:1, :] * RUN_ALIGN
    local = prefix + run_start
    lp1 = jnp.sum(jnp.where(hit1, local, 0.0), axis=-1, keepdims=True)
    lp2 = jnp.sum(jnp.where(hit2, local, 0.0), axis=-1, keepdims=True)
    route = jnp.where(lane == 0, w1, jnp.where(lane == 1, w2, jnp.where(
        lane == 2, lp1, jnp.where(lane == 3, lp2, 0.0))))
    return route, count, run_start


def _router_kernel(x_ref, g_ref, wr_ref, br_ref, route_ref, meta_ref, carry_scr, *, tm):
    @pl.when(pl.program_id(0) == 0)
    def _():
        carry_scr[...] = jnp.zeros_like(carry_scr)

    tiles = [_route_tile(x_ref[k * tm:(k + 1) * tm, :], g_ref[...], wr_ref[0], wr_ref[1], br_ref[...])
             for k in range(x_ref.shape[0] // tm)]
    earlier = carry_scr[0:1, :]
    sub = lax.broadcasted_iota(I32, (SUBLANE, LANE), 0)
    for k, (route, count, run_start) in enumerate(tiles):
        route_ref[k * tm:(k + 1) * tm, :] = route
        total = earlier + count
        meta_ref[k * SUBLANE:(k + 1) * SUBLANE, :] = jnp.where(sub == 0, count, jnp.where(
            sub == 1, earlier, jnp.where(sub == 2, run_start, jnp.where(sub == 3, total, 0.0))))
        earlier = total
    carry_scr[...] = jnp.broadcast_to(earlier, carry_scr.shape)


def router(x2d, g, wr, br, layer, tm, tiles_per_step):
    t = x2d.shape[0]
    rows = tm * tiles_per_step
    row = lambda i: (i, 0)
    return pl.pallas_call(
        functools.partial(_router_kernel, tm=tm),
        grid=(t // rows,),
        in_specs=[pl.BlockSpec((rows, D_MODEL), row), _layer_spec(g, layer),
                  _layer_spec(wr, layer), _layer_spec(br, layer)],
        out_specs=[pl.BlockSpec((rows, LANE), row), pl.BlockSpec((tiles_per_step * SUBLANE, LANE), row)],
        out_shape=[jax.ShapeDtypeStruct((t, LANE), F32),
                   jax.ShapeDtypeStruct((t // tm * SUBLANE, LANE), F32)],
        scratch_shapes=[pltpu.VMEM((SUBLANE, LANE), F32)],
        compiler_params=_params(1),
        name="router",
    )(x2d, g, wr, br)


def _sort_matrix(route, k, n_sorted):
    col = lax.broadcasted_iota(I32, (route.shape[0], n_sorted), 1)
    return jnp.where(col == route[:, 2 + k:3 + k].astype(I32), 1.0, 0.0).astype(BF16)


def _sorted_rows(tm):
    return -(-(2 * tm + N_EXPERTS * (RUN_ALIGN - 1)) // LANE) * LANE


def _start_runs(cnt_ref, lo_ref, dst_ref, tile, make_copy):
    def body(e, c):
        idx = tile * N_EXPERTS + e
        n = cnt_ref[idx]

        @pl.when(n > 0)
        def _():
            make_copy(pl.multiple_of(lo_ref[idx], RUN_ALIGN), pl.multiple_of(dst_ref[idx], RUN_ALIGN),
                      pl.multiple_of(n, RUN_ALIGN)).start()
        return c
    lax.fori_loop(0, N_EXPERTS, body, 0)


def _wait_runs(tot_ref, tile, make_copy):
    make_copy(0, 0, pl.multiple_of(tot_ref[tile], RUN_ALIGN)).wait()


def _dispatch_kernel(cnt_ref, lo_ref, dst_ref, tot_ref, zst_ref, zln_ref, nused_ref, x_ref, g_ref, route_ref,
                     o_ref, srt_scr, zero_scr, sem, zsem):
    tm = x_ref.shape[0]
    bm = zero_scr.shape[0]
    i = pl.program_id(0)
    n = pl.num_programs(0)
    slot = i % 2

    def run_copy(s):
        return lambda lo, dst, cnt: pltpu.make_async_copy(
            srt_scr.at[s, pl.ds(lo, cnt)], o_ref.at[pl.ds(dst, cnt)], sem.at[s])

    def pad_copy(e):
        ln = pl.multiple_of(zln_ref[e], RUN_ALIGN)
        return pltpu.make_async_copy(zero_scr.at[pl.ds(0, ln)],
                                     o_ref.at[pl.ds(pl.multiple_of(zst_ref[e], RUN_ALIGN), ln)], zsem)

    def tail_copy(b):
        return pltpu.make_async_copy(zero_scr, o_ref.at[pl.ds(pl.multiple_of(b * bm, bm), bm)], zsem)

    @pl.when(i == 0)
    def _():
        zero_scr[...] = jnp.zeros_like(zero_scr)
        for wait in (False, True):
            def body(e, c):
                @pl.when(zln_ref[e] > 0)
                def _():
                    cp = pad_copy(e)
                    cp.wait() if wait else cp.start()
                return c
            lax.fori_loop(0, N_EXPERTS, body, 0)

            def tail(b, c):
                cp = tail_copy(b)
                cp.wait() if wait else cp.start()
                return c
            lax.fori_loop(nused_ref[0], o_ref.shape[0] // bm, tail, 0)

    @pl.when(i >= 2)
    def _():
        _wait_runs(tot_ref, i - 2, run_copy(slot))

    hb = _rms(x_ref[...], g_ref[...]).astype(BF16)
    route = route_ref[...]
    ns = srt_scr.shape[1]
    pt = jnp.maximum(_sort_matrix(route, 0, ns), _sort_matrix(route, 1, ns))
    srt_scr[slot] = lax.dot_general(pt, hb, (((0,), (0,)), ((), ())),
                                    preferred_element_type=F32).astype(ROW_DTYPE)
    _start_runs(cnt_ref, lo_ref, dst_ref, i, run_copy(slot))

    @pl.when(i == n - 1)
    def _():
        @pl.when(i >= 1)
        def _():
            _wait_runs(tot_ref, i - 1, run_copy(1 - slot))
        _wait_runs(tot_ref, i, run_copy(slot))


def dispatch(cnt, lo, dst, tot, zst, zln, nused, x2d, g, layer, route, n_rows, tm, bm):
    t, d = x2d.shape
    row = lambda i, *_: (i, 0)
    return pl.pallas_call(
        _dispatch_kernel,
        grid_spec=pltpu.PrefetchScalarGridSpec(
            num_scalar_prefetch=7,
            grid=(t // tm,),
            in_specs=[pl.BlockSpec((tm, d), row), pl.BlockSpec((None, 1, d), lambda i, *_: (layer, 0, 0)),
                      pl.BlockSpec((tm, LANE), row)],
            out_specs=pl.BlockSpec(memory_space=pl.ANY),
            scratch_shapes=[pltpu.VMEM((2, _sorted_rows(tm), d), ROW_DTYPE), pltpu.VMEM((bm, d), ROW_DTYPE),
                            pltpu.SemaphoreType.DMA((2,)), pltpu.SemaphoreType.DMA(())]),
        out_shape=jax.ShapeDtypeStruct((n_rows, d), ROW_DTYPE),
        compiler_params=_params(1),
        name="dispatch",
    )(cnt, lo, dst, tot, zst, zln, nused, x2d, g, route)


def _expert_kernel(bexp_ref, nused_ref, slot_ref, next_ref, xs_ref, wg_hbm, wu_hbm, wd_hbm, ys_ref,
                   wg_buf, wu_buf, wd_buf, wg_scr, wu_scr, wd_scr, sem, *, layer):
    i = pl.program_id(0)

    def weight_copies(e, s):
        return [pltpu.make_async_copy(hbm.at[layer, e], buf.at[s], sem.at[s])
                for hbm, buf in ((wg_hbm, wg_buf), (wu_hbm, wu_buf), (wd_hbm, wd_buf))]

    @pl.when(i < nused_ref[0])
    def _():
        e, s = bexp_ref[i], slot_ref[i]
        changed = jnp.logical_or(i == 0, e != bexp_ref[jnp.maximum(i - 1, 0)])

        @pl.when(i == 0)
        def _():
            for cp in weight_copies(e, s):
                cp.start()

        @pl.when(changed)
        def _():
            for cp in weight_copies(e, s):
                cp.wait()
            wg_scr[...] = wg_buf[s].astype(BF16)
            wu_scr[...] = wu_buf[s].astype(BF16)
            wd_scr[...] = wd_buf[s].astype(BF16)

            @pl.when(next_ref[i] < N_EXPERTS)
            def _():
                for cp in weight_copies(next_ref[i], 1 - s):
                    cp.start()

        xb = xs_ref[...].astype(BF16)
        hg = _dot(xb, wg_scr[...])
        hu = _dot(xb, wu_scr[...])
        act = (hg * _sigmoid(hg) * hu).astype(BF16)
        ys_ref[...] = _dot(act, wd_scr[...]).astype(ROW_DTYPE)

    @pl.when(i >= nused_ref[0])
    def _():
        ys_ref[...] = jnp.zeros_like(ys_ref)


def experts(bexp, nused, slot, nxt, xs, w_gate, w_up, w_down, layer, bm):
    n_rows, d = xs.shape
    de = w_gate.shape[3]
    anyspace = pl.BlockSpec(memory_space=pl.ANY)
    return pl.pallas_call(
        functools.partial(_expert_kernel, layer=layer),
        grid_spec=pltpu.PrefetchScalarGridSpec(
            num_scalar_prefetch=4,
            grid=(n_rows // bm,),
            in_specs=[pl.BlockSpec((bm, d), lambda i, bexp, nused, *_: (jnp.minimum(i, nused[0] - 1), 0)),
                      anyspace, anyspace, anyspace],
            out_specs=pl.BlockSpec((bm, d), lambda i, *_: (i, 0)),
            scratch_shapes=[pltpu.VMEM((2, d, de), F32), pltpu.VMEM((2, d, de), F32), pltpu.VMEM((2, de, d), F32),
                            pltpu.VMEM((d, de), BF16), pltpu.VMEM((d, de), BF16), pltpu.VMEM((de, d), BF16),
                            pltpu.SemaphoreType.DMA((2,))]),
        out_shape=jax.ShapeDtypeStruct(xs.shape, ROW_DTYPE),
        compiler_params=_params(1),
        name="experts",
    )(bexp, nused, slot, nxt, xs, w_gate, w_up, w_down)


def _combine_kernel(cnt_ref, lo_ref, dst_ref, tot_ref, x_ref, route_ref, g_ref, *rest, final):
    if final:
        ys_ref, o_ref, buf, sem = rest
    else:
        w_ref, ys_ref, o_ref, xc_ref, buf, sem, xc_scr = rest
    i = pl.program_id(0)
    n = pl.num_programs(0)
    slot = i % 2

    def run_copy(s):
        return lambda lo, dst, cnt: pltpu.make_async_copy(
            ys_ref.at[pl.ds(dst, cnt)], buf.at[s, pl.ds(lo, cnt)], sem.at[s])

    @pl.when(i == 0)
    def _():
        buf[...] = jnp.zeros_like(buf)
        _start_runs(cnt_ref, lo_ref, dst_ref, 0, run_copy(0))

    @pl.when(i + 1 < n)
    def _():
        _start_runs(cnt_ref, lo_ref, dst_ref, i + 1, run_copy(1 - slot))

    _wait_runs(tot_ref, i, run_copy(slot))
    ysb = buf[slot].astype(BF16)
    route = route_ref[...]
    col = lax.broadcasted_iota(I32, (route.shape[0], buf.shape[1]), 1)
    gate_at = jnp.where(col == route[:, 2:3].astype(I32), route[:, 0:1],
                        jnp.where(col == route[:, 3:4].astype(I32), route[:, 1:2], 0.0)).astype(BF16)
    out = x_ref[...] + _dot(gate_at, ysb)
    if final:
        out = _rms(out, g_ref[...])
    else:
        _write_ssm_input(out, g_ref[...], w_ref[...], xc_ref, xc_scr)
    o_ref[...] = out


def combine(cnt, lo, dst, tot, x2d, route, ys, tm, g_final=None, next_mix=None):
    t, d = x2d.shape
    final = next_mix is None
    row = lambda i, *_: (i, 0)
    buf = [pltpu.VMEM((2, _sorted_rows(tm), d), ROW_DTYPE), pltpu.SemaphoreType.DMA((2,))]
    if final:
        extra_in, extra_args = [pl.BlockSpec((1, d), lambda i, *_: (0, 0))], (g_final[None, :],)
        out_specs, out_shape, scratch = pl.BlockSpec((tm, d), row), jax.ShapeDtypeStruct((t, d), F32), buf
    else:
        g_next, w_next, layer, col_block = next_mix
        width = SSM_SLABS * LANE
        extra_in = [pl.BlockSpec((None, 1, d), lambda i, *_: (layer, 0, 0)),
                    pl.BlockSpec((None, d, width), lambda i, *_: (layer, 0, col_block))]
        extra_args = (g_next, w_next)
        out_specs = [pl.BlockSpec((tm, d), row),
                     pl.BlockSpec((SSM_SLABS, tm // SSM_Q, SSM_Q * LANE), lambda i, *_: (0, i, 0))]
        out_shape = [jax.ShapeDtypeStruct((t, d), F32),
                     jax.ShapeDtypeStruct((SSM_SLABS, t // SSM_Q, SSM_Q * LANE), BF16)]
        scratch = buf + [pltpu.VMEM((SSM_SLABS, tm, LANE), F32)]
    return pl.pallas_call(
        functools.partial(_combine_kernel, final=final),
        grid_spec=pltpu.PrefetchScalarGridSpec(
            num_scalar_prefetch=4,
            grid=(t // tm,),
            in_specs=[pl.BlockSpec((tm, d), row), pl.BlockSpec((tm, LANE), row)] + extra_in
            + [pl.BlockSpec(memory_space=pl.ANY)],
            out_specs=out_specs,
            scratch_shapes=scratch),
        out_shape=out_shape,
        compiler_params=_params(1),
        name="combine",
    )(cnt, lo, dst, tot, x2d, route, *extra_args, ys)


def router_params(rg_w, rg_b, re_w, re_b):
    depth = rg_w.shape[0]
    pad = LANE - N_GROUPS - N_EXPERTS
    wr = jnp.concatenate([rg_w, re_w, jnp.zeros((depth, D_MODEL, pad), F32)], axis=2)
    wr_hi = wr.astype(BF16)
    wr = jnp.stack([wr_hi, (wr - wr_hi.astype(F32)).astype(BF16)], axis=1)
    br = jnp.concatenate([rg_b, re_b, jnp.zeros((depth, pad), F32)], axis=1)[:, None, :]
    return wr, br


def moe(x2d, g2, wr, br, w_gate, w_up, w_down, layer, tm, bm, g_final=None, next_mix=None):
    t = x2d.shape[0]
    nt = t // tm
    route, meta = router(x2d, g2, wr, br, layer, tm, 4)
    meta = meta.reshape(nt, SUBLANE, LANE)[:, :, :N_EXPERTS].astype(I32)
    cnt, earlier, lo = meta[:, 0], meta[:, 1], meta[:, 2]
    counts = meta[nt - 1, 3]
    padded = (counts + bm - 1) // bm * bm
    ends = jnp.cumsum(padded)
    offs = ends - padded
    dst = offs[None, :] + earlier
    n_rows = -(-(2 * t + nt * N_EXPERTS * (RUN_ALIGN - 1)) // bm) * bm + N_EXPERTS * bm
    nused = (ends[-1] // bm).astype(I32)
    blk_start = jnp.arange(n_rows // bm, dtype=I32) * bm
    bexp = jnp.sum((ends[None, :] <= blk_start[:, None]).astype(I32), axis=1)
    last = jnp.sum((ends <= (nused - 1) * bm).astype(I32))
    bexp = jnp.minimum(bexp, last)
    tot = jnp.sum(cnt, axis=1)
    cnt, lo, dst = cnt.reshape(-1), lo.reshape(-1), dst.reshape(-1)
    xs = dispatch(cnt, lo, dst, tot, offs + counts, padded - counts, nused[None], x2d, g2, layer, route,
                  n_rows, tm, bm)
    ar = jnp.arange(N_EXPERTS, dtype=I32)
    active = padded > 0
    rank = jnp.cumsum(active.astype(I32)) - active.astype(I32)
    later = (ar[None, :] > ar[:, None]) & active[None, :]
    next_e = jnp.min(jnp.where(later, ar[None, :], N_EXPERTS), axis=1).astype(I32)
    ys = experts(bexp, nused[None], (rank % 2)[bexp], next_e[bexp], xs, w_gate, w_up, w_down, layer, bm)
    return combine(cnt, lo, dst, tot, x2d, route, ys, tm, g_final=g_final, next_mix=next_mix)


def kernel(x, norm_mix_g, w_in, b_gate, pool_w, pool_scale, sgu_ln_g, sgu_ln_b, sgu_w, sgu_b, ssm_a_re, ssm_a_im, ssm_log_dt, ssm_b_re, ssm_b_im, ssm_c_re, ssm_c_im, ssm_d, glu_w, glu_b, w_branch, w_out, norm_ffn_g, router_group_w, router_group_b, router_expert_w, router_expert_b, exp_w_gate, exp_w_up, exp_w_down, norm_final_g):
    batch, seq, d = x.shape
    depth = w_in.shape[0]
    tm_in, tm_mix, tm_moe, bm = 1024, 512, 512, 512
    s5_seqs = 2
    x2d = x.reshape(batch * seq, d)
    ssm_col_block = 3
    row = lambda a: a[:, None, :]
    p = {
        "norm_mix_g": row(norm_mix_g), "w_in": w_in.astype(BF16), "b_gate": row(b_gate),
        "pool_w": pool_w.astype(BF16), "pool_scale": row(pool_scale),
        "sgu_ln_g": row(sgu_ln_g), "sgu_ln_b": row(sgu_ln_b), "sgu_w": sgu_w,
        "sgu_b_full": jnp.repeat(jnp.swapaxes(sgu_b, 1, 2), LANE, axis=2),
        "glu_w": glu_w.astype(BF16), "glu_b": row(glu_b),
        "w_branch": w_branch.astype(BF16), "w_out": w_out.astype(BF16),
    }
    bq, cq, kq, lam_q = jax.vmap(ssm_tables)(ssm_a_re, ssm_a_im, ssm_log_dt, ssm_b_re, ssm_b_im,
                                             ssm_c_re, ssm_c_im, ssm_d)
    wr, br = router_params(router_group_w, router_group_b, router_expert_w, router_expert_b)
    g_ffn = row(norm_ffn_g)
    xc4 = ssm_input(x2d, p["norm_mix_g"], p["w_in"], 0, ssm_col_block, tm_in)
    for l in range(depth):
        ys4 = s5_scan(xc4, bq, cq, kq, lam_q, l, batch, s5_seqs)
        x2d = mixer(x2d, ys4, p, l, batch, tm_mix)
        if l == depth - 1:
            x2d = moe(x2d, g_ffn, wr, br, exp_w_gate, exp_w_up, exp_w_down, l, tm_moe, bm, g_final=norm_final_g)
        else:
            x2d, xc4 = moe(x2d, g_ffn, wr, br, exp_w_gate, exp_w_up, exp_w_down, l, tm_moe, bm,
                           next_mix=(p["norm_mix_g"], p["w_in"], l + 1, ssm_col_block))
    return x2d.reshape(batch, seq, d)
```

```python
import functools
import math

import jax
import jax.numpy as jnp
from jax import lax
from jax.experimental import pallas as pl
from jax.experimental.pallas import tpu as pltpu

F32 = jnp.float32
BF16 = jnp.bfloat16
I32 = jnp.int32

D_MODEL = 1024
POOL_WINDOWS = (2, 4, 8, 16)
POOL_HALO = 16
LANE = 128
SUBLANE = 8
ROW_DTYPE = BF16
RUN_ALIGN = 8 * (4 // jnp.dtype(ROW_DTYPE).itemsize)
SGU_CHUNK = 128
SSM_GROUP_DIM = 16
SSM_STATE = 64
SSM_Q = 8
SSM_SLABS = 4
GROUPS_PER_SLAB = LANE // SSM_GROUP_DIM
N_GROUPS = 4
EXPERTS_PER_GROUP = 8
N_EXPERTS = 32
EPS = 1e-6
VMEM_LIMIT = 56 * 1024 * 1024


def _rms(x, g):
    return x * lax.rsqrt(jnp.mean(x * x, axis=-1, keepdims=True) + EPS) * g


def _gelu(x):
    c = math.sqrt(2.0 / math.pi)
    return x * (0.5 * (1.0 + jnp.tanh(c * (x + 0.044715 * (x * x * x)))))


def _sigmoid(x):
    return 0.5 * (1.0 + jnp.tanh(0.5 * x))


def _dot(a, b):
    return jnp.dot(a, b, preferred_element_type=F32)


def _layer_spec(arr, layer):
    nd = arr.ndim - 1
    return pl.BlockSpec((None,) + arr.shape[1:], lambda *_: (layer,) + (0,) * nd, pipeline_mode=pl.Buffered(1))


def _params(n_axes):
    return pltpu.CompilerParams(dimension_semantics=("arbitrary",) * n_axes, vmem_limit_bytes=VMEM_LIMIT)


def _write_ssm_input(x, g, w, o_ref, xc_scr):
    h = _rms(x, g).astype(BF16)
    xc = _dot(h, w)
    rows = xc_scr.shape[1] // SSM_Q
    for j in range(SSM_SLABS):
        xc_scr[j] = xc[:, j * LANE:(j + 1) * LANE]
        for s in range(SSM_Q):
            o_ref[j, :, s * LANE:(s + 1) * LANE] = xc_scr[j, pl.ds(s, rows, stride=SSM_Q), :].astype(BF16)


def _xc_kernel(x_ref, g_ref, w_ref, o_ref, xc_scr):
    _write_ssm_input(x_ref[...], g_ref[...], w_ref[...], o_ref, xc_scr)


def ssm_input(x2d, g, w_in_bf, layer, col_block, tm):
    t = x2d.shape[0]
    width = SSM_SLABS * LANE
    return pl.pallas_call(
        _xc_kernel,
        grid=(t // tm,),
        in_specs=[pl.BlockSpec((tm, D_MODEL), lambda i: (i, 0)),
                  _layer_spec(g, layer),
                  pl.BlockSpec((None, D_MODEL, width), lambda i: (layer, 0, col_block),
                               pipeline_mode=pl.Buffered(1))],
        out_specs=pl.BlockSpec((SSM_SLABS, tm // SSM_Q, SSM_Q * LANE), lambda i: (0, i, 0)),
        out_shape=jax.ShapeDtypeStruct((SSM_SLABS, t // SSM_Q, SSM_Q * LANE), BF16),
        scratch_shapes=[pltpu.VMEM((SSM_SLABS, tm, LANE), F32)],
        compiler_params=_params(1),
        name="ssm_input",
    )(x2d, g, w_in_bf)


def ssm_tables(a_re, a_im, log_dt, b_re, b_im, c_re, c_im, d_skip):
    q, nc = SSM_Q, SSM_GROUP_DIM
    ng = a_re.shape[0]
    sl = ng // GROUPS_PER_SLAB
    dt = jnp.exp(log_dt)[:, None]

    def powers(kvec):
        kk = kvec[:, None, None]
        mag = jnp.exp(a_re * dt * kk)
        return mag * jnp.cos(a_im * dt * kk), mag * jnp.sin(a_im * dt * kk)

    pw_re, pw_im = powers(jnp.arange(q + 1, dtype=F32))
    nr, ni = pw_re[1] - 1.0, pw_im[1]
    den = a_re * a_re + a_im * a_im
    fr, fi = (nr * a_re + ni * a_im) / den, (ni * a_re - nr * a_im) / den
    bb_re = fr[..., None] * b_re - fi[..., None] * b_im
    bb_im = fr[..., None] * b_im + fi[..., None] * b_re

    def to_slab_rows(a, lead):
        a = a.reshape(lead, sl, GROUPS_PER_SLAB, a.shape[2], a.shape[3]).transpose(1, 0, 2, 3, 4)
        return a.reshape(sl, -1, a.shape[-1]).astype(BF16)

    rev_re, rev_im = powers((q - 1.0) - jnp.arange(q, dtype=F32))
    bt_re, bt_im = bb_re.transpose(0, 2, 1), bb_im.transpose(0, 2, 1)
    m_re = rev_re[:, :, None, :] * bt_re[None] - rev_im[:, :, None, :] * bt_im[None]
    m_im = rev_re[:, :, None, :] * bt_im[None] + rev_im[:, :, None, :] * bt_re[None]
    bq = to_slab_rows(jnp.concatenate([m_re, m_im], axis=-1), q)
    ct_re, ct_im = c_re.transpose(0, 2, 1), c_im.transpose(0, 2, 1)
    ct_re, ct_im = jnp.tile(ct_re, (1, 1, q + 1)), jnp.tile(ct_im, (1, 1, q + 1))
    pl_re = jnp.repeat(pw_re.transpose(1, 2, 0), nc, axis=-1)
    pl_im = jnp.repeat(pw_im.transpose(1, 2, 0), nc, axis=-1)
    r_re = ct_re * pl_re - ct_im * pl_im
    r_im = ct_re * pl_im + ct_im * pl_re
    cq = to_slab_rows(jnp.stack([r_re[..., nc:], -r_im[..., nc:]], axis=0), 2)
    hi = lax.Precision.HIGHEST
    kern = (jnp.einsum('gpd,gpl->gdl', bb_re, r_re[..., :q * nc], precision=hi)
            - jnp.einsum('gpd,gpl->gdl', bb_im, r_im[..., :q * nc], precision=hi))
    lane = jnp.arange(q * nc)[None, None, :]
    d_pad = jnp.pad(d_skip.reshape(ng, nc), ((0, 0), (0, (q - 1) * nc)))
    kern = kern + jnp.where(lane == jnp.arange(nc)[None, :, None], d_pad[:, None, :], 0.0)
    kq = jnp.stack([jnp.pad(kern[..., :(q - s) * nc], ((0, 0), (0, 0), (s * nc, 0))) for s in range(q)], axis=0)
    kq = to_slab_rows(kq, q)
    lam_q = jnp.concatenate([pw_re[q].reshape(sl, 1, -1), pw_im[q].reshape(sl, 1, -1)], axis=-1)
    return bq, cq, kq, lam_q


def _expand_block_diag(small, row_inner, col_inner):
    n_rows, n_small = small.shape
    n_cols = n_small * GROUPS_PER_SLAB
    g_bits = GROUPS_PER_SLAB.bit_length() - 1
    r_shift, c_shift = row_inner.bit_length() - 1, col_inner.bit_length() - 1
    k = lax.broadcasted_iota(I32, (n_small, n_cols), 0)
    col = lax.broadcasted_iota(I32, (n_small, n_cols), 1)
    src = ((col >> (c_shift + g_bits)) << c_shift) | (col & (col_inner - 1))
    spread = jnp.where(k == src, 1.0, 0.0).astype(BF16)
    r = lax.broadcasted_iota(I32, (n_rows, n_cols), 0)
    c = lax.broadcasted_iota(I32, (n_rows, n_cols), 1)
    same = ((r >> r_shift) & (GROUPS_PER_SLAB - 1)) == ((c >> c_shift) & (GROUPS_PER_SLAB - 1))
    return jnp.where(same, _dot(small, spread), 0.0).astype(BF16)


def _s5_kernel(u_ref, bq_ref, cq_ref, kq_ref, lam_ref, y_ref, g_scr, s_scr, t_scr, tbq, tcq, tkq, *, seqs):
    half = GROUPS_PER_SLAB * SSM_STATE
    rows = u_ref.shape[1]

    @pl.when(pl.program_id(1) == 0)
    def _():
        tbq[...] = _expand_block_diag(bq_ref[0], SSM_GROUP_DIM, SSM_STATE)
        tcq[...] = _expand_block_diag(cq_ref[0], SSM_STATE, SSM_GROUP_DIM)
        tkq[...] = _expand_block_diag(kq_ref[0], SSM_GROUP_DIM, SSM_GROUP_DIM)

    u = u_ref[0]
    g_scr[...] = _dot(u, tbq[...])
    lam = lam_ref[0]
    lre, lim = lam[:, :half], lam[:, half:]
    seq_rows = rows // seqs

    def body(n, carry):
        out = []
        for b in range(seqs):
            cre, cim = carry[2 * b], carry[2 * b + 1]
            row = b * seq_rows + n
            s_scr[pl.ds(row, 1), :] = jnp.concatenate([cre, cim], axis=1)
            g = g_scr[pl.ds(row, 1), :]
            out.append(lre * cre - lim * cim + g[:, :half])
            out.append(lre * cim + lim * cre + g[:, half:])
        return tuple(out)

    zero = jnp.zeros((1, half), F32)
    lax.fori_loop(0, seq_rows, body, (zero,) * (2 * seqs), unroll=4)
    y = _dot(s_scr[...].astype(BF16), tcq[...]) + _dot(u, tkq[...])
    for t in range(SSM_Q):
        t_scr[pl.ds(t, rows, stride=SSM_Q), :] = y[:, t * LANE:(t + 1) * LANE]
    y_ref[0] = t_scr[...].astype(BF16)


def s5_scan(u4, bq, cq, kq, lam_q, layer, batch, seqs):
    sl, rows_total, width = u4.shape
    rows = rows_total // batch * seqs
    tab = pl.BlockSpec((None, 1, width, width // GROUPS_PER_SLAB), lambda j, b: (layer, j, 0, 0))
    return pl.pallas_call(
        functools.partial(_s5_kernel, seqs=seqs),
        grid=(sl, batch // seqs),
        in_specs=[pl.BlockSpec((1, rows, width), lambda j, b: (j, b, 0)), tab, tab, tab,
                  pl.BlockSpec((None, 1, 1, width), lambda j, b: (layer, j, 0, 0))],
        out_specs=pl.BlockSpec((1, rows * SSM_Q, LANE), lambda j, b: (j, b, 0)),
        out_shape=jax.ShapeDtypeStruct((sl, rows_total * SSM_Q, LANE), BF16),
        scratch_shapes=[pltpu.VMEM((rows, width), F32), pltpu.VMEM((rows, width), F32),
                        pltpu.VMEM((rows * SSM_Q, LANE), F32)] + [pltpu.VMEM((width, width), BF16)] * 3,
        compiler_params=_params(2),
        name="s5_scan",
    )(u4, bq, cq, kq, lam_q)


def _mixer_kernel(x_ref, ys_ref, ng_ref, win_ref, bgate_ref, poolw_ref, pscale_ref, lng_ref, lnb_ref,
                  sguw_ref, sgub_ref, gluw_ref, glub_ref, wbr_ref, wout_ref, o_ref, xe_scr):
    tm = x_ref.shape[0]
    width = SSM_SLABS * LANE
    a_end, b_end = width, 3 * width
    c_end = b_end + width
    i = pl.program_id(1)
    x = x_ref[...]
    hb = _rms(x, ng_ref[...]).astype(BF16)

    xa = _dot(hb, win_ref[:, 0:a_end])

    @pl.when(i == 0)
    def _():
        xe_scr[0:POOL_HALO, :] = jnp.zeros((POOL_HALO, xe_scr.shape[1]), F32)

    xe_scr[POOL_HALO:POOL_HALO + tm, :] = xa
    pos = i * tm + lax.broadcasted_iota(I32, (tm, 1), 0)
    ya_parts = []
    for gi, w in enumerate(POOL_WINDOWS):
        ch = slice(gi * LANE, (gi + 1) * LANE)
        win = xa[:, ch]
        for j in range(1, w):
            win = win + xe_scr[POOL_HALO - j:POOL_HALO - j + tm, ch]
        cnt = jnp.minimum(pos + 1, w).astype(F32)
        diff = win / cnt - xa[:, ch]
        ya_parts.append(_dot(diff.astype(BF16), poolw_ref[gi]) * pscale_ref[:, ch])
    ya = jnp.concatenate(ya_parts, axis=1)
    xe_scr[0:POOL_HALO, :] = xe_scr[tm:tm + POOL_HALO, :]

    z = _gelu(_dot(hb, win_ref[:, a_end:b_end]))
    u, v = z[:, :width], z[:, width:]
    mu = jnp.mean(v, axis=-1, keepdims=True)
    vc = v - mu
    var = jnp.mean(vc * vc, axis=-1, keepdims=True)
    vb = (vc * lax.rsqrt(var + EPS) * lng_ref[...] + lnb_ref[...]).astype(BF16)
    r_i = lax.broadcasted_iota(I32, (SGU_CHUNK, SGU_CHUNK), 0)
    c_i = lax.broadcasted_iota(I32, (SGU_CHUNK, SGU_CHUNK), 1)
    n_heads = width // LANE
    wt = [jnp.where(r_i >= c_i, sguw_ref[hd], 0.0).astype(BF16) for hd in range(n_heads)]
    s_rows = []
    for c in range(tm // SGU_CHUNK):
        rs = slice(c * SGU_CHUNK, (c + 1) * SGU_CHUNK)
        s_rows.append(jnp.concatenate(
            [_dot(wt[hd], vb[rs, hd * LANE:(hd + 1) * LANE]) for hd in range(n_heads)], axis=1)
            + sgub_ref[...])
    yb = u * jnp.concatenate(s_rows, axis=0)

    yc = _gelu(jnp.concatenate([ys_ref[j] for j in range(SSM_SLABS)], axis=1).astype(F32))
    yc = yc * _sigmoid(_dot(yc.astype(BF16), gluw_ref[...]) + glub_ref[...])

    merged = None
    for k, yk in enumerate((ya, yb, yc)):
        gate = _sigmoid(_dot(hb, win_ref[:, c_end + k * D_MODEL:c_end + (k + 1) * D_MODEL])
                        + bgate_ref[:, k * D_MODEL:(k + 1) * D_MODEL])
        term = gate * _dot(yk.astype(BF16), wbr_ref[k])
        merged = term if merged is None else merged + term
    o_ref[...] = x + _dot(merged.astype(BF16), wout_ref[...])


def mixer(x2d, ys4, p, layer, batch, tm):
    t = x2d.shape[0]
    nt = t // batch // tm
    row = lambda b, i: (b * nt + i, 0)
    slab = lambda b, i: (0, b * nt + i, 0)
    consts = [p["norm_mix_g"], p["w_in"], p["b_gate"], p["pool_w"], p["pool_scale"], p["sgu_ln_g"],
              p["sgu_ln_b"], p["sgu_w"], p["sgu_b_full"], p["glu_w"], p["glu_b"], p["w_branch"], p["w_out"]]
    return pl.pallas_call(
        _mixer_kernel,
        grid=(batch, nt),
        in_specs=[pl.BlockSpec((tm, D_MODEL), row),
                  pl.BlockSpec((SSM_SLABS, tm, LANE), slab)] + [_layer_spec(c, layer) for c in consts],
        out_specs=pl.BlockSpec((tm, D_MODEL), row),
        out_shape=jax.ShapeDtypeStruct(x2d.shape, F32),
        scratch_shapes=[pltpu.VMEM((tm + POOL_HALO, SSM_SLABS * LANE), F32)],
        compiler_params=_params(2),
        name="mixer",
    )(x2d, ys4, *consts)


def _route_tile(x, g, w_hi, w_lo, br):
    tm = x.shape[0]
    h = _rms(x, g)
    h_hi = h.astype(BF16)
    h_lo = (h - h_hi.astype(F32)).astype(BF16)
    logits = (_dot(h_hi, w_hi) + _dot(h_hi, w_lo) + _dot(h_lo, w_hi)) + br
    lane = lax.broadcasted_iota(I32, logits.shape, 1)
    neg = -jnp.inf
    gl = jnp.where(lane < N_GROUPS, logits, neg)
    gmax = jnp.max(gl, axis=-1, keepdims=True)
    gsum = jnp.sum(jnp.where(lane < N_GROUPS, jnp.exp(logits - gmax), 0.0), axis=-1, keepdims=True)
    g_val = 1.0 / gsum
    g_idx = jnp.min(jnp.where(gl == gmax, lane, LANE), axis=-1, keepdims=True)
    lo = N_GROUPS + EXPERTS_PER_GROUP * g_idx
    el = jnp.where((lane >= lo) & (lane < lo + EXPERTS_PER_GROUP), logits, neg)
    e1v = jnp.max(el, axis=-1, keepdims=True)
    e1l = jnp.min(jnp.where(el == e1v, lane, LANE), axis=-1, keepdims=True)
    el2 = jnp.where(lane == e1l, neg, el)
    e2v = jnp.max(el2, axis=-1, keepdims=True)
    e2l = jnp.min(jnp.where(el2 == e2v, lane, LANE), axis=-1, keepdims=True)
    tt = jnp.exp(e2v - e1v)
    w1 = g_val / (1.0 + tt)
    w2 = g_val * tt / (1.0 + tt)
    hit1, hit2 = lane == e1l - N_GROUPS, lane == e2l - N_GROUPS
    onehot = jnp.where(hit1 | hit2, 1.0, 0.0)
    r_i = lax.broadcasted_iota(I32, (tm, tm), 0)
    c_i = lax.broadcasted_iota(I32, (tm, tm), 1)
    tri = jnp.where(r_i > c_i, 1.0, 0.0).astype(BF16)
    prefix = _dot(tri, onehot.astype(BF16))
    count = jnp.sum(onehot, axis=0, keepdims=True)
    groups8 = jnp.floor((count + (RUN_ALIGN - 1.0)) * (1.0 / RUN_ALIGN))
    count = groups8 * RUN_ALIGN
    e_r = lax.broadcasted_iota(I32, (LANE, LANE), 0)
    e_c = lax.broadcasted_iota(I32, (LANE, LANE), 1)
    before = jnp.where(e_r < e_c, 1.0, 0.0).astype(BF16)
    run_start = _dot(jnp.broadcast_to(groups8, (SUBLANE, LANE)).astype(BF16), before)[0:1, :] * RUN_ALIGN
    local = prefix + run_start
    lp1 = jnp.sum(jnp.where(hit1, local, 0.0), axis=-1, keepdims=True)
    lp2 = jnp.sum(jnp.where(hit2, local, 0.0), axis=-1, keepdims=True)
    route = jnp.where(lane == 0, w1, jnp.where(lane == 1, w2, jnp.where(
        lane == 2, lp1, jnp.where(lane == 3, lp2, 0.0))))
    return route, count, run_start


def _router_kernel(x_ref, g_ref, wr_ref, br_ref, route_ref, meta_ref, carry_scr, *, tm):
    @pl.when(pl.program_id(0) == 0)
    def _():
        carry_scr[...] = jnp.zeros_like(carry_scr)

    tiles = [_route_tile(x_ref[k * tm:(k + 1) * tm, :], g_ref[...], wr_ref[0], wr_ref[1], br_ref[...])
             for k in range(x_ref.shape[0] // tm)]
    earlier = carry_scr[0:1, :]
    sub = lax.broadcasted_iota(I32, (SUBLANE, LANE), 0)
    for k, (route, count, run_start) in enumerate(tiles):
        route_ref[k * tm:(k + 1) * tm, :] = route
        total = earlier + count
        meta_ref[k * SUBLANE:(k + 1) * SUBLANE, :] = jnp.where(sub == 0, count, jnp.where(
            sub == 1, earlier, jnp.where(sub == 2, run_start, jnp.where(sub == 3, total, 0.0))))
        earlier = total
    carry_scr[...] = jnp.broadcast_to(earlier, carry_scr.shape)


def router(x2d, g, wr, br, layer, tm, tiles_per_step):
    t = x2d.shape[0]
    rows = tm * tiles_per_step
    row = lambda i: (i, 0)
    return pl.pallas_call(
        functools.partial(_router_kernel, tm=tm),
        grid=(t // rows,),
        in_specs=[pl.BlockSpec((rows, D_MODEL), row), _layer_spec(g, layer),
                  _layer_spec(wr, layer), _layer_spec(br, layer)],
        out_specs=[pl.BlockSpec((rows, LANE), row), pl.BlockSpec((tiles_per_step * SUBLANE, LANE), row)],
        out_shape=[jax.ShapeDtypeStruct((t, LANE), F32),
                   jax.ShapeDtypeStruct((t // tm * SUBLANE, LANE), F32)],
        scratch_shapes=[pltpu.VMEM((SUBLANE, LANE), F32)],
        compiler_params=_params(1),
        name="router",
    )(x2d, g, wr, br)


def _sort_matrix(route, k, n_sorted):
    col = lax.broadcasted_iota(I32, (route.shape[0], n_sorted), 1)
    return jnp.where(col == route[:, 2 + k:3 + k].astype(I32), 1.0, 0.0).astype(BF16)


def _sorted_rows(tm):
    return -(-(2 * tm + N_EXPERTS * (RUN_ALIGN - 1)) // LANE) * LANE


def _start_runs(cnt_ref, lo_ref, dst_ref, tile, make_copy):
    def body(e, c):
        idx = tile * N_EXPERTS + e
        n = cnt_ref[idx]

        @pl.when(n > 0)
        def _():
            make_copy(pl.multiple_of(lo_ref[idx], RUN_ALIGN), pl.multiple_of(dst_ref[idx], RUN_ALIGN),
                      pl.multiple_of(n, RUN_ALIGN)).start()
        return c
    lax.fori_loop(0, N_EXPERTS, body, 0)


def _wait_runs(tot_ref, tile, make_copy):
    make_copy(0, 0, pl.multiple_of(tot_ref[tile], RUN_ALIGN)).wait()


def _dispatch_kernel(cnt_ref, lo_ref, dst_ref, tot_ref, zst_ref, zln_ref, nused_ref, x_ref, g_ref, route_ref,
                     o_ref, srt_scr, zero_scr, sem, zsem):
    tm = x_ref.shape[0]
    bm = zero_scr.shape[0]
    i = pl.program_id(0)
    n = pl.num_programs(0)
    slot = i % 2

    def run_copy(s):
        return lambda lo, dst, cnt: pltpu.make_async_copy(
            srt_scr.at[s, pl.ds(lo, cnt)], o_ref.at[pl.ds(dst, cnt)], sem.at[s])

    def pad_copy(e):
        ln = pl.multiple_of(zln_ref[e], RUN_ALIGN)
        return pltpu.make_async_copy(zero_scr.at[pl.ds(0, ln)],
                                     o_ref.at[pl.ds(pl.multiple_of(zst_ref[e], RUN_ALIGN), ln)], zsem)

    def tail_copy(b):
        return pltpu.make_async_copy(zero_scr, o_ref.at[pl.ds(pl.multiple_of(b * bm, bm), bm)], zsem)

    @pl.when(i == 0)
    def _():
        zero_scr[...] = jnp.zeros_like(zero_scr)
        for wait in (False, True):
            def body(e, c):
                @pl.when(zln_ref[e] > 0)
                def _():
                    cp = pad_copy(e)
                    cp.wait() if wait else cp.start()
                return c
            lax.fori_loop(0, N_EXPERTS, body, 0)

            def tail(b, c):
                cp = tail_copy(b)
                cp.wait() if wait else cp.start()
                return c
            lax.fori_loop(nused_ref[0], o_ref.shape[0] // bm, tail, 0)

    @pl.when(i >= 2)
    def _():
        _wait_runs(tot_ref, i - 2, run_copy(slot))

    hb = _rms(x_ref[...], g_ref[...]).astype(BF16)
    route = route_ref[...]
    ns = srt_scr.shape[1]
    pt = jnp.maximum(_sort_matrix(route, 0, ns), _sort_matrix(route, 1, ns))
    srt_scr[slot] = lax.dot_general(pt, hb, (((0,), (0,)), ((), ())),
                                    preferred_element_type=F32).astype(ROW_DTYPE)
    _start_runs(cnt_ref, lo_ref, dst_ref, i, run_copy(slot))

    @pl.when(i == n - 1)
    def _():
        @pl.when(i >= 1)
        def _():
            _wait_runs(tot_ref, i - 1, run_copy(1 - slot))
        _wait_runs(tot_ref, i, run_copy(slot))


def dispatch(cnt, lo, dst, tot, zst, zln, nused, x2d, g, layer, route, n_rows, tm, bm):
    t, d = x2d.shape
    row = lambda i, *_: (i, 0)
    return pl.pallas_call(
        _dispatch_kernel,
        grid_spec=pltpu.PrefetchScalarGridSpec(
            num_scalar_prefetch=7,
            grid=(t // tm,),
            in_specs=[pl.BlockSpec((tm, d), row), pl.BlockSpec((None, 1, d), lambda i, *_: (layer, 0, 0)),
                      pl.BlockSpec((tm, LANE), row)],
            out_specs=pl.BlockSpec(memory_space=pl.ANY),
            scratch_shapes=[pltpu.VMEM((2, _sorted_rows(tm), d), ROW_DTYPE), pltpu.VMEM((bm, d), ROW_DTYPE),
                            pltpu.SemaphoreType.DMA((2,)), pltpu.SemaphoreType.DMA(())]),
        out_shape=jax.ShapeDtypeStruct((n_rows, d), ROW_DTYPE),
        compiler_params=_params(1),
        name="dispatch",
    )(cnt, lo, dst, tot, zst, zln, nused, x2d, g, route)


def _expert_kernel(bexp_ref, nused_ref, slot_ref, next_ref, xs_ref, wg_hbm, wu_hbm, wd_hbm, ys_ref,
                   wg_buf, wu_buf, wd_buf, wg_scr, wu_scr, wd_scr, sem, *, layer):
    i = pl.program_id(0)

    def weight_copies(e, s):
        return [pltpu.make_async_copy(hbm.at[layer, e], buf.at[s], sem.at[s])
                for hbm, buf in ((wg_hbm, wg_buf), (wu_hbm, wu_buf), (wd_hbm, wd_buf))]

    @pl.when(i < nused_ref[0])
    def _():
        e, s = bexp_ref[i], slot_ref[i]
        changed = jnp.logical_or(i == 0, e != bexp_ref[jnp.maximum(i - 1, 0)])

        @pl.when(i == 0)
        def _():
            for cp in weight_copies(e, s):
                cp.start()

        @pl.when(changed)
        def _():
            for cp in weight_copies(e, s):
                cp.wait()
            wg_scr[...] = wg_buf[s].astype(BF16)
            wu_scr[...] = wu_buf[s].astype(BF16)
            wd_scr[...] = wd_buf[s].astype(BF16)

            @pl.when(next_ref[i] < N_EXPERTS)
            def _():
                for cp in weight_copies(next_ref[i], 1 - s):
                    cp.start()

        xb = xs_ref[...].astype(BF16)
        half = wg_scr.shape[1] // 2
        y = None
        for c in (slice(0, half), slice(half, 2 * half)):
            hg = _dot(xb, wg_scr[:, c])
            hu = _dot(xb, wu_scr[:, c])
            act = (hg * _sigmoid(hg) * hu).astype(BF16)
            part = _dot(act, wd_scr[c, :])
            y = part if y is None else y + part
        ys_ref[...] = y.astype(ROW_DTYPE)

    @pl.when(i >= nused_ref[0])
    def _():
        ys_ref[...] = jnp.zeros_like(ys_ref)


def experts(bexp, nused, slot, nxt, xs, w_gate, w_up, w_down, layer, bm):
    n_rows, d = xs.shape
    de = w_gate.shape[3]
    anyspace = pl.BlockSpec(memory_space=pl.ANY)
    return pl.pallas_call(
        functools.partial(_expert_kernel, layer=layer),
        grid_spec=pltpu.PrefetchScalarGridSpec(
            num_scalar_prefetch=4,
            grid=(n_rows // bm,),
            in_specs=[pl.BlockSpec((bm, d), lambda i, bexp, nused, *_: (jnp.minimum(i, nused[0] - 1), 0)),
                      anyspace, anyspace, anyspace],
            out_specs=pl.BlockSpec((bm, d), lambda i, *_: (i, 0)),
            scratch_shapes=[pltpu.VMEM((2, d, de), F32), pltpu.VMEM((2, d, de), F32), pltpu.VMEM((2, de, d), F32),
                            pltpu.VMEM((d, de), BF16), pltpu.VMEM((d, de), BF16), pltpu.VMEM((de, d), BF16),
                            pltpu.SemaphoreType.DMA((2,))]),
        out_shape=jax.ShapeDtypeStruct(xs.shape, ROW_DTYPE),
        compiler_params=_params(1),
        name="experts",
    )(bexp, nused, slot, nxt, xs, w_gate, w_up, w_down)


def _combine_kernel(cnt_ref, lo_ref, dst_ref, tot_ref, x_ref, route_ref, g_ref, *rest, final):
    if final:
        ys_ref, o_ref, buf, sem = rest
    else:
        w_ref, ys_ref, o_ref, xc_ref, buf, sem, xc_scr = rest
    i = pl.program_id(0)
    n = pl.num_programs(0)
    slot = i % 2

    def run_copy(s):
        return lambda lo, dst, cnt: pltpu.make_async_copy(
            ys_ref.at[pl.ds(dst, cnt)], buf.at[s, pl.ds(lo, cnt)], sem.at[s])

    @pl.when(i == 0)
    def _():
        buf[...] = jnp.zeros_like(buf)
        _start_runs(cnt_ref, lo_ref, dst_ref, 0, run_copy(0))

    @pl.when(i + 1 < n)
    def _():
        _start_runs(cnt_ref, lo_ref, dst_ref, i + 1, run_copy(1 - slot))

    _wait_runs(tot_ref, i, run_copy(slot))
    ysb = buf[slot].astype(BF16)
    route = route_ref[...]
    col = lax.broadcasted_iota(I32, (route.shape[0], buf.shape[1]), 1)
    gate_at = jnp.where(col == route[:, 2:3].astype(I32), route[:, 0:1],
                        jnp.where(col == route[:, 3:4].astype(I32), route[:, 1:2], 0.0)).astype(BF16)
    out = x_ref[...] + _dot(gate_at, ysb)
    if final:
        out = _rms(out, g_ref[...])
    else:
        _write_ssm_input(out, g_ref[...], w_ref[...], xc_ref, xc_scr)
    o_ref[...] = out


def combine(cnt, lo, dst, tot, x2d, route, ys, tm, g_final=None, next_mix=None):
    t, d = x2d.shape
    final = next_mix is None
    row = lambda i, *_: (i, 0)
    buf = [pltpu.VMEM((2, _sorted_rows(tm), d), ROW_DTYPE), pltpu.SemaphoreType.DMA((2,))]
    if final:
        extra_in, extra_args = [pl.BlockSpec((1, d), lambda i, *_: (0, 0))], (g_final[None, :],)
        out_specs, out_shape, scratch = pl.BlockSpec((tm, d), row), jax.ShapeDtypeStruct((t, d), F32), buf
    else:
        g_next, w_next, layer, col_block = next_mix
        width = SSM_SLABS * LANE
        extra_in = [pl.BlockSpec((None, 1, d), lambda i, *_: (layer, 0, 0)),
                    pl.BlockSpec((None, d, width), lambda i, *_: (layer, 0, col_block))]
        extra_args = (g_next, w_next)
        out_specs = [pl.BlockSpec((tm, d), row),
                     pl.BlockSpec((SSM_SLABS, tm // SSM_Q, SSM_Q * LANE), lambda i, *_: (0, i, 0))]
        out_shape = [jax.ShapeDtypeStruct((t, d), F32),
                     jax.ShapeDtypeStruct((SSM_SLABS, t // SSM_Q, SSM_Q * LANE), BF16)]
        scratch = buf + [pltpu.VMEM((SSM_SLABS, tm, LANE), F32)]
    return pl.pallas_call(
        functools.partial(_combine_kernel, final=final),
        grid_spec=pltpu.PrefetchScalarGridSpec(
            num_scalar_prefetch=4,
            grid=(t // tm,),
            in_specs=[pl.BlockSpec((tm, d), row), pl.BlockSpec((tm, LANE), row)] + extra_in
            + [pl.BlockSpec(memory_space=pl.ANY)],
            out_specs=out_specs,
            scratch_shapes=scratch),
        out_shape=out_shape,
        compiler_params=_params(1),
        name="combine",
    )(cnt, lo, dst, tot, x2d, route, *extra_args, ys)


def router_params(rg_w, rg_b, re_w, re_b):
    depth = rg_w.shape[0]
    pad = LANE - N_GROUPS - N_EXPERTS
    wr = jnp.concatenate([rg_w, re_w, jnp.zeros((depth, D_MODEL, pad), F32)], axis=2)
    wr_hi = wr.astype(BF16)
    wr = jnp.stack([wr_hi, (wr - wr_hi.astype(F32)).astype(BF16)], axis=1)
    br = jnp.concatenate([rg_b, re_b, jnp.zeros((depth, pad), F32)], axis=1)[:, None, :]
    return wr, br


def moe(x2d, g2, wr, br, w_gate, w_up, w_down, layer, tm, bm, g_final=None, next_mix=None):
    t = x2d.shape[0]
    nt = t // tm
    route, meta = router(x2d, g2, wr, br, layer, tm, 4)
    meta = meta.reshape(nt, SUBLANE, LANE)[:, :, :N_EXPERTS].astype(I32)
    cnt, earlier, lo = meta[:, 0], meta[:, 1], meta[:, 2]
    counts = meta[nt - 1, 3]
    padded = (counts + bm - 1) // bm * bm
    ends = jnp.cumsum(padded)
    offs = ends - padded
    dst = offs[None, :] + earlier
    n_rows = -(-(2 * t + nt * N_EXPERTS * (RUN_ALIGN - 1)) // bm) * bm + N_EXPERTS * bm
    nused = (ends[-1] // bm).astype(I32)
    blk_start = jnp.arange(n_rows // bm, dtype=I32) * bm
    bexp = jnp.sum((ends[None, :] <= blk_start[:, None]).astype(I32), axis=1)
    last = jnp.sum((ends <= (nused - 1) * bm).astype(I32))
    bexp = jnp.minimum(bexp, last)
    tot = jnp.sum(cnt, axis=1)
    cnt, lo, dst = cnt.reshape(-1), lo.reshape(-1), dst.reshape(-1)
    xs = dispatch(cnt, lo, dst, tot, offs + counts, padded - counts, nused[None], x2d, g2, layer, route,
                  n_rows, tm, bm)
    ar = jnp.arange(N_EXPERTS, dtype=I32)
    active = padded > 0
    rank = jnp.cumsum(active.astype(I32)) - active.astype(I32)
    later = (ar[None, :] > ar[:, None]) & active[None, :]
    next_e = jnp.min(jnp.where(later, ar[None, :], N_EXPERTS), axis=1).astype(I32)
    ys = experts(bexp, nused[None], (rank % 2)[bexp], next_e[bexp], xs, w_gate, w_up, w_down, layer, bm)
    return combine(cnt, lo, dst, tot, x2d, route, ys, tm, g_final=g_final, next_mix=next_mix)


def kernel(x, norm_mix_g, w_in, b_gate, pool_w, pool_scale, sgu_ln_g, sgu_ln_b, sgu_w, sgu_b, ssm_a_re, ssm_a_im, ssm_log_dt, ssm_b_re, ssm_b_im, ssm_c_re, ssm_c_im, ssm_d, glu_w, glu_b, w_branch, w_out, norm_ffn_g, router_group_w, router_group_b, router_expert_w, router_expert_b, exp_w_gate, exp_w_up, exp_w_down, norm_final_g):
    batch, seq, d = x.shape
    depth = w_in.shape[0]
    tm_in, tm_mix, tm_moe, bm = 1024, 512, 512, 512
    s5_seqs = 2
    x2d = x.reshape(batch * seq, d)
    ssm_col_block = 3
    row = lambda a: a[:, None, :]
    p = {
        "norm_mix_g": row(norm_mix_g), "w_in": w_in.astype(BF16), "b_gate": row(b_gate),
        "pool_w": pool_w.astype(BF16), "pool_scale": row(pool_scale),
        "sgu_ln_g": row(sgu_ln_g), "sgu_ln_b": row(sgu_ln_b), "sgu_w": sgu_w,
        "sgu_b_full": jnp.repeat(jnp.swapaxes(sgu_b, 1, 2), LANE, axis=2),
        "glu_w": glu_w.astype(BF16), "glu_b": row(glu_b),
        "w_branch": w_branch.astype(BF16), "w_out": w_out.astype(BF16),
    }
    bq, cq, kq, lam_q = jax.vmap(ssm_tables)(ssm_a_re, ssm_a_im, ssm_log_dt, ssm_b_re, ssm_b_im,
                                             ssm_c_re, ssm_c_im, ssm_d)
    wr, br = router_params(router_group_w, router_group_b, router_expert_w, router_expert_b)
    g_ffn = row(norm_ffn_g)
    xc4 = ssm_input(x2d, p["norm_mix_g"], p["w_in"], 0, ssm_col_block, tm_in)
    for l in range(depth):
        ys4 = s5_scan(xc4, bq, cq, kq, lam_q, l, batch, s5_seqs)
        x2d = mixer(x2d, ys4, p, l, batch, tm_mix)
        if l == depth - 1:
            x2d = moe(x2d, g_ffn, wr, br, exp_w_gate, exp_w_up, exp_w_down, l, tm_moe, bm, g_final=norm_final_g)
        else:
            x2d, xc4 = moe(x2d, g_ffn, wr, br, exp_w_gate, exp_w_up, exp_w_down, l, tm_moe, bm,
                           next_mix=(p["norm_mix_g"], p["w_in"], l + 1, ssm_col_block))
    return x2d.reshape(batch, seq, d)
```
